```python
import math
import jax, jax.numpy as jnp
from jax import lax
import numpy as np

D_MODEL = 1024
BATCH = 1
SEQ = 16384
DEPTH = 1

D_MIX = D_MODEL
D_CONV = D_MIX // 2
CONV_WIDTH = 31
N_HEADS = 8
N_KV = 2
HPG = N_HEADS // N_KV
HEAD_DIM = 64
D_ATT = N_HEADS * HEAD_DIM
D_KV = N_KV * HEAD_DIM
N_GATES = 3
D_IN = 2 * D_CONV + D_ATT + 6 * D_KV + N_GATES * N_HEADS
CMP_LEN = 32
CMP_STRIDE = 16
CMP_HIDDEN = 128
SLC_LEN = 64
SLC_TOPK = 16
WINDOW = 512
Q_BLOCK = 128
FORCE_SCORE = 1.0e4
N_BUCKETS = 32
MAX_DIST = 128
D_FF = 2816
FFN_RES = 0.5
EPS = 1e-6

kernel_name = "hymba_conformer_nsa_macaron_block"


def rmsnorm(x, g):
    xf = x.astype(jnp.float32)
    y = xf * lax.rsqrt(jnp.mean(xf * xf, -1, keepdims=True) + EPS)
    return (y * g.astype(jnp.float32)).astype(x.dtype)


def layernorm(x, g, b):
    xf = x.astype(jnp.float32)
    mu = jnp.mean(xf, -1, keepdims=True)
    var = jnp.mean(jnp.square(xf - mu), -1, keepdims=True)
    y = (xf - mu) * lax.rsqrt(var + EPS)
    return (y * g.astype(jnp.float32) + b.astype(jnp.float32)).astype(x.dtype)


def modulate(h, shift, scale):
    return h * (1.0 + scale[:, None, :]) + shift[:, None, :]


def swiglu(h, w_gu, w_down):
    g, u = jnp.split(h @ w_gu, 2, axis=-1)
    return (jax.nn.silu(g) * u) @ w_down


def t5_bucket(dist):
    max_exact = N_BUCKETS // 2
    d = jnp.maximum(dist, 0)
    df = jnp.maximum(d, 1).astype(jnp.float32)
    large = max_exact + (jnp.log(df / max_exact) / math.log(MAX_DIST / max_exact)
                         * (N_BUCKETS - max_exact)).astype(jnp.int32)
    large = jnp.minimum(large, N_BUCKETS - 1)
    return jnp.where(d < max_exact, d, large)


def masked_softmax(logits, mask):
    s = jnp.where(mask, logits.astype(jnp.float32), -1e30)
    m = jnp.max(s, -1, keepdims=True)
    p = jnp.exp(s - m) * mask
    return p / jnp.maximum(jnp.sum(p, -1, keepdims=True), 1e-30)


def conformer_conv(u, w_dw, b_dw, ln_g, ln_b):
    a, gt = jnp.split(u, 2, axis=-1)
    h = a * jax.nn.sigmoid(gt)
    hp = jnp.pad(h, ((0, 0), (CONV_WIDTH - 1, 0), (0, 0)))
    y = lax.conv_general_dilated(hp, w_dw[:, None, :], window_strides=(1,), padding='VALID',
                                 dimension_numbers=('NWC', 'WIO', 'NWC'),
                                 feature_group_count=D_CONV) + b_dw
    return jax.nn.silu(layernorm(y, ln_g, ln_b))


def compress(kv, pe, w1, w2):
    b = kv.shape[0]
    n_cmp = (SEQ - CMP_LEN) // CMP_STRIDE + 1
    idx = jnp.arange(n_cmp)[:, None] * CMP_STRIDE + jnp.arange(CMP_LEN)[None, :]
    blk = kv[:, idx] + pe[None, None, :, None, :]
    blk = jnp.swapaxes(blk, 2, 3).reshape(b, n_cmp, N_KV, CMP_LEN * HEAD_DIM)
    return jax.nn.silu(blk @ w1) @ w2


def nsa_attention(q, k_cmp, v_cmp, k_slc, v_slc, k_win, v_win, gates, rel_bias):
    b = q.shape[0]
    n_cmp = k_cmp.shape[1]
    n_slc = SEQ // SLC_LEN
    n_top = min(SLC_TOPK, n_slc)
    n_qb = SEQ // Q_BLOCK
    scale = HEAD_DIM ** -0.5
    tbl = rel_bias.reshape(N_KV, HPG, N_BUCKETS)
    g_ix = jnp.arange(N_KV)[None, :, None, None, None]
    h_ix = jnp.arange(HPG)[None, None, :, None, None]
    cmp_start = jnp.arange(n_cmp) * CMP_STRIDE
    cmp_end = cmp_start + CMP_LEN - 1
    slc_start = jnp.arange(n_slc) * SLC_LEN
    overlap = ((cmp_start[:, None] < slc_start[None, :] + SLC_LEN)
               & (cmp_start[:, None] + CMP_LEN > slc_start[None, :])).astype(jnp.float32)
    k_slc_t = jnp.swapaxes(k_slc, 1, 2)
    v_slc_t = jnp.swapaxes(v_slc, 1, 2)
    pad = ((0, 0), (WINDOW, 0), (0, 0), (0, 0))
    k_win_p = jnp.pad(k_win, pad)
    v_win_p = jnp.pad(v_win, pad)
    blk_off = jnp.arange(SLC_LEN)
    win_off = jnp.arange(WINDOW + Q_BLOCK)

    def one_block(qb):
        q0 = qb * Q_BLOCK
        t = q0 + jnp.arange(Q_BLOCK)
        qq = lax.dynamic_slice_in_dim(q, q0, Q_BLOCK, 1)
        d_c = t[:, None] - cmp_end[None, :]
        s = jnp.einsum('bqghd,bngd->bghqn', qq, k_cmp) * scale + tbl[:, :, t5_bucket(d_c)]
        p_c = masked_softmax(s, d_c >= 0)
        o_c = jnp.einsum('bghqn,bngd->bqghd', p_c, v_cmp)
        imp = jnp.einsum('bghqn,nj->bgqj', p_c, overlap)
        cur = (t // SLC_LEN)[:, None]
        jj = jnp.arange(n_slc)[None, :]
        forced = (jj == 0) | (jj == cur) | (jj == cur - 1)
        score = jnp.where(forced, FORCE_SCORE, jnp.where(jj <= cur, imp, -1.0))
        _, sel = lax.top_k(score, n_top)
        pos = (sel[..., None] * SLC_LEN + blk_off).reshape(b, N_KV, Q_BLOCK, n_top * SLC_LEN)
        flat = pos.reshape(b, N_KV, Q_BLOCK * n_top * SLC_LEN, 1)
        k_g = jnp.take_along_axis(k_slc_t, flat, axis=2).reshape(b, N_KV, Q_BLOCK, n_top * SLC_LEN, HEAD_DIM)
        v_g = jnp.take_along_axis(v_slc_t, flat, axis=2).reshape(b, N_KV, Q_BLOCK, n_top * SLC_LEN, HEAD_DIM)
        d_s = t[None, None, :, None] - pos
        s = jnp.einsum('bqghd,bgqkd->bghqk', qq, k_g) * scale
        s = s + tbl[g_ix, h_ix, t5_bucket(d_s)[:, :, None]]
        p_s = masked_softmax(s, (d_s >= 0)[:, :, None])
        o_s = jnp.einsum('bghqk,bgqkd->bqghd', p_s, v_g)
        k_wb = lax.dynamic_slice_in_dim(k_win_p, q0, WINDOW + Q_BLOCK, 1)
        v_wb = lax.dynamic_slice_in_dim(v_win_p, q0, WINDOW + Q_BLOCK, 1)
        kpos = q0 - WINDOW + win_off
        d_w = t[:, None] - kpos[None, :]
        m_w = (d_w >= 0) & (d_w < WINDOW) & (kpos[None, :] >= 0)
        s = jnp.einsum('bqghd,bkgd->bghqk', qq, k_wb) * scale + tbl[:, :, t5_bucket(d_w)]
        p_w = masked_softmax(s, m_w)
        o_w = jnp.einsum('bghqk,bkgd->bqghd', p_w, v_wb)
        g = jax.nn.sigmoid(lax.dynamic_slice_in_dim(gates, q0, Q_BLOCK, 1).astype(jnp.float32))
        return g[..., 0:1] * o_c + g[..., 1:2] * o_s + g[..., 2:3] * o_w

    o = lax.map(one_block, jnp.arange(n_qb))
    return jnp.moveaxis(o, 0, 1).reshape(b, SEQ, D_ATT)


def setup_inputs(seed: int = 0) -> dict:
    key = jax.random.key(seed)
    ks = jax.random.split(key, 32)
    L = DEPTH

    def nrm(k, shape, s):
        return jax.random.normal(k, shape, jnp.float32) * s

    return {
        "x": nrm(ks[0], (BATCH, SEQ, D_MODEL), 1.0),
        "c": nrm(ks[1], (BATCH, D_MODEL), 1.0),
        "w_ada": nrm(ks[2], (L, D_MODEL, 9 * D_MODEL), D_MODEL ** -0.5),
        "b_ada": nrm(ks[3], (L, 9 * D_MODEL), 0.01),
        "g_ffn1": 1.0 + nrm(ks[4], (L, D_MODEL), 0.02),
        "w_gu1": nrm(ks[5], (L, D_MODEL, 2 * D_FF), D_MODEL ** -0.5),
        "w_down1": nrm(ks[6], (L, D_FF, D_MODEL), D_FF ** -0.5),
        "g_mix": 1.0 + nrm(ks[7], (L, D_MODEL), 0.02),
        "w_in": nrm(ks[8], (L, D_MODEL, D_IN), D_MODEL ** -0.5),
        "w_dw": nrm(ks[9], (L, CONV_WIDTH, D_CONV), CONV_WIDTH ** -0.5),
        "b_dw": nrm(ks[10], (L, D_CONV), 0.01),
        "ln_g": 1.0 + nrm(ks[11], (L, D_CONV), 0.02),
        "ln_b": nrm(ks[12], (L, D_CONV), 0.01),
        "pe_k": nrm(ks[13], (L, CMP_LEN, HEAD_DIM), 0.1),
        "pe_v": nrm(ks[14], (L, CMP_LEN, HEAD_DIM), 0.1),
        "w_ck1": nrm(ks[15], (L, CMP_LEN * HEAD_DIM, CMP_HIDDEN), (CMP_LEN * HEAD_DIM) ** -0.5),
        "w_ck2": nrm(ks[16], (L, CMP_HIDDEN, HEAD_DIM), CMP_HIDDEN ** -0.5),
        "w_cv1": nrm(ks[17], (L, CMP_LEN * HEAD_DIM, CMP_HIDDEN), (CMP_LEN * HEAD_DIM) ** -0.5),
        "w_cv2": nrm(ks[18], (L, CMP_HIDDEN, HEAD_DIM), CMP_HIDDEN ** -0.5),
        "rel_bias": nrm(ks[19], (N_HEADS, N_BUCKETS), 0.5),
        "w_out": nrm(ks[20], (L, D_MIX, D_MODEL), D_MIX ** -0.5),
        "g_ffn2": 1.0 + nrm(ks[21], (L, D_MODEL), 0.02),
        "w_gu2": nrm(ks[22], (L, D_MODEL, 2 * D_FF), D_MODEL ** -0.5),
        "w_down2": nrm(ks[23], (L, D_FF, D_MODEL), D_FF ** -0.5),
        "g_final": 1.0 + nrm(ks[24], (D_MODEL,), 0.02),
    }


def reference(x, c, w_ada, b_ada, g_ffn1, w_gu1, w_down1, g_mix, w_in, w_dw, b_dw, ln_g, ln_b,
              pe_k, pe_v, w_ck1, w_ck2, w_cv1, w_cv2, rel_bias, w_out, g_ffn2, w_gu2, w_down2,
              g_final):
    b = x.shape[0]
    splits = [2 * D_CONV, 2 * D_CONV + D_ATT]
    splits += [splits[-1] + D_KV * (i + 1) for i in range(6)]
    for l in range(DEPTH):
        mod = jax.nn.silu(c) @ w_ada[l] + b_ada[l]
        sh1, sc1, gt1, sh2, sc2, gt2, sh3, sc3, gt3 = jnp.split(mod, 9, axis=-1)
        h = modulate(rmsnorm(x, g_ffn1[l]), sh1, sc1)
        x = x + FFN_RES * gt1[:, None, :] * swiglu(h, w_gu1[l], w_down1[l])
        h = modulate(rmsnorm(x, g_mix[l]), sh2, sc2)
        u = h @ w_in[l]
        u_conv, q, kc, vc, ksl, vsl, kw, vw, gl = jnp.split(u, splits, axis=-1)
        y_conv = conformer_conv(u_conv, w_dw[l], b_dw[l], ln_g[l], ln_b[l])
        to_kv = lambda a: a.reshape(b, SEQ, N_KV, HEAD_DIM)
        k_cmp = compress(to_kv(kc), pe_k[l], w_ck1[l], w_ck2[l])
        v_cmp = compress(to_kv(vc), pe_v[l], w_cv1[l], w_cv2[l])
        y_att = nsa_attention(q.reshape(b, SEQ, N_KV, HPG, HEAD_DIM), k_cmp, v_cmp,
                              to_kv(ksl), to_kv(vsl), to_kv(kw), to_kv(vw),
                              gl.reshape(b, SEQ, N_KV, HPG, N_GATES), rel_bias).astype(x.dtype)
        y = jnp.concatenate([y_conv, y_att], axis=-1)
        x = x + gt2[:, None, :] * (y @ w_out[l])
        h = modulate(rmsnorm(x, g_ffn2[l]), sh3, sc3)
        x = x + FFN_RES * gt3[:, None, :] * swiglu(h, w_gu2[l], w_down2[l])
    return rmsnorm(x, g_final)
```

```python
import functools
import math

import jax
import jax.numpy as jnp
from jax import lax
from jax.experimental import pallas as pl
from jax.experimental.pallas import tpu as pltpu

F32 = jnp.float32
BF16 = jnp.bfloat16

D_MODEL = 1024
SEQ = 16384
D_CONV = 512
CONV_WIDTH = 31
N_HEADS = 8
N_KV = 2
HPG = 4
HEAD_DIM = 64
D_ATT = 512
D_KV = 128
CMP_LEN = 32
CMP_STRIDE = 16
CMP_HIDDEN = 128
N_CMP = (SEQ - CMP_LEN) // CMP_STRIDE + 1
N_CMP_PAD = 1024
SLC_LEN = 64
N_SLC = SEQ // SLC_LEN
SLC_TOPK = 16
WINDOW = 512
FORCE_SCORE = 1.0e4
N_BUCKETS = 32
MAX_DIST = 128
D_FF = 2816
FFN_RES = 0.5
EPS = 1e-6
NEG = -1e30
M_FLOOR = -1e29

V7X_VMEM_BYTES = 64 * 1024 * 1024
VMEM_LIMIT = V7X_VMEM_BYTES - 6 * 1024 * 1024
LANES = 128

TQ = 256
TK = 256
CMP_BACK = N_CMP_PAD - 16
FFN_TM = 512
FFN_TF = D_FF // 2
PROJ_TM = 512
CONV_TM = 256
CONV_HALO = 32
D_IN_PAD = 2432


def _params(sem):
    return pltpu.CompilerParams(dimension_semantics=sem, vmem_limit_bytes=VMEM_LIMIT)


def _const_spec(shape):
    nd = len(shape)
    return pl.BlockSpec(shape, lambda *_: (0,) * nd, pipeline_mode=pl.Buffered(1))


def _sigmoid(v):
    return 1.0 / (1.0 + jnp.exp(-v))


def _nt_dot(a, b):
    return lax.dot_general(a, b, (((1,), (1,)), ((), ())), preferred_element_type=F32)


def _dot(a, b):
    return jnp.dot(a, b, preferred_element_type=F32)


def _ada_kernel(c_ref, w_ref, b_ref, o_ref):
    c = c_ref[...]
    sc = c * _sigmoid(c)
    o_ref[...] = jnp.sum(w_ref[...] * sc, axis=0, keepdims=True) + b_ref[...]


def _ada(c_col, w, b):
    n = w.shape[1]
    tn = n // 8
    return pl.pallas_call(
        _ada_kernel,
        out_shape=jax.ShapeDtypeStruct((1, n), F32),
        grid=(8,),
        in_specs=[pl.BlockSpec((D_MODEL, 1), lambda j: (0, 0)),
                  pl.BlockSpec((D_MODEL, tn), lambda j: (0, j)),
                  pl.BlockSpec((1, tn), lambda j: (0, j))],
        out_specs=pl.BlockSpec((1, tn), lambda j: (0, j)),
        compiler_params=_params(("arbitrary",)),
        name="ada",
    )(c_col, w, b)


def _rms_mod(x, g, sh, sc):
    ms = jnp.mean(x * x, axis=-1, keepdims=True)
    y = x * lax.rsqrt(ms + EPS) * g
    return y * (1.0 + sc) + sh


def _ffn_body(x, g_ref, sh_ref, sc_ref, gt_ref, wgu_ref, wd_ref):
    hb = _rms_mod(x, g_ref[...], sh_ref[...], sc_ref[...]).astype(BF16)
    acc = jnp.zeros((x.shape[0], D_MODEL), F32)
    for f in range(D_FF // FFN_TF):
        gg = _dot(hb, wgu_ref[:, f * FFN_TF:(f + 1) * FFN_TF])
        uu = _dot(hb, wgu_ref[:, D_FF + f * FFN_TF:D_FF + (f + 1) * FFN_TF])
        a = (gg * _sigmoid(gg) * uu).astype(BF16)
        acc = acc + _dot(a, wd_ref[f * FFN_TF:(f + 1) * FFN_TF, :])
    return x + (FFN_RES * gt_ref[...]) * acc


def _ffn1_kernel(x_ref, g_ref, sh_ref, sc_ref, gt_ref, wgu_ref, wd_ref, o_ref):
    o_ref[...] = _ffn_body(x_ref[...], g_ref, sh_ref, sc_ref, gt_ref, wgu_ref, wd_ref)


def _ffn2_kernel(x_ref, yc_ref, ya_ref, woc_ref, woa_ref, gt2_ref,
                 g_ref, sh_ref, sc_ref, gt_ref, wgu_ref, wd_ref, gf_ref, o_ref):
    tm = x_ref.shape[0]
    y = _dot(yc_ref[...], woc_ref[...])
    for h in range(HPG):
        y = y + _dot(ya_ref[:, h].reshape(tm, LANES), woa_ref[h])
    x = x_ref[...] + gt2_ref[...] * y
    out = _ffn_body(x, g_ref, sh_ref, sc_ref, gt_ref, wgu_ref, wd_ref)
    ms = jnp.mean(out * out, axis=-1, keepdims=True)
    o_ref[...] = out * lax.rsqrt(ms + EPS) * gf_ref[...]


def _row_spec(tm, n):
    return pl.BlockSpec((tm, n), lambda i: (i, 0))


def _ffn1(x, g, sh, sc, gt, wgu, wd):
    vec = _const_spec((1, D_MODEL))
    return pl.pallas_call(
        _ffn1_kernel,
        out_shape=jax.ShapeDtypeStruct((SEQ, D_MODEL), F32),
        grid=(SEQ // FFN_TM,),
        in_specs=[_row_spec(FFN_TM, D_MODEL), vec, vec, vec, vec,
                  _const_spec((D_MODEL, 2 * D_FF)), _const_spec((D_FF, D_MODEL))],
        out_specs=_row_spec(FFN_TM, D_MODEL),
        compiler_params=_params(("arbitrary",)),
        name="ffn1",
    )(x, g, sh, sc, gt, wgu, wd)


def _ffn2(x, yc, ya, woc, woa, gt2, g, sh, sc, gt, wgu, wd, gf):
    vec = _const_spec((1, D_MODEL))
    nt = FFN_TM // TQ
    return pl.pallas_call(
        _ffn2_kernel,
        out_shape=jax.ShapeDtypeStruct((SEQ, D_MODEL), F32),
        grid=(SEQ // FFN_TM,),
        in_specs=[_row_spec(FFN_TM, D_MODEL), _row_spec(FFN_TM, D_CONV),
                  pl.BlockSpec((nt, HPG, TQ, LANES), lambda i: (i, 0, 0, 0)),
                  _const_spec((D_CONV, D_MODEL)), _const_spec((HPG, LANES, D_MODEL)), vec,
                  vec, vec, vec, vec,
                  _const_spec((D_MODEL, 2 * D_FF)), _const_spec((D_FF, D_MODEL)), vec],
        out_specs=_row_spec(FFN_TM, D_MODEL),
        compiler_params=_params(("arbitrary",)),
        name="ffn2",
    )(x, yc, ya, woc, woa, gt2, g, sh, sc, gt, wgu, wd, gf)


def _proj_kernel(x_ref, g_ref, sh_ref, sc_ref, w_ref,
                 hc_ref, q_ref, kcvc_ref, ksl_ref, vsl_ref, kw_ref, vw_ref, gate_ref):
    hb = _rms_mod(x_ref[...], g_ref[...], sh_ref[...], sc_ref[...]).astype(BF16)
    u = _dot(hb, w_ref[...])
    hc_ref[...] = u[:, 0:512] * _sigmoid(u[:, 512:1024])
    q_ref[...] = u[:, 1024:1536].astype(BF16)
    kcvc_ref[...] = u[:, 1536:1792]
    ksl_ref[...] = u[:, 1792:1920].astype(BF16)
    vsl_ref[...] = u[:, 1920:2048].astype(BF16)
    kw_ref[...] = u[:, 2048:2176].astype(BF16)
    vw_ref[...] = u[:, 2176:2304].astype(BF16)
    gate_ref[...] = _sigmoid(u[:, 2304:2432])


def _proj(x, g, sh, sc, w):
    vec = _const_spec((1, D_MODEL))
    tm = PROJ_TM
    shapes = [(D_CONV, F32), (D_ATT, BF16), (2 * D_KV, F32), (D_KV, BF16), (D_KV, BF16),
              (D_KV, BF16), (D_KV, BF16), (LANES, F32)]
    return pl.pallas_call(
        _proj_kernel,
        out_shape=[jax.ShapeDtypeStruct((SEQ, n), dt) for n, dt in shapes],
        grid=(SEQ // tm,),
        in_specs=[_row_spec(tm, D_MODEL), vec, vec, vec, _const_spec((D_MODEL, D_IN_PAD))],
        out_specs=[_row_spec(tm, n) for n, _ in shapes],
        compiler_params=_params(("arbitrary",)),
        name="proj",
    )(x, g, sh, sc, w)


def _conv_kernel(prev_ref, cur_ref, w_ref, b_ref, g_ref, bb_ref, o_ref):
    i = pl.program_id(0)
    tm = cur_ref.shape[0]
    prev = jnp.where(i > 0, prev_ref[...], 0.0)
    xx = jnp.concatenate([prev, cur_ref[...]], axis=0)
    acc = jnp.zeros((tm, D_CONV), F32) + b_ref[...]
    off = CONV_HALO - (CONV_WIDTH - 1)
    for w in range(CONV_WIDTH):
        acc = acc + xx[off + w:off + w + tm, :] * w_ref[w:w + 1, :]
    mu = jnp.mean(acc, axis=-1, keepdims=True)
    dlt = acc - mu
    var = jnp.mean(dlt * dlt, axis=-1, keepdims=True)
    y = dlt * lax.rsqrt(var + EPS) * g_ref[...] + bb_ref[...]
    o_ref[...] = (y * _sigmoid(y)).astype(BF16)


def _conv(hc, w, b, g, bb):
    tm = CONV_TM
    r = tm // CONV_HALO
    vec = _const_spec((1, D_CONV))
    return pl.pallas_call(
        _conv_kernel,
        out_shape=jax.ShapeDtypeStruct((SEQ, D_CONV), BF16),
        grid=(SEQ // tm,),
        in_specs=[pl.BlockSpec((CONV_HALO, D_CONV), lambda i: (jnp.maximum(i * r - 1, 0), 0)),
                  _row_spec(tm, D_CONV),
                  _const_spec((CONV_WIDTH, D_CONV)), vec, vec, vec],
        out_specs=_row_spec(tm, D_CONV),
        compiler_params=_params(("arbitrary",)),
        name="conv",
    )(hc, hc, w, b, g, bb)


def _compress_kernel(c_ref, pe_ref, w1_ref, w2_ref, o_ref):
    half = CMP_STRIDE * HEAD_DIM
    cm = c_ref[0]
    pe = pe_ref[0]
    top = (cm + pe[:, :half]).astype(BF16)
    bot = (cm + pe[:, half:]).astype(BF16)
    a = _dot(top, w1_ref[0, :half, :])
    b = _dot(bot, w1_ref[0, half:, :])
    b_up = jnp.concatenate([b[1:], jnp.zeros((1, CMP_HIDDEN), F32)], axis=0)
    pre = a + b_up
    hid = (pre * _sigmoid(pre)).astype(BF16)
    o_ref[0] = _dot(hid, w2_ref[0])


def _compress(cmat, pe, w1, w2):
    return pl.pallas_call(
        _compress_kernel,
        out_shape=jax.ShapeDtypeStruct((4, N_CMP_PAD, HEAD_DIM), F32),
        grid=(4,),
        in_specs=[pl.BlockSpec((1, N_CMP_PAD, 1024), lambda i: (i, 0, 0)),
                  pl.BlockSpec((1, 1, 2048), lambda i: (i // 2, 0, 0)),
                  pl.BlockSpec((1, 2048, CMP_HIDDEN), lambda i: (i // 2, 0, 0)),
                  pl.BlockSpec((1, CMP_HIDDEN, HEAD_DIM), lambda i: (i // 2, 0, 0))],
        out_specs=pl.BlockSpec((1, N_CMP_PAD, HEAD_DIM), lambda i: (i, 0, 0)),
        compiler_params=_params(("arbitrary",)),
        name="compress",
    )(cmat, pe, w1, w2)


def _softmax_parts(s):
    m = jnp.maximum(jnp.max(s, axis=1, keepdims=True), M_FLOOR)
    return m, jnp.exp(s - m)


def _attn_kernel(q_ref, gate_ref, kc_ref, vc_ref, ov_ref, ksa_ref, vsl_ref, kw_ref, vw_ref,
                 acmp_ref, band_ref, mwin_ref, mnear_ref, o_ref):
    qb = pl.program_id(0)
    q0 = pl.multiple_of(qb * TQ, TQ)
    rows = HPG * TQ
    qt = q_ref[...]
    gates = gate_ref[...]
    lane = lax.broadcasted_iota(jnp.int32, (TQ, LANES), 1)
    ones_k = jnp.ones((TK, LANES), BF16)

    def with_ones(v):
        return jnp.concatenate([v, jnp.ones((v.shape[0], LANES), BF16)], axis=1)

    comb = []
    for g in range(N_KV):
        keep = (lane >= HEAD_DIM * g) & (lane < HEAD_DIM * (g + 1))
        qpad = jnp.concatenate(
            [jnp.where(keep, qt[:, h * LANES:(h + 1) * LANES], jnp.zeros((), BF16))
             for h in range(HPG)], axis=0)

        cs = pl.multiple_of(qb * (TQ // CMP_STRIDE), TQ // CMP_STRIDE)
        s = _nt_dot(qpad, kc_ref[pl.ds(cs, N_CMP_PAD), :])
        ccol = lax.broadcasted_iota(jnp.int32, (1, N_CMP_PAD), 1)
        s = s + jnp.where(ccol + cs >= CMP_BACK, 0.0, NEG)
        s = jnp.concatenate([s[:, :N_CMP_PAD - LANES], s[:, N_CMP_PAD - LANES:] + acmp_ref[g]],
                            axis=1)
        m, p = _softmax_parts(s)
        rinv = 1.0 / jnp.maximum(jnp.sum(p, axis=1, keepdims=True), 1e-30)
        o_c = _dot(p.astype(BF16), vc_ref[pl.ds(cs, N_CMP_PAD), :]) * rinv
        pn = p * rinv
        psum = pn[0:TQ] + pn[TQ:2 * TQ] + pn[2 * TQ:3 * TQ] + pn[3 * TQ:4 * TQ]
        ov = ov_ref[pl.ds(cs, N_CMP_PAD), :]
        hi = psum.astype(BF16)
        r1 = psum - hi.astype(F32)
        mid = r1.astype(BF16)
        lo = (r1 - mid.astype(F32)).astype(BF16)
        imp = _dot(hi, ov) + _dot(mid, ov) + _dot(lo, ov)

        ti = q0 + lax.broadcasted_iota(jnp.int32, (TQ, 1), 0)
        cur = jnp.right_shift(ti, 6)
        jj = lax.broadcasted_iota(jnp.int32, (1, N_SLC), 1)
        forced = (jj == 0) | (jj == cur) | (jj == cur - 1)
        score = jnp.where(forced, FORCE_SCORE, jnp.where(jj <= cur, imp, -1.0))
        sc_t = score.T
        jf = lax.broadcasted_iota(jnp.int32, (N_SLC, TQ), 0).astype(F32)
        sel_t = jnp.zeros((N_SLC, TQ), F32)
        for _ in range(SLC_TOPK):
            mx = jnp.max(sc_t, axis=0, keepdims=True)
            idx = jnp.min(jnp.where(sc_t == mx, jf, 1e9), axis=0, keepdims=True)
            hit = jf == idx
            sel_t = jnp.where(hit, 1.0, sel_t)
            sc_t = jnp.where(hit, -2.0, sc_t)
        selneg = jnp.where(sel_t.T > 0.5, 0.0, NEG).astype(BF16)
        lhs = jnp.concatenate([qpad, jnp.concatenate([selneg] * HPG, axis=0)], axis=1)

        s = _nt_dot(lhs, ksa_ref[pl.ds(q0, 2 * TQ), :])
        ncol = lax.broadcasted_iota(jnp.int32, (1, 2 * TQ), 1)
        s = s.reshape(HPG, TQ, 2 * TQ) + (mnear_ref[...] + jnp.where(ncol + q0 >= TQ, 0.0, NEG))[None]
        s = s.reshape(rows, 2 * TQ)
        s = jnp.concatenate([s[:, :LANES], s[:, LANES:] + band_ref[g]], axis=1)
        m, p = _softmax_parts(s)
        acc = _dot(p.astype(BF16), with_ones(vsl_ref[pl.ds(q0, 2 * TQ), :]))

        def far_body(kt, carry):
            m_run, acc_run = carry
            k0 = pl.multiple_of(TQ + kt * TK, TK)
            sf = _nt_dot(lhs, ksa_ref[pl.ds(k0, TK), :])
            m_new = jnp.maximum(m_run, jnp.max(sf, axis=1, keepdims=True))
            alpha = jnp.exp(m_run - m_new)
            pf = jnp.exp(sf - m_new).astype(BF16)
            v_aug = jnp.concatenate([vsl_ref[pl.ds(k0, TK), :], ones_k], axis=1)
            return m_new, alpha * acc_run + _dot(pf, v_aug)

        m, acc = lax.fori_loop(0, jnp.maximum(qb - 1, 0), far_body, (m, acc))
        o_s = acc[:, :LANES] / jnp.maximum(acc[:, LANES:], 1e-30)

        wk = WINDOW + TQ
        s = _nt_dot(qpad, kw_ref[pl.ds(q0, wk), :])
        wcol = lax.broadcasted_iota(jnp.int32, (1, wk), 1)
        s = s.reshape(HPG, TQ, wk) + (mwin_ref[...] + jnp.where(wcol + q0 >= WINDOW, 0.0, NEG))[None]
        s = s.reshape(rows, wk)
        s = jnp.concatenate([s[:, :wk - 3 * LANES], s[:, wk - 3 * LANES:] + band_ref[g]], axis=1)
        m, p = _softmax_parts(s)
        ow = _dot(p.astype(BF16), with_ones(vw_ref[pl.ds(q0, wk), :]))
        o_w = ow[:, :LANES] / jnp.maximum(ow[:, LANES:], 1e-30)

        per_head = []
        for h in range(HPG):
            c0 = 3 * (g * HPG + h)
            r = slice(h * TQ, (h + 1) * TQ)
            per_head.append(gates[:, c0:c0 + 1] * o_c[r] + gates[:, c0 + 1:c0 + 2] * o_s[r]
                            + gates[:, c0 + 2:c0 + 3] * o_w[r])
        comb.append(per_head)

    for h in range(HPG):
        o_ref[0, h] = jnp.where(lane < HEAD_DIM, comb[0][h], comb[1][h]).astype(BF16)


def _attn(q, gates, kc, vc, ov, ksa, vsl, kw, vw, acmp, band, mwin, mnear):
    consts = [kc, vc, ov, ksa, vsl, kw, vw, acmp, band, mwin, mnear]
    return pl.pallas_call(
        _attn_kernel,
        out_shape=jax.ShapeDtypeStruct((SEQ // TQ, HPG, TQ, LANES), BF16),
        grid=(SEQ // TQ,),
        in_specs=[_row_spec(TQ, D_ATT), _row_spec(TQ, LANES)]
                 + [_const_spec(a.shape) for a in consts],
        out_specs=pl.BlockSpec((1, HPG, TQ, LANES), lambda i: (i, 0, 0, 0)),
        compiler_params=_params(("arbitrary",)),
        name="nsa_attn",
    )(q, gates, *consts)


def _t5_bucket(dist):
    max_exact = N_BUCKETS // 2
    d = jnp.maximum(dist, 0)
    df = jnp.maximum(d, 1).astype(F32)
    large = max_exact + (jnp.log(df / max_exact) / math.log(MAX_DIST / max_exact)
                         * (N_BUCKETS - max_exact)).astype(jnp.int32)
    large = jnp.minimum(large, N_BUCKETS - 1)
    return jnp.where(d < max_exact, d, large)


def _bias_tables(rel_bias):
    biasd = rel_bias[:, _t5_bucket(jnp.arange(MAX_DIST))]
    biasp = (biasd - rel_bias[:, N_BUCKETS - 1:]).reshape(N_KV, HPG, MAX_DIST)
    i = jnp.arange(TQ)[:, None]

    def lookup(d, valid, fill):
        t = biasp[:, :, jnp.clip(d, 0, MAX_DIST - 1)]
        t = jnp.where(valid[None, None], t, fill)
        return t.reshape(N_KV, HPG * TQ, d.shape[1]).astype(F32)

    d_band = i + LANES - jnp.arange(3 * LANES)[None, :]
    band = lookup(d_band, (d_band >= 0) & (d_band < MAX_DIST), 0.0)
    d_win = i + WINDOW - jnp.arange(WINDOW + TQ)[None, :]
    mwin = jnp.where((d_win >= 0) & (d_win < WINDOW), 0.0, NEG).astype(F32)
    d_near = i + TQ - jnp.arange(2 * TQ)[None, :]
    mnear = jnp.where(d_near >= 0, 0.0, NEG).astype(F32)
    r_cmp = jnp.arange(N_CMP_PAD - LANES, N_CMP_PAD)[None, :] - CMP_BACK
    d_cmp = i - (CMP_LEN - 1) - CMP_STRIDE * r_cmp
    acmp = lookup(d_cmp, d_cmp >= 0, NEG)
    return acmp, band, mwin, mnear


def _overlap_matrix():
    cmp_start = jnp.arange(N_CMP) * CMP_STRIDE
    slc_start = jnp.arange(N_SLC) * SLC_LEN
    ov = ((cmp_start[:, None] < slc_start[None, :] + SLC_LEN)
          & (cmp_start[:, None] + CMP_LEN > slc_start[None, :]))
    return jnp.pad(ov.astype(BF16), ((CMP_BACK, 2 * N_CMP_PAD - CMP_BACK - N_CMP), (0, 0)))


def _block_indicator():
    pos = jnp.arange(SEQ)[:, None] // SLC_LEN
    return (pos == jnp.arange(N_SLC)[None, :]).astype(BF16)


def kernel(x, c, w_ada, b_ada, g_ffn1, w_gu1, w_down1, g_mix, w_in, w_dw, b_dw, ln_g, ln_b,
           pe_k, pe_v, w_ck1, w_ck2, w_cv1, w_cv2, rel_bias, w_out, g_ffn2, w_gu2, w_down2,
           g_final):
    assert x.shape == (1, SEQ, D_MODEL) and w_ada.shape[0] == 1
    x2 = x[0]
    mod = _ada(c.reshape(D_MODEL, 1), w_ada[0], b_ada)
    sh1, sc1, gt1, sh2, sc2, gt2, sh3, sc3, gt3 = [mod[:, k * D_MODEL:(k + 1) * D_MODEL]
                                                   for k in range(9)]

    x1 = _ffn1(x2, g_ffn1, sh1, sc1, gt1, w_gu1[0].astype(BF16), w_down1[0].astype(BF16))

    wi = w_in[0]
    wq = wi[:, 1024:1536].reshape(D_MODEL, N_KV, HPG, HEAD_DIM).transpose(0, 2, 1, 3)
    wq = wq.reshape(D_MODEL, D_ATT) * (HEAD_DIM ** -0.5)
    w_in_p = jnp.concatenate(
        [wi[:, :1024], wq, wi[:, 1536:], jnp.zeros((D_MODEL, D_IN_PAD - wi.shape[1]), F32)],
        axis=1).astype(BF16)
    hc, q, kcvc, ksl, vsl, kw, vw, gates = _proj(x1, g_mix, sh2, sc2, w_in_p)

    y_conv = _conv(hc, w_dw[0], b_dw, ln_g, ln_b)

    cmat = kcvc.reshape(SEQ, 4, HEAD_DIM).transpose(1, 0, 2).reshape(4, N_CMP_PAD, 1024)
    pe = jnp.stack([pe_k[0].reshape(1, -1), pe_v[0].reshape(1, -1)])
    w1 = jnp.stack([w_ck1[0], w_cv1[0]]).astype(BF16)
    w2 = jnp.stack([w_ck2[0], w_cv2[0]]).astype(BF16)
    cmp = _compress(cmat, pe, w1, w2)
    back = ((CMP_BACK, 2 * N_CMP_PAD - CMP_BACK - N_CMP_PAD), (0, 0))
    kc = jnp.pad(jnp.concatenate([cmp[0], cmp[1]], axis=1).astype(BF16), back)
    vc = jnp.pad(jnp.concatenate([cmp[2], cmp[3]], axis=1).astype(BF16), back)

    acmp, band, mwin, mnear = _bias_tables(rel_bias)
    ksa = jnp.pad(jnp.concatenate([ksl, _block_indicator()], axis=1), ((TQ, 0), (0, 0)))
    vsl_p = jnp.pad(vsl, ((TQ, 0), (0, 0)))
    kw_p = jnp.pad(kw, ((WINDOW, 0), (0, 0)))
    vw_p = jnp.pad(vw, ((WINDOW, 0), (0, 0)))
    y_att = _attn(q, gates, kc, vc, _overlap_matrix(), ksa, vsl_p, kw_p, vw_p,
                  acmp, band, mwin, mnear)

    wo = w_out[0]
    woa = wo[D_CONV:].reshape(N_KV, HPG, HEAD_DIM, D_MODEL).transpose(1, 0, 2, 3)
    woa = woa.reshape(HPG, LANES, D_MODEL).astype(BF16)
    out = _ffn2(x1, y_conv, y_att, wo[:D_CONV].astype(BF16), woa, gt2,
                g_ffn2, sh3, sc3, gt3, w_gu2[0].astype(BF16), w_down2[0].astype(BF16),
                g_final.reshape(1, D_MODEL))
    return out[None]
```

```python
import math

import jax
import jax.numpy as jnp
from jax import lax
from jax.experimental import pallas as pl
from jax.experimental.pallas import tpu as pltpu

F32 = jnp.float32
BF16 = jnp.bfloat16

D_MODEL = 1024
SEQ = 16384
D_CONV = 512
CONV_WIDTH = 31
N_HEADS = 8
N_KV = 2
HPG = 4
HEAD_DIM = 64
D_ATT = 512
D_KV = 128
CMP_LEN = 32
CMP_STRIDE = 16
CMP_HIDDEN = 128
N_CMP = (SEQ - CMP_LEN) // CMP_STRIDE + 1
N_CMP_PAD = 1024
SLC_LEN = 64
N_SLC = SEQ // SLC_LEN
SLC_TOPK = 16
WINDOW = 512
FORCE_SCORE = 1.0e4
N_BUCKETS = 32
MAX_DIST = 128
D_FF = 2816
FFN_RES = 0.5
EPS = 1e-6
NEG = -1e30
M_FLOOR = -1e29

V7X_VMEM_BYTES = 64 * 1024 * 1024
VMEM_LIMIT = V7X_VMEM_BYTES - 6 * 1024 * 1024
LANES = 128
SUBLANES = 8

TQ = 256
TK = 256
COLS = HPG * TQ
KPAD = WINDOW
VROWS = D_KV + 16
BLK_PER_TILE = TK // SLC_LEN
CT_ZERO = 520
CT_ROWS = 800
FFN_TM = 512
FFN_TF = D_FF // 2
PROJ_TM = 512
CONV_TM = 256
CONV_HALO = 32
D_IN_PAD = 2432


def _params(sem):
    return pltpu.CompilerParams(dimension_semantics=sem, vmem_limit_bytes=VMEM_LIMIT)


def _const_spec(shape):
    nd = len(shape)
    return pl.BlockSpec(shape, lambda *_: (0,) * nd, pipeline_mode=pl.Buffered(1))


def _sigmoid(v):
    return 1.0 / (1.0 + jnp.exp(-v))


def _dot(a, b):
    return jnp.dot(a, b, preferred_element_type=F32)


def _ada_kernel(c_ref, w_ref, b_ref, o_ref):
    c = c_ref[...]
    sc = c * _sigmoid(c)
    o_ref[...] = jnp.sum(w_ref[...] * sc, axis=0, keepdims=True) + b_ref[...]


def _ada(c_col, w, b):
    n = w.shape[1]
    tn = n // 8
    return pl.pallas_call(
        _ada_kernel,
        out_shape=jax.ShapeDtypeStruct((1, n), F32),
        grid=(8,),
        in_specs=[pl.BlockSpec((D_MODEL, 1), lambda j: (0, 0)),
                  pl.BlockSpec((D_MODEL, tn), lambda j: (0, j)),
                  pl.BlockSpec((1, tn), lambda j: (0, j))],
        out_specs=pl.BlockSpec((1, tn), lambda j: (0, j)),
        compiler_params=_params(("arbitrary",)),
        name="ada",
    )(c_col, w, b)


def _rms_mod(x, g, sh, sc):
    ms = jnp.mean(x * x, axis=-1, keepdims=True)
    y = x * lax.rsqrt(ms + EPS) * g
    return y * (1.0 + sc) + sh


def _ffn_body(x, g_ref, sh_ref, sc_ref, gt_ref, wgu_ref, wd_ref):
    hb = _rms_mod(x, g_ref[...], sh_ref[...], sc_ref[...]).astype(BF16)
    acc = jnp.zeros((x.shape[0], D_MODEL), F32)
    for f in range(D_FF // FFN_TF):
        gg = _dot(hb, wgu_ref[:, f * FFN_TF:(f + 1) * FFN_TF])
        uu = _dot(hb, wgu_ref[:, D_FF + f * FFN_TF:D_FF + (f + 1) * FFN_TF])
        a = (gg * _sigmoid(gg) * uu).astype(BF16)
        acc = acc + _dot(a, wd_ref[f * FFN_TF:(f + 1) * FFN_TF, :])
    return x + (FFN_RES * gt_ref[...]) * acc


def _ffn1_kernel(x_ref, g_ref, sh_ref, sc_ref, gt_ref, wgu_ref, wd_ref, o_ref):
    o_ref[...] = _ffn_body(x_ref[...], g_ref, sh_ref, sc_ref, gt_ref, wgu_ref, wd_ref)


def _ffn2_kernel(x_ref, yc_ref, ya_ref, woc_ref, woa_ref, gt2_ref,
                 g_ref, sh_ref, sc_ref, gt_ref, wgu_ref, wd_ref, gf_ref, o_ref):
    tm = x_ref.shape[0]
    y = _dot(yc_ref[...], woc_ref[...])
    for h in range(HPG):
        y = y + _dot(ya_ref[:, h].reshape(tm, LANES), woa_ref[h])
    x = x_ref[...] + gt2_ref[...] * y
    out = _ffn_body(x, g_ref, sh_ref, sc_ref, gt_ref, wgu_ref, wd_ref)
    ms = jnp.mean(out * out, axis=-1, keepdims=True)
    o_ref[...] = out * lax.rsqrt(ms + EPS) * gf_ref[...]


def _row_spec(tm, n):
    return pl.BlockSpec((tm, n), lambda i: (i, 0))


def _ffn1(x, g, sh, sc, gt, wgu, wd):
    vec = _const_spec((1, D_MODEL))
    return pl.pallas_call(
        _ffn1_kernel,
        out_shape=jax.ShapeDtypeStruct((SEQ, D_MODEL), F32),
        grid=(SEQ // FFN_TM,),
        in_specs=[_row_spec(FFN_TM, D_MODEL), vec, vec, vec, vec,
                  _const_spec((D_MODEL, 2 * D_FF)), _const_spec((D_FF, D_MODEL))],
        out_specs=_row_spec(FFN_TM, D_MODEL),
        compiler_params=_params(("arbitrary",)),
        name="ffn1",
    )(x, g, sh, sc, gt, wgu, wd)


def _ffn2(x, yc, ya, woc, woa, gt2, g, sh, sc, gt, wgu, wd, gf):
    vec = _const_spec((1, D_MODEL))
    nt = FFN_TM // TQ
    return pl.pallas_call(
        _ffn2_kernel,
        out_shape=jax.ShapeDtypeStruct((SEQ, D_MODEL), F32),
        grid=(SEQ // FFN_TM,),
        in_specs=[_row_spec(FFN_TM, D_MODEL), _row_spec(FFN_TM, D_CONV),
                  pl.BlockSpec((nt, HPG, TQ, LANES), lambda i: (i, 0, 0, 0)),
                  _const_spec((D_CONV, D_MODEL)), _const_spec((HPG, LANES, D_MODEL)), vec,
                  vec, vec, vec, vec,
                  _const_spec((D_MODEL, 2 * D_FF)), _const_spec((D_FF, D_MODEL)), vec],
        out_specs=_row_spec(FFN_TM, D_MODEL),
        compiler_params=_params(("arbitrary",)),
        name="ffn2",
    )(x, yc, ya, woc, woa, gt2, g, sh, sc, gt, wgu, wd, gf)


def _proj_kernel(x_ref, g_ref, sh_ref, sc_ref, w_ref,
                 hc_ref, kcvc_ref, ksl_ref, kw_ref, qt_ref, vslt_ref, vwt_ref, gatet_ref):
    hb = _rms_mod(x_ref[...], g_ref[...], sh_ref[...], sc_ref[...]).astype(BF16)
    u = _dot(hb, w_ref[...])
    hc_ref[...] = u[:, 0:512] * _sigmoid(u[:, 512:1024])
    kcvc_ref[...] = u[:, 1536:1792]
    ksl_ref[...] = u[:, 1792:1920].astype(BF16)
    kw_ref[...] = u[:, 2048:2176].astype(BF16)
    qt_ref[...] = u[:, 1024:1536].T.astype(BF16)
    vslt_ref[...] = u[:, 1920:2048].T.astype(BF16)
    vwt_ref[...] = u[:, 2176:2304].T.astype(BF16)
    gatet_ref[...] = _sigmoid(u[:, 2304:2432]).T


def _proj(x, g, sh, sc, w):
    vec = _const_spec((1, D_MODEL))
    tm = PROJ_TM
    rows = [(D_CONV, F32), (2 * D_KV, F32), (D_KV, BF16), (D_KV, BF16)]
    cols = [(D_ATT, BF16), (D_KV, BF16), (D_KV, BF16), (LANES, F32)]
    return pl.pallas_call(
        _proj_kernel,
        out_shape=[jax.ShapeDtypeStruct((SEQ, n), dt) for n, dt in rows]
                  + [jax.ShapeDtypeStruct((n, SEQ), dt) for n, dt in cols],
        grid=(SEQ // tm,),
        in_specs=[_row_spec(tm, D_MODEL), vec, vec, vec, _const_spec((D_MODEL, D_IN_PAD))],
        out_specs=[_row_spec(tm, n) for n, _ in rows]
                  + [pl.BlockSpec((n, tm), lambda i: (0, i)) for n, _ in cols],
        compiler_params=_params(("arbitrary",)),
        name="proj",
    )(x, g, sh, sc, w)


def _conv_kernel(prev_ref, cur_ref, w_ref, b_ref, g_ref, bb_ref, o_ref):
    i = pl.program_id(0)
    tm = cur_ref.shape[0]
    prev = jnp.where(i > 0, prev_ref[...], 0.0)
    xx = jnp.concatenate([prev, cur_ref[...]], axis=0)
    acc = jnp.zeros((tm, D_CONV), F32) + b_ref[...]
    off = CONV_HALO - (CONV_WIDTH - 1)
    for w in range(CONV_WIDTH):
        acc = acc + xx[off + w:off + w + tm, :] * w_ref[w:w + 1, :]
    mu = jnp.mean(acc, axis=-1, keepdims=True)
    dlt = acc - mu
    var = jnp.mean(dlt * dlt, axis=-1, keepdims=True)
    y = dlt * lax.rsqrt(var + EPS) * g_ref[...] + bb_ref[...]
    o_ref[...] = (y * _sigmoid(y)).astype(BF16)


def _conv(hc, w, b, g, bb):
    tm = CONV_TM
    r = tm // CONV_HALO
    vec = _const_spec((1, D_CONV))
    return pl.pallas_call(
        _conv_kernel,
        out_shape=jax.ShapeDtypeStruct((SEQ, D_CONV), BF16),
        grid=(SEQ // tm,),
        in_specs=[pl.BlockSpec((CONV_HALO, D_CONV), lambda i: (jnp.maximum(i * r - 1, 0), 0)),
                  _row_spec(tm, D_CONV),
                  _const_spec((CONV_WIDTH, D_CONV)), vec, vec, vec],
        out_specs=_row_spec(tm, D_CONV),
        compiler_params=_params(("arbitrary",)),
        name="conv",
    )(hc, hc, w, b, g, bb)


def _compress_kernel(c_ref, pe_ref, w1_ref, w2_ref, o_ref):
    half = CMP_STRIDE * HEAD_DIM
    cm = c_ref[0]
    pe = pe_ref[0]
    top = (cm + pe[:, :half]).astype(BF16)
    bot = (cm + pe[:, half:]).astype(BF16)
    a = _dot(top, w1_ref[0, :half, :])
    b = _dot(bot, w1_ref[0, half:, :])
    b_up = jnp.concatenate([b[1:], jnp.zeros((1, CMP_HIDDEN), F32)], axis=0)
    pre = a + b_up
    hid = (pre * _sigmoid(pre)).astype(BF16)
    o_ref[0] = _dot(hid, w2_ref[0])


def _compress(cmat, pe, w1, w2):
    return pl.pallas_call(
        _compress_kernel,
        out_shape=jax.ShapeDtypeStruct((4, N_CMP_PAD, HEAD_DIM), F32),
        grid=(4,),
        in_specs=[pl.BlockSpec((1, N_CMP_PAD, 1024), lambda i: (i, 0, 0)),
                  pl.BlockSpec((1, 1, 2048), lambda i: (i // 2, 0, 0)),
                  pl.BlockSpec((1, 2048, CMP_HIDDEN), lambda i: (i // 2, 0, 0)),
                  pl.BlockSpec((1, CMP_HIDDEN, HEAD_DIM), lambda i: (i // 2, 0, 0))],
        out_specs=pl.BlockSpec((1, N_CMP_PAD, HEAD_DIM), lambda i: (i, 0, 0)),
        compiler_params=_params(("arbitrary",)),
        name="compress",
    )(cmat, pe, w1, w2)


def _tile_update(s, m, acc, vt):
    m_new = jnp.maximum(m, jnp.max(s, axis=0, keepdims=True))
    alpha = jnp.exp(m - m_new)
    p = jnp.exp(s - m_new).astype(BF16)
    return m_new, alpha * acc + _dot(vt, p)


def _finish(acc):
    return acc[:D_KV] / jnp.maximum(acc[D_KV:D_KV + 1], 1e-30)


def _attn_kernel(qt_ref, gt_ref, kc_ref, vct_ref, ovt_ref, ksl_ref, vslt_ref, kw_ref, vwt_ref,
                 ct_ref, wt_ref, o_ref, s_scr, p_scr, sel_scr):
    qb = pl.program_id(0)
    q0 = qb * TQ
    qt = qt_ref[...]
    gt = gt_ref[...]
    drow = lax.broadcasted_iota(jnp.int32, (D_KV, TQ), 0)
    krow = lax.broadcasted_iota(jnp.int32, (TK, 1), 0)
    m0 = jnp.full((1, COLS), M_FLOOR, F32)
    acc0 = jnp.zeros((VROWS, COLS), F32)

    def pad_mask(first_row):
        return jnp.where(krow + first_row >= KPAD, 0.0, NEG)

    comb = []
    for g in range(N_KV):
        keep = (drow >= HEAD_DIM * g) & (drow < HEAD_DIM * (g + 1))
        q_g = jnp.concatenate(
            [jnp.where(keep, qt[h * D_KV:(h + 1) * D_KV, :], jnp.zeros((), BF16))
             for h in range(HPG)], axis=1)

        mcol = m0
        for ct in range(N_CMP_PAD // TK):
            rows = slice(ct * TK, (ct + 1) * TK)
            r0 = pl.multiple_of(jnp.clip(TK * ct - (TQ // CMP_STRIDE) * qb + CT_ZERO,
                                         0, CT_ROWS - TK), SUBLANES)
            s = _dot(kc_ref[rows, :], q_g) + ct_ref[g, pl.ds(r0, TK), :]
            s_scr[rows, :] = s
            mcol = jnp.maximum(mcol, jnp.max(s, axis=0, keepdims=True))
        lsum = jnp.zeros((1, COLS), F32)
        acc_c = jnp.zeros((D_KV, COLS), F32)
        for ct in range(N_CMP_PAD // TK):
            rows = slice(ct * TK, (ct + 1) * TK)
            p = jnp.exp(s_scr[rows, :] - mcol)
            p_scr[rows, :] = p
            lsum = lsum + jnp.sum(p, axis=0, keepdims=True)
            acc_c = acc_c + _dot(vct_ref[:, rows], p.astype(BF16))
        rinv = 1.0 / jnp.maximum(lsum, 1e-30)
        o_c = acc_c * rinv
        imp = jnp.zeros((N_SLC, TQ), F32)
        for ct in range(N_CMP_PAD // TK):
            rows = slice(ct * TK, (ct + 1) * TK)
            pn = p_scr[rows, :] * rinv
            ps = pn[:, 0:TQ] + pn[:, TQ:2 * TQ] + pn[:, 2 * TQ:3 * TQ] + pn[:, 3 * TQ:4 * TQ]
            hi = ps.astype(BF16)
            r1 = ps - hi.astype(F32)
            mid = r1.astype(BF16)
            lo = (r1 - mid.astype(F32)).astype(BF16)
            ov = ovt_ref[:, rows]
            imp = imp + _dot(ov, hi) + _dot(ov, mid) + _dot(ov, lo)

        cur = jnp.right_shift(q0 + lax.broadcasted_iota(jnp.int32, (1, TQ), 1), 6)
        jcol = lax.broadcasted_iota(jnp.int32, (N_SLC, 1), 0)
        forced = (jcol == 0) | (jcol == cur) | (jcol == cur - 1)
        sc = jnp.where(forced, FORCE_SCORE, jnp.where(jcol <= cur, imp, -1.0))
        jf = lax.broadcasted_iota(jnp.int32, (N_SLC, TQ), 0).astype(F32)
        sel = jnp.zeros((N_SLC, TQ), F32)
        for _ in range(SLC_TOPK):
            mx = jnp.max(sc, axis=0, keepdims=True)
            idx = jnp.min(jnp.where(sc == mx, jf, 1e9), axis=0, keepdims=True)
            hit = jf == idx
            sel = jnp.where(hit, 1.0, sel)
            sc = jnp.where(hit, -2.0, sc)
        selneg = jnp.where(sel > 0.5, 0.0, NEG)
        for kt in range(SEQ // TK):
            sel_scr[kt, 0:BLK_PER_TILE, :] = selneg[kt * BLK_PER_TILE:(kt + 1) * BLK_PER_TILE, :]

        def add_sel(s, kt):
            blk = sel_scr[kt]
            mask = jnp.concatenate(
                [jnp.broadcast_to(blk[j:j + 1, :], (SLC_LEN, TQ)) for j in range(BLK_PER_TILE)],
                axis=0)
            return jnp.concatenate([s[:, h * TQ:(h + 1) * TQ] + mask for h in range(HPG)], axis=1)

        m, acc = m0, acc0
        for a in range(2):
            first = q0 + TQ + TK * a
            s = _dot(ksl_ref[pl.ds(pl.multiple_of(first, TK), TK), :], q_g)
            s = s + (wt_ref[g, TQ + TK * a:TQ + TK * (a + 1), :] + pad_mask(first))
            s = add_sel(s, jnp.maximum(qb - 1 + a, 0))
            m, acc = _tile_update(s, m, acc, vslt_ref[qb + 1 + a])

        def far_body(kt, carry):
            k0 = pl.multiple_of(KPAD + kt * TK, TK)
            s = add_sel(_dot(ksl_ref[pl.ds(k0, TK), :], q_g), kt)
            return _tile_update(s, carry[0], carry[1], vslt_ref[kt + KPAD // TK])

        m, acc = lax.fori_loop(0, jnp.maximum(qb - 1, 0), far_body, (m, acc))
        o_s = _finish(acc)

        m, acc = m0, acc0
        for a in range(3):
            first = q0 + TK * a
            s = _dot(kw_ref[pl.ds(pl.multiple_of(first, TK), TK), :], q_g)
            s = s + (wt_ref[g, TK * a:TK * (a + 1), :] + pad_mask(first))
            m, acc = _tile_update(s, m, acc, vwt_ref[qb + a])
        o_w = _finish(acc)

        per_head = []
        for h in range(HPG):
            c0 = 3 * (g * HPG + h)
            cs = slice(h * TQ, (h + 1) * TQ)
            per_head.append(gt[c0:c0 + 1, :] * o_c[:, cs] + gt[c0 + 1:c0 + 2, :] * o_s[:, cs]
                            + gt[c0 + 2:c0 + 3, :] * o_w[:, cs])
        comb.append(per_head)

    for h in range(HPG):
        merged = jnp.where(drow < HEAD_DIM, comb[0][h], comb[1][h])
        o_ref[0, h] = merged.T.astype(BF16)


def _attn(qt, gatet, kc, vct, ovt, ksl, vslt, kw, vwt, ctmpl, wtmpl):
    consts = [kc, vct, ovt, ksl, vslt, kw, vwt, ctmpl, wtmpl]
    return pl.pallas_call(
        _attn_kernel,
        out_shape=jax.ShapeDtypeStruct((SEQ // TQ, HPG, TQ, LANES), BF16),
        grid=(SEQ // TQ,),
        in_specs=[pl.BlockSpec((D_ATT, TQ), lambda i: (0, i)),
                  pl.BlockSpec((LANES, TQ), lambda i: (0, i))]
                 + [_const_spec(a.shape) for a in consts],
        out_specs=pl.BlockSpec((1, HPG, TQ, LANES), lambda i: (i, 0, 0, 0)),
        scratch_shapes=[pltpu.VMEM((N_CMP_PAD, COLS), F32), pltpu.VMEM((N_CMP_PAD, COLS), F32),
                        pltpu.VMEM((SEQ // TK, SUBLANES, TQ), F32)],
        compiler_params=_params(("arbitrary",)),
        name="nsa_attn",
    )(qt, gatet, *consts)


def _t5_bucket(dist):
    max_exact = N_BUCKETS // 2
    d = jnp.maximum(dist, 0)
    df = jnp.maximum(d, 1).astype(F32)
    large = max_exact + (jnp.log(df / max_exact) / math.log(MAX_DIST / max_exact)
                         * (N_BUCKETS - max_exact)).astype(jnp.int32)
    large = jnp.minimum(large, N_BUCKETS - 1)
    return jnp.where(d < max_exact, d, large)


def _to_group_cols(t):
    r = t.shape[1]
    return t.reshape(N_KV, HPG, r, TQ).transpose(0, 2, 1, 3).reshape(N_KV, r, COLS)


def _bias_templates(rel_bias):
    biasp = rel_bias[:, _t5_bucket(jnp.arange(MAX_DIST))] - rel_bias[:, N_BUCKETS - 1:]

    wk = WINDOW + TQ
    span = wk + TQ - 1
    d = jnp.arange(span) - (wk - 1) + WINDOW
    f = jnp.where((d >= 0) & (d < WINDOW), biasp[:, jnp.clip(d, 0, MAX_DIST - 1)], NEG)
    fext = jnp.concatenate([f, jnp.zeros((N_HEADS, 1), F32)], axis=1)
    skew = jnp.tile(fext, (1, wk))[:, :wk * span].reshape(N_HEADS, wk, span)
    wtmpl = _to_group_cols(skew[:, :, wk - 1:wk - 1 + TQ])

    r = jnp.arange(-9, 15)[:, None]
    dc = jnp.arange(TQ)[None, :] - (CMP_LEN - 1) - CMP_STRIDE * r
    band = jnp.where(dc >= 0, biasp[:, jnp.clip(dc, 0, MAX_DIST - 1)], NEG)
    lo_rows = CT_ZERO - 9
    ctmpl = jnp.concatenate(
        [jnp.zeros((N_HEADS, lo_rows, TQ), F32), band,
         jnp.full((N_HEADS, CT_ROWS - lo_rows - band.shape[1], TQ), NEG, F32)], axis=1)
    return _to_group_cols(ctmpl).astype(F32), wtmpl.astype(F32)


def _overlap_t():
    cmp_start = jnp.arange(N_CMP_PAD) * CMP_STRIDE
    slc_start = jnp.arange(N_SLC) * SLC_LEN
    ov = ((cmp_start[None, :] < slc_start[:, None] + SLC_LEN)
          & (cmp_start[None, :] + CMP_LEN > slc_start[:, None])
          & (jnp.arange(N_CMP_PAD)[None, :] < N_CMP))
    return ov.astype(BF16)


def _vt_tiles(vt):
    ones = jnp.concatenate([jnp.ones((1, SEQ), BF16), jnp.zeros((VROWS - D_KV - 1, SEQ), BF16)])
    t = jnp.concatenate([vt, ones], axis=0).reshape(VROWS, SEQ // TK, TK).transpose(1, 0, 2)
    return jnp.pad(t, ((KPAD // TK, 0), (0, 0), (0, 0)))


def kernel(x, c, w_ada, b_ada, g_ffn1, w_gu1, w_down1, g_mix, w_in, w_dw, b_dw, ln_g, ln_b,
           pe_k, pe_v, w_ck1, w_ck2, w_cv1, w_cv2, rel_bias, w_out, g_ffn2, w_gu2, w_down2,
           g_final):
    assert x.shape == (1, SEQ, D_MODEL) and w_ada.shape[0] == 1
    x2 = x[0]
    mod = _ada(c.reshape(D_MODEL, 1), w_ada[0], b_ada)
    sh1, sc1, gt1, sh2, sc2, gt2, sh3, sc3, gt3 = [mod[:, k * D_MODEL:(k + 1) * D_MODEL]
                                                   for k in range(9)]

    x1 = _ffn1(x2, g_ffn1, sh1, sc1, gt1, w_gu1[0].astype(BF16), w_down1[0].astype(BF16))

    wi = w_in[0]
    wq = wi[:, 1024:1536].reshape(D_MODEL, N_KV, HPG, HEAD_DIM).transpose(0, 2, 1, 3)
    wq = wq.reshape(D_MODEL, D_ATT) * (HEAD_DIM ** -0.5)
    w_in_p = jnp.concatenate(
        [wi[:, :1024], wq, wi[:, 1536:], jnp.zeros((D_MODEL, D_IN_PAD - wi.shape[1]), F32)],
        axis=1).astype(BF16)
    hc, kcvc, ksl, kw, qt, vslt, vwt, gatet = _proj(x1, g_mix, sh2, sc2, w_in_p)

    y_conv = _conv(hc, w_dw[0], b_dw, ln_g, ln_b)

    cmat = kcvc.reshape(SEQ, 4, HEAD_DIM).transpose(1, 0, 2).reshape(4, N_CMP_PAD, 1024)
    pe = jnp.stack([pe_k[0].reshape(1, -1), pe_v[0].reshape(1, -1)])
    w1 = jnp.stack([w_ck1[0], w_cv1[0]]).astype(BF16)
    w2 = jnp.stack([w_ck2[0], w_cv2[0]]).astype(BF16)
    cmp = _compress(cmat, pe, w1, w2)
    kc = jnp.concatenate([cmp[0], cmp[1]], axis=1).astype(BF16)
    vct = jnp.concatenate([cmp[2], cmp[3]], axis=1).T.astype(BF16)

    ctmpl, wtmpl = _bias_templates(rel_bias)
    front = ((KPAD, 0), (0, 0))
    y_att = _attn(qt, gatet, kc, vct, _overlap_t(), jnp.pad(ksl, front), _vt_tiles(vslt),
                  jnp.pad(kw, front), _vt_tiles(vwt), ctmpl, wtmpl)

    wo = w_out[0]
    woa = wo[D_CONV:].reshape(N_KV, HPG, HEAD_DIM, D_MODEL).transpose(1, 0, 2, 3)
    woa = woa.reshape(HPG, LANES, D_MODEL).astype(BF16)
    out = _ffn2(x1, y_conv, y_att, wo[:D_CONV].astype(BF16), woa, gt2,
                g_ffn2, sh3, sc3, gt3, w_gu2[0].astype(BF16), w_down2[0].astype(BF16),
                g_final.reshape(1, D_MODEL))
    return out[None]
```

```python
import math

import jax
import jax.numpy as jnp
from jax import lax
from jax.experimental import pallas as pl
from jax.experimental.pallas import tpu as pltpu

F32 = jnp.float32
BF16 = jnp.bfloat16

D_MODEL = 1024
SEQ = 16384
D_CONV = 512
CONV_WIDTH = 31
N_HEADS = 8
N_KV = 2
HPG = 4
HEAD_DIM = 64
D_ATT = 512
D_KV = 128
CMP_LEN = 32
CMP_STRIDE = 16
CMP_HIDDEN = 128
N_CMP = (SEQ - CMP_LEN) // CMP_STRIDE + 1
N_CMP_PAD = 1024
SLC_LEN = 64
N_SLC = SEQ // SLC_LEN
SLC_TOPK = 16
WINDOW = 512
FORCE_SCORE = 1.0e4
N_BUCKETS = 32
MAX_DIST = 128
D_FF = 2816
FFN_RES = 0.5
EPS = 1e-6
NEG = -1e30
M_FLOOR = -1e29
LOG2E = math.log2(math.e)

V7X_VMEM_BYTES = 64 * 1024 * 1024
VMEM_LIMIT = V7X_VMEM_BYTES - 6 * 1024 * 1024
LANES = 128
SUBLANES = 8

TQ = 256
TK = 256
COLS = HPG * TQ
KPAD = WINDOW
VROWS = D_KV + 16
BLK_PER_TILE = TK // SLC_LEN
CT_ZERO = 520
CT_ROWS = 800
FFN_TM = 512
FFN_TF = D_FF // 2
PROJ_TM = 512
CONV_TM = 256
CONV_HALO = 32
D_IN_PAD = 2432


def _params(sem):
    return pltpu.CompilerParams(dimension_semantics=sem, vmem_limit_bytes=VMEM_LIMIT)


def _const_spec(shape):
    nd = len(shape)
    return pl.BlockSpec(shape, lambda *_: (0,) * nd, pipeline_mode=pl.Buffered(1))


def _sigmoid(v):
    return 1.0 / (1.0 + jnp.exp(-v))


def _dot(a, b):
    return jnp.dot(a, b, preferred_element_type=F32)


def _ada_kernel(c_ref, w_ref, b_ref, o_ref):
    c = c_ref[...]
    sc = c * _sigmoid(c)
    o_ref[...] = jnp.sum(w_ref[...] * sc, axis=0, keepdims=True) + b_ref[...]


def _ada(c_col, w, b):
    n = w.shape[1]
    tn = n // 8
    return pl.pallas_call(
        _ada_kernel,
        out_shape=jax.ShapeDtypeStruct((1, n), F32),
        grid=(8,),
        in_specs=[pl.BlockSpec((D_MODEL, 1), lambda j: (0, 0)),
                  pl.BlockSpec((D_MODEL, tn), lambda j: (0, j)),
                  pl.BlockSpec((1, tn), lambda j: (0, j))],
        out_specs=pl.BlockSpec((1, tn), lambda j: (0, j)),
        compiler_params=_params(("arbitrary",)),
        name="ada",
    )(c_col, w, b)


def _rms_mod(x, g, sh, sc):
    ms = jnp.mean(x * x, axis=-1, keepdims=True)
    y = x * lax.rsqrt(ms + EPS) * g
    return y * (1.0 + sc) + sh


def _ffn_body(x, g_ref, sh_ref, sc_ref, gt_ref, wgu_ref, wd_ref):
    hb = _rms_mod(x, g_ref[...], sh_ref[...], sc_ref[...]).astype(BF16)
    acc = jnp.zeros((x.shape[0], D_MODEL), F32)
    for f in range(D_FF // FFN_TF):
        gg = _dot(hb, wgu_ref[:, f * FFN_TF:(f + 1) * FFN_TF])
        uu = _dot(hb, wgu_ref[:, D_FF + f * FFN_TF:D_FF + (f + 1) * FFN_TF])
        a = (gg * _sigmoid(gg) * uu).astype(BF16)
        acc = acc + _dot(a, wd_ref[f * FFN_TF:(f + 1) * FFN_TF, :])
    return x + (FFN_RES * gt_ref[...]) * acc


def _ffn1_kernel(x_ref, g_ref, sh_ref, sc_ref, gt_ref, wgu_ref, wd_ref, o_ref):
    o_ref[...] = _ffn_body(x_ref[...], g_ref, sh_ref, sc_ref, gt_ref, wgu_ref, wd_ref)


def _ffn2_kernel(x_ref, yc_ref, ya_ref, woc_ref, woa_ref, gt2_ref,
                 g_ref, sh_ref, sc_ref, gt_ref, wgu_ref, wd_ref, gf_ref, o_ref):
    tm = x_ref.shape[0]
    y = _dot(yc_ref[...], woc_ref[...])
    for h in range(HPG):
        y = y + _dot(ya_ref[:, h].reshape(tm, LANES), woa_ref[h])
    x = x_ref[...] + gt2_ref[...] * y
    out = _ffn_body(x, g_ref, sh_ref, sc_ref, gt_ref, wgu_ref, wd_ref)
    ms = jnp.mean(out * out, axis=-1, keepdims=True)
    o_ref[...] = out * lax.rsqrt(ms + EPS) * gf_ref[...]


def _row_spec(tm, n):
    return pl.BlockSpec((tm, n), lambda i: (i, 0))


def _ffn1(x, g, sh, sc, gt, wgu, wd):
    vec = _const_spec((1, D_MODEL))
    return pl.pallas_call(
        _ffn1_kernel,
        out_shape=jax.ShapeDtypeStruct((SEQ, D_MODEL), F32),
        grid=(SEQ // FFN_TM,),
        in_specs=[_row_spec(FFN_TM, D_MODEL), vec, vec, vec, vec,
                  _const_spec((D_MODEL, 2 * D_FF)), _const_spec((D_FF, D_MODEL))],
        out_specs=_row_spec(FFN_TM, D_MODEL),
        compiler_params=_params(("arbitrary",)),
        name="ffn1",
    )(x, g, sh, sc, gt, wgu, wd)


def _ffn2(x, yc, ya, woc, woa, gt2, g, sh, sc, gt, wgu, wd, gf):
    vec = _const_spec((1, D_MODEL))
    nt = FFN_TM // TQ
    return pl.pallas_call(
        _ffn2_kernel,
        out_shape=jax.ShapeDtypeStruct((SEQ, D_MODEL), F32),
        grid=(SEQ // FFN_TM,),
        in_specs=[_row_spec(FFN_TM, D_MODEL), _row_spec(FFN_TM, D_CONV),
                  pl.BlockSpec((nt, HPG, TQ, LANES), lambda i: (i, 0, 0, 0)),
                  _const_spec((D_CONV, D_MODEL)), _const_spec((HPG, LANES, D_MODEL)), vec,
                  vec, vec, vec, vec,
                  _const_spec((D_MODEL, 2 * D_FF)), _const_spec((D_FF, D_MODEL)), vec],
        out_specs=_row_spec(FFN_TM, D_MODEL),
        compiler_params=_params(("arbitrary",)),
        name="ffn2",
    )(x, yc, ya, woc, woa, gt2, g, sh, sc, gt, wgu, wd, gf)


def _proj_kernel(x_ref, g_ref, sh_ref, sc_ref, w_ref,
                 hc_ref, kcvc_ref, ksl_ref, kw_ref, qt_ref, vslt_ref, vwt_ref, gatet_ref):
    hb = _rms_mod(x_ref[...], g_ref[...], sh_ref[...], sc_ref[...]).astype(BF16)
    u = _dot(hb, w_ref[...])
    hc_ref[...] = u[:, 0:512] * _sigmoid(u[:, 512:1024])
    kcvc_ref[...] = u[:, 1536:1792]
    ksl_ref[...] = u[:, 1792:1920].astype(BF16)
    kw_ref[...] = u[:, 2048:2176].astype(BF16)
    qt_ref[...] = u[:, 1024:1536].T.astype(BF16)
    vslt_ref[...] = u[:, 1920:2048].T.astype(BF16)
    vwt_ref[...] = u[:, 2176:2304].T.astype(BF16)
    gatet_ref[...] = _sigmoid(u[:, 2304:2432]).T


def _proj(x, g, sh, sc, w):
    vec = _const_spec((1, D_MODEL))
    tm = PROJ_TM
    rows = [(D_CONV, F32), (2 * D_KV, F32), (D_KV, BF16), (D_KV, BF16)]
    cols = [(D_ATT, BF16), (D_KV, BF16), (D_KV, BF16), (LANES, F32)]
    return pl.pallas_call(
        _proj_kernel,
        out_shape=[jax.ShapeDtypeStruct((SEQ, n), dt) for n, dt in rows]
                  + [jax.ShapeDtypeStruct((n, SEQ), dt) for n, dt in cols],
        grid=(SEQ // tm,),
        in_specs=[_row_spec(tm, D_MODEL), vec, vec, vec, _const_spec((D_MODEL, D_IN_PAD))],
        out_specs=[_row_spec(tm, n) for n, _ in rows]
                  + [pl.BlockSpec((n, tm), lambda i: (0, i)) for n, _ in cols],
        compiler_params=_params(("arbitrary",)),
        name="proj",
    )(x, g, sh, sc, w)


def _conv_kernel(prev_ref, cur_ref, w_ref, b_ref, g_ref, bb_ref, o_ref):
    i = pl.program_id(0)
    tm = cur_ref.shape[0]
    prev = jnp.where(i > 0, prev_ref[...], 0.0)
    xx = jnp.concatenate([prev, cur_ref[...]], axis=0)
    acc = jnp.zeros((tm, D_CONV), F32) + b_ref[...]
    off = CONV_HALO - (CONV_WIDTH - 1)
    for w in range(CONV_WIDTH):
        acc = acc + xx[off + w:off + w + tm, :] * w_ref[w:w + 1, :]
    mu = jnp.mean(acc, axis=-1, keepdims=True)
    dlt = acc - mu
    var = jnp.mean(dlt * dlt, axis=-1, keepdims=True)
    y = dlt * lax.rsqrt(var + EPS) * g_ref[...] + bb_ref[...]
    o_ref[...] = (y * _sigmoid(y)).astype(BF16)


def _conv(hc, w, b, g, bb):
    tm = CONV_TM
    r = tm // CONV_HALO
    vec = _const_spec((1, D_CONV))
    return pl.pallas_call(
        _conv_kernel,
        out_shape=jax.ShapeDtypeStruct((SEQ, D_CONV), BF16),
        grid=(SEQ // tm,),
        in_specs=[pl.BlockSpec((CONV_HALO, D_CONV), lambda i: (jnp.maximum(i * r - 1, 0), 0)),
                  _row_spec(tm, D_CONV),
                  _const_spec((CONV_WIDTH, D_CONV)), vec, vec, vec],
        out_specs=_row_spec(tm, D_CONV),
        compiler_params=_params(("arbitrary",)),
        name="conv",
    )(hc, hc, w, b, g, bb)


def _compress_kernel(c_ref, pe_ref, w1_ref, w2_ref, o_ref):
    half = CMP_STRIDE * HEAD_DIM
    cm = c_ref[0]
    pe = pe_ref[0]
    top = (cm + pe[:, :half]).astype(BF16)
    bot = (cm + pe[:, half:]).astype(BF16)
    a = _dot(top, w1_ref[0, :half, :])
    b = _dot(bot, w1_ref[0, half:, :])
    b_up = jnp.concatenate([b[1:], jnp.zeros((1, CMP_HIDDEN), F32)], axis=0)
    pre = a + b_up
    hid = (pre * _sigmoid(pre)).astype(BF16)
    o_ref[0] = _dot(hid, w2_ref[0])


def _compress(cmat, pe, w1, w2):
    return pl.pallas_call(
        _compress_kernel,
        out_shape=jax.ShapeDtypeStruct((4, N_CMP_PAD, HEAD_DIM), F32),
        grid=(4,),
        in_specs=[pl.BlockSpec((1, N_CMP_PAD, 1024), lambda i: (i, 0, 0)),
                  pl.BlockSpec((1, 1, 2048), lambda i: (i // 2, 0, 0)),
                  pl.BlockSpec((1, 2048, CMP_HIDDEN), lambda i: (i // 2, 0, 0)),
                  pl.BlockSpec((1, CMP_HIDDEN, HEAD_DIM), lambda i: (i // 2, 0, 0))],
        out_specs=pl.BlockSpec((1, N_CMP_PAD, HEAD_DIM), lambda i: (i, 0, 0)),
        compiler_params=_params(("arbitrary",)),
        name="compress",
    )(cmat, pe, w1, w2)


def _tile_update(s, m, acc, vt):
    m_new = jnp.maximum(m, jnp.max(s, axis=0, keepdims=True))
    alpha = jnp.exp2(m - m_new)
    p = jnp.exp2(s - m_new).astype(BF16)
    return m_new, alpha * acc + _dot(vt, p)


def _finish(acc):
    return acc[:D_KV] / jnp.maximum(acc[D_KV:D_KV + 1], 1e-30)


def _attn_kernel(qt_ref, gt_ref, kc_ref, vct_ref, ovt_ref, ksl_ref, vslt_ref, kw_ref, vwt_ref,
                 ct_ref, wt_ref, o_ref, s_scr, p_scr, sel_scr, wa_scr, wb_scr, sa_scr, sb_scr):
    qb = pl.program_id(0)
    q0 = qb * TQ
    qt = qt_ref[...]
    gt = gt_ref[...]
    drow = lax.broadcasted_iota(jnp.int32, (D_KV, TQ), 0)
    krow = lax.broadcasted_iota(jnp.int32, (TK, 1), 0)
    m0 = jnp.full((1, COLS), M_FLOOR, F32)
    acc0 = jnp.zeros((VROWS, COLS), F32)

    def pad_mask(first_row):
        return jnp.where(krow + first_row >= KPAD, 0.0, NEG)

    comb = []
    for g in range(N_KV):
        keep = (drow >= HEAD_DIM * g) & (drow < HEAD_DIM * (g + 1))
        q_g = jnp.concatenate(
            [jnp.where(keep, qt[h * D_KV:(h + 1) * D_KV, :], jnp.zeros((), BF16))
             for h in range(HPG)], axis=1)

        mcol = m0
        for ct in range(N_CMP_PAD // TK):
            rows = slice(ct * TK, (ct + 1) * TK)
            r0 = pl.multiple_of(jnp.clip(TK * ct - (TQ // CMP_STRIDE) * qb + CT_ZERO,
                                         0, CT_ROWS - TK), SUBLANES)
            s = _dot(kc_ref[rows, :], q_g) + ct_ref[g, pl.ds(r0, TK), :]
            s_scr[rows, :] = s
            mcol = jnp.maximum(mcol, jnp.max(s, axis=0, keepdims=True))
        lsum = jnp.zeros((1, COLS), F32)
        acc_c = jnp.zeros((D_KV, COLS), F32)
        for ct in range(N_CMP_PAD // TK):
            rows = slice(ct * TK, (ct + 1) * TK)
            p = jnp.exp2(s_scr[rows, :] - mcol)
            p_scr[rows, :] = p
            lsum = lsum + jnp.sum(p, axis=0, keepdims=True)
            acc_c = acc_c + _dot(vct_ref[:, rows], p.astype(BF16))
        rinv = 1.0 / jnp.maximum(lsum, 1e-30)
        o_c = acc_c * rinv
        imp = jnp.zeros((N_SLC, TQ), F32)
        for ct in range(N_CMP_PAD // TK):
            rows = slice(ct * TK, (ct + 1) * TK)
            pn = p_scr[rows, :] * rinv
            ps = pn[:, 0:TQ] + pn[:, TQ:2 * TQ] + pn[:, 2 * TQ:3 * TQ] + pn[:, 3 * TQ:4 * TQ]
            hi = ps.astype(BF16)
            r1 = ps - hi.astype(F32)
            mid = r1.astype(BF16)
            lo = (r1 - mid.astype(F32)).astype(BF16)
            ov = ovt_ref[:, rows]
            imp = imp + _dot(ov, hi) + _dot(ov, mid) + _dot(ov, lo)

        cur = jnp.right_shift(q0 + lax.broadcasted_iota(jnp.int32, (1, TQ), 1), 6)
        jcol = lax.broadcasted_iota(jnp.int32, (N_SLC, 1), 0)
        forced = (jcol == 0) | (jcol == cur) | (jcol == cur - 1)
        sc = jnp.where(forced, FORCE_SCORE, jnp.where(jcol <= cur, imp, -1.0))
        jf = lax.broadcasted_iota(jnp.int32, (N_SLC, TQ), 0).astype(F32)
        sel = jnp.zeros((N_SLC, TQ), F32)
        for _ in range(SLC_TOPK):
            mx = jnp.max(sc, axis=0, keepdims=True)
            idx = jnp.min(jnp.where(sc == mx, jf, 1e9), axis=0, keepdims=True)
            hit = jf == idx
            sel = jnp.where(hit, 1.0, sel)
            sc = jnp.where(hit, -2.0, sc)
        selneg = jnp.where(sel > 0.5, 0.0, NEG)
        unused = jnp.zeros((SUBLANES - BLK_PER_TILE, TQ), F32)
        for kt in range(SEQ // TK):
            sel_scr[kt] = jnp.concatenate(
                [selneg[kt * BLK_PER_TILE:(kt + 1) * BLK_PER_TILE, :], unused], axis=0)
        sel_scr[SEQ // TK] = jnp.concatenate(
            [jnp.full((BLK_PER_TILE, TQ), NEG, F32), unused], axis=0)

        zrows = jnp.zeros((2 * D_KV - D_KV - 2 * SUBLANES, COLS), BF16)
        for w_scr in (wa_scr, wb_scr):
            w_scr[0:D_KV, :] = q_g
            w_scr[D_KV + 2 * SUBLANES:, :] = zrows

        def sel_scores(w_scr, first_row, sel_idx):
            blk = sel_scr[sel_idx]
            rows8 = jnp.concatenate([blk] * HPG, axis=1).astype(BF16)
            w_scr[D_KV:D_KV + 2 * SUBLANES, :] = jnp.concatenate(
                [rows8, jnp.zeros((SUBLANES, COLS), BF16)], axis=0)
            return _dot(ksl_ref[pl.ds(pl.multiple_of(first_row, TK), TK), :], w_scr[...])

        m, acc = m0, acc0
        for a, w_scr in ((0, wa_scr), (1, wb_scr)):
            first = q0 + TQ + TK * a
            s = sel_scores(w_scr, first, jnp.maximum(qb - 1 + a, 0))
            s = s + (wt_ref[g, TQ + TK * a:TQ + TK * (a + 1), :] + pad_mask(first))
            m, acc = _tile_update(s, m, acc, vslt_ref[qb + 1 + a])

        n_far = jnp.maximum(qb - 1, 0)

        def far_scores(w_scr, kt):
            valid = kt < n_far
            ks = jnp.where(valid, kt, 0)
            return sel_scores(w_scr, KPAD + ks * TK, jnp.where(valid, kt, SEQ // TK))

        def far_v(kt):
            return vslt_ref[jnp.where(kt < n_far, kt, 0) + KPAD // TK]

        sa_scr[...] = far_scores(wa_scr, 0)

        def far_body(i, carry):
            m_i, acc_i = carry
            sb_scr[...] = far_scores(wb_scr, 2 * i + 1)
            m_i, acc_i = _tile_update(sa_scr[...], m_i, acc_i, far_v(2 * i))
            sa_scr[...] = far_scores(wa_scr, 2 * i + 2)
            return _tile_update(sb_scr[...], m_i, acc_i, far_v(2 * i + 1))

        m, acc = lax.fori_loop(0, (n_far + 1) // 2, far_body, (m, acc))
        o_s = _finish(acc)

        m, acc = m0, acc0
        for a in range(3):
            first = q0 + TK * a
            s = _dot(kw_ref[pl.ds(pl.multiple_of(first, TK), TK), :], q_g)
            s = s + (wt_ref[g, TK * a:TK * (a + 1), :] + pad_mask(first))
            m, acc = _tile_update(s, m, acc, vwt_ref[qb + a])
        o_w = _finish(acc)

        per_head = []
        for h in range(HPG):
            c0 = 3 * (g * HPG + h)
            cs = slice(h * TQ, (h + 1) * TQ)
            per_head.append(gt[c0:c0 + 1, :] * o_c[:, cs] + gt[c0 + 1:c0 + 2, :] * o_s[:, cs]
                            + gt[c0 + 2:c0 + 3, :] * o_w[:, cs])
        comb.append(per_head)

    for h in range(HPG):
        merged = jnp.where(drow < HEAD_DIM, comb[0][h], comb[1][h])
        o_ref[0, h] = merged.T.astype(BF16)


def _attn(qt, gatet, kc, vct, ovt, ksl, vslt, kw, vwt, ctmpl, wtmpl):
    consts = [kc, vct, ovt, ksl, vslt, kw, vwt, ctmpl, wtmpl]
    return pl.pallas_call(
        _attn_kernel,
        out_shape=jax.ShapeDtypeStruct((SEQ // TQ, HPG, TQ, LANES), BF16),
        grid=(SEQ // TQ,),
        in_specs=[pl.BlockSpec((D_ATT, TQ), lambda i: (0, i)),
                  pl.BlockSpec((LANES, TQ), lambda i: (0, i))]
                 + [_const_spec(a.shape) for a in consts],
        out_specs=pl.BlockSpec((1, HPG, TQ, LANES), lambda i: (i, 0, 0, 0)),
        scratch_shapes=[pltpu.VMEM((N_CMP_PAD, COLS), F32), pltpu.VMEM((N_CMP_PAD, COLS), F32),
                        pltpu.VMEM((SEQ // TK + 1, SUBLANES, TQ), F32),
                        pltpu.VMEM((2 * D_KV, COLS), BF16), pltpu.VMEM((2 * D_KV, COLS), BF16),
                        pltpu.VMEM((TK, COLS), F32), pltpu.VMEM((TK, COLS), F32)],
        compiler_params=_params(("arbitrary",)),
        name="nsa_attn",
    )(qt, gatet, *consts)


def _t5_bucket(dist):
    max_exact = N_BUCKETS // 2
    d = jnp.maximum(dist, 0)
    df = jnp.maximum(d, 1).astype(F32)
    large = max_exact + (jnp.log(df / max_exact) / math.log(MAX_DIST / max_exact)
                         * (N_BUCKETS - max_exact)).astype(jnp.int32)
    large = jnp.minimum(large, N_BUCKETS - 1)
    return jnp.where(d < max_exact, d, large)


def _to_group_cols(t):
    r = t.shape[1]
    return t.reshape(N_KV, HPG, r, TQ).transpose(0, 2, 1, 3).reshape(N_KV, r, COLS)


def _bias_templates(rel_bias):
    biasp = rel_bias[:, _t5_bucket(jnp.arange(MAX_DIST))] - rel_bias[:, N_BUCKETS - 1:]
    biasp = biasp * LOG2E

    wk = WINDOW + TQ
    span = wk + TQ - 1
    d = jnp.arange(span) - (wk - 1) + WINDOW
    f = jnp.where((d >= 0) & (d < WINDOW), biasp[:, jnp.clip(d, 0, MAX_DIST - 1)], NEG)
    fext = jnp.concatenate([f, jnp.zeros((N_HEADS, 1), F32)], axis=1)
    rep = jnp.broadcast_to(fext[:, None, :], (N_HEADS, wk, span + 1)).reshape(N_HEADS, -1)
    skew = rep[:, :wk * span].reshape(N_HEADS, wk, span)
    wtmpl = _to_group_cols(skew[:, :, wk - 1:wk - 1 + TQ])

    r = jnp.arange(-9, 15)[:, None]
    dc = jnp.arange(TQ)[None, :] - (CMP_LEN - 1) - CMP_STRIDE * r
    band = jnp.where(dc >= 0, biasp[:, jnp.clip(dc, 0, MAX_DIST - 1)], NEG)
    lo_rows = CT_ZERO - 9
    ctmpl = jnp.concatenate(
        [jnp.zeros((N_HEADS, lo_rows, TQ), F32), band,
         jnp.full((N_HEADS, CT_ROWS - lo_rows - band.shape[1], TQ), NEG, F32)], axis=1)
    return _to_group_cols(ctmpl).astype(F32), wtmpl.astype(F32)


def _overlap_t():
    cmp_start = jnp.arange(N_CMP_PAD) * CMP_STRIDE
    slc_start = jnp.arange(N_SLC) * SLC_LEN
    ov = ((cmp_start[None, :] < slc_start[:, None] + SLC_LEN)
          & (cmp_start[None, :] + CMP_LEN > slc_start[:, None])
          & (jnp.arange(N_CMP_PAD)[None, :] < N_CMP))
    return ov.astype(BF16)


def _with_block_indicator(k):
    blk = (jnp.arange(SEQ)[:, None] % TK) // SLC_LEN
    ind = (blk == jnp.arange(D_KV)[None, :]).astype(BF16)
    return jnp.concatenate([k, ind], axis=1)


def _vt_tiles(vt):
    ones = jnp.concatenate([jnp.ones((1, SEQ), BF16), jnp.zeros((VROWS - D_KV - 1, SEQ), BF16)])
    t = jnp.concatenate([vt, ones], axis=0).reshape(VROWS, SEQ // TK, TK).transpose(1, 0, 2)
    return jnp.pad(t, ((KPAD // TK, 0), (0, 0), (0, 0)))


def kernel(x, c, w_ada, b_ada, g_ffn1, w_gu1, w_down1, g_mix, w_in, w_dw, b_dw, ln_g, ln_b,
           pe_k, pe_v, w_ck1, w_ck2, w_cv1, w_cv2, rel_bias, w_out, g_ffn2, w_gu2, w_down2,
           g_final):
    assert x.shape == (1, SEQ, D_MODEL) and w_ada.shape[0] == 1
    x2 = x[0]
    mod = _ada(c.reshape(D_MODEL, 1), w_ada[0], b_ada)
    sh1, sc1, gt1, sh2, sc2, gt2, sh3, sc3, gt3 = [mod[:, k * D_MODEL:(k + 1) * D_MODEL]
                                                   for k in range(9)]

    x1 = _ffn1(x2, g_ffn1, sh1, sc1, gt1, w_gu1[0].astype(BF16), w_down1[0].astype(BF16))

    wi = w_in[0]
    wq = wi[:, 1024:1536].reshape(D_MODEL, N_KV, HPG, HEAD_DIM).transpose(0, 2, 1, 3)
    wq = wq.reshape(D_MODEL, D_ATT) * (HEAD_DIM ** -0.5 * LOG2E)
    w_in_p = jnp.concatenate(
        [wi[:, :1024], wq, wi[:, 1536:], jnp.zeros((D_MODEL, D_IN_PAD - wi.shape[1]), F32)],
        axis=1).astype(BF16)
    hc, kcvc, ksl, kw, qt, vslt, vwt, gatet = _proj(x1, g_mix, sh2, sc2, w_in_p)

    y_conv = _conv(hc, w_dw[0], b_dw, ln_g, ln_b)

    cmat = kcvc.reshape(SEQ, 4, HEAD_DIM).transpose(1, 0, 2).reshape(4, N_CMP_PAD, 1024)
    pe = jnp.stack([pe_k[0].reshape(1, -1), pe_v[0].reshape(1, -1)])
    w1 = jnp.stack([w_ck1[0], w_cv1[0]]).astype(BF16)
    w2 = jnp.stack([w_ck2[0], w_cv2[0]]).astype(BF16)
    cmp = _compress(cmat, pe, w1, w2)
    kc = jnp.concatenate([cmp[0], cmp[1]], axis=1).astype(BF16)
    vct = jnp.concatenate([cmp[2], cmp[3]], axis=1).T.astype(BF16)

    ctmpl, wtmpl = _bias_templates(rel_bias)
    front = ((KPAD, 0), (0, 0))
    y_att = _attn(qt, gatet, kc, vct, _overlap_t(), jnp.pad(_with_block_indicator(ksl), front),
                  _vt_tiles(vslt),
                  jnp.pad(kw, front), _vt_tiles(vwt), ctmpl, wtmpl)

    wo = w_out[0]
    woa = wo[D_CONV:].reshape(N_KV, HPG, HEAD_DIM, D_MODEL).transpose(1, 0, 2, 3)
    woa = woa.reshape(HPG, LANES, D_MODEL).astype(BF16)
    out = _ffn2(x1, y_conv, y_att, wo[:D_CONV].astype(BF16), woa, gt2,
                g_ffn2, sh3, sc3, gt3, w_gu2[0].astype(BF16), w_down2[0].astype(BF16),
                g_final.reshape(1, D_MODEL))
    return out[None]
```

```python
import math

import jax
import jax.numpy as jnp
from jax import lax
from jax.experimental import pallas as pl
from jax.experimental.pallas import tpu as pltpu

F32 = jnp.float32
BF16 = jnp.bfloat16

D_MODEL = 1024
SEQ = 16384
D_CONV = 512
CONV_WIDTH = 31
N_HEADS = 8
N_KV = 2
HPG = 4
HEAD_DIM = 64
D_ATT = 512
D_KV = 128
CMP_LEN = 32
CMP_STRIDE = 16
CMP_HIDDEN = 128
N_CMP = (SEQ - CMP_LEN) // CMP_STRIDE + 1
N_CMP_PAD = 1024
SLC_LEN = 64
N_SLC = SEQ // SLC_LEN
SLC_TOPK = 16
WINDOW = 512
N_FORCED = 3
TAKEN = -2.0
N_BUCKETS = 32
MAX_DIST = 128
D_FF = 2816
FFN_RES = 0.5
EPS = 1e-6
NEG = -1e30
M_FLOOR = -1e29
LOG2E = math.log2(math.e)

V7X_VMEM_BYTES = 64 * 1024 * 1024
VMEM_LIMIT = V7X_VMEM_BYTES - 6 * 1024 * 1024
LANES = 128
SUBLANES = 8

TQ = 256
TK = 256
COLS = HPG * TQ
KPAD = WINDOW
VROWS = D_KV + 16
BLK_PER_TILE = TK // SLC_LEN
CT_ZERO = 520
CT_ROWS = 800
FFN_TM = 512
FFN_TF = D_FF // 2
PROJ_TM = 512
CONV_TM = 256
CONV_HALO = 32
D_IN_PAD = 2432


def _params(sem):
    return pltpu.CompilerParams(dimension_semantics=sem, vmem_limit_bytes=VMEM_LIMIT)


def _const_spec(shape):
    nd = len(shape)
    return pl.BlockSpec(shape, lambda *_: (0,) * nd, pipeline_mode=pl.Buffered(1))


def _sigmoid(v):
    return 1.0 / (1.0 + jnp.exp(-v))


def _dot(a, b):
    return jnp.dot(a, b, preferred_element_type=F32)


def _ada_kernel(c_ref, w_ref, b_ref, o_ref):
    c = c_ref[...]
    sc = c * _sigmoid(c)
    o_ref[...] = jnp.sum(w_ref[...] * sc, axis=0, keepdims=True) + b_ref[...]


def _ada(c_col, w, b):
    n = w.shape[1]
    tn = n // 8
    return pl.pallas_call(
        _ada_kernel,
        out_shape=jax.ShapeDtypeStruct((1, n), F32),
        grid=(8,),
        in_specs=[pl.BlockSpec((D_MODEL, 1), lambda j: (0, 0)),
                  pl.BlockSpec((D_MODEL, tn), lambda j: (0, j)),
                  pl.BlockSpec((1, tn), lambda j: (0, j))],
        out_specs=pl.BlockSpec((1, tn), lambda j: (0, j)),
        compiler_params=_params(("arbitrary",)),
        name="ada",
    )(c_col, w, b)


def _rms_mod(x, g, sh, sc):
    ms = jnp.mean(x * x, axis=-1, keepdims=True)
    y = x * lax.rsqrt(ms + EPS) * g
    return y * (1.0 + sc) + sh


def _ffn_body(x, g_ref, sh_ref, sc_ref, gt_ref, wgu_ref, wd_ref):
    hb = _rms_mod(x, g_ref[...], sh_ref[...], sc_ref[...]).astype(BF16)
    acc = jnp.zeros((x.shape[0], D_MODEL), F32)
    for f in range(D_FF // FFN_TF):
        gg = _dot(hb, wgu_ref[:, f * FFN_TF:(f + 1) * FFN_TF])
        uu = _dot(hb, wgu_ref[:, D_FF + f * FFN_TF:D_FF + (f + 1) * FFN_TF])
        a = (gg * _sigmoid(gg) * uu).astype(BF16)
        acc = acc + _dot(a, wd_ref[f * FFN_TF:(f + 1) * FFN_TF, :])
    return x + (FFN_RES * gt_ref[...]) * acc


def _ffn1_kernel(x_ref, g_ref, sh_ref, sc_ref, gt_ref, wgu_ref, wd_ref, o_ref):
    o_ref[...] = _ffn_body(x_ref[...], g_ref, sh_ref, sc_ref, gt_ref, wgu_ref, wd_ref)


def _ffn2_kernel(x_ref, yc_ref, ya_ref, woc_ref, woa_ref, gt2_ref,
                 g_ref, sh_ref, sc_ref, gt_ref, wgu_ref, wd_ref, gf_ref, o_ref):
    tm = x_ref.shape[0]
    y = _dot(yc_ref[...], woc_ref[...])
    for h in range(HPG):
        y = y + _dot(ya_ref[:, h].reshape(tm, LANES), woa_ref[h])
    x = x_ref[...] + gt2_ref[...] * y
    out = _ffn_body(x, g_ref, sh_ref, sc_ref, gt_ref, wgu_ref, wd_ref)
    ms = jnp.mean(out * out, axis=-1, keepdims=True)
    o_ref[...] = out * lax.rsqrt(ms + EPS) * gf_ref[...]


def _row_spec(tm, n):
    return pl.BlockSpec((tm, n), lambda i: (i, 0))


def _ffn1(x, g, sh, sc, gt, wgu, wd):
    vec = _const_spec((1, D_MODEL))
    return pl.pallas_call(
        _ffn1_kernel,
        out_shape=jax.ShapeDtypeStruct((SEQ, D_MODEL), F32),
        grid=(SEQ // FFN_TM,),
        in_specs=[_row_spec(FFN_TM, D_MODEL), vec, vec, vec, vec,
                  _const_spec((D_MODEL, 2 * D_FF)), _const_spec((D_FF, D_MODEL))],
        out_specs=_row_spec(FFN_TM, D_MODEL),
        compiler_params=_params(("arbitrary",)),
        name="ffn1",
    )(x, g, sh, sc, gt, wgu, wd)


def _ffn2(x, yc, ya, woc, woa, gt2, g, sh, sc, gt, wgu, wd, gf):
    vec = _const_spec((1, D_MODEL))
    nt = FFN_TM // TQ
    return pl.pallas_call(
        _ffn2_kernel,
        out_shape=jax.ShapeDtypeStruct((SEQ, D_MODEL), F32),
        grid=(SEQ // FFN_TM,),
        in_specs=[_row_spec(FFN_TM, D_MODEL), _row_spec(FFN_TM, D_CONV),
                  pl.BlockSpec((nt, HPG, TQ, LANES), lambda i: (i, 0, 0, 0)),
                  _const_spec((D_CONV, D_MODEL)), _const_spec((HPG, LANES, D_MODEL)), vec,
                  vec, vec, vec, vec,
                  _const_spec((D_MODEL, 2 * D_FF)), _const_spec((D_FF, D_MODEL)), vec],
        out_specs=_row_spec(FFN_TM, D_MODEL),
        compiler_params=_params(("arbitrary",)),
        name="ffn2",
    )(x, yc, ya, woc, woa, gt2, g, sh, sc, gt, wgu, wd, gf)


def _proj_kernel(x_ref, g_ref, sh_ref, sc_ref, w_ref,
                 hc_ref, kcvc_ref, ksl_ref, kw_ref, qt_ref, vslt_ref, vwt_ref, gatet_ref):
    hb = _rms_mod(x_ref[...], g_ref[...], sh_ref[...], sc_ref[...]).astype(BF16)
    u = _dot(hb, w_ref[...])
    hc_ref[...] = u[:, 0:512] * _sigmoid(u[:, 512:1024])
    kcvc_ref[...] = u[:, 1536:1792]
    ksl_ref[...] = u[:, 1792:1920].astype(BF16)
    kw_ref[...] = u[:, 2048:2176].astype(BF16)
    qt_ref[...] = u[:, 1024:1536].T.astype(BF16)
    vslt_ref[...] = u[:, 1920:2048].T.astype(BF16)
    vwt_ref[...] = u[:, 2176:2304].T.astype(BF16)
    gatet_ref[...] = _sigmoid(u[:, 2304:2432]).T


def _proj(x, g, sh, sc, w):
    vec = _const_spec((1, D_MODEL))
    tm = PROJ_TM
    rows = [(D_CONV, F32), (2 * D_KV, F32), (D_KV, BF16), (D_KV, BF16)]
    cols = [(D_ATT, BF16), (D_KV, BF16), (D_KV, BF16), (LANES, F32)]
    return pl.pallas_call(
        _proj_kernel,
        out_shape=[jax.ShapeDtypeStruct((SEQ, n), dt) for n, dt in rows]
                  + [jax.ShapeDtypeStruct((n, SEQ), dt) for n, dt in cols],
        grid=(SEQ // tm,),
        in_specs=[_row_spec(tm, D_MODEL), vec, vec, vec, _const_spec((D_MODEL, D_IN_PAD))],
        out_specs=[_row_spec(tm, n) for n, _ in rows]
                  + [pl.BlockSpec((n, tm), lambda i: (0, i)) for n, _ in cols],
        compiler_params=_params(("arbitrary",)),
        name="proj",
    )(x, g, sh, sc, w)


def _conv_kernel(prev_ref, cur_ref, w_ref, b_ref, g_ref, bb_ref, o_ref):
    i = pl.program_id(0)
    tm = cur_ref.shape[0]
    prev = jnp.where(i > 0, prev_ref[...], 0.0)
    xx = jnp.concatenate([prev, cur_ref[...]], axis=0)
    acc = jnp.zeros((tm, D_CONV), F32) + b_ref[...]
    off = CONV_HALO - (CONV_WIDTH - 1)
    for w in range(CONV_WIDTH):
        acc = acc + xx[off + w:off + w + tm, :] * w_ref[w:w + 1, :]
    mu = jnp.mean(acc, axis=-1, keepdims=True)
    dlt = acc - mu
    var = jnp.mean(dlt * dlt, axis=-1, keepdims=True)
    y = dlt * lax.rsqrt(var + EPS) * g_ref[...] + bb_ref[...]
    o_ref[...] = (y * _sigmoid(y)).astype(BF16)


def _conv(hc, w, b, g, bb):
    tm = CONV_TM
    r = tm // CONV_HALO
    vec = _const_spec((1, D_CONV))
    return pl.pallas_call(
        _conv_kernel,
        out_shape=jax.ShapeDtypeStruct((SEQ, D_CONV), BF16),
        grid=(SEQ // tm,),
        in_specs=[pl.BlockSpec((CONV_HALO, D_CONV), lambda i: (jnp.maximum(i * r - 1, 0), 0)),
                  _row_spec(tm, D_CONV),
                  _const_spec((CONV_WIDTH, D_CONV)), vec, vec, vec],
        out_specs=_row_spec(tm, D_CONV),
        compiler_params=_params(("arbitrary",)),
        name="conv",
    )(hc, hc, w, b, g, bb)


def _compress_kernel(c_ref, pe_ref, w1_ref, w2_ref, o_ref):
    half = CMP_STRIDE * HEAD_DIM
    cm = c_ref[0]
    pe = pe_ref[0]
    top = (cm + pe[:, :half]).astype(BF16)
    bot = (cm + pe[:, half:]).astype(BF16)
    a = _dot(top, w1_ref[0, :half, :])
    b = _dot(bot, w1_ref[0, half:, :])
    b_up = jnp.concatenate([b[1:], jnp.zeros((1, CMP_HIDDEN), F32)], axis=0)
    pre = a + b_up
    hid = (pre * _sigmoid(pre)).astype(BF16)
    o_ref[0] = _dot(hid, w2_ref[0])


def _compress(cmat, pe, w1, w2):
    return pl.pallas_call(
        _compress_kernel,
        out_shape=jax.ShapeDtypeStruct((4, N_CMP_PAD, HEAD_DIM), F32),
        grid=(4,),
        in_specs=[pl.BlockSpec((1, N_CMP_PAD, 1024), lambda i: (i, 0, 0)),
                  pl.BlockSpec((1, 1, 2048), lambda i: (i // 2, 0, 0)),
                  pl.BlockSpec((1, 2048, CMP_HIDDEN), lambda i: (i // 2, 0, 0)),
                  pl.BlockSpec((1, CMP_HIDDEN, HEAD_DIM), lambda i: (i // 2, 0, 0))],
        out_specs=pl.BlockSpec((1, N_CMP_PAD, HEAD_DIM), lambda i: (i, 0, 0)),
        compiler_params=_params(("arbitrary",)),
        name="compress",
    )(cmat, pe, w1, w2)


def _tile_update(s, m, acc, vt):
    m_new = jnp.maximum(m, jnp.max(s, axis=0, keepdims=True))
    alpha = jnp.exp2(m - m_new)
    p = jnp.exp2(s - m_new).astype(BF16)
    return m_new, alpha * acc + _dot(vt, p)


def _finish(acc):
    return acc[:D_KV] / jnp.maximum(acc[D_KV:D_KV + 1], 1e-30)


def _attn_kernel(qt_ref, gt_ref, kc_ref, vct_ref, ovt_ref, ksl_ref, vslt_ref, kw_ref, vwt_ref,
                 ct_ref, wt_ref, o_ref, s_scr, p_scr, sel_scr, wa_scr, wb_scr, sa_scr, sb_scr):
    qb = pl.program_id(0)
    q0 = qb * TQ
    qt = qt_ref[...]
    gt = gt_ref[...]
    drow = lax.broadcasted_iota(jnp.int32, (D_KV, TQ), 0)
    krow = lax.broadcasted_iota(jnp.int32, (TK, 1), 0)
    m0 = jnp.full((1, COLS), M_FLOOR, F32)
    acc0 = jnp.zeros((VROWS, COLS), F32)

    def pad_mask(first_row):
        return jnp.where(krow + first_row >= KPAD, 0.0, NEG)

    comb = []
    for g in range(N_KV):
        keep = (drow >= HEAD_DIM * g) & (drow < HEAD_DIM * (g + 1))
        q_g = jnp.concatenate(
            [jnp.where(keep, qt[h * D_KV:(h + 1) * D_KV, :], jnp.zeros((), BF16))
             for h in range(HPG)], axis=1)

        mcol = m0
        for ct in range(N_CMP_PAD // TK):
            rows = slice(ct * TK, (ct + 1) * TK)
            r0 = pl.multiple_of(jnp.clip(TK * ct - (TQ // CMP_STRIDE) * qb + CT_ZERO,
                                         0, CT_ROWS - TK), SUBLANES)
            s = _dot(kc_ref[rows, :], q_g) + ct_ref[g, pl.ds(r0, TK), :]
            s_scr[rows, :] = s
            mcol = jnp.maximum(mcol, jnp.max(s, axis=0, keepdims=True))
        lsum = jnp.zeros((1, COLS), F32)
        acc_c = jnp.zeros((D_KV, COLS), F32)
        for ct in range(N_CMP_PAD // TK):
            rows = slice(ct * TK, (ct + 1) * TK)
            p = jnp.exp2(s_scr[rows, :] - mcol)
            p_scr[rows, :] = p
            lsum = lsum + jnp.sum(p, axis=0, keepdims=True)
            acc_c = acc_c + _dot(vct_ref[:, rows], p.astype(BF16))
        rinv = 1.0 / jnp.maximum(lsum, 1e-30)
        o_c = acc_c * rinv
        imp = jnp.zeros((N_SLC, TQ), F32)
        for ct in range(N_CMP_PAD // TK):
            rows = slice(ct * TK, (ct + 1) * TK)
            pn = p_scr[rows, :] * rinv
            ps = pn[:, 0:TQ] + pn[:, TQ:2 * TQ] + pn[:, 2 * TQ:3 * TQ] + pn[:, 3 * TQ:4 * TQ]
            hi = ps.astype(BF16)
            lo = (ps - hi.astype(F32)).astype(BF16)
            ov = ovt_ref[:, rows]
            imp = imp + _dot(ov, hi) + _dot(ov, lo)

        cur = jnp.right_shift(q0 + lax.broadcasted_iota(jnp.int32, (1, TQ), 1), 6)
        jcol = lax.broadcasted_iota(jnp.int32, (N_SLC, 1), 0)
        forced = (jcol == 0) | (jcol == cur) | (jcol == cur - 1)
        sc = jnp.where(forced, TAKEN, jnp.where(jcol <= cur, imp, -1.0))
        jf = lax.broadcasted_iota(jnp.int32, (N_SLC, TQ), 0).astype(F32)
        for _ in range(SLC_TOPK - N_FORCED):
            mx = jnp.max(sc, axis=0, keepdims=True)
            idx = jnp.min(jnp.where(sc == mx, jf, 1e9), axis=0, keepdims=True)
            sc = jnp.where(jf == idx, TAKEN, sc)
        selneg = jnp.where(sc == TAKEN, 0.0, NEG)
        unused = jnp.zeros((SUBLANES - BLK_PER_TILE, TQ), F32)
        for kt in range(SEQ // TK):
            sel_scr[kt] = jnp.concatenate(
                [selneg[kt * BLK_PER_TILE:(kt + 1) * BLK_PER_TILE, :], unused], axis=0)
        sel_scr[SEQ // TK] = jnp.concatenate(
            [jnp.full((BLK_PER_TILE, TQ), NEG, F32), unused], axis=0)

        zrows = jnp.zeros((2 * D_KV - D_KV - 2 * SUBLANES, COLS), BF16)
        for w_scr in (wa_scr, wb_scr):
            w_scr[0:D_KV, :] = q_g
            w_scr[D_KV + 2 * SUBLANES:, :] = zrows

        def sel_scores(w_scr, first_row, sel_idx):
            blk = sel_scr[sel_idx]
            rows8 = jnp.concatenate([blk] * HPG, axis=1).astype(BF16)
            w_scr[D_KV:D_KV + 2 * SUBLANES, :] = jnp.concatenate(
                [rows8, jnp.zeros((SUBLANES, COLS), BF16)], axis=0)
            return _dot(ksl_ref[pl.ds(pl.multiple_of(first_row, TK), TK), :], w_scr[...])

        m, acc = m0, acc0
        for a, w_scr in ((0, wa_scr), (1, wb_scr)):
            first = q0 + TQ + TK * a
            s = sel_scores(w_scr, first, jnp.maximum(qb - 1 + a, 0))
            s = s + (wt_ref[g, TQ + TK * a:TQ + TK * (a + 1), :] + pad_mask(first))
            m, acc = _tile_update(s, m, acc, vslt_ref[qb + 1 + a])

        n_far = jnp.maximum(qb - 1, 0)

        def far_scores(w_scr, kt):
            valid = kt < n_far
            ks = jnp.where(valid, kt, 0)
            return sel_scores(w_scr, KPAD + ks * TK, jnp.where(valid, kt, SEQ // TK))

        def far_v(kt):
            return vslt_ref[jnp.where(kt < n_far, kt, 0) + KPAD // TK]

        sa_scr[...] = far_scores(wa_scr, 0)

        def far_body(i, carry):
            m_i, acc_i = carry
            sb_scr[...] = far_scores(wb_scr, 2 * i + 1)
            m_i, acc_i = _tile_update(sa_scr[...], m_i, acc_i, far_v(2 * i))
            sa_scr[...] = far_scores(wa_scr, 2 * i + 2)
            return _tile_update(sb_scr[...], m_i, acc_i, far_v(2 * i + 1))

        m, acc = lax.fori_loop(0, (n_far + 1) // 2, far_body, (m, acc))
        o_s = _finish(acc)

        m, acc = m0, acc0
        for a in range(3):
            first = q0 + TK * a
            s = _dot(kw_ref[pl.ds(pl.multiple_of(first, TK), TK), :], q_g)
            s = s + (wt_ref[g, TK * a:TK * (a + 1), :] + pad_mask(first))
            m, acc = _tile_update(s, m, acc, vwt_ref[qb + a])
        o_w = _finish(acc)

        per_head = []
        for h in range(HPG):
            c0 = 3 * (g * HPG + h)
            cs = slice(h * TQ, (h + 1) * TQ)
            per_head.append(gt[c0:c0 + 1, :] * o_c[:, cs] + gt[c0 + 1:c0 + 2, :] * o_s[:, cs]
                            + gt[c0 + 2:c0 + 3, :] * o_w[:, cs])
        comb.append(per_head)

    for h in range(HPG):
        merged = jnp.where(drow < HEAD_DIM, comb[0][h], comb[1][h])
        o_ref[0, h] = merged.T.astype(BF16)


def _attn(qt, gatet, kc, vct, ovt, ksl, vslt, kw, vwt, ctmpl, wtmpl):
    consts = [kc, vct, ovt, ksl, vslt, kw, vwt, ctmpl, wtmpl]
    return pl.pallas_call(
        _attn_kernel,
        out_shape=jax.ShapeDtypeStruct((SEQ // TQ, HPG, TQ, LANES), BF16),
        grid=(SEQ // TQ,),
        in_specs=[pl.BlockSpec((D_ATT, TQ), lambda i: (0, i)),
                  pl.BlockSpec((LANES, TQ), lambda i: (0, i))]
                 + [_const_spec(a.shape) for a in consts],
        out_specs=pl.BlockSpec((1, HPG, TQ, LANES), lambda i: (i, 0, 0, 0)),
        scratch_shapes=[pltpu.VMEM((N_CMP_PAD, COLS), F32), pltpu.VMEM((N_CMP_PAD, COLS), F32),
                        pltpu.VMEM((SEQ // TK + 1, SUBLANES, TQ), F32),
                        pltpu.VMEM((2 * D_KV, COLS), BF16), pltpu.VMEM((2 * D_KV, COLS), BF16),
                        pltpu.VMEM((TK, COLS), F32), pltpu.VMEM((TK, COLS), F32)],
        compiler_params=_params(("arbitrary",)),
        name="nsa_attn",
    )(qt, gatet, *consts)


def _t5_bucket(dist):
    max_exact = N_BUCKETS // 2
    d = jnp.maximum(dist, 0)
    df = jnp.maximum(d, 1).astype(F32)
    large = max_exact + (jnp.log(df / max_exact) / math.log(MAX_DIST / max_exact)
                         * (N_BUCKETS - max_exact)).astype(jnp.int32)
    large = jnp.minimum(large, N_BUCKETS - 1)
    return jnp.where(d < max_exact, d, large)


def _toeplitz_kernel(f_ref, o_ref):
    wk, span1 = o_ref.shape[1], f_ref.shape[1]
    for hd in range(N_HEADS):
        x = jnp.broadcast_to(f_ref[hd:hd + 1, :], (wk, span1))
        y = pltpu.roll(x, span1 - (wk - 1), axis=1, stride=1, stride_axis=0)
        o_ref[hd // HPG, :, (hd % HPG) * TQ:(hd % HPG + 1) * TQ] = y[:, :TQ]


def _toeplitz(fext):
    wk = WINDOW + TQ
    return pl.pallas_call(
        _toeplitz_kernel,
        out_shape=jax.ShapeDtypeStruct((N_KV, wk, COLS), F32),
        compiler_params=_params(None),
        name="toeplitz",
    )(fext)


def _to_group_cols(t):
    r = t.shape[1]
    return t.reshape(N_KV, HPG, r, TQ).transpose(0, 2, 1, 3).reshape(N_KV, r, COLS)


def _bias_templates(rel_bias):
    biasp = rel_bias[:, _t5_bucket(jnp.arange(MAX_DIST))] - rel_bias[:, N_BUCKETS - 1:]
    biasp = biasp * LOG2E

    wk = WINDOW + TQ
    span = wk + TQ - 1
    d = jnp.arange(span) - (wk - 1) + WINDOW
    f = jnp.where((d >= 0) & (d < WINDOW), biasp[:, jnp.clip(d, 0, MAX_DIST - 1)], NEG)
    fext = jnp.concatenate([f, jnp.zeros((N_HEADS, 1), F32)], axis=1)
    wtmpl = _toeplitz(fext)

    r = jnp.arange(-9, 15)[:, None]
    dc = jnp.arange(TQ)[None, :] - (CMP_LEN - 1) - CMP_STRIDE * r
    band = jnp.where(dc >= 0, biasp[:, jnp.clip(dc, 0, MAX_DIST - 1)], NEG)
    lo_rows = CT_ZERO - 9
    ctmpl = jnp.concatenate(
        [jnp.zeros((N_HEADS, lo_rows, TQ), F32), band,
         jnp.full((N_HEADS, CT_ROWS - lo_rows - band.shape[1], TQ), NEG, F32)], axis=1)
    return _to_group_cols(ctmpl).astype(F32), wtmpl.astype(F32)


def _overlap_t():
    cmp_start = jnp.arange(N_CMP_PAD) * CMP_STRIDE
    slc_start = jnp.arange(N_SLC) * SLC_LEN
    ov = ((cmp_start[None, :] < slc_start[:, None] + SLC_LEN)
          & (cmp_start[None, :] + CMP_LEN > slc_start[:, None])
          & (jnp.arange(N_CMP_PAD)[None, :] < N_CMP))
    return ov.astype(BF16)


def _with_block_indicator(k):
    blk = (jnp.arange(SEQ)[:, None] % TK) // SLC_LEN
    ind = (blk == jnp.arange(D_KV)[None, :]).astype(BF16)
    return jnp.concatenate([k, ind], axis=1)


def _vt_tiles(vt):
    ones = jnp.concatenate([jnp.ones((1, SEQ), BF16), jnp.zeros((VROWS - D_KV - 1, SEQ), BF16)])
    t = jnp.concatenate([vt, ones], axis=0).reshape(VROWS, SEQ // TK, TK).transpose(1, 0, 2)
    return jnp.pad(t, ((KPAD // TK, 0), (0, 0), (0, 0)))


def kernel(x, c, w_ada, b_ada, g_ffn1, w_gu1, w_down1, g_mix, w_in, w_dw, b_dw, ln_g, ln_b,
           pe_k, pe_v, w_ck1, w_ck2, w_cv1, w_cv2, rel_bias, w_out, g_ffn2, w_gu2, w_down2,
           g_final):
    assert x.shape == (1, SEQ, D_MODEL) and w_ada.shape[0] == 1
    x2 = x[0]
    mod = _ada(c.reshape(D_MODEL, 1), w_ada[0], b_ada)
    sh1, sc1, gt1, sh2, sc2, gt2, sh3, sc3, gt3 = [mod[:, k * D_MODEL:(k + 1) * D_MODEL]
                                                   for k in range(9)]

    x1 = _ffn1(x2, g_ffn1, sh1, sc1, gt1, w_gu1[0].astype(BF16), w_down1[0].astype(BF16))

    wi = w_in[0]
    wq = wi[:, 1024:1536].reshape(D_MODEL, N_KV, HPG, HEAD_DIM).transpose(0, 2, 1, 3)
    wq = wq.reshape(D_MODEL, D_ATT) * (HEAD_DIM ** -0.5 * LOG2E)
    w_in_p = jnp.concatenate(
        [wi[:, :1024], wq, wi[:, 1536:], jnp.zeros((D_MODEL, D_IN_PAD - wi.shape[1]), F32)],
        axis=1).astype(BF16)
    hc, kcvc, ksl, kw, qt, vslt, vwt, gatet = _proj(x1, g_mix, sh2, sc2, w_in_p)

    y_conv = _conv(hc, w_dw[0], b_dw, ln_g, ln_b)

    cmat = kcvc.reshape(SEQ, 4, HEAD_DIM).transpose(1, 0, 2).reshape(4, N_CMP_PAD, 1024)
    pe = jnp.stack([pe_k[0].reshape(1, -1), pe_v[0].reshape(1, -1)])
    w1 = jnp.stack([w_ck1[0], w_cv1[0]]).astype(BF16)
    w2 = jnp.stack([w_ck2[0], w_cv2[0]]).astype(BF16)
    cmp = _compress(cmat, pe, w1, w2)
    kc = jnp.concatenate([cmp[0], cmp[1]], axis=1).astype(BF16)
    vct = jnp.concatenate([cmp[2], cmp[3]], axis=1).T.astype(BF16)

    ctmpl, wtmpl = _bias_templates(rel_bias)
    front = ((KPAD, 0), (0, 0))
    y_att = _attn(qt, gatet, kc, vct, _overlap_t(), jnp.pad(_with_block_indicator(ksl), front),
                  _vt_tiles(vslt),
                  jnp.pad(kw, front), _vt_tiles(vwt), ctmpl, wtmpl)

    wo = w_out[0]
    woa = wo[D_CONV:].reshape(N_KV, HPG, HEAD_DIM, D_MODEL).transpose(1, 0, 2, 3)
    woa = woa.reshape(HPG, LANES, D_MODEL).astype(BF16)
    out = _ffn2(x1, y_conv, y_att, wo[:D_CONV].astype(BF16), woa, gt2,
                g_ffn2, sh3, sc3, gt3, w_gu2[0].astype(BF16), w_down2[0].astype(BF16),
                g_final.reshape(1, D_MODEL))
    return out[None]
```

```python
import math

import jax
import jax.numpy as jnp
from jax import lax
from jax.experimental import pallas as pl
from jax.experimental.pallas import tpu as pltpu

F32 = jnp.float32
BF16 = jnp.bfloat16

D_MODEL = 1024
SEQ = 16384
D_CONV = 512
CONV_WIDTH = 31
N_HEADS = 8
N_KV = 2
HPG = 4
HEAD_DIM = 64
D_ATT = 512
D_KV = 128
CMP_LEN = 32
CMP_STRIDE = 16
CMP_HIDDEN = 128
N_CMP = (SEQ - CMP_LEN) // CMP_STRIDE + 1
N_CMP_PAD = 1024
SLC_LEN = 64
N_SLC = SEQ // SLC_LEN
SLC_TOPK = 16
WINDOW = 512
N_FORCED = 3
TAKEN = -2.0
N_BUCKETS = 32
MAX_DIST = 128
D_FF = 2816
FFN_RES = 0.5
EPS = 1e-6
NEG = -1e30
M_FLOOR = -1e29
LOG2E = math.log2(math.e)

V7X_VMEM_BYTES = 64 * 1024 * 1024
VMEM_LIMIT = V7X_VMEM_BYTES - 6 * 1024 * 1024
LANES = 128
SUBLANES = 8

TQ = 256
TK = 256
COLS = HPG * TQ
KPAD = WINDOW
VROWS = D_KV + 16
BLK_PER_TILE = TK // SLC_LEN
FAR_UNROLL = 4
CT_ZERO = 520
CT_ROWS = 800
FFN_TM = 512
FFN_TF = D_FF
PROJ_TM = 512
CONV_TM = 256
CONV_HALO = 32
CONV_CHUNK = 32
D_IN_PAD = 2432


def _params(sem):
    return pltpu.CompilerParams(dimension_semantics=sem, vmem_limit_bytes=VMEM_LIMIT)


def _const_spec(shape):
    nd = len(shape)
    return pl.BlockSpec(shape, lambda *_: (0,) * nd, pipeline_mode=pl.Buffered(1))


def _sigmoid(v):
    return 1.0 / (1.0 + jnp.exp(-v))


def _dot(a, b):
    return jnp.dot(a, b, preferred_element_type=F32)


def _ada_kernel(c_ref, w_ref, b_ref, o_ref):
    c = c_ref[...]
    sc = c * _sigmoid(c)
    o_ref[...] = jnp.sum(w_ref[...] * sc, axis=0, keepdims=True) + b_ref[...]


def _ada(c_col, w, b):
    n = w.shape[1]
    tn = n // 8
    return pl.pallas_call(
        _ada_kernel,
        out_shape=jax.ShapeDtypeStruct((1, n), F32),
        grid=(8,),
        in_specs=[pl.BlockSpec((D_MODEL, 1), lambda j: (0, 0)),
                  pl.BlockSpec((D_MODEL, tn), lambda j: (0, j)),
                  pl.BlockSpec((1, tn), lambda j: (0, j))],
        out_specs=pl.BlockSpec((1, tn), lambda j: (0, j)),
        compiler_params=_params(("arbitrary",)),
        name="ada",
    )(c_col, w, b)


def _rms_mod(x, g, sh, sc):
    ms = jnp.mean(x * x, axis=-1, keepdims=True)
    y = x * lax.rsqrt(ms + EPS) * g
    return y * (1.0 + sc) + sh


def _ffn_body(x, g_ref, sh_ref, sc_ref, gt_ref, wgu_ref, wd_ref):
    hb = _rms_mod(x, g_ref[...], sh_ref[...], sc_ref[...]).astype(BF16)
    acc = jnp.zeros((x.shape[0], D_MODEL), F32)
    for f in range(D_FF // FFN_TF):
        gg = _dot(hb, wgu_ref[:, f * FFN_TF:(f + 1) * FFN_TF])
        uu = _dot(hb, wgu_ref[:, D_FF + f * FFN_TF:D_FF + (f + 1) * FFN_TF])
        a = (gg * _sigmoid(gg) * uu).astype(BF16)
        acc = acc + _dot(a, wd_ref[f * FFN_TF:(f + 1) * FFN_TF, :])
    return x + (FFN_RES * gt_ref[...]) * acc


def _ffn1_kernel(x_ref, g_ref, sh_ref, sc_ref, gt_ref, wgu_ref, wd_ref, o_ref):
    o_ref[...] = _ffn_body(x_ref[...], g_ref, sh_ref, sc_ref, gt_ref, wgu_ref, wd_ref)


def _ffn2_kernel(x_ref, yc_ref, ya_ref, woc_ref, woa_ref, gt2_ref,
                 g_ref, sh_ref, sc_ref, gt_ref, wgu_ref, wd_ref, gf_ref, o_ref):
    tm = x_ref.shape[0]
    y = _dot(yc_ref[...], woc_ref[...])
    for h in range(HPG):
        y = y + _dot(ya_ref[:, h].reshape(tm, LANES), woa_ref[h])
    x = x_ref[...] + gt2_ref[...] * y
    out = _ffn_body(x, g_ref, sh_ref, sc_ref, gt_ref, wgu_ref, wd_ref)
    ms = jnp.mean(out * out, axis=-1, keepdims=True)
    o_ref[...] = out * lax.rsqrt(ms + EPS) * gf_ref[...]


def _row_spec(tm, n):
    return pl.BlockSpec((tm, n), lambda i: (i, 0))


def _ffn1(x, g, sh, sc, gt, wgu, wd):
    vec = _const_spec((1, D_MODEL))
    return pl.pallas_call(
        _ffn1_kernel,
        out_shape=jax.ShapeDtypeStruct((SEQ, D_MODEL), F32),
        grid=(SEQ // FFN_TM,),
        in_specs=[_row_spec(FFN_TM, D_MODEL), vec, vec, vec, vec,
                  _const_spec((D_MODEL, 2 * D_FF)), _const_spec((D_FF, D_MODEL))],
        out_specs=_row_spec(FFN_TM, D_MODEL),
        compiler_params=_params(("arbitrary",)),
        name="ffn1",
    )(x, g, sh, sc, gt, wgu, wd)


def _ffn2(x, yc, ya, woc, woa, gt2, g, sh, sc, gt, wgu, wd, gf):
    vec = _const_spec((1, D_MODEL))
    nt = FFN_TM // TQ
    return pl.pallas_call(
        _ffn2_kernel,
        out_shape=jax.ShapeDtypeStruct((SEQ, D_MODEL), F32),
        grid=(SEQ // FFN_TM,),
        in_specs=[_row_spec(FFN_TM, D_MODEL), _row_spec(FFN_TM, D_CONV),
                  pl.BlockSpec((nt, HPG, TQ, LANES), lambda i: (i, 0, 0, 0)),
                  _const_spec((D_CONV, D_MODEL)), _const_spec((HPG, LANES, D_MODEL)), vec,
                  vec, vec, vec, vec,
                  _const_spec((D_MODEL, 2 * D_FF)), _const_spec((D_FF, D_MODEL)), vec],
        out_specs=_row_spec(FFN_TM, D_MODEL),
        compiler_params=_params(("arbitrary",)),
        name="ffn2",
    )(x, yc, ya, woc, woa, gt2, g, sh, sc, gt, wgu, wd, gf)


def _proj_kernel(x_ref, g_ref, sh_ref, sc_ref, w_ref,
                 hc_ref, kcvc_ref, ksl_ref, kw_ref, qt_ref, vslt_ref, vwt_ref, gatet_ref):
    hb = _rms_mod(x_ref[...], g_ref[...], sh_ref[...], sc_ref[...]).astype(BF16)
    u = _dot(hb, w_ref[...])
    hc_ref[...] = u[:, 0:512] * _sigmoid(u[:, 512:1024])
    kcvc_ref[...] = u[:, 1536:1792]
    ksl_ref[...] = u[:, 1792:1920].astype(BF16)
    kw_ref[...] = u[:, 2048:2176].astype(BF16)
    qt_ref[...] = u[:, 1024:1536].T.astype(BF16)
    vslt_ref[...] = u[:, 1920:2048].T.astype(BF16)
    vwt_ref[...] = u[:, 2176:2304].T.astype(BF16)
    gatet_ref[...] = _sigmoid(u[:, 2304:2432]).T


def _proj(x, g, sh, sc, w):
    vec = _const_spec((1, D_MODEL))
    tm = PROJ_TM
    rows = [(D_CONV, F32), (2 * D_KV, F32), (D_KV, BF16), (D_KV, BF16)]
    cols = [(D_ATT, BF16), (D_KV, BF16), (D_KV, BF16), (LANES, F32)]
    return pl.pallas_call(
        _proj_kernel,
        out_shape=[jax.ShapeDtypeStruct((SEQ, n), dt) for n, dt in rows]
                  + [jax.ShapeDtypeStruct((n, SEQ), dt) for n, dt in cols],
        grid=(SEQ // tm,),
        in_specs=[_row_spec(tm, D_MODEL), vec, vec, vec, _const_spec((D_MODEL, D_IN_PAD))],
        out_specs=[_row_spec(tm, n) for n, _ in rows]
                  + [pl.BlockSpec((n, tm), lambda i: (0, i)) for n, _ in cols],
        compiler_params=_params(("arbitrary",)),
        name="proj",
    )(x, g, sh, sc, w)


def _conv_kernel(prev_ref, cur_ref, w_ref, b_ref, g_ref, bb_ref, o_ref, xs_ref):
    i = pl.program_id(0)
    tm = cur_ref.shape[0]
    prev = jnp.where(i > 0, prev_ref[...], 0.0)
    xx = jnp.concatenate([prev, cur_ref[...]], axis=0)
    span = tm + CONV_HALO - SUBLANES
    xs_ref[0] = xx
    for b in range(1, SUBLANES):
        xs_ref[b, 0:span, :] = xx[b:b + span, :]
    off = CONV_HALO - (CONV_WIDTH - 1)
    wts = w_ref[...]
    for r0 in range(0, tm, CONV_CHUNK):
        acc = jnp.zeros((CONV_CHUNK, D_CONV), F32) + b_ref[...]
        for w in range(CONV_WIDTH):
            a8, b = divmod(off + w, SUBLANES)
            lo = r0 + a8 * SUBLANES
            acc = acc + xs_ref[b, lo:lo + CONV_CHUNK, :] * wts[w:w + 1, :]
        mu = jnp.mean(acc, axis=-1, keepdims=True)
        dlt = acc - mu
        var = jnp.mean(dlt * dlt, axis=-1, keepdims=True)
        y = dlt * lax.rsqrt(var + EPS) * g_ref[...] + bb_ref[...]
        o_ref[r0:r0 + CONV_CHUNK, :] = (y * _sigmoid(y)).astype(BF16)


def _conv(hc, w, b, g, bb):
    tm = CONV_TM
    r = tm // CONV_HALO
    vec = _const_spec((1, D_CONV))
    return pl.pallas_call(
        _conv_kernel,
        out_shape=jax.ShapeDtypeStruct((SEQ, D_CONV), BF16),
        grid=(SEQ // tm,),
        in_specs=[pl.BlockSpec((CONV_HALO, D_CONV), lambda i: (jnp.maximum(i * r - 1, 0), 0)),
                  _row_spec(tm, D_CONV),
                  _const_spec((CONV_WIDTH, D_CONV)), vec, vec, vec],
        out_specs=_row_spec(tm, D_CONV),
        scratch_shapes=[pltpu.VMEM((SUBLANES, CONV_HALO + tm, D_CONV), F32)],
        compiler_params=_params(("arbitrary",)),
        name="conv",
    )(hc, hc, w, b, g, bb)


def _compress_kernel(c_ref, pe_ref, w1_ref, w2_ref, o_ref):
    half = CMP_STRIDE * HEAD_DIM
    cm = c_ref[0]
    pe = pe_ref[0]
    top = (cm + pe[:, :half]).astype(BF16)
    bot = (cm + pe[:, half:]).astype(BF16)
    a = _dot(top, w1_ref[0, :half, :])
    b = _dot(bot, w1_ref[0, half:, :])
    b_up = jnp.concatenate([b[1:], jnp.zeros((1, CMP_HIDDEN), F32)], axis=0)
    pre = a + b_up
    hid = (pre * _sigmoid(pre)).astype(BF16)
    o_ref[0] = _dot(hid, w2_ref[0])


def _compress(cmat, pe, w1, w2):
    return pl.pallas_call(
        _compress_kernel,
        out_shape=jax.ShapeDtypeStruct((4, N_CMP_PAD, HEAD_DIM), F32),
        grid=(4,),
        in_specs=[pl.BlockSpec((1, N_CMP_PAD, 1024), lambda i: (i, 0, 0)),
                  pl.BlockSpec((1, 1, 2048), lambda i: (i // 2, 0, 0)),
                  pl.BlockSpec((1, 2048, CMP_HIDDEN), lambda i: (i // 2, 0, 0)),
                  pl.BlockSpec((1, CMP_HIDDEN, HEAD_DIM), lambda i: (i // 2, 0, 0))],
        out_specs=pl.BlockSpec((1, N_CMP_PAD, HEAD_DIM), lambda i: (i, 0, 0)),
        compiler_params=_params(("arbitrary",)),
        name="compress",
    )(cmat, pe, w1, w2)


def _tile_update(s, m, acc, vt):
    m_new = jnp.maximum(m, jnp.max(s, axis=0, keepdims=True))
    alpha = jnp.exp2(m - m_new)
    p = jnp.exp2(s - m_new).astype(BF16)
    return m_new, alpha * acc + _dot(vt, p)


def _finish(acc):
    return acc[:D_KV] / jnp.maximum(acc[D_KV:D_KV + 1], 1e-30)


def _attn_kernel(qt_ref, gt_ref, kc_ref, vct_ref, ovt_ref, ksl_ref, vslt_ref, kw_ref, vwt_ref,
                 ct_ref, wt_ref, o_ref, s_scr, p_scr, sel_scr, wa_scr, wb_scr, sa_scr, sb_scr):
    qb = pl.program_id(0)
    q0 = qb * TQ
    qt = qt_ref[...]
    gt = gt_ref[...]
    drow = lax.broadcasted_iota(jnp.int32, (D_KV, TQ), 0)
    krow = lax.broadcasted_iota(jnp.int32, (TK, 1), 0)
    m0 = jnp.full((1, COLS), M_FLOOR, F32)
    acc0 = jnp.zeros((VROWS, COLS), F32)

    def pad_mask(first_row):
        return jnp.where(krow + first_row >= KPAD, 0.0, NEG)

    comb = []
    for g in range(N_KV):
        keep = (drow >= HEAD_DIM * g) & (drow < HEAD_DIM * (g + 1))
        q_g = jnp.concatenate(
            [jnp.where(keep, qt[h * D_KV:(h + 1) * D_KV, :], jnp.zeros((), BF16))
             for h in range(HPG)], axis=1)

        mcol = m0
        for ct in range(N_CMP_PAD // TK):
            rows = slice(ct * TK, (ct + 1) * TK)
            r0 = pl.multiple_of(jnp.clip(TK * ct - (TQ // CMP_STRIDE) * qb + CT_ZERO,
                                         0, CT_ROWS - TK), SUBLANES)
            s = _dot(kc_ref[rows, :], q_g) + ct_ref[g, pl.ds(r0, TK), :]
            s_scr[rows, :] = s
            mcol = jnp.maximum(mcol, jnp.max(s, axis=0, keepdims=True))
        lsum = jnp.zeros((1, COLS), F32)
        acc_c = jnp.zeros((D_KV, COLS), F32)
        for ct in range(N_CMP_PAD // TK):
            rows = slice(ct * TK, (ct + 1) * TK)
            p = jnp.exp2(s_scr[rows, :] - mcol)
            p_scr[rows, :] = p
            lsum = lsum + jnp.sum(p, axis=0, keepdims=True)
            acc_c = acc_c + _dot(vct_ref[:, rows], p.astype(BF16))
        rinv = 1.0 / jnp.maximum(lsum, 1e-30)
        o_c = acc_c * rinv
        imp = jnp.zeros((N_SLC, TQ), F32)
        for ct in range(N_CMP_PAD // TK):
            rows = slice(ct * TK, (ct + 1) * TK)
            pn = p_scr[rows, :] * rinv
            ps = pn[:, 0:TQ] + pn[:, TQ:2 * TQ] + pn[:, 2 * TQ:3 * TQ] + pn[:, 3 * TQ:4 * TQ]
            hi = ps.astype(BF16)
            lo = (ps - hi.astype(F32)).astype(BF16)
            ov = ovt_ref[:, rows]
            imp = imp + _dot(ov, hi) + _dot(ov, lo)

        m, acc = m0, acc0
        for a in range(3):
            first = q0 + TK * a
            s = _dot(kw_ref[pl.ds(pl.multiple_of(first, TK), TK), :], q_g)
            s = s + (wt_ref[g, TK * a:TK * (a + 1), :] + pad_mask(first))
            m, acc = _tile_update(s, m, acc, vwt_ref[qb + a])
        o_w = _finish(acc)

        cur = jnp.right_shift(q0 + lax.broadcasted_iota(jnp.int32, (1, TQ), 1), 6)
        jcol = lax.broadcasted_iota(jnp.int32, (N_SLC, 1), 0)
        forced = (jcol == 0) | (jcol == cur) | (jcol == cur - 1)
        sc = jnp.where(forced, TAKEN, jnp.where(jcol <= cur, imp, -1.0))
        jf = lax.broadcasted_iota(jnp.int32, (N_SLC, TQ), 0).astype(F32)
        for _ in range(SLC_TOPK - N_FORCED):
            mx = jnp.max(sc, axis=0, keepdims=True)
            idx = jnp.min(jnp.where(sc == mx, jf, 1e9), axis=0, keepdims=True)
            sc = jnp.where(jf == idx, TAKEN, sc)
        selneg = jnp.where(sc == TAKEN, 0.0, NEG)
        unused = jnp.zeros((SUBLANES - BLK_PER_TILE, TQ), F32)
        for kt in range(SEQ // TK):
            sel_scr[kt] = jnp.concatenate(
                [selneg[kt * BLK_PER_TILE:(kt + 1) * BLK_PER_TILE, :], unused], axis=0)
        sel_scr[SEQ // TK] = jnp.concatenate(
            [jnp.full((BLK_PER_TILE, TQ), NEG, F32), unused], axis=0)

        zrows = jnp.zeros((2 * D_KV - D_KV - 2 * SUBLANES, COLS), BF16)
        for w_scr in (wa_scr, wb_scr):
            w_scr[0:D_KV, :] = q_g
            w_scr[D_KV + 2 * SUBLANES:, :] = zrows

        def sel_scores(w_scr, first_row, sel_idx):
            blk = sel_scr[sel_idx]
            rows8 = jnp.concatenate([blk] * HPG, axis=1).astype(BF16)
            w_scr[D_KV:D_KV + 2 * SUBLANES, :] = jnp.concatenate(
                [rows8, jnp.zeros((SUBLANES, COLS), BF16)], axis=0)
            return _dot(ksl_ref[pl.ds(pl.multiple_of(first_row, TK), TK), :], w_scr[...])

        m, acc = m0, acc0
        for a, w_scr in ((0, wa_scr), (1, wb_scr)):
            first = q0 + TQ + TK * a
            s = sel_scores(w_scr, first, jnp.maximum(qb - 1 + a, 0))
            s = s + (wt_ref[g, TQ + TK * a:TQ + TK * (a + 1), :] + pad_mask(first))
            m, acc = _tile_update(s, m, acc, vslt_ref[qb + 1 + a])

        n_far = jnp.maximum(qb - 1, 0)

        def far_scores(w_scr, kt):
            valid = kt < n_far
            ks = jnp.where(valid, kt, 0)
            return sel_scores(w_scr, KPAD + ks * TK, jnp.where(valid, kt, SEQ // TK))

        def far_v(kt):
            return vslt_ref[jnp.where(kt < n_far, kt, 0) + KPAD // TK]

        bufs = ((sa_scr, wa_scr), (sb_scr, wb_scr))
        sa_scr[...] = far_scores(wa_scr, 0)

        def far_body(i, carry):
            m_i, acc_i = carry
            for u in range(FAR_UNROLL):
                kt = i * FAR_UNROLL + u
                s_nxt, w_nxt = bufs[(u + 1) % 2]
                s_nxt[...] = far_scores(w_nxt, kt + 1)
                m_i, acc_i = _tile_update(bufs[u % 2][0][...], m_i, acc_i, far_v(kt))
            return m_i, acc_i

        m, acc = lax.fori_loop(0, (n_far + FAR_UNROLL - 1) // FAR_UNROLL, far_body, (m, acc))
        o_s = _finish(acc)

        per_head = []
        for h in range(HPG):
            c0 = 3 * (g * HPG + h)
            cs = slice(h * TQ, (h + 1) * TQ)
            per_head.append(gt[c0:c0 + 1, :] * o_c[:, cs] + gt[c0 + 1:c0 + 2, :] * o_s[:, cs]
                            + gt[c0 + 2:c0 + 3, :] * o_w[:, cs])
        comb.append(per_head)

    for h in range(HPG):
        merged = jnp.where(drow < HEAD_DIM, comb[0][h], comb[1][h])
        o_ref[0, h] = merged.T.astype(BF16)


def _attn(qt, gatet, kc, vct, ovt, ksl, vslt, kw, vwt, ctmpl, wtmpl):
    consts = [kc, vct, ovt, ksl, vslt, kw, vwt, ctmpl, wtmpl]
    return pl.pallas_call(
        _attn_kernel,
        out_shape=jax.ShapeDtypeStruct((SEQ // TQ, HPG, TQ, LANES), BF16),
        grid=(SEQ // TQ,),
        in_specs=[pl.BlockSpec((D_ATT, TQ), lambda i: (0, i)),
                  pl.BlockSpec((LANES, TQ), lambda i: (0, i))]
                 + [_const_spec(a.shape) for a in consts],
        out_specs=pl.BlockSpec((1, HPG, TQ, LANES), lambda i: (i, 0, 0, 0)),
        scratch_shapes=[pltpu.VMEM((N_CMP_PAD, COLS), F32), pltpu.VMEM((N_CMP_PAD, COLS), F32),
                        pltpu.VMEM((SEQ // TK + 1, SUBLANES, TQ), F32),
                        pltpu.VMEM((2 * D_KV, COLS), BF16), pltpu.VMEM((2 * D_KV, COLS), BF16),
                        pltpu.VMEM((TK, COLS), F32), pltpu.VMEM((TK, COLS), F32)],
        compiler_params=_params(("arbitrary",)),
        name="nsa_attn",
    )(qt, gatet, *consts)


def _t5_bucket(dist):
    max_exact = N_BUCKETS // 2
    d = jnp.maximum(dist, 0)
    df = jnp.maximum(d, 1).astype(F32)
    large = max_exact + (jnp.log(df / max_exact) / math.log(MAX_DIST / max_exact)
                         * (N_BUCKETS - max_exact)).astype(jnp.int32)
    large = jnp.minimum(large, N_BUCKETS - 1)
    return jnp.where(d < max_exact, d, large)


def _toeplitz_kernel(f_ref, o_ref):
    wk, span1 = o_ref.shape[1], f_ref.shape[1]
    for hd in range(N_HEADS):
        x = jnp.broadcast_to(f_ref[hd:hd + 1, :], (wk, span1))
        y = pltpu.roll(x, span1 - (wk - 1), axis=1, stride=1, stride_axis=0)
        o_ref[hd // HPG, :, (hd % HPG) * TQ:(hd % HPG + 1) * TQ] = y[:, :TQ]


def _toeplitz(fext):
    wk = WINDOW + TQ
    return pl.pallas_call(
        _toeplitz_kernel,
        out_shape=jax.ShapeDtypeStruct((N_KV, wk, COLS), F32),
        compiler_params=_params(None),
        name="toeplitz",
    )(fext)


def _to_group_cols(t):
    r = t.shape[1]
    return t.reshape(N_KV, HPG, r, TQ).transpose(0, 2, 1, 3).reshape(N_KV, r, COLS)


def _bias_templates(rel_bias):
    biasp = rel_bias[:, _t5_bucket(jnp.arange(MAX_DIST))] - rel_bias[:, N_BUCKETS - 1:]
    biasp = biasp * LOG2E

    wk = WINDOW + TQ
    span = wk + TQ - 1
    d = jnp.arange(span) - (wk - 1) + WINDOW
    f = jnp.where((d >= 0) & (d < WINDOW), biasp[:, jnp.clip(d, 0, MAX_DIST - 1)], NEG)
    fext = jnp.concatenate([f, jnp.zeros((N_HEADS, 1), F32)], axis=1)
    wtmpl = _toeplitz(fext)

    r = jnp.arange(-9, 15)[:, None]
    dc = jnp.arange(TQ)[None, :] - (CMP_LEN - 1) - CMP_STRIDE * r
    band = jnp.where(dc >= 0, biasp[:, jnp.clip(dc, 0, MAX_DIST - 1)], NEG)
    lo_rows = CT_ZERO - 9
    ctmpl = jnp.concatenate(
        [jnp.zeros((N_HEADS, lo_rows, TQ), F32), band,
         jnp.full((N_HEADS, CT_ROWS - lo_rows - band.shape[1], TQ), NEG, F32)], axis=1)
    return _to_group_cols(ctmpl).astype(F32), wtmpl.astype(F32)


def _overlap_t():
    cmp_start = jnp.arange(N_CMP_PAD) * CMP_STRIDE
    slc_start = jnp.arange(N_SLC) * SLC_LEN
    ov = ((cmp_start[None, :] < slc_start[:, None] + SLC_LEN)
          & (cmp_start[None, :] + CMP_LEN > slc_start[:, None])
          & (jnp.arange(N_CMP_PAD)[None, :] < N_CMP))
    return ov.astype(BF16)


def _with_block_indicator(k):
    blk = (jnp.arange(SEQ)[:, None] % TK) // SLC_LEN
    ind = (blk == jnp.arange(D_KV)[None, :]).astype(BF16)
    return jnp.concatenate([k, ind], axis=1)


def _vt_tiles(vt):
    ones = jnp.concatenate([jnp.ones((1, SEQ), BF16), jnp.zeros((VROWS - D_KV - 1, SEQ), BF16)])
    t = jnp.concatenate([vt, ones], axis=0).reshape(VROWS, SEQ // TK, TK).transpose(1, 0, 2)
    return jnp.pad(t, ((KPAD // TK, 0), (0, 0), (0, 0)))


def kernel(x, c, w_ada, b_ada, g_ffn1, w_gu1, w_down1, g_mix, w_in, w_dw, b_dw, ln_g, ln_b,
           pe_k, pe_v, w_ck1, w_ck2, w_cv1, w_cv2, rel_bias, w_out, g_ffn2, w_gu2, w_down2,
           g_final):
    assert x.shape == (1, SEQ, D_MODEL) and w_ada.shape[0] == 1
    x2 = x[0]
    mod = _ada(c.reshape(D_MODEL, 1), w_ada[0], b_ada)
    sh1, sc1, gt1, sh2, sc2, gt2, sh3, sc3, gt3 = [mod[:, k * D_MODEL:(k + 1) * D_MODEL]
                                                   for k in range(9)]

    x1 = _ffn1(x2, g_ffn1, sh1, sc1, gt1, w_gu1[0].astype(BF16), w_down1[0].astype(BF16))

    wi = w_in[0]
    wq = wi[:, 1024:1536].reshape(D_MODEL, N_KV, HPG, HEAD_DIM).transpose(0, 2, 1, 3)
    wq = wq.reshape(D_MODEL, D_ATT) * (HEAD_DIM ** -0.5 * LOG2E)
    w_in_p = jnp.concatenate(
        [wi[:, :1024], wq, wi[:, 1536:], jnp.zeros((D_MODEL, D_IN_PAD - wi.shape[1]), F32)],
        axis=1).astype(BF16)
    hc, kcvc, ksl, kw, qt, vslt, vwt, gatet = _proj(x1, g_mix, sh2, sc2, w_in_p)

    y_conv = _conv(hc, w_dw[0], b_dw, ln_g, ln_b)

    cmat = kcvc.reshape(SEQ, 4, HEAD_DIM).transpose(1, 0, 2).reshape(4, N_CMP_PAD, 1024)
    pe = jnp.stack([pe_k[0].reshape(1, -1), pe_v[0].reshape(1, -1)])
    w1 = jnp.stack([w_ck1[0], w_cv1[0]]).astype(BF16)
    w2 = jnp.stack([w_ck2[0], w_cv2[0]]).astype(BF16)
    cmp = _compress(cmat, pe, w1, w2)
    kc = jnp.concatenate([cmp[0], cmp[1]], axis=1).astype(BF16)
    vct = jnp.concatenate([cmp[2], cmp[3]], axis=1).T.astype(BF16)

    ctmpl, wtmpl = _bias_templates(rel_bias)
    front = ((KPAD, 0), (0, 0))
    y_att = _attn(qt, gatet, kc, vct, _overlap_t(), jnp.pad(_with_block_indicator(ksl), front),
                  _vt_tiles(vslt),
                  jnp.pad(kw, front), _vt_tiles(vwt), ctmpl, wtmpl)

    wo = w_out[0]
    woa = wo[D_CONV:].reshape(N_KV, HPG, HEAD_DIM, D_MODEL).transpose(1, 0, 2, 3)
    woa = woa.reshape(HPG, LANES, D_MODEL).astype(BF16)
    out = _ffn2(x1, y_conv, y_att, wo[:D_CONV].astype(BF16), woa, gt2,
                g_ffn2, sh3, sc3, gt3, w_gu2[0].astype(BF16), w_down2[0].astype(BF16),
                g_final.reshape(1, D_MODEL))
    return out[None]
```

```python
import math

import jax
import jax.numpy as jnp
from jax import lax
from jax.experimental import pallas as pl
from jax.experimental.pallas import tpu as pltpu

F32 = jnp.float32
BF16 = jnp.bfloat16

D_MODEL = 1024
SEQ = 16384
D_CONV = 512
CONV_WIDTH = 31
N_HEADS = 8
N_KV = 2
HPG = 4
HEAD_DIM = 64
D_ATT = 512
D_KV = 128
CMP_LEN = 32
CMP_STRIDE = 16
CMP_HIDDEN = 128
N_CMP = (SEQ - CMP_LEN) // CMP_STRIDE + 1
N_CMP_PAD = 1024
SLC_LEN = 64
N_SLC = SEQ // SLC_LEN
SLC_TOPK = 16
WINDOW = 512
N_FORCED = 3
TAKEN = -2.0
N_BUCKETS = 32
MAX_DIST = 128
D_FF = 2816
FFN_RES = 0.5
EPS = 1e-6
NEG = -1e30
M_FLOOR = -1e29
LOG2E = math.log2(math.e)

V7X_VMEM_BYTES = 64 * 1024 * 1024
VMEM_LIMIT = V7X_VMEM_BYTES - 6 * 1024 * 1024
LANES = 128
SUBLANES = 8

TQ = 256
TK = 256
COLS = HPG * TQ
KPAD = WINDOW
VROWS = HEAD_DIM + 16
BLK_PER_TILE = TK // SLC_LEN
FAR_UNROLL = 4
CT_ZERO = 520
CT_ROWS = 800
FFN_TM = 512
FFN_TF = D_FF
PROJ_TM = 512
CONV_TM = 256
CONV_HALO = 32
CONV_CHUNK = 32
D_IN_PAD = 2432


def _params(sem):
    return pltpu.CompilerParams(dimension_semantics=sem, vmem_limit_bytes=VMEM_LIMIT)


def _const_spec(shape):
    nd = len(shape)
    return pl.BlockSpec(shape, lambda *_: (0,) * nd, pipeline_mode=pl.Buffered(1))


def _sigmoid(v):
    return 1.0 / (1.0 + jnp.exp(-v))


def _dot(a, b):
    return jnp.dot(a, b, preferred_element_type=F32)


def _ada_kernel(c_ref, w_ref, b_ref, o_ref):
    c = c_ref[...]
    sc = c * _sigmoid(c)
    o_ref[...] = jnp.sum(w_ref[...] * sc, axis=0, keepdims=True) + b_ref[...]


def _ada(c_col, w, b):
    n = w.shape[1]
    tn = n // 8
    return pl.pallas_call(
        _ada_kernel,
        out_shape=jax.ShapeDtypeStruct((1, n), F32),
        grid=(8,),
        in_specs=[pl.BlockSpec((D_MODEL, 1), lambda j: (0, 0)),
                  pl.BlockSpec((D_MODEL, tn), lambda j: (0, j)),
                  pl.BlockSpec((1, tn), lambda j: (0, j))],
        out_specs=pl.BlockSpec((1, tn), lambda j: (0, j)),
        compiler_params=_params(("arbitrary",)),
        name="ada",
    )(c_col, w, b)


def _rms_mod(x, g, sh, sc):
    ms = jnp.mean(x * x, axis=-1, keepdims=True)
    y = x * lax.rsqrt(ms + EPS) * g
    return y * (1.0 + sc) + sh


def _ffn_body(x, g_ref, sh_ref, sc_ref, gt_ref, wgu_ref, wd_ref):
    hb = _rms_mod(x, g_ref[...], sh_ref[...], sc_ref[...]).astype(BF16)
    acc = jnp.zeros((x.shape[0], D_MODEL), F32)
    for f in range(D_FF // FFN_TF):
        gg = _dot(hb, wgu_ref[:, f * FFN_TF:(f + 1) * FFN_TF])
        uu = _dot(hb, wgu_ref[:, D_FF + f * FFN_TF:D_FF + (f + 1) * FFN_TF])
        a = (gg * _sigmoid(gg) * uu).astype(BF16)
        acc = acc + _dot(a, wd_ref[f * FFN_TF:(f + 1) * FFN_TF, :])
    return x + (FFN_RES * gt_ref[...]) * acc


def _ffn1_kernel(x_ref, g_ref, sh_ref, sc_ref, gt_ref, wgu_ref, wd_ref, o_ref):
    o_ref[...] = _ffn_body(x_ref[...], g_ref, sh_ref, sc_ref, gt_ref, wgu_ref, wd_ref)


def _ffn2_kernel(x_ref, yc_ref, ya_ref, woc_ref, woa_ref, gt2_ref,
                 g_ref, sh_ref, sc_ref, gt_ref, wgu_ref, wd_ref, gf_ref, o_ref):
    tm = x_ref.shape[0]
    y = _dot(yc_ref[...], woc_ref[...])
    for h in range(HPG):
        y = y + _dot(ya_ref[:, h].reshape(tm, LANES), woa_ref[h])
    x = x_ref[...] + gt2_ref[...] * y
    out = _ffn_body(x, g_ref, sh_ref, sc_ref, gt_ref, wgu_ref, wd_ref)
    ms = jnp.mean(out * out, axis=-1, keepdims=True)
    o_ref[...] = out * lax.rsqrt(ms + EPS) * gf_ref[...]


def _row_spec(tm, n):
    return pl.BlockSpec((tm, n), lambda i: (i, 0))


def _mod_spec(k):
    return pl.BlockSpec((1, D_MODEL), lambda *_: (0, k), pipeline_mode=pl.Buffered(1))


def _ffn1(x, g, mod, wgu, wd):
    vec = _const_spec((1, D_MODEL))
    return pl.pallas_call(
        _ffn1_kernel,
        out_shape=jax.ShapeDtypeStruct((SEQ, D_MODEL), F32),
        grid=(SEQ // FFN_TM,),
        in_specs=[_row_spec(FFN_TM, D_MODEL), vec, _mod_spec(0), _mod_spec(1), _mod_spec(2),
                  _const_spec((D_MODEL, 2 * D_FF)), _const_spec((D_FF, D_MODEL))],
        out_specs=_row_spec(FFN_TM, D_MODEL),
        compiler_params=_params(("arbitrary",)),
        name="ffn1",
    )(x, g, mod, mod, mod, wgu, wd)


def _ffn2(x, yc, ya, woc, woa, mod, g, wgu, wd, gf):
    vec = _const_spec((1, D_MODEL))
    nt = FFN_TM // TQ
    return pl.pallas_call(
        _ffn2_kernel,
        out_shape=jax.ShapeDtypeStruct((SEQ, D_MODEL), F32),
        grid=(SEQ // FFN_TM,),
        in_specs=[_row_spec(FFN_TM, D_MODEL), _row_spec(FFN_TM, D_CONV),
                  pl.BlockSpec((nt, HPG, TQ, LANES), lambda i: (i, 0, 0, 0)),
                  _const_spec((D_CONV, D_MODEL)), _const_spec((HPG, LANES, D_MODEL)), _mod_spec(5),
                  vec, _mod_spec(6), _mod_spec(7), _mod_spec(8),
                  _const_spec((D_MODEL, 2 * D_FF)), _const_spec((D_FF, D_MODEL)), vec],
        out_specs=_row_spec(FFN_TM, D_MODEL),
        compiler_params=_params(("arbitrary",)),
        name="ffn2",
    )(x, yc, ya, woc, woa, mod, g, mod, mod, mod, wgu, wd, gf)


def _proj_kernel(x_ref, g_ref, sh_ref, sc_ref, w_ref,
                 hc_ref, kcvc_ref, ksl_ref, kw_ref, qt_ref, vslt_ref, vwt_ref, gatet_ref):
    hb = _rms_mod(x_ref[...], g_ref[...], sh_ref[...], sc_ref[...]).astype(BF16)
    u = _dot(hb, w_ref[...])
    hc_ref[...] = u[:, 0:512] * _sigmoid(u[:, 512:1024])
    kcvc_ref[...] = u[:, 1536:1792]
    ksl_ref[...] = u[:, 1792:1920].astype(BF16)
    kw_ref[...] = u[:, 2048:2176].astype(BF16)
    qt_ref[...] = u[:, 1024:1536].T.astype(BF16)
    vslt_ref[...] = u[:, 1920:2048].T.astype(BF16)
    vwt_ref[...] = u[:, 2176:2304].T.astype(BF16)
    gatet_ref[...] = _sigmoid(u[:, 2304:2432]).T


def _proj(x, g, mod, w):
    vec = _const_spec((1, D_MODEL))
    tm = PROJ_TM
    rows = [(D_CONV, F32), (2 * D_KV, F32), (D_KV, BF16), (D_KV, BF16)]
    cols = [(D_ATT, BF16), (D_KV, BF16), (D_KV, BF16), (LANES, F32)]
    return pl.pallas_call(
        _proj_kernel,
        out_shape=[jax.ShapeDtypeStruct((SEQ, n), dt) for n, dt in rows]
                  + [jax.ShapeDtypeStruct((n, SEQ), dt) for n, dt in cols],
        grid=(SEQ // tm,),
        in_specs=[_row_spec(tm, D_MODEL), vec, _mod_spec(3), _mod_spec(4),
                  _const_spec((D_MODEL, D_IN_PAD))],
        out_specs=[_row_spec(tm, n) for n, _ in rows]
                  + [pl.BlockSpec((n, tm), lambda i: (0, i)) for n, _ in cols],
        compiler_params=_params(("arbitrary",)),
        name="proj",
    )(x, g, mod, mod, w)


def _conv_kernel(prev_ref, cur_ref, w_ref, b_ref, g_ref, bb_ref, o_ref, xs_ref):
    i = pl.program_id(0)
    tm = cur_ref.shape[0]
    prev = jnp.where(i > 0, prev_ref[...], 0.0)
    xx = jnp.concatenate([prev, cur_ref[...]], axis=0)
    span = tm + CONV_HALO - SUBLANES
    xs_ref[0] = xx
    for b in range(1, SUBLANES):
        xs_ref[b, 0:span, :] = xx[b:b + span, :]
    off = CONV_HALO - (CONV_WIDTH - 1)
    wts = w_ref[...]
    for r0 in range(0, tm, CONV_CHUNK):
        acc = jnp.zeros((CONV_CHUNK, D_CONV), F32) + b_ref[...]
        for w in range(CONV_WIDTH):
            a8, b = divmod(off + w, SUBLANES)
            lo = r0 + a8 * SUBLANES
            acc = acc + xs_ref[b, lo:lo + CONV_CHUNK, :] * wts[w:w + 1, :]
        mu = jnp.mean(acc, axis=-1, keepdims=True)
        dlt = acc - mu
        var = jnp.mean(dlt * dlt, axis=-1, keepdims=True)
        y = dlt * lax.rsqrt(var + EPS) * g_ref[...] + bb_ref[...]
        o_ref[r0:r0 + CONV_CHUNK, :] = (y * _sigmoid(y)).astype(BF16)


def _conv(hc, w, b, g, bb):
    tm = CONV_TM
    r = tm // CONV_HALO
    vec = _const_spec((1, D_CONV))
    return pl.pallas_call(
        _conv_kernel,
        out_shape=jax.ShapeDtypeStruct((SEQ, D_CONV), BF16),
        grid=(SEQ // tm,),
        in_specs=[pl.BlockSpec((CONV_HALO, D_CONV), lambda i: (jnp.maximum(i * r - 1, 0), 0)),
                  _row_spec(tm, D_CONV),
                  _const_spec((CONV_WIDTH, D_CONV)), vec, vec, vec],
        out_specs=_row_spec(tm, D_CONV),
        scratch_shapes=[pltpu.VMEM((SUBLANES, CONV_HALO + tm, D_CONV), F32)],
        compiler_params=_params(("arbitrary",)),
        name="conv",
    )(hc, hc, w, b, g, bb)


def _compress_kernel(c_ref, pe_ref, w1_ref, w2_ref, o_ref):
    half = CMP_STRIDE * HEAD_DIM
    cm = c_ref[0]
    pe = pe_ref[0]
    top = (cm + pe[:, :half]).astype(BF16)
    bot = (cm + pe[:, half:]).astype(BF16)
    a = _dot(top, w1_ref[0, :half, :])
    b = _dot(bot, w1_ref[0, half:, :])
    b_up = jnp.concatenate([b[1:], jnp.zeros((1, CMP_HIDDEN), F32)], axis=0)
    pre = a + b_up
    hid = (pre * _sigmoid(pre)).astype(BF16)
    o_ref[0] = _dot(hid, w2_ref[0])


def _compress(cmat, pe, w1, w2):
    return pl.pallas_call(
        _compress_kernel,
        out_shape=jax.ShapeDtypeStruct((4, N_CMP_PAD, HEAD_DIM), F32),
        grid=(4,),
        in_specs=[pl.BlockSpec((1, N_CMP_PAD, 1024), lambda i: (i, 0, 0)),
                  pl.BlockSpec((1, 1, 2048), lambda i: (i // 2, 0, 0)),
                  pl.BlockSpec((1, 2048, CMP_HIDDEN), lambda i: (i // 2, 0, 0)),
                  pl.BlockSpec((1, CMP_HIDDEN, HEAD_DIM), lambda i: (i // 2, 0, 0))],
        out_specs=pl.BlockSpec((1, N_CMP_PAD, HEAD_DIM), lambda i: (i, 0, 0)),
        compiler_params=_params(("arbitrary",)),
        name="compress",
    )(cmat, pe, w1, w2)


def _tile_update(s, m, acc, vt):
    m_new = jnp.maximum(m, jnp.max(s, axis=0, keepdims=True))
    alpha = jnp.exp2(m - m_new)
    p = jnp.exp2(s - m_new).astype(BF16)
    return m_new, alpha * acc + _dot(vt, p)


def _finish(acc):
    return acc[:HEAD_DIM] / jnp.maximum(acc[HEAD_DIM:HEAD_DIM + 1], 1e-30)


def _attn_kernel(qt_ref, gt_ref, kc_ref, vct_ref, ovt_ref, ksl_ref, vslt_ref, kw_ref, vwt_ref,
                 ct_ref, wt_ref, o_ref, s_scr, p_scr, sel_scr, wa_scr, wb_scr, sa_scr, sb_scr):
    qb = pl.program_id(0)
    q0 = qb * TQ
    qt = qt_ref[...]
    gt = gt_ref[...]
    drow = lax.broadcasted_iota(jnp.int32, (D_KV, TQ), 0)
    krow = lax.broadcasted_iota(jnp.int32, (TK, 1), 0)
    m0 = jnp.full((1, COLS), M_FLOOR, F32)
    acc0 = jnp.zeros((VROWS, COLS), F32)

    def pad_mask(first_row):
        return jnp.where(krow + first_row >= KPAD, 0.0, NEG)

    comb = []
    for g in range(N_KV):
        keep = (drow >= HEAD_DIM * g) & (drow < HEAD_DIM * (g + 1))
        q_g = jnp.concatenate(
            [jnp.where(keep, qt[h * D_KV:(h + 1) * D_KV, :], jnp.zeros((), BF16))
             for h in range(HPG)], axis=1)

        mcol = m0
        for ct in range(N_CMP_PAD // TK):
            rows = slice(ct * TK, (ct + 1) * TK)
            r0 = pl.multiple_of(jnp.clip(TK * ct - (TQ // CMP_STRIDE) * qb + CT_ZERO,
                                         0, CT_ROWS - TK), SUBLANES)
            s = _dot(kc_ref[rows, :], q_g) + ct_ref[g, pl.ds(r0, TK), :]
            s_scr[rows, :] = s
            mcol = jnp.maximum(mcol, jnp.max(s, axis=0, keepdims=True))
        lsum = jnp.zeros((1, COLS), F32)
        acc_c = jnp.zeros((HEAD_DIM, COLS), F32)
        for ct in range(N_CMP_PAD // TK):
            rows = slice(ct * TK, (ct + 1) * TK)
            p = jnp.exp2(s_scr[rows, :] - mcol)
            p_scr[rows, :] = p
            lsum = lsum + jnp.sum(p, axis=0, keepdims=True)
            acc_c = acc_c + _dot(vct_ref[g, :, rows], p.astype(BF16))
        rinv = 1.0 / jnp.maximum(lsum, 1e-30)
        o_c = acc_c * rinv
        imp = jnp.zeros((N_SLC, TQ), F32)
        for ct in range(N_CMP_PAD // TK):
            rows = slice(ct * TK, (ct + 1) * TK)
            pn = p_scr[rows, :] * rinv
            ps = pn[:, 0:TQ] + pn[:, TQ:2 * TQ] + pn[:, 2 * TQ:3 * TQ] + pn[:, 3 * TQ:4 * TQ]
            hi = ps.astype(BF16)
            lo = (ps - hi.astype(F32)).astype(BF16)
            ov = ovt_ref[:, rows]
            imp = imp + _dot(ov, hi) + _dot(ov, lo)

        m, acc = m0, acc0
        for a in range(3):
            first = q0 + TK * a
            s = _dot(kw_ref[pl.ds(pl.multiple_of(first, TK), TK), :], q_g)
            s = s + (wt_ref[g, TK * a:TK * (a + 1), :] + pad_mask(first))
            m, acc = _tile_update(s, m, acc, vwt_ref[g, qb + a])
        o_w = _finish(acc)

        cur = jnp.right_shift(q0 + lax.broadcasted_iota(jnp.int32, (1, TQ), 1), 6)
        jcol = lax.broadcasted_iota(jnp.int32, (N_SLC, 1), 0)
        forced = (jcol == 0) | (jcol == cur) | (jcol == cur - 1)
        sc = jnp.where(forced, TAKEN, jnp.where(jcol <= cur, imp, -1.0))
        jf = lax.broadcasted_iota(jnp.int32, (N_SLC, TQ), 0).astype(F32)
        for _ in range(SLC_TOPK - N_FORCED):
            mx = jnp.max(sc, axis=0, keepdims=True)
            idx = jnp.min(jnp.where(sc == mx, jf, 1e9), axis=0, keepdims=True)
            sc = jnp.where(jf == idx, TAKEN, sc)
        selneg = jnp.where(sc == TAKEN, 0.0, NEG)
        unused = jnp.zeros((SUBLANES - BLK_PER_TILE, TQ), F32)
        for kt in range(SEQ // TK):
            sel_scr[kt] = jnp.concatenate(
                [selneg[kt * BLK_PER_TILE:(kt + 1) * BLK_PER_TILE, :], unused], axis=0)
        sel_scr[SEQ // TK] = jnp.concatenate(
            [jnp.full((BLK_PER_TILE, TQ), NEG, F32), unused], axis=0)

        zrows = jnp.zeros((2 * D_KV - D_KV - 2 * SUBLANES, COLS), BF16)
        for w_scr in (wa_scr, wb_scr):
            w_scr[0:D_KV, :] = q_g
            w_scr[D_KV + 2 * SUBLANES:, :] = zrows

        def sel_scores(w_scr, first_row, sel_idx):
            blk = sel_scr[sel_idx]
            rows8 = jnp.concatenate([blk] * HPG, axis=1).astype(BF16)
            w_scr[D_KV:D_KV + 2 * SUBLANES, :] = jnp.concatenate(
                [rows8, jnp.zeros((SUBLANES, COLS), BF16)], axis=0)
            return _dot(ksl_ref[pl.ds(pl.multiple_of(first_row, TK), TK), :], w_scr[...])

        m, acc = m0, acc0
        for a, w_scr in ((0, wa_scr), (1, wb_scr)):
            first = q0 + TQ + TK * a
            s = sel_scores(w_scr, first, jnp.maximum(qb - 1 + a, 0))
            s = s + (wt_ref[g, TQ + TK * a:TQ + TK * (a + 1), :] + pad_mask(first))
            m, acc = _tile_update(s, m, acc, vslt_ref[g, qb + 1 + a])

        n_far = jnp.maximum(qb - 1, 0)

        def far_scores(w_scr, kt):
            valid = kt < n_far
            ks = jnp.where(valid, kt, 0)
            return sel_scores(w_scr, KPAD + ks * TK, jnp.where(valid, kt, SEQ // TK))

        def far_v(kt):
            return vslt_ref[g, jnp.where(kt < n_far, kt, 0) + KPAD // TK]

        bufs = ((sa_scr, wa_scr), (sb_scr, wb_scr))
        sa_scr[...] = far_scores(wa_scr, 0)

        def far_body(i, carry):
            m_i, acc_i = carry
            for u in range(FAR_UNROLL):
                kt = i * FAR_UNROLL + u
                s_nxt, w_nxt = bufs[(u + 1) % 2]
                s_nxt[...] = far_scores(w_nxt, kt + 1)
                m_i, acc_i = _tile_update(bufs[u % 2][0][...], m_i, acc_i, far_v(kt))
            return m_i, acc_i

        m, acc = lax.fori_loop(0, (n_far + FAR_UNROLL - 1) // FAR_UNROLL, far_body, (m, acc))
        o_s = _finish(acc)

        per_head = []
        for h in range(HPG):
            c0 = 3 * (g * HPG + h)
            cs = slice(h * TQ, (h + 1) * TQ)
            per_head.append(gt[c0:c0 + 1, :] * o_c[:, cs] + gt[c0 + 1:c0 + 2, :] * o_s[:, cs]
                            + gt[c0 + 2:c0 + 3, :] * o_w[:, cs])
        comb.append(per_head)

    for h in range(HPG):
        merged = jnp.concatenate([comb[0][h], comb[1][h]], axis=0)
        o_ref[0, h] = merged.T.astype(BF16)


def _attn(qt, gatet, kc, vct, ovt, ksl, vslt, kw, vwt, ctmpl, wtmpl):
    consts = [kc, vct, ovt, ksl, vslt, kw, vwt, ctmpl, wtmpl]
    return pl.pallas_call(
        _attn_kernel,
        out_shape=jax.ShapeDtypeStruct((SEQ // TQ, HPG, TQ, LANES), BF16),
        grid=(SEQ // TQ,),
        in_specs=[pl.BlockSpec((D_ATT, TQ), lambda i: (0, i)),
                  pl.BlockSpec((LANES, TQ), lambda i: (0, i))]
                 + [_const_spec(a.shape) for a in consts],
        out_specs=pl.BlockSpec((1, HPG, TQ, LANES), lambda i: (i, 0, 0, 0)),
        scratch_shapes=[pltpu.VMEM((N_CMP_PAD, COLS), F32), pltpu.VMEM((N_CMP_PAD, COLS), F32),
                        pltpu.VMEM((SEQ // TK + 1, SUBLANES, TQ), F32),
                        pltpu.VMEM((2 * D_KV, COLS), BF16), pltpu.VMEM((2 * D_KV, COLS), BF16),
                        pltpu.VMEM((TK, COLS), F32), pltpu.VMEM((TK, COLS), F32)],
        compiler_params=_params(("arbitrary",)),
        name="nsa_attn",
    )(qt, gatet, *consts)


def _t5_bucket(dist):
    max_exact = N_BUCKETS // 2
    d = jnp.maximum(dist, 0)
    df = jnp.maximum(d, 1).astype(F32)
    large = max_exact + (jnp.log(df / max_exact) / math.log(MAX_DIST / max_exact)
                         * (N_BUCKETS - max_exact)).astype(jnp.int32)
    large = jnp.minimum(large, N_BUCKETS - 1)
    return jnp.where(d < max_exact, d, large)


def _toeplitz_kernel(f_ref, o_ref):
    wk, span1 = o_ref.shape[1], f_ref.shape[1]
    for hd in range(N_HEADS):
        x = jnp.broadcast_to(f_ref[hd:hd + 1, :], (wk, span1))
        y = pltpu.roll(x, span1 - (wk - 1), axis=1, stride=1, stride_axis=0)
        o_ref[hd // HPG, :, (hd % HPG) * TQ:(hd % HPG + 1) * TQ] = y[:, :TQ]


def _toeplitz(fext):
    wk = WINDOW + TQ
    return pl.pallas_call(
        _toeplitz_kernel,
        out_shape=jax.ShapeDtypeStruct((N_KV, wk, COLS), F32),
        compiler_params=_params(None),
        name="toeplitz",
    )(fext)


def _to_group_cols(t):
    r = t.shape[1]
    return t.reshape(N_KV, HPG, r, TQ).transpose(0, 2, 1, 3).reshape(N_KV, r, COLS)


def _bias_templates(rel_bias):
    biasp = rel_bias[:, _t5_bucket(jnp.arange(MAX_DIST))] - rel_bias[:, N_BUCKETS - 1:]
    biasp = biasp * LOG2E

    def by_distance(n_neg, n_zero, n_neg_after):
        return jnp.concatenate(
            [jnp.full((N_HEADS, n_neg), NEG, F32), biasp, jnp.zeros((N_HEADS, n_zero), F32),
             jnp.full((N_HEADS, n_neg_after), NEG, F32)], axis=1)

    wk = WINDOW + TQ
    f = by_distance(TQ - 1, WINDOW - MAX_DIST, wk - WINDOW)
    fext = jnp.concatenate([f, jnp.zeros((N_HEADS, 1), F32)], axis=1)
    wtmpl = _toeplitz(fext)

    r_lo, r_hi = -9, 14
    d_min = -(CMP_LEN - 1) - CMP_STRIDE * r_hi
    d_max = TQ - 1 - (CMP_LEN - 1) - CMP_STRIDE * r_lo
    gx = by_distance(-d_min, d_max + 1 - MAX_DIST, 0)
    band = jnp.stack([gx[:, CMP_STRIDE * (r_hi - r):CMP_STRIDE * (r_hi - r) + TQ]
                      for r in range(r_lo, r_hi + 1)], axis=1)
    lo_rows = CT_ZERO + r_lo
    ctmpl = jnp.concatenate(
        [jnp.zeros((N_HEADS, lo_rows, TQ), F32), band,
         jnp.full((N_HEADS, CT_ROWS - lo_rows - band.shape[1], TQ), NEG, F32)], axis=1)
    return _to_group_cols(ctmpl).astype(F32), wtmpl.astype(F32)


def _overlap_t():
    cmp_start = jnp.arange(N_CMP_PAD) * CMP_STRIDE
    slc_start = jnp.arange(N_SLC) * SLC_LEN
    ov = ((cmp_start[None, :] < slc_start[:, None] + SLC_LEN)
          & (cmp_start[None, :] + CMP_LEN > slc_start[:, None])
          & (jnp.arange(N_CMP_PAD)[None, :] < N_CMP))
    return ov.astype(BF16)


def _with_block_indicator(k):
    blk = (jnp.arange(SEQ)[:, None] % TK) // SLC_LEN
    ind = (blk == jnp.arange(D_KV)[None, :]).astype(BF16)
    return jnp.concatenate([k, ind], axis=1)


def _vt_tiles(vt):
    ones = jnp.concatenate([jnp.ones((1, SEQ), BF16), jnp.zeros((VROWS - HEAD_DIM - 1, SEQ), BF16)])
    t = jnp.stack([jnp.concatenate([vt[g * HEAD_DIM:(g + 1) * HEAD_DIM], ones], axis=0)
                   for g in range(N_KV)])
    t = t.reshape(N_KV, VROWS, SEQ // TK, TK).transpose(0, 2, 1, 3)
    return jnp.pad(t, ((0, 0), (KPAD // TK, 0), (0, 0), (0, 0)))


def kernel(x, c, w_ada, b_ada, g_ffn1, w_gu1, w_down1, g_mix, w_in, w_dw, b_dw, ln_g, ln_b,
           pe_k, pe_v, w_ck1, w_ck2, w_cv1, w_cv2, rel_bias, w_out, g_ffn2, w_gu2, w_down2,
           g_final):
    assert x.shape == (1, SEQ, D_MODEL) and w_ada.shape[0] == 1
    x2 = x[0]
    mod = _ada(c.reshape(D_MODEL, 1), w_ada[0], b_ada)
    x1 = _ffn1(x2, g_ffn1, mod, w_gu1[0].astype(BF16), w_down1[0].astype(BF16))

    wi = w_in[0]
    wq = wi[:, 1024:1536].reshape(D_MODEL, N_KV, HPG, HEAD_DIM).transpose(0, 2, 1, 3)
    wq = wq.reshape(D_MODEL, D_ATT) * (HEAD_DIM ** -0.5 * LOG2E)
    w_in_p = jnp.concatenate(
        [wi[:, :1024], wq, wi[:, 1536:], jnp.zeros((D_MODEL, D_IN_PAD - wi.shape[1]), F32)],
        axis=1).astype(BF16)
    hc, kcvc, ksl, kw, qt, vslt, vwt, gatet = _proj(x1, g_mix, mod, w_in_p)

    y_conv = _conv(hc, w_dw[0], b_dw, ln_g, ln_b)

    cmat = kcvc.reshape(SEQ, 4, HEAD_DIM).transpose(1, 0, 2).reshape(4, N_CMP_PAD, 1024)
    pe = jnp.stack([pe_k[0].reshape(1, -1), pe_v[0].reshape(1, -1)])
    w1 = jnp.stack([w_ck1[0], w_cv1[0]]).astype(BF16)
    w2 = jnp.stack([w_ck2[0], w_cv2[0]]).astype(BF16)
    cmp = _compress(cmat, pe, w1, w2)
    kc = jnp.concatenate([cmp[0], cmp[1]], axis=1).astype(BF16)
    vct = jnp.swapaxes(cmp[2:4], 1, 2).astype(BF16)

    ctmpl, wtmpl = _bias_templates(rel_bias)
    front = ((KPAD, 0), (0, 0))
    y_att = _attn(qt, gatet, kc, vct, _overlap_t(), jnp.pad(_with_block_indicator(ksl), front),
                  _vt_tiles(vslt),
                  jnp.pad(kw, front), _vt_tiles(vwt), ctmpl, wtmpl)

    wo = w_out[0]
    woa = wo[D_CONV:].reshape(N_KV, HPG, HEAD_DIM, D_MODEL).transpose(1, 0, 2, 3)
    woa = woa.reshape(HPG, LANES, D_MODEL).astype(BF16)
    out = _ffn2(x1, y_conv, y_att, wo[:D_CONV].astype(BF16), woa, mod,
                g_ffn2, w_gu2[0].astype(BF16), w_down2[0].astype(BF16),
                g_final.reshape(1, D_MODEL))
    return out[None]
```

```python
import math

import jax
import jax.numpy as jnp
from jax import lax
from jax.experimental import pallas as pl
from jax.experimental.pallas import tpu as pltpu

F32 = jnp.float32
BF16 = jnp.bfloat16

D_MODEL = 1024
SEQ = 16384
D_CONV = 512
CONV_WIDTH = 31
N_HEADS = 8
N_KV = 2
HPG = 4
HEAD_DIM = 64
D_ATT = 512
D_KV = 128
CMP_LEN = 32
CMP_STRIDE = 16
CMP_HIDDEN = 128
N_CMP = (SEQ - CMP_LEN) // CMP_STRIDE + 1
N_CMP_PAD = 1024
SLC_LEN = 64
N_SLC = SEQ // SLC_LEN
SLC_TOPK = 16
WINDOW = 512
N_FORCED = 3
TAKEN = -2.0
N_BUCKETS = 32
MAX_DIST = 128
D_FF = 2816
FFN_RES = 0.5
EPS = 1e-6
NEG = -1e30
M_FLOOR = -1e29
LOG2E = math.log2(math.e)

V7X_VMEM_BYTES = 64 * 1024 * 1024
VMEM_LIMIT = V7X_VMEM_BYTES - 6 * 1024 * 1024
LANES = 128
SUBLANES = 8

TQ = 256
TK = 256
COLS = HPG * TQ
KPAD = WINDOW
VROWS = HEAD_DIM + 16
BLK_PER_TILE = TK // SLC_LEN
FAR_UNROLL = 4
SEL_ENTRIES = SEQ // TK + 2
CT_ZERO = 520
CT_ROWS = 800
CMP_R_LO, CMP_R_HI = -9, 14
FFN_TM = 512
FFN_TF = D_FF
PROJ_TM = 512
CONV_TM = 256
CONV_HALO = 32
CONV_CHUNK = 32
D_IN_PAD = 2432


def _params(sem):
    return pltpu.CompilerParams(dimension_semantics=sem, vmem_limit_bytes=VMEM_LIMIT)


def _const_spec(shape):
    nd = len(shape)
    return pl.BlockSpec(shape, lambda *_: (0,) * nd, pipeline_mode=pl.Buffered(1))


def _sigmoid(v):
    return 1.0 / (1.0 + jnp.exp(-v))


def _dot(a, b):
    return jnp.dot(a, b, preferred_element_type=F32)


def _ada_kernel(c_ref, w_ref, b_ref, o_ref):
    c = c_ref[...]
    sc = c * _sigmoid(c)
    o_ref[...] = jnp.sum(w_ref[...] * sc, axis=0, keepdims=True) + b_ref[...]


def _ada(c_col, w, b):
    n = w.shape[1]
    tn = n // 8
    return pl.pallas_call(
        _ada_kernel,
        out_shape=jax.ShapeDtypeStruct((1, n), F32),
        grid=(8,),
        in_specs=[pl.BlockSpec((D_MODEL, 1), lambda j: (0, 0)),
                  pl.BlockSpec((D_MODEL, tn), lambda j: (0, j)),
                  pl.BlockSpec((1, tn), lambda j: (0, j))],
        out_specs=pl.BlockSpec((1, tn), lambda j: (0, j)),
        compiler_params=_params(("arbitrary",)),
        name="ada",
    )(c_col, w, b)


def _rms_mod(x, g, sh, sc):
    ms = jnp.mean(x * x, axis=-1, keepdims=True)
    y = x * lax.rsqrt(ms + EPS) * g
    return y * (1.0 + sc) + sh


def _ffn_body(x, g_ref, sh_ref, sc_ref, gt_ref, wgu_ref, wd_ref):
    hb = _rms_mod(x, g_ref[...], sh_ref[...], sc_ref[...]).astype(BF16)
    acc = jnp.zeros((x.shape[0], D_MODEL), F32)
    for f in range(D_FF // FFN_TF):
        gg = _dot(hb, wgu_ref[:, f * FFN_TF:(f + 1) * FFN_TF])
        uu = _dot(hb, wgu_ref[:, D_FF + f * FFN_TF:D_FF + (f + 1) * FFN_TF])
        a = (gg * _sigmoid(gg) * uu).astype(BF16)
        acc = acc + _dot(a, wd_ref[f * FFN_TF:(f + 1) * FFN_TF, :])
    return x + (FFN_RES * gt_ref[...]) * acc


def _ffn1_kernel(x_ref, g_ref, sh_ref, sc_ref, gt_ref, wgu_ref, wd_ref, o_ref):
    o_ref[...] = _ffn_body(x_ref[...], g_ref, sh_ref, sc_ref, gt_ref, wgu_ref, wd_ref)


def _ffn2_kernel(x_ref, yc_ref, ya_ref, woc_ref, woa_ref, gt2_ref,
                 g_ref, sh_ref, sc_ref, gt_ref, wgu_ref, wd_ref, gf_ref, o_ref):
    tm = x_ref.shape[0]
    y = _dot(yc_ref[...], woc_ref[...])
    for h in range(HPG):
        y = y + _dot(ya_ref[:, h].reshape(tm, LANES), woa_ref[h])
    x = x_ref[...] + gt2_ref[...] * y
    out = _ffn_body(x, g_ref, sh_ref, sc_ref, gt_ref, wgu_ref, wd_ref)
    ms = jnp.mean(out * out, axis=-1, keepdims=True)
    o_ref[...] = out * lax.rsqrt(ms + EPS) * gf_ref[...]


def _row_spec(tm, n):
    return pl.BlockSpec((tm, n), lambda i: (i, 0))


def _mod_spec(k):
    return pl.BlockSpec((1, D_MODEL), lambda *_: (0, k), pipeline_mode=pl.Buffered(1))


def _ffn1(x, g, mod, wgu, wd):
    vec = _const_spec((1, D_MODEL))
    return pl.pallas_call(
        _ffn1_kernel,
        out_shape=jax.ShapeDtypeStruct((SEQ, D_MODEL), F32),
        grid=(SEQ // FFN_TM,),
        in_specs=[_row_spec(FFN_TM, D_MODEL), vec, _mod_spec(0), _mod_spec(1), _mod_spec(2),
                  _const_spec((D_MODEL, 2 * D_FF)), _const_spec((D_FF, D_MODEL))],
        out_specs=_row_spec(FFN_TM, D_MODEL),
        compiler_params=_params(("arbitrary",)),
        name="ffn1",
    )(x, g, mod, mod, mod, wgu, wd)


def _ffn2(x, yc, ya, woc, woa, mod, g, wgu, wd, gf):
    vec = _const_spec((1, D_MODEL))
    nt = FFN_TM // TQ
    return pl.pallas_call(
        _ffn2_kernel,
        out_shape=jax.ShapeDtypeStruct((SEQ, D_MODEL), F32),
        grid=(SEQ // FFN_TM,),
        in_specs=[_row_spec(FFN_TM, D_MODEL), _row_spec(FFN_TM, D_CONV),
                  pl.BlockSpec((nt, HPG, TQ, LANES), lambda i: (i, 0, 0, 0)),
                  _const_spec((D_CONV, D_MODEL)), _const_spec((HPG, LANES, D_MODEL)), _mod_spec(5),
                  vec, _mod_spec(6), _mod_spec(7), _mod_spec(8),
                  _const_spec((D_MODEL, 2 * D_FF)), _const_spec((D_FF, D_MODEL)), vec],
        out_specs=_row_spec(FFN_TM, D_MODEL),
        compiler_params=_params(("arbitrary",)),
        name="ffn2",
    )(x, yc, ya, woc, woa, mod, g, mod, mod, mod, wgu, wd, gf)


def _proj_kernel(x_ref, g_ref, sh_ref, sc_ref, w_ref,
                 hc_ref, kcvc_ref, ksl_ref, kw_ref, qt_ref, vslt_ref, vwt_ref, gatet_ref):
    hb = _rms_mod(x_ref[...], g_ref[...], sh_ref[...], sc_ref[...]).astype(BF16)
    u = _dot(hb, w_ref[...])
    hc_ref[...] = u[:, 0:512] * _sigmoid(u[:, 512:1024])
    kcvc_ref[...] = u[:, 1536:1792]
    tm = x_ref.shape[0]
    blk = (lax.broadcasted_iota(jnp.int32, (tm, D_KV), 0) % TK) // SLC_LEN
    ind = (blk == lax.broadcasted_iota(jnp.int32, (tm, D_KV), 1)).astype(BF16)
    ksl_ref[...] = jnp.concatenate([u[:, 1792:1920].astype(BF16), ind], axis=1)
    kw_ref[...] = u[:, 2048:2176].astype(BF16)
    qt_ref[...] = u[:, 1024:1536].T.astype(BF16)
    vslt_ref[...] = u[:, 1920:2048].T.astype(BF16)
    vwt_ref[...] = u[:, 2176:2304].T.astype(BF16)
    gatet_ref[...] = _sigmoid(u[:, 2304:2432]).T


def _proj(x, g, mod, w):
    vec = _const_spec((1, D_MODEL))
    tm = PROJ_TM
    rows = [(D_CONV, F32), (2 * D_KV, F32), (2 * D_KV, BF16), (D_KV, BF16)]
    cols = [(D_ATT, BF16), (D_KV, BF16), (D_KV, BF16), (LANES, F32)]
    return pl.pallas_call(
        _proj_kernel,
        out_shape=[jax.ShapeDtypeStruct((SEQ, n), dt) for n, dt in rows]
                  + [jax.ShapeDtypeStruct((n, SEQ), dt) for n, dt in cols],
        grid=(SEQ // tm,),
        in_specs=[_row_spec(tm, D_MODEL), vec, _mod_spec(3), _mod_spec(4),
                  _const_spec((D_MODEL, D_IN_PAD))],
        out_specs=[_row_spec(tm, n) for n, _ in rows]
                  + [pl.BlockSpec((n, tm), lambda i: (0, i)) for n, _ in cols],
        compiler_params=_params(("arbitrary",)),
        name="proj",
    )(x, g, mod, mod, w)


def _conv_kernel(prev_ref, cur_ref, w_ref, b_ref, g_ref, bb_ref, o_ref, xs_ref):
    i = pl.program_id(0)
    tm = cur_ref.shape[0]
    prev = jnp.where(i > 0, prev_ref[...], 0.0)
    xx = jnp.concatenate([prev, cur_ref[...]], axis=0)
    span = tm + CONV_HALO - SUBLANES
    xs_ref[0] = xx
    for b in range(1, SUBLANES):
        xs_ref[b, 0:span, :] = xx[b:b + span, :]
    off = CONV_HALO - (CONV_WIDTH - 1)
    wts = w_ref[...]
    for r0 in range(0, tm, CONV_CHUNK):
        acc = jnp.zeros((CONV_CHUNK, D_CONV), F32) + b_ref[...]
        for w in range(CONV_WIDTH):
            a8, b = divmod(off + w, SUBLANES)
            lo = r0 + a8 * SUBLANES
            acc = acc + xs_ref[b, lo:lo + CONV_CHUNK, :] * wts[w:w + 1, :]
        mu = jnp.mean(acc, axis=-1, keepdims=True)
        dlt = acc - mu
        var = jnp.mean(dlt * dlt, axis=-1, keepdims=True)
        y = dlt * lax.rsqrt(var + EPS) * g_ref[...] + bb_ref[...]
        o_ref[r0:r0 + CONV_CHUNK, :] = (y * _sigmoid(y)).astype(BF16)


def _conv(hc, w, b, g, bb):
    tm = CONV_TM
    r = tm // CONV_HALO
    vec = _const_spec((1, D_CONV))
    return pl.pallas_call(
        _conv_kernel,
        out_shape=jax.ShapeDtypeStruct((SEQ, D_CONV), BF16),
        grid=(SEQ // tm,),
        in_specs=[pl.BlockSpec((CONV_HALO, D_CONV), lambda i: (jnp.maximum(i * r - 1, 0), 0)),
                  _row_spec(tm, D_CONV),
                  _const_spec((CONV_WIDTH, D_CONV)), vec, vec, vec],
        out_specs=_row_spec(tm, D_CONV),
        scratch_shapes=[pltpu.VMEM((SUBLANES, CONV_HALO + tm, D_CONV), F32)],
        compiler_params=_params(("arbitrary",)),
        name="conv",
    )(hc, hc, w, b, g, bb)


def _compress_kernel(c_ref, pe_ref, w1_ref, w2_ref, o_ref):
    half = CMP_STRIDE * HEAD_DIM
    cm = c_ref[0]
    pe = pe_ref[0]
    top = (cm + pe[:, :half]).astype(BF16)
    bot = (cm + pe[:, half:]).astype(BF16)
    a = _dot(top, w1_ref[0, :half, :])
    b = _dot(bot, w1_ref[0, half:, :])
    b_up = jnp.concatenate([b[1:], jnp.zeros((1, CMP_HIDDEN), F32)], axis=0)
    pre = a + b_up
    hid = (pre * _sigmoid(pre)).astype(BF16)
    o_ref[0] = _dot(hid, w2_ref[0])


def _compress(cmat, pe, w1, w2):
    return pl.pallas_call(
        _compress_kernel,
        out_shape=jax.ShapeDtypeStruct((4, N_CMP_PAD, HEAD_DIM), F32),
        grid=(4,),
        in_specs=[pl.BlockSpec((1, N_CMP_PAD, 1024), lambda i: (i, 0, 0)),
                  pl.BlockSpec((1, 1, 2048), lambda i: (i // 2, 0, 0)),
                  pl.BlockSpec((1, 2048, CMP_HIDDEN), lambda i: (i // 2, 0, 0)),
                  pl.BlockSpec((1, CMP_HIDDEN, HEAD_DIM), lambda i: (i // 2, 0, 0))],
        out_specs=pl.BlockSpec((1, N_CMP_PAD, HEAD_DIM), lambda i: (i, 0, 0)),
        compiler_params=_params(("arbitrary",)),
        name="compress",
    )(cmat, pe, w1, w2)


def _tile_update(s, m, acc, vt):
    m_new = jnp.maximum(m, jnp.max(s, axis=0, keepdims=True))
    alpha = jnp.exp2(m - m_new)
    p = jnp.exp2(s - m_new).astype(BF16)
    return m_new, alpha * acc + _dot(vt, p)


def _finish(acc):
    return acc[:HEAD_DIM] / jnp.maximum(acc[HEAD_DIM:HEAD_DIM + 1], 1e-30)


def _attn_kernel(qt_ref, gt_ref, kc_ref, vct_ref, ovt_ref, ksl_ref, vslt_ref, kw_ref, vwt_ref,
                 ct_ref, wt_ref, o_ref, s_scr, p_scr, sel_scr, wa_scr, wb_scr, sa_scr, sb_scr):
    qb = pl.program_id(0)
    q0 = qb * TQ
    qt = qt_ref[...]
    gt = gt_ref[...]
    drow = lax.broadcasted_iota(jnp.int32, (D_KV, TQ), 0)
    krow = lax.broadcasted_iota(jnp.int32, (TK, 1), 0)
    m0 = jnp.full((1, COLS), M_FLOOR, F32)
    acc0 = jnp.zeros((VROWS, COLS), F32)

    def pad_mask(first_row):
        return jnp.where(krow + first_row >= KPAD, 0.0, NEG)

    comb = []
    for g in range(N_KV):
        keep = (drow >= HEAD_DIM * g) & (drow < HEAD_DIM * (g + 1))
        q_g = jnp.concatenate(
            [jnp.where(keep, qt[h * D_KV:(h + 1) * D_KV, :], jnp.zeros((), BF16))
             for h in range(HPG)], axis=1)

        mcol = m0
        for ct in range(N_CMP_PAD // TK):
            rows = slice(ct * TK, (ct + 1) * TK)
            r0 = pl.multiple_of(jnp.clip(TK * ct - (TQ // CMP_STRIDE) * qb + CT_ZERO,
                                         0, CT_ROWS - TK), SUBLANES)
            s = _dot(kc_ref[rows, :], q_g) + ct_ref[g, pl.ds(r0, TK), :]
            s_scr[rows, :] = s
            mcol = jnp.maximum(mcol, jnp.max(s, axis=0, keepdims=True))
        lsum = jnp.zeros((1, COLS), F32)
        acc_c = jnp.zeros((HEAD_DIM, COLS), F32)
        for ct in range(N_CMP_PAD // TK):
            rows = slice(ct * TK, (ct + 1) * TK)
            p = jnp.exp2(s_scr[rows, :] - mcol)
            p_scr[rows, :] = p
            lsum = lsum + jnp.sum(p, axis=0, keepdims=True)
            acc_c = acc_c + _dot(vct_ref[g, :, rows], p.astype(BF16))
        rinv = 1.0 / jnp.maximum(lsum, 1e-30)
        o_c = acc_c * rinv
        imp = jnp.zeros((N_SLC, TQ), F32)
        for ct in range(N_CMP_PAD // TK):
            rows = slice(ct * TK, (ct + 1) * TK)
            pn = p_scr[rows, :] * rinv
            ps = pn[:, 0:TQ] + pn[:, TQ:2 * TQ] + pn[:, 2 * TQ:3 * TQ] + pn[:, 3 * TQ:4 * TQ]
            hi = ps.astype(BF16)
            lo = (ps - hi.astype(F32)).astype(BF16)
            ov = ovt_ref[:, rows]
            imp = imp + _dot(ov, hi) + _dot(ov, lo)

        m, acc = m0, acc0
        for a in range(3):
            first = q0 + TK * a
            s = _dot(kw_ref[pl.ds(pl.multiple_of(first, TK), TK), :], q_g)
            s = s + (wt_ref[g, TK * a:TK * (a + 1), :] + pad_mask(first))
            m, acc = _tile_update(s, m, acc, vwt_ref[g, qb + a])
        o_w = _finish(acc)

        cur = jnp.right_shift(q0 + lax.broadcasted_iota(jnp.int32, (1, TQ), 1), 6)
        jcol = lax.broadcasted_iota(jnp.int32, (N_SLC, 1), 0)
        forced = (jcol == 0) | (jcol == cur) | (jcol == cur - 1)
        sc = jnp.where(forced, TAKEN, jnp.where(jcol <= cur, imp, -1.0))
        jf = lax.broadcasted_iota(jnp.int32, (N_SLC, TQ), 0).astype(F32)
        for _ in range(SLC_TOPK - N_FORCED):
            mx = jnp.max(sc, axis=0, keepdims=True)
            idx = jnp.min(jnp.where(sc == mx, jf, 1e9), axis=0, keepdims=True)
            sc = jnp.where(jf == idx, TAKEN, sc)
        selneg = jnp.where(sc == TAKEN, 0.0, NEG)
        unused = jnp.zeros((SUBLANES - BLK_PER_TILE, TQ), F32)
        for kt in range(SEQ // TK):
            sel_scr[kt] = jnp.concatenate(
                [selneg[kt * BLK_PER_TILE:(kt + 1) * BLK_PER_TILE, :], unused], axis=0)
        sel_scr[SEQ // TK] = jnp.concatenate(
            [jnp.full((BLK_PER_TILE, TQ), NEG, F32), unused], axis=0)

        zrows = jnp.zeros((2 * D_KV - D_KV - 2 * SUBLANES, COLS), BF16)
        for w_scr in (wa_scr, wb_scr):
            w_scr[0:D_KV, :] = q_g
            w_scr[D_KV + 2 * SUBLANES:, :] = zrows

        def sel_scores(w_scr, first_row, sel_idx):
            blk = sel_scr[sel_idx]
            rows8 = jnp.concatenate([blk] * HPG, axis=1).astype(BF16)
            w_scr[D_KV:D_KV + 2 * SUBLANES, :] = jnp.concatenate(
                [rows8, jnp.zeros((SUBLANES, COLS), BF16)], axis=0)
            return _dot(ksl_ref[pl.ds(pl.multiple_of(first_row, TK), TK), :], w_scr[...])

        m, acc = m0, acc0
        for a, w_scr in ((0, wa_scr), (1, wb_scr)):
            first = q0 + TQ + TK * a
            s = sel_scores(w_scr, first, jnp.maximum(qb - 1 + a, 0))
            s = s + (wt_ref[g, TQ + TK * a:TQ + TK * (a + 1), :] + pad_mask(first))
            m, acc = _tile_update(s, m, acc, vslt_ref[g, qb + 1 + a])

        n_far = jnp.maximum(qb - 1, 0)

        def far_scores(w_scr, kt):
            valid = kt < n_far
            ks = jnp.where(valid, kt, 0)
            return sel_scores(w_scr, KPAD + ks * TK, jnp.where(valid, kt, SEQ // TK))

        def far_v(kt):
            return vslt_ref[g, jnp.where(kt < n_far, kt, 0) + KPAD // TK]

        bufs = ((sa_scr, wa_scr), (sb_scr, wb_scr))
        sa_scr[...] = far_scores(wa_scr, 0)

        def far_body(i, carry):
            m_i, acc_i = carry
            for u in range(FAR_UNROLL):
                kt = i * FAR_UNROLL + u
                s_nxt, w_nxt = bufs[(u + 1) % 2]
                s_nxt[...] = far_scores(w_nxt, kt + 1)
                m_i, acc_i = _tile_update(bufs[u % 2][0][...], m_i, acc_i, far_v(kt))
            return m_i, acc_i

        m, acc = lax.fori_loop(0, (n_far + FAR_UNROLL - 1) // FAR_UNROLL, far_body, (m, acc))
        o_s = _finish(acc)

        per_head = []
        for h in range(HPG):
            c0 = 3 * (g * HPG + h)
            cs = slice(h * TQ, (h + 1) * TQ)
            per_head.append(gt[c0:c0 + 1, :] * o_c[:, cs] + gt[c0 + 1:c0 + 2, :] * o_s[:, cs]
                            + gt[c0 + 2:c0 + 3, :] * o_w[:, cs])
        comb.append(per_head)

    for h in range(HPG):
        merged = jnp.concatenate([comb[0][h], comb[1][h]], axis=0)
        o_ref[0, h] = merged.T.astype(BF16)


def _attn(qt, gatet, kc, vct, ovt, ksl, vslt, kw, vwt, ctmpl, wtmpl):
    consts = [kc, vct, ovt, ksl, vslt, kw, vwt, ctmpl, wtmpl]
    return pl.pallas_call(
        _attn_kernel,
        out_shape=jax.ShapeDtypeStruct((SEQ // TQ, HPG, TQ, LANES), BF16),
        grid=(SEQ // TQ,),
        in_specs=[pl.BlockSpec((D_ATT, TQ), lambda i: (0, i)),
                  pl.BlockSpec((LANES, TQ), lambda i: (0, i))]
                 + [_const_spec(a.shape) for a in consts],
        out_specs=pl.BlockSpec((1, HPG, TQ, LANES), lambda i: (i, 0, 0, 0)),
        scratch_shapes=[pltpu.VMEM((N_CMP_PAD, COLS), F32), pltpu.VMEM((N_CMP_PAD, COLS), F32),
                        pltpu.VMEM((SEL_ENTRIES, SUBLANES, TQ), F32),
                        pltpu.VMEM((2 * D_KV, COLS), BF16), pltpu.VMEM((2 * D_KV, COLS), BF16),
                        pltpu.VMEM((TK, COLS), F32), pltpu.VMEM((TK, COLS), F32)],
        compiler_params=_params(("arbitrary",)),
        name="nsa_attn",
    )(qt, gatet, *consts)


def _t5_bucket(dist):
    max_exact = N_BUCKETS // 2
    d = jnp.maximum(dist, 0)
    df = jnp.maximum(d, 1).astype(F32)
    large = max_exact + (jnp.log(df / max_exact) / math.log(MAX_DIST / max_exact)
                         * (N_BUCKETS - max_exact)).astype(jnp.int32)
    large = jnp.minimum(large, N_BUCKETS - 1)
    return jnp.where(d < max_exact, d, large)


def _template_kernel(f_ref, gx_ref, wt_ref, ct_ref):
    wk, lanes = wt_ref.shape[1], f_ref.shape[1]
    n_band = CMP_R_HI - CMP_R_LO + 1
    lo = CT_ZERO + CMP_R_LO
    for hd in range(N_HEADS):
        g, cols = hd // HPG, slice((hd % HPG) * TQ, (hd % HPG + 1) * TQ)
        x = jnp.broadcast_to(f_ref[hd:hd + 1, :], (wk, lanes))
        y = pltpu.roll(x, lanes - (wk - 1), axis=1, stride=1, stride_axis=0)
        wt_ref[g, :, cols] = y[:, :TQ]
        xb = jnp.broadcast_to(gx_ref[hd:hd + 1, :], (n_band, lanes))
        yb = pltpu.roll(xb, lanes - CMP_STRIDE * (n_band - 1), axis=1,
                        stride=CMP_STRIDE, stride_axis=0)
        ct_ref[g, 0:lo, cols] = jnp.zeros((lo, TQ), F32)
        ct_ref[g, lo:lo + n_band, cols] = yb[:, :TQ]
        ct_ref[g, lo + n_band:, cols] = jnp.full((CT_ROWS - lo - n_band, TQ), NEG, F32)


def _templates(fext, gxext):
    return pl.pallas_call(
        _template_kernel,
        out_shape=[jax.ShapeDtypeStruct((N_KV, WINDOW + TQ, COLS), F32),
                   jax.ShapeDtypeStruct((N_KV, CT_ROWS, COLS), F32)],
        compiler_params=_params(None),
        name="templates",
    )(fext, gxext)


def _bias_templates(rel_bias):
    biasp = rel_bias[:, _t5_bucket(jnp.arange(MAX_DIST))] - rel_bias[:, N_BUCKETS - 1:]
    biasp = biasp * LOG2E

    def by_distance(n_neg, n_zero, n_neg_after):
        return jnp.concatenate(
            [jnp.full((N_HEADS, n_neg), NEG, F32), biasp, jnp.zeros((N_HEADS, n_zero), F32),
             jnp.full((N_HEADS, n_neg_after), NEG, F32)], axis=1)

    wk = WINDOW + TQ
    f = by_distance(TQ - 1, WINDOW - MAX_DIST, wk - WINDOW)
    fext = jnp.concatenate([f, jnp.zeros((N_HEADS, 1), F32)], axis=1)

    d_min = -(CMP_LEN - 1) - CMP_STRIDE * CMP_R_HI
    d_max = TQ - 1 - (CMP_LEN - 1) - CMP_STRIDE * CMP_R_LO
    gx = by_distance(-d_min, d_max + 1 - MAX_DIST, 0)
    gxext = jnp.pad(gx, ((0, 0), (0, fext.shape[1] - gx.shape[1])))
    wtmpl, ctmpl = _templates(fext, gxext)
    return ctmpl, wtmpl


def _overlap_t():
    cmp_start = jnp.arange(N_CMP_PAD) * CMP_STRIDE
    slc_start = jnp.arange(N_SLC) * SLC_LEN
    ov = ((cmp_start[None, :] < slc_start[:, None] + SLC_LEN)
          & (cmp_start[None, :] + CMP_LEN > slc_start[:, None])
          & (jnp.arange(N_CMP_PAD)[None, :] < N_CMP))
    return ov.astype(BF16)


def _vt_tiles(vt):
    ones = jnp.concatenate([jnp.ones((1, SEQ), BF16), jnp.zeros((VROWS - HEAD_DIM - 1, SEQ), BF16)])
    t = jnp.stack([jnp.concatenate([vt[g * HEAD_DIM:(g + 1) * HEAD_DIM], ones], axis=0)
                   for g in range(N_KV)])
    t = t.reshape(N_KV, VROWS, SEQ // TK, TK).transpose(0, 2, 1, 3)
    return jnp.pad(t, ((0, 0), (KPAD // TK, 0), (0, 0), (0, 0)))


def kernel(x, c, w_ada, b_ada, g_ffn1, w_gu1, w_down1, g_mix, w_in, w_dw, b_dw, ln_g, ln_b,
           pe_k, pe_v, w_ck1, w_ck2, w_cv1, w_cv2, rel_bias, w_out, g_ffn2, w_gu2, w_down2,
           g_final):
    assert x.shape == (1, SEQ, D_MODEL) and w_ada.shape[0] == 1
    x2 = x[0]
    mod = _ada(c.reshape(D_MODEL, 1), w_ada[0], b_ada)
    x1 = _ffn1(x2, g_ffn1, mod, w_gu1[0].astype(BF16), w_down1[0].astype(BF16))

    wi = w_in[0]
    wq = wi[:, 1024:1536].reshape(D_MODEL, N_KV, HPG, HEAD_DIM).transpose(0, 2, 1, 3)
    wq = wq.reshape(D_MODEL, D_ATT) * (HEAD_DIM ** -0.5 * LOG2E)
    w_in_p = jnp.concatenate(
        [wi[:, :1024], wq, wi[:, 1536:], jnp.zeros((D_MODEL, D_IN_PAD - wi.shape[1]), F32)],
        axis=1).astype(BF16)
    hc, kcvc, ksl, kw, qt, vslt, vwt, gatet = _proj(x1, g_mix, mod, w_in_p)

    y_conv = _conv(hc, w_dw[0], b_dw, ln_g, ln_b)

    cmat = kcvc.reshape(SEQ, 4, HEAD_DIM).transpose(1, 0, 2).reshape(4, N_CMP_PAD, 1024)
    pe = jnp.stack([pe_k[0].reshape(1, -1), pe_v[0].reshape(1, -1)])
    w1 = jnp.stack([w_ck1[0], w_cv1[0]]).astype(BF16)
    w2 = jnp.stack([w_ck2[0], w_cv2[0]]).astype(BF16)
    cmp = _compress(cmat, pe, w1, w2)
    kc = jnp.concatenate([cmp[0], cmp[1]], axis=1).astype(BF16)
    vct = jnp.swapaxes(cmp[2:4], 1, 2).astype(BF16)

    ctmpl, wtmpl = _bias_templates(rel_bias)
    front = ((KPAD, 0), (0, 0))
    y_att = _attn(qt, gatet, kc, vct, _overlap_t(), jnp.pad(ksl, front), _vt_tiles(vslt),
                  jnp.pad(kw, front), _vt_tiles(vwt), ctmpl, wtmpl)

    wo = w_out[0]
    woa = wo[D_CONV:].reshape(N_KV, HPG, HEAD_DIM, D_MODEL).transpose(1, 0, 2, 3)
    woa = woa.reshape(HPG, LANES, D_MODEL).astype(BF16)
    out = _ffn2(x1, y_conv, y_att, wo[:D_CONV].astype(BF16), woa, mod,
                g_ffn2, w_gu2[0].astype(BF16), w_down2[0].astype(BF16),
                g_final.reshape(1, D_MODEL))
    return out[None]
```

```python
import math

import jax
import jax.numpy as jnp
from jax import lax
from jax.experimental import pallas as pl
from jax.experimental.pallas import tpu as pltpu

F32 = jnp.float32
BF16 = jnp.bfloat16

D_MODEL = 1024
SEQ = 16384
D_CONV = 512
CONV_WIDTH = 31
N_HEADS = 8
N_KV = 2
HPG = 4
HEAD_DIM = 64
D_ATT = 512
D_KV = 128
CMP_LEN = 32
CMP_STRIDE = 16
CMP_HIDDEN = 128
N_CMP = (SEQ - CMP_LEN) // CMP_STRIDE + 1
N_CMP_PAD = 1024
SLC_LEN = 64
N_SLC = SEQ // SLC_LEN
SLC_TOPK = 16
WINDOW = 512
N_FORCED = 3
TAKEN = -2.0
N_BUCKETS = 32
MAX_DIST = 128
D_FF = 2816
FFN_RES = 0.5
EPS = 1e-6
NEG = -1e30
M_FLOOR = -1e29
LOG2E = math.log2(math.e)

V7X_VMEM_BYTES = 64 * 1024 * 1024
VMEM_LIMIT = V7X_VMEM_BYTES - 6 * 1024 * 1024
LANES = 128
SUBLANES = 8

TQ = 256
TK = 256
COLS = HPG * TQ
KPAD = WINDOW
VROWS = HEAD_DIM + 16
BLK_PER_TILE = TK // SLC_LEN
FAR_UNROLL = 4
SEL_ENTRIES = SEQ // TK + 2
CT_ZERO = 520
CT_ROWS = 800
CMP_R_LO, CMP_R_HI = -9, 14
FFN_TM = 512
FFN_TF = D_FF
PROJ_TM = 512
CONV_TM = 256
CONV_HALO = 32
CONV_CHUNK = 32
D_IN_PAD = 2432


def _params(sem):
    return pltpu.CompilerParams(dimension_semantics=sem, vmem_limit_bytes=VMEM_LIMIT)


def _const_spec(shape):
    nd = len(shape)
    return pl.BlockSpec(shape, lambda *_: (0,) * nd, pipeline_mode=pl.Buffered(1))


def _sigmoid(v):
    return 1.0 / (1.0 + jnp.exp(-v))


def _dot(a, b):
    return jnp.dot(a, b, preferred_element_type=F32)


def _ada_kernel(c_ref, w_ref, b_ref, o_ref):
    c = c_ref[...]
    sc = c * _sigmoid(c)
    o_ref[...] = jnp.sum(w_ref[...] * sc, axis=0, keepdims=True) + b_ref[...]


def _ada(c_col, w, b):
    n = w.shape[1]
    tn = n // 8
    return pl.pallas_call(
        _ada_kernel,
        out_shape=jax.ShapeDtypeStruct((1, n), F32),
        grid=(8,),
        in_specs=[pl.BlockSpec((D_MODEL, 1), lambda j: (0, 0)),
                  pl.BlockSpec((D_MODEL, tn), lambda j: (0, j)),
                  pl.BlockSpec((1, tn), lambda j: (0, j))],
        out_specs=pl.BlockSpec((1, tn), lambda j: (0, j)),
        compiler_params=_params(("arbitrary",)),
        name="ada",
    )(c_col, w, b)


def _rms_mod(x, g, sh, sc):
    ms = jnp.mean(x * x, axis=-1, keepdims=True)
    y = x * lax.rsqrt(ms + EPS) * g
    return y * (1.0 + sc) + sh


def _ffn_body(x, g_ref, sh_ref, sc_ref, gt_ref, wgu_ref, wd_ref):
    hb = _rms_mod(x, g_ref[...], sh_ref[...], sc_ref[...]).astype(BF16)
    acc = jnp.zeros((x.shape[0], D_MODEL), F32)
    for f in range(D_FF // FFN_TF):
        gg = _dot(hb, wgu_ref[:, f * FFN_TF:(f + 1) * FFN_TF])
        uu = _dot(hb, wgu_ref[:, D_FF + f * FFN_TF:D_FF + (f + 1) * FFN_TF])
        a = (gg * _sigmoid(gg) * uu).astype(BF16)
        acc = acc + _dot(a, wd_ref[f * FFN_TF:(f + 1) * FFN_TF, :])
    return x + (FFN_RES * gt_ref[...]) * acc


def _ffn1_kernel(x_ref, g_ref, sh_ref, sc_ref, gt_ref, wgu_ref, wd_ref, o_ref):
    o_ref[...] = _ffn_body(x_ref[...], g_ref, sh_ref, sc_ref, gt_ref, wgu_ref, wd_ref)


def _ffn2_kernel(x_ref, yc_ref, ya_ref, woc_ref, woa_ref, gt2_ref,
                 g_ref, sh_ref, sc_ref, gt_ref, wgu_ref, wd_ref, gf_ref, o_ref):
    tm = x_ref.shape[0]
    y = _dot(yc_ref[...], woc_ref[...])
    for h in range(HPG):
        y = y + _dot(ya_ref[:, h].reshape(tm, LANES), woa_ref[h])
    x = x_ref[...] + gt2_ref[...] * y
    out = _ffn_body(x, g_ref, sh_ref, sc_ref, gt_ref, wgu_ref, wd_ref)
    ms = jnp.mean(out * out, axis=-1, keepdims=True)
    o_ref[...] = out * lax.rsqrt(ms + EPS) * gf_ref[...]


def _row_spec(tm, n):
    return pl.BlockSpec((tm, n), lambda i: (i, 0))


def _mod_spec(k):
    return pl.BlockSpec((1, D_MODEL), lambda *_: (0, k), pipeline_mode=pl.Buffered(1))


def _ffn1(x, g, mod, wgu, wd):
    vec = _const_spec((1, D_MODEL))
    return pl.pallas_call(
        _ffn1_kernel,
        out_shape=jax.ShapeDtypeStruct((SEQ, D_MODEL), F32),
        grid=(SEQ // FFN_TM,),
        in_specs=[_row_spec(FFN_TM, D_MODEL), vec, _mod_spec(0), _mod_spec(1), _mod_spec(2),
                  _const_spec((D_MODEL, 2 * D_FF)), _const_spec((D_FF, D_MODEL))],
        out_specs=_row_spec(FFN_TM, D_MODEL),
        compiler_params=_params(("arbitrary",)),
        name="ffn1",
    )(x, g, mod, mod, mod, wgu, wd)


def _ffn2(x, yc, ya, woc, woa, mod, g, wgu, wd, gf):
    vec = _const_spec((1, D_MODEL))
    nt = FFN_TM // TQ
    return pl.pallas_call(
        _ffn2_kernel,
        out_shape=jax.ShapeDtypeStruct((SEQ, D_MODEL), F32),
        grid=(SEQ // FFN_TM,),
        in_specs=[_row_spec(FFN_TM, D_MODEL), _row_spec(FFN_TM, D_CONV),
                  pl.BlockSpec((nt, HPG, TQ, LANES), lambda i: (i, 0, 0, 0)),
                  _const_spec((D_CONV, D_MODEL)), _const_spec((HPG, LANES, D_MODEL)), _mod_spec(5),
                  vec, _mod_spec(6), _mod_spec(7), _mod_spec(8),
                  _const_spec((D_MODEL, 2 * D_FF)), _const_spec((D_FF, D_MODEL)), vec],
        out_specs=_row_spec(FFN_TM, D_MODEL),
        compiler_params=_params(("arbitrary",)),
        name="ffn2",
    )(x, yc, ya, woc, woa, mod, g, mod, mod, mod, wgu, wd, gf)


def _proj_kernel(x_ref, g_ref, sh_ref, sc_ref, w_ref,
                 hc_ref, kcvc_ref, ksl_ref, kw_ref, qt_ref, vslt_ref, vwt_ref, gatet_ref):
    hb = _rms_mod(x_ref[...], g_ref[...], sh_ref[...], sc_ref[...]).astype(BF16)
    u = _dot(hb, w_ref[...])
    hc_ref[...] = u[:, 0:512] * _sigmoid(u[:, 512:1024])
    kcvc_ref[...] = u[:, 1536:1792]
    tm = x_ref.shape[0]
    blk = (lax.broadcasted_iota(jnp.int32, (tm, D_KV), 0) % TK) // SLC_LEN
    ind = (blk == lax.broadcasted_iota(jnp.int32, (tm, D_KV), 1)).astype(BF16)
    ksl_ref[...] = jnp.concatenate([u[:, 1792:1920].astype(BF16), ind], axis=1)
    kw_ref[...] = u[:, 2048:2176].astype(BF16)
    qt_ref[...] = u[:, 1024:1536].T.astype(BF16)
    vslt_ref[...] = u[:, 1920:2048].T.astype(BF16)
    vwt_ref[...] = u[:, 2176:2304].T.astype(BF16)
    gatet_ref[...] = _sigmoid(u[:, 2304:2432]).T


def _proj(x, g, mod, w):
    vec = _const_spec((1, D_MODEL))
    tm = PROJ_TM
    rows = [(D_CONV, F32), (2 * D_KV, F32), (2 * D_KV, BF16), (D_KV, BF16)]
    cols = [(D_ATT, BF16), (D_KV, BF16), (D_KV, BF16), (LANES, F32)]
    return pl.pallas_call(
        _proj_kernel,
        out_shape=[jax.ShapeDtypeStruct((SEQ, n), dt) for n, dt in rows]
                  + [jax.ShapeDtypeStruct((n, SEQ), dt) for n, dt in cols],
        grid=(SEQ // tm,),
        in_specs=[_row_spec(tm, D_MODEL), vec, _mod_spec(3), _mod_spec(4),
                  _const_spec((D_MODEL, D_IN_PAD))],
        out_specs=[_row_spec(tm, n) for n, _ in rows]
                  + [pl.BlockSpec((n, tm), lambda i: (0, i)) for n, _ in cols],
        compiler_params=_params(("arbitrary",)),
        name="proj",
    )(x, g, mod, mod, w)


def _conv_kernel(prev_ref, cur_ref, w_ref, b_ref, g_ref, bb_ref, o_ref, xs_ref):
    i = pl.program_id(0)
    tm = cur_ref.shape[0]
    prev = jnp.where(i > 0, prev_ref[...], 0.0)
    xx = jnp.concatenate([prev, cur_ref[...]], axis=0)
    span = tm + CONV_HALO - SUBLANES
    xs_ref[0] = xx
    for b in range(1, SUBLANES):
        xs_ref[b, 0:span, :] = xx[b:b + span, :]
    off = CONV_HALO - (CONV_WIDTH - 1)
    wts = w_ref[...]
    for r0 in range(0, tm, CONV_CHUNK):
        acc = jnp.zeros((CONV_CHUNK, D_CONV), F32) + b_ref[...]
        for w in range(CONV_WIDTH):
            a8, b = divmod(off + w, SUBLANES)
            lo = r0 + a8 * SUBLANES
            acc = acc + xs_ref[b, lo:lo + CONV_CHUNK, :] * wts[w:w + 1, :]
        mu = jnp.mean(acc, axis=-1, keepdims=True)
        dlt = acc - mu
        var = jnp.mean(dlt * dlt, axis=-1, keepdims=True)
        y = dlt * lax.rsqrt(var + EPS) * g_ref[...] + bb_ref[...]
        o_ref[r0:r0 + CONV_CHUNK, :] = (y * _sigmoid(y)).astype(BF16)


def _conv(hc, w, b, g, bb):
    tm = CONV_TM
    r = tm // CONV_HALO
    vec = _const_spec((1, D_CONV))
    return pl.pallas_call(
        _conv_kernel,
        out_shape=jax.ShapeDtypeStruct((SEQ, D_CONV), BF16),
        grid=(SEQ // tm,),
        in_specs=[pl.BlockSpec((CONV_HALO, D_CONV), lambda i: (jnp.maximum(i * r - 1, 0), 0)),
                  _row_spec(tm, D_CONV),
                  _const_spec((CONV_WIDTH, D_CONV)), vec, vec, vec],
        out_specs=_row_spec(tm, D_CONV),
        scratch_shapes=[pltpu.VMEM((SUBLANES, CONV_HALO + tm, D_CONV), F32)],
        compiler_params=_params(("arbitrary",)),
        name="conv",
    )(hc, hc, w, b, g, bb)


def _compress_kernel(c_ref, pe_ref, w1_ref, w2_ref, o_ref):
    half = CMP_STRIDE * HEAD_DIM
    cm = c_ref[0]
    pe = pe_ref[0]
    top = (cm + pe[:, :half]).astype(BF16)
    bot = (cm + pe[:, half:]).astype(BF16)
    a = _dot(top, w1_ref[0, :half, :])
    b = _dot(bot, w1_ref[0, half:, :])
    b_up = jnp.concatenate([b[1:], jnp.zeros((1, CMP_HIDDEN), F32)], axis=0)
    pre = a + b_up
    hid = (pre * _sigmoid(pre)).astype(BF16)
    o_ref[0] = _dot(hid, w2_ref[0])


def _compress(cmat, pe, w1, w2):
    return pl.pallas_call(
        _compress_kernel,
        out_shape=jax.ShapeDtypeStruct((4, N_CMP_PAD, HEAD_DIM), F32),
        grid=(4,),
        in_specs=[pl.BlockSpec((1, N_CMP_PAD, 1024), lambda i: (i, 0, 0)),
                  pl.BlockSpec((1, 1, 2048), lambda i: (i // 2, 0, 0)),
                  pl.BlockSpec((1, 2048, CMP_HIDDEN), lambda i: (i // 2, 0, 0)),
                  pl.BlockSpec((1, CMP_HIDDEN, HEAD_DIM), lambda i: (i // 2, 0, 0))],
        out_specs=pl.BlockSpec((1, N_CMP_PAD, HEAD_DIM), lambda i: (i, 0, 0)),
        compiler_params=_params(("arbitrary",)),
        name="compress",
    )(cmat, pe, w1, w2)


def _tile_update(s, m, acc, vt):
    m_new = jnp.maximum(m, jnp.max(s, axis=0, keepdims=True))
    alpha = jnp.exp2(m - m_new)
    p = jnp.exp2(s - m_new).astype(BF16)
    return m_new, alpha * acc + _dot(vt, p)


def _finish(acc):
    return acc[:HEAD_DIM] / jnp.maximum(acc[HEAD_DIM:HEAD_DIM + 1], 1e-30)


def _attn_kernel(qt_ref, gt_ref, kc_ref, vct_ref, ovt_ref, ksl_ref, vslt_ref, kw_ref, vwt_ref,
                 ct_ref, wt_ref, o_ref, s_scr, p_scr, sel_scr, wa_scr, wb_scr, sa_scr, sb_scr):
    qb = pl.program_id(0)
    q0 = qb * TQ
    qt = qt_ref[...]
    gt = gt_ref[...]
    drow = lax.broadcasted_iota(jnp.int32, (D_KV, TQ), 0)
    krow = lax.broadcasted_iota(jnp.int32, (TK, 1), 0)
    m0 = jnp.full((1, COLS), M_FLOOR, F32)
    acc0 = jnp.zeros((VROWS, COLS), F32)

    def pad_mask(first_row):
        return jnp.where(krow + first_row >= KPAD, 0.0, NEG)

    comb = []
    for g in range(N_KV):
        keep = (drow >= HEAD_DIM * g) & (drow < HEAD_DIM * (g + 1))
        q_g = jnp.concatenate(
            [jnp.where(keep, qt[h * D_KV:(h + 1) * D_KV, :], jnp.zeros((), BF16))
             for h in range(HPG)], axis=1)

        mcol = m0
        for ct in range(N_CMP_PAD // TK):
            rows = slice(ct * TK, (ct + 1) * TK)
            r0 = pl.multiple_of(jnp.clip(TK * ct - (TQ // CMP_STRIDE) * qb + CT_ZERO,
                                         0, CT_ROWS - TK), SUBLANES)
            s = _dot(kc_ref[rows, :], q_g) + ct_ref[g, pl.ds(r0, TK), :]
            s_scr[rows, :] = s
            mcol = jnp.maximum(mcol, jnp.max(s, axis=0, keepdims=True))
        lsum = jnp.zeros((1, COLS), F32)
        acc_c = jnp.zeros((HEAD_DIM, COLS), F32)
        for ct in range(N_CMP_PAD // TK):
            rows = slice(ct * TK, (ct + 1) * TK)
            p = jnp.exp2(s_scr[rows, :] - mcol)
            p_scr[rows, :] = p
            lsum = lsum + jnp.sum(p, axis=0, keepdims=True)
            acc_c = acc_c + _dot(vct_ref[g, :, rows], p.astype(BF16))
        rinv = 1.0 / jnp.maximum(lsum, 1e-30)
        o_c = acc_c * rinv
        imp = jnp.zeros((N_SLC, TQ), F32)
        for ct in range(N_CMP_PAD // TK):
            rows = slice(ct * TK, (ct + 1) * TK)
            pn = p_scr[rows, :] * rinv
            ps = pn[:, 0:TQ] + pn[:, TQ:2 * TQ] + pn[:, 2 * TQ:3 * TQ] + pn[:, 3 * TQ:4 * TQ]
            hi = ps.astype(BF16)
            lo = (ps - hi.astype(F32)).astype(BF16)
            ov = ovt_ref[:, rows]
            imp = imp + _dot(ov, hi) + _dot(ov, lo)

        m, acc = m0, acc0
        for a in range(3):
            first = q0 + TK * a
            s = _dot(kw_ref[pl.ds(pl.multiple_of(first, TK), TK), :], q_g)
            s = s + (wt_ref[g, TK * a:TK * (a + 1), :] + pad_mask(first))
            m, acc = _tile_update(s, m, acc, vwt_ref[g, qb + a])
        o_w = _finish(acc)

        cur = jnp.right_shift(q0 + lax.broadcasted_iota(jnp.int32, (1, TQ), 1), 6)
        jcol = lax.broadcasted_iota(jnp.int32, (N_SLC, 1), 0)
        forced = (jcol == 0) | (jcol == cur) | (jcol == cur - 1)
        sc = jnp.where(forced, TAKEN, jnp.where(jcol <= cur, imp, -1.0))
        jf = lax.broadcasted_iota(jnp.int32, (N_SLC, TQ), 0).astype(F32)
        for _ in range(SLC_TOPK - N_FORCED):
            mx = jnp.max(sc, axis=0, keepdims=True)
            idx = jnp.min(jnp.where(sc == mx, jf, 1e9), axis=0, keepdims=True)
            sc = jnp.where(jf == idx, TAKEN, sc)
        selneg = jnp.where(sc == TAKEN, 0.0, NEG)
        unused = jnp.zeros((SUBLANES - BLK_PER_TILE, TQ), F32)
        for kt in range(SEQ // TK):
            sel_scr[kt] = jnp.concatenate(
                [selneg[kt * BLK_PER_TILE:(kt + 1) * BLK_PER_TILE, :], unused], axis=0)
        sel_scr[SEQ // TK] = jnp.concatenate(
            [jnp.full((BLK_PER_TILE, TQ), NEG, F32), unused], axis=0)

        zrows = jnp.zeros((2 * D_KV - D_KV - 2 * SUBLANES, COLS), BF16)
        for w_scr in (wa_scr, wb_scr):
            w_scr[0:D_KV, :] = q_g
            w_scr[D_KV + 2 * SUBLANES:, :] = zrows

        def sel_scores(w_scr, first_row, sel_idx):
            blk = sel_scr[sel_idx]
            rows8 = jnp.concatenate([blk] * HPG, axis=1).astype(BF16)
            w_scr[D_KV:D_KV + 2 * SUBLANES, :] = jnp.concatenate(
                [rows8, jnp.zeros((SUBLANES, COLS), BF16)], axis=0)
            return _dot(ksl_ref[pl.ds(pl.multiple_of(first_row, TK), TK), :], w_scr[...])

        m, acc = m0, acc0
        for a, w_scr in ((0, wa_scr), (1, wb_scr)):
            first = q0 + TQ + TK * a
            s = sel_scores(w_scr, first, jnp.maximum(qb - 1 + a, 0))
            s = s + (wt_ref[g, TQ + TK * a:TQ + TK * (a + 1), :] + pad_mask(first))
            m, acc = _tile_update(s, m, acc, vslt_ref[g, qb + 1 + a])

        n_far = jnp.maximum(qb - 1, 0)

        def far_scores(w_scr, kt):
            valid = kt < n_far
            ks = jnp.where(valid, kt, 0)
            return sel_scores(w_scr, KPAD + ks * TK, jnp.where(valid, kt, SEQ // TK))

        def far_v(kt):
            return vslt_ref[g, jnp.where(kt < n_far, kt, 0) + KPAD // TK]

        bufs = ((sa_scr, wa_scr), (sb_scr, wb_scr))

        def issue_scores(buf, kt):
            s_scr_k, w_scr_k = bufs[buf]
            s = far_scores(w_scr_k, kt)
            s_scr_k[...] = s
            return jnp.max(s, axis=0, keepdims=True)

        def far_body(i, carry):
            m_i, acc_i, smax = carry
            for u in range(FAR_UNROLL):
                kt = i * FAR_UNROLL + u
                smax_nxt = issue_scores((u + 1) % 2, kt + 1)
                m_new = jnp.maximum(m_i, smax)
                alpha = jnp.exp2(m_i - m_new)
                p = jnp.exp2(bufs[u % 2][0][...] - m_new).astype(BF16)
                acc_i = alpha * acc_i + _dot(far_v(kt), p)
                m_i, smax = m_new, smax_nxt
            return m_i, acc_i, smax

        trips = (n_far + FAR_UNROLL - 1) // FAR_UNROLL
        m, acc, _ = lax.fori_loop(0, trips, far_body, (m, acc, issue_scores(0, 0)))
        o_s = _finish(acc)

        per_head = []
        for h in range(HPG):
            c0 = 3 * (g * HPG + h)
            cs = slice(h * TQ, (h + 1) * TQ)
            per_head.append(gt[c0:c0 + 1, :] * o_c[:, cs] + gt[c0 + 1:c0 + 2, :] * o_s[:, cs]
                            + gt[c0 + 2:c0 + 3, :] * o_w[:, cs])
        comb.append(per_head)

    for h in range(HPG):
        merged = jnp.concatenate([comb[0][h], comb[1][h]], axis=0)
        o_ref[0, h] = merged.T.astype(BF16)


def _attn(qt, gatet, kc, vct, ovt, ksl, vslt, kw, vwt, ctmpl, wtmpl):
    consts = [kc, vct, ovt, ksl, vslt, kw, vwt, ctmpl, wtmpl]
    return pl.pallas_call(
        _attn_kernel,
        out_shape=jax.ShapeDtypeStruct((SEQ // TQ, HPG, TQ, LANES), BF16),
        grid=(SEQ // TQ,),
        in_specs=[pl.BlockSpec((D_ATT, TQ), lambda i: (0, i)),
                  pl.BlockSpec((LANES, TQ), lambda i: (0, i))]
                 + [_const_spec(a.shape) for a in consts],
        out_specs=pl.BlockSpec((1, HPG, TQ, LANES), lambda i: (i, 0, 0, 0)),
        scratch_shapes=[pltpu.VMEM((N_CMP_PAD, COLS), F32), pltpu.VMEM((N_CMP_PAD, COLS), F32),
                        pltpu.VMEM((SEL_ENTRIES, SUBLANES, TQ), F32),
                        pltpu.VMEM((2 * D_KV, COLS), BF16), pltpu.VMEM((2 * D_KV, COLS), BF16),
                        pltpu.VMEM((TK, COLS), F32), pltpu.VMEM((TK, COLS), F32)],
        compiler_params=_params(("arbitrary",)),
        name="nsa_attn",
    )(qt, gatet, *consts)


def _t5_bucket(dist):
    max_exact = N_BUCKETS // 2
    d = jnp.maximum(dist, 0)
    df = jnp.maximum(d, 1).astype(F32)
    large = max_exact + (jnp.log(df / max_exact) / math.log(MAX_DIST / max_exact)
                         * (N_BUCKETS - max_exact)).astype(jnp.int32)
    large = jnp.minimum(large, N_BUCKETS - 1)
    return jnp.where(d < max_exact, d, large)


def _template_kernel(f_ref, gx_ref, wt_ref, ct_ref):
    wk, lanes = wt_ref.shape[1], f_ref.shape[1]
    n_band = CMP_R_HI - CMP_R_LO + 1
    lo = CT_ZERO + CMP_R_LO
    for hd in range(N_HEADS):
        g, cols = hd // HPG, slice((hd % HPG) * TQ, (hd % HPG + 1) * TQ)
        x = jnp.broadcast_to(f_ref[hd:hd + 1, :], (wk, lanes))
        y = pltpu.roll(x, lanes - (wk - 1), axis=1, stride=1, stride_axis=0)
        wt_ref[g, :, cols] = y[:, :TQ]
        xb = jnp.broadcast_to(gx_ref[hd:hd + 1, :], (n_band, lanes))
        yb = pltpu.roll(xb, lanes - CMP_STRIDE * (n_band - 1), axis=1,
                        stride=CMP_STRIDE, stride_axis=0)
        ct_ref[g, 0:lo, cols] = jnp.zeros((lo, TQ), F32)
        ct_ref[g, lo:lo + n_band, cols] = yb[:, :TQ]
        ct_ref[g, lo + n_band:, cols] = jnp.full((CT_ROWS - lo - n_band, TQ), NEG, F32)


def _templates(fext, gxext):
    return pl.pallas_call(
        _template_kernel,
        out_shape=[jax.ShapeDtypeStruct((N_KV, WINDOW + TQ, COLS), F32),
                   jax.ShapeDtypeStruct((N_KV, CT_ROWS, COLS), F32)],
        compiler_params=_params(None),
        name="templates",
    )(fext, gxext)


def _bias_templates(rel_bias):
    biasp = rel_bias[:, _t5_bucket(jnp.arange(MAX_DIST))] - rel_bias[:, N_BUCKETS - 1:]
    biasp = biasp * LOG2E

    def by_distance(n_neg, n_zero, n_neg_after):
        return jnp.concatenate(
            [jnp.full((N_HEADS, n_neg), NEG, F32), biasp, jnp.zeros((N_HEADS, n_zero), F32),
             jnp.full((N_HEADS, n_neg_after), NEG, F32)], axis=1)

    wk = WINDOW + TQ
    f = by_distance(TQ - 1, WINDOW - MAX_DIST, wk - WINDOW)
    fext = jnp.concatenate([f, jnp.zeros((N_HEADS, 1), F32)], axis=1)

    d_min = -(CMP_LEN - 1) - CMP_STRIDE * CMP_R_HI
    d_max = TQ - 1 - (CMP_LEN - 1) - CMP_STRIDE * CMP_R_LO
    gx = by_distance(-d_min, d_max + 1 - MAX_DIST, 0)
    gxext = jnp.pad(gx, ((0, 0), (0, fext.shape[1] - gx.shape[1])))
    wtmpl, ctmpl = _templates(fext, gxext)
    return ctmpl, wtmpl


def _overlap_t():
    cmp_start = jnp.arange(N_CMP_PAD) * CMP_STRIDE
    slc_start = jnp.arange(N_SLC) * SLC_LEN
    ov = ((cmp_start[None, :] < slc_start[:, None] + SLC_LEN)
          & (cmp_start[None, :] + CMP_LEN > slc_start[:, None])
          & (jnp.arange(N_CMP_PAD)[None, :] < N_CMP))
    return ov.astype(BF16)


def _vt_tiles(vt):
    ones = jnp.concatenate([jnp.ones((1, SEQ), BF16), jnp.zeros((VROWS - HEAD_DIM - 1, SEQ), BF16)])
    t = jnp.stack([jnp.concatenate([vt[g * HEAD_DIM:(g + 1) * HEAD_DIM], ones], axis=0)
                   for g in range(N_KV)])
    t = t.reshape(N_KV, VROWS, SEQ // TK, TK).transpose(0, 2, 1, 3)
    return jnp.pad(t, ((0, 0), (KPAD // TK, 0), (0, 0), (0, 0)))


def kernel(x, c, w_ada, b_ada, g_ffn1, w_gu1, w_down1, g_mix, w_in, w_dw, b_dw, ln_g, ln_b,
           pe_k, pe_v, w_ck1, w_ck2, w_cv1, w_cv2, rel_bias, w_out, g_ffn2, w_gu2, w_down2,
           g_final):
    assert x.shape == (1, SEQ, D_MODEL) and w_ada.shape[0] == 1
    x2 = x[0]
    mod = _ada(c.reshape(D_MODEL, 1), w_ada[0], b_ada)
    x1 = _ffn1(x2, g_ffn1, mod, w_gu1[0].astype(BF16), w_down1[0].astype(BF16))

    wi = w_in[0]
    wq = wi[:, 1024:1536].reshape(D_MODEL, N_KV, HPG, HEAD_DIM).transpose(0, 2, 1, 3)
    wq = wq.reshape(D_MODEL, D_ATT) * (HEAD_DIM ** -0.5 * LOG2E)
    w_in_p = jnp.concatenate(
        [wi[:, :1024], wq, wi[:, 1536:], jnp.zeros((D_MODEL, D_IN_PAD - wi.shape[1]), F32)],
        axis=1).astype(BF16)
    hc, kcvc, ksl, kw, qt, vslt, vwt, gatet = _proj(x1, g_mix, mod, w_in_p)

    y_conv = _conv(hc, w_dw[0], b_dw, ln_g, ln_b)

    cmat = kcvc.reshape(SEQ, 4, HEAD_DIM).transpose(1, 0, 2).reshape(4, N_CMP_PAD, 1024)
    pe = jnp.stack([pe_k[0].reshape(1, -1), pe_v[0].reshape(1, -1)])
    w1 = jnp.stack([w_ck1[0], w_cv1[0]]).astype(BF16)
    w2 = jnp.stack([w_ck2[0], w_cv2[0]]).astype(BF16)
    cmp = _compress(cmat, pe, w1, w2)
    kc = jnp.concatenate([cmp[0], cmp[1]], axis=1).astype(BF16)
    vct = jnp.swapaxes(cmp[2:4], 1, 2).astype(BF16)

    ctmpl, wtmpl = _bias_templates(rel_bias)
    front = ((KPAD, 0), (0, 0))
    y_att = _attn(qt, gatet, kc, vct, _overlap_t(), jnp.pad(ksl, front), _vt_tiles(vslt),
                  jnp.pad(kw, front), _vt_tiles(vwt), ctmpl, wtmpl)

    wo = w_out[0]
    woa = wo[D_CONV:].reshape(N_KV, HPG, HEAD_DIM, D_MODEL).transpose(1, 0, 2, 3)
    woa = woa.reshape(HPG, LANES, D_MODEL).astype(BF16)
    out = _ffn2(x1, y_conv, y_att, wo[:D_CONV].astype(BF16), woa, mod,
                g_ffn2, w_gu2[0].astype(BF16), w_down2[0].astype(BF16),
                g_final.reshape(1, D_MODEL))
    return out[None]
```

```python
import math

import jax
import jax.numpy as jnp
from jax import lax
from jax.experimental import pallas as pl
from jax.experimental.pallas import tpu as pltpu

F32 = jnp.float32
BF16 = jnp.bfloat16

D_MODEL = 1024
SEQ = 16384
D_CONV = 512
CONV_WIDTH = 31
N_HEADS = 8
N_KV = 2
HPG = 4
HEAD_DIM = 64
D_ATT = 512
D_KV = 128
CMP_LEN = 32
CMP_STRIDE = 16
CMP_HIDDEN = 128
N_CMP = (SEQ - CMP_LEN) // CMP_STRIDE + 1
N_CMP_PAD = 1024
SLC_LEN = 64
N_SLC = SEQ // SLC_LEN
SLC_TOPK = 16
WINDOW = 512
N_FORCED = 3
TAKEN = -2.0
N_BUCKETS = 32
MAX_DIST = 128
D_FF = 2816
FFN_RES = 0.5
EPS = 1e-6
NEG = -1e30
M_FLOOR = -1e29
LOG2E = math.log2(math.e)

V7X_VMEM_BYTES = 64 * 1024 * 1024
VMEM_LIMIT = V7X_VMEM_BYTES - 6 * 1024 * 1024
LANES = 128
SUBLANES = 8

TQ = 256
TK = 256
COLS = HPG * TQ
KPAD = WINDOW
VROWS = HEAD_DIM + 16
BLK_PER_TILE = TK // SLC_LEN
FAR_UNROLL = 4
SEL_ENTRIES = SEQ // TK + 2
CT_ZERO = 520
CT_ROWS = 800
CMP_R_LO, CMP_R_HI = -9, 14
FFN_TM = 512
FFN_TF = D_FF
PROJ_TM = 512
CONV_TM = 256
CONV_HALO = 32
CONV_CHUNK = 32
D_IN_PAD = 2432


def _params(sem):
    return pltpu.CompilerParams(dimension_semantics=sem, vmem_limit_bytes=VMEM_LIMIT)


def _const_spec(shape):
    nd = len(shape)
    return pl.BlockSpec(shape, lambda *_: (0,) * nd, pipeline_mode=pl.Buffered(1))


def _sigmoid(v):
    return 1.0 / (1.0 + jnp.exp(-v))


def _dot(a, b):
    return jnp.dot(a, b, preferred_element_type=F32)


def _ada_kernel(c_ref, w_ref, b_ref, o_ref):
    c = c_ref[...]
    sc = c * _sigmoid(c)
    o_ref[...] = jnp.sum(w_ref[...] * sc, axis=0, keepdims=True) + b_ref[...]


def _ada(c_col, w, b):
    n = w.shape[1]
    tn = n // 8
    return pl.pallas_call(
        _ada_kernel,
        out_shape=jax.ShapeDtypeStruct((1, n), F32),
        grid=(8,),
        in_specs=[pl.BlockSpec((D_MODEL, 1), lambda j: (0, 0)),
                  pl.BlockSpec((D_MODEL, tn), lambda j: (0, j)),
                  pl.BlockSpec((1, tn), lambda j: (0, j))],
        out_specs=pl.BlockSpec((1, tn), lambda j: (0, j)),
        compiler_params=_params(("arbitrary",)),
        name="ada",
    )(c_col, w, b)


def _rms_mod(x, g, sh, sc):
    ms = jnp.mean(x * x, axis=-1, keepdims=True)
    y = x * lax.rsqrt(ms + EPS) * g
    return y * (1.0 + sc) + sh


def _ffn_body(x, g_ref, sh_ref, sc_ref, gt_ref, wgu_ref, wd_ref):
    hb = _rms_mod(x, g_ref[...], sh_ref[...], sc_ref[...]).astype(BF16)
    acc = jnp.zeros((x.shape[0], D_MODEL), F32)
    for f in range(D_FF // FFN_TF):
        gg = _dot(hb, wgu_ref[:, f * FFN_TF:(f + 1) * FFN_TF])
        uu = _dot(hb, wgu_ref[:, D_FF + f * FFN_TF:D_FF + (f + 1) * FFN_TF])
        a = (gg * _sigmoid(gg) * uu).astype(BF16)
        acc = acc + _dot(a, wd_ref[f * FFN_TF:(f + 1) * FFN_TF, :])
    return x + (FFN_RES * gt_ref[...]) * acc


def _ffn1_kernel(x_ref, g_ref, sh_ref, sc_ref, gt_ref, wgu_ref, wd_ref, o_ref):
    o_ref[...] = _ffn_body(x_ref[...], g_ref, sh_ref, sc_ref, gt_ref, wgu_ref, wd_ref)


def _ffn2_kernel(x_ref, yc_ref, ya_ref, woc_ref, woa_ref, gt2_ref,
                 g_ref, sh_ref, sc_ref, gt_ref, wgu_ref, wd_ref, gf_ref, o_ref):
    tm = x_ref.shape[0]
    y = _dot(yc_ref[...], woc_ref[...])
    for h in range(HPG):
        y = y + _dot(ya_ref[:, h].reshape(tm, LANES), woa_ref[h])
    x = x_ref[...] + gt2_ref[...] * y
    out = _ffn_body(x, g_ref, sh_ref, sc_ref, gt_ref, wgu_ref, wd_ref)
    ms = jnp.mean(out * out, axis=-1, keepdims=True)
    o_ref[...] = out * lax.rsqrt(ms + EPS) * gf_ref[...]


def _row_spec(tm, n):
    return pl.BlockSpec((tm, n), lambda i: (i, 0))


def _mod_spec(k):
    return pl.BlockSpec((1, D_MODEL), lambda *_: (0, k), pipeline_mode=pl.Buffered(1))


def _ffn1(x, g, mod, wgu, wd):
    vec = _const_spec((1, D_MODEL))
    return pl.pallas_call(
        _ffn1_kernel,
        out_shape=jax.ShapeDtypeStruct((SEQ, D_MODEL), F32),
        grid=(SEQ // FFN_TM,),
        in_specs=[_row_spec(FFN_TM, D_MODEL), vec, _mod_spec(0), _mod_spec(1), _mod_spec(2),
                  _const_spec((D_MODEL, 2 * D_FF)), _const_spec((D_FF, D_MODEL))],
        out_specs=_row_spec(FFN_TM, D_MODEL),
        compiler_params=_params(("arbitrary",)),
        name="ffn1",
    )(x, g, mod, mod, mod, wgu, wd)


def _ffn2(x, yc, ya, woc, woa, mod, g, wgu, wd, gf):
    vec = _const_spec((1, D_MODEL))
    nt = FFN_TM // TQ
    return pl.pallas_call(
        _ffn2_kernel,
        out_shape=jax.ShapeDtypeStruct((SEQ, D_MODEL), F32),
        grid=(SEQ // FFN_TM,),
        in_specs=[_row_spec(FFN_TM, D_MODEL), _row_spec(FFN_TM, D_CONV),
                  pl.BlockSpec((nt, HPG, TQ, LANES), lambda i: (i, 0, 0, 0)),
                  _const_spec((D_CONV, D_MODEL)), _const_spec((HPG, LANES, D_MODEL)), _mod_spec(5),
                  vec, _mod_spec(6), _mod_spec(7), _mod_spec(8),
                  _const_spec((D_MODEL, 2 * D_FF)), _const_spec((D_FF, D_MODEL)), vec],
        out_specs=_row_spec(FFN_TM, D_MODEL),
        compiler_params=_params(("arbitrary",)),
        name="ffn2",
    )(x, yc, ya, woc, woa, mod, g, mod, mod, mod, wgu, wd, gf)


def _proj_kernel(x_ref, g_ref, sh_ref, sc_ref, w_ref,
                 hc_ref, kcvc_ref, ksl_ref, kw_ref, qt_ref, vslt_ref, vwt_ref, gatet_ref):
    hb = _rms_mod(x_ref[...], g_ref[...], sh_ref[...], sc_ref[...]).astype(BF16)
    u = _dot(hb, w_ref[...])
    hc_ref[...] = u[:, 0:512] * _sigmoid(u[:, 512:1024])
    kcvc_ref[...] = u[:, 1536:1792]
    tm = x_ref.shape[0]
    blk = (lax.broadcasted_iota(jnp.int32, (tm, D_KV), 0) % TK) // SLC_LEN
    ind = (blk == lax.broadcasted_iota(jnp.int32, (tm, D_KV), 1)).astype(BF16)
    ksl_ref[...] = jnp.concatenate([u[:, 1792:1920].astype(BF16), ind], axis=1)
    kw_ref[...] = u[:, 2048:2176].astype(BF16)
    qt_ref[...] = u[:, 1024:1536].T.astype(BF16)
    vslt_ref[...] = u[:, 1920:2048].T.astype(BF16)
    vwt_ref[...] = u[:, 2176:2304].T.astype(BF16)
    gatet_ref[...] = _sigmoid(u[:, 2304:2432]).T


def _proj(x, g, mod, w):
    vec = _const_spec((1, D_MODEL))
    tm = PROJ_TM
    rows = [(D_CONV, F32), (2 * D_KV, F32), (2 * D_KV, BF16), (D_KV, BF16)]
    cols = [(D_ATT, BF16), (D_KV, BF16), (D_KV, BF16), (LANES, F32)]
    return pl.pallas_call(
        _proj_kernel,
        out_shape=[jax.ShapeDtypeStruct((SEQ, n), dt) for n, dt in rows]
                  + [jax.ShapeDtypeStruct((n, SEQ), dt) for n, dt in cols],
        grid=(SEQ // tm,),
        in_specs=[_row_spec(tm, D_MODEL), vec, _mod_spec(3), _mod_spec(4),
                  _const_spec((D_MODEL, D_IN_PAD))],
        out_specs=[_row_spec(tm, n) for n, _ in rows]
                  + [pl.BlockSpec((n, tm), lambda i: (0, i)) for n, _ in cols],
        compiler_params=_params(("arbitrary",)),
        name="proj",
    )(x, g, mod, mod, w)


def _conv_kernel(prev_ref, cur_ref, w_ref, b_ref, g_ref, bb_ref, o_ref, xs_ref):
    i = pl.program_id(0)
    tm = cur_ref.shape[0]
    prev = jnp.where(i > 0, prev_ref[...], 0.0)
    xx = jnp.concatenate([prev, cur_ref[...]], axis=0)
    span = tm + CONV_HALO - SUBLANES
    xs_ref[0] = xx
    for b in range(1, SUBLANES):
        xs_ref[b, 0:span, :] = xx[b:b + span, :]
    off = CONV_HALO - (CONV_WIDTH - 1)
    wts = w_ref[...]
    for r0 in range(0, tm, CONV_CHUNK):
        acc = jnp.zeros((CONV_CHUNK, D_CONV), F32) + b_ref[...]
        for w in range(CONV_WIDTH):
            a8, b = divmod(off + w, SUBLANES)
            lo = r0 + a8 * SUBLANES
            acc = acc + xs_ref[b, lo:lo + CONV_CHUNK, :] * wts[w:w + 1, :]
        mu = jnp.mean(acc, axis=-1, keepdims=True)
        dlt = acc - mu
        var = jnp.mean(dlt * dlt, axis=-1, keepdims=True)
        y = dlt * lax.rsqrt(var + EPS) * g_ref[...] + bb_ref[...]
        o_ref[r0:r0 + CONV_CHUNK, :] = (y * _sigmoid(y)).astype(BF16)


def _conv(hc, w, b, g, bb):
    tm = CONV_TM
    r = tm // CONV_HALO
    vec = _const_spec((1, D_CONV))
    return pl.pallas_call(
        _conv_kernel,
        out_shape=jax.ShapeDtypeStruct((SEQ, D_CONV), BF16),
        grid=(SEQ // tm,),
        in_specs=[pl.BlockSpec((CONV_HALO, D_CONV), lambda i: (jnp.maximum(i * r - 1, 0), 0)),
                  _row_spec(tm, D_CONV),
                  _const_spec((CONV_WIDTH, D_CONV)), vec, vec, vec],
        out_specs=_row_spec(tm, D_CONV),
        scratch_shapes=[pltpu.VMEM((SUBLANES, CONV_HALO + tm, D_CONV), F32)],
        compiler_params=_params(("arbitrary",)),
        name="conv",
    )(hc, hc, w, b, g, bb)


def _compress_kernel(c_ref, pe_ref, w1_ref, w2_ref, o_ref):
    half = CMP_STRIDE * HEAD_DIM
    cm = c_ref[0]
    pe = pe_ref[0]
    top = (cm + pe[:, :half]).astype(BF16)
    bot = (cm + pe[:, half:]).astype(BF16)
    a = _dot(top, w1_ref[0, :half, :])
    b = _dot(bot, w1_ref[0, half:, :])
    b_up = jnp.concatenate([b[1:], jnp.zeros((1, CMP_HIDDEN), F32)], axis=0)
    pre = a + b_up
    hid = (pre * _sigmoid(pre)).astype(BF16)
    o_ref[0] = _dot(hid, w2_ref[0])


def _compress(cmat, pe, w1, w2):
    return pl.pallas_call(
        _compress_kernel,
        out_shape=jax.ShapeDtypeStruct((4, N_CMP_PAD, HEAD_DIM), F32),
        grid=(4,),
        in_specs=[pl.BlockSpec((1, N_CMP_PAD, 1024), lambda i: (i, 0, 0)),
                  pl.BlockSpec((1, 1, 2048), lambda i: (i // 2, 0, 0)),
                  pl.BlockSpec((1, 2048, CMP_HIDDEN), lambda i: (i // 2, 0, 0)),
                  pl.BlockSpec((1, CMP_HIDDEN, HEAD_DIM), lambda i: (i // 2, 0, 0))],
        out_specs=pl.BlockSpec((1, N_CMP_PAD, HEAD_DIM), lambda i: (i, 0, 0)),
        compiler_params=_params(("arbitrary",)),
        name="compress",
    )(cmat, pe, w1, w2)


def _tile_update(s, m, acc, vt):
    m_new = jnp.maximum(m, jnp.max(s, axis=0, keepdims=True))
    alpha = jnp.exp2(m - m_new)
    p = jnp.exp2(s - m_new).astype(BF16)
    return m_new, alpha * acc + _dot(vt, p)


def _finish(acc):
    return acc[:HEAD_DIM] / jnp.maximum(acc[HEAD_DIM:HEAD_DIM + 1], 1e-30)


def _attn_kernel(qt_ref, gt_ref, kc_ref, vct_ref, ovt_ref, ksl_ref, vslt_ref, kw_ref, vwt_ref,
                 ct_ref, wt_ref, o_ref, s_scr, p_scr, sel_scr, wa_scr, wb_scr, sa_scr, sb_scr):
    qb = pl.program_id(0)
    q0 = qb * TQ
    qt = qt_ref[...]
    gt = gt_ref[...]
    drow = lax.broadcasted_iota(jnp.int32, (D_KV, TQ), 0)
    krow = lax.broadcasted_iota(jnp.int32, (TK, 1), 0)
    m0 = jnp.full((1, COLS), M_FLOOR, F32)
    acc0 = jnp.zeros((VROWS, COLS), F32)

    def pad_mask(first_row):
        return jnp.where(krow + first_row >= KPAD, 0.0, NEG)

    def real_row(first_row):
        return pl.multiple_of(jnp.maximum(first_row - KPAD, 0), TK)

    comb = []
    for g in range(N_KV):
        keep = (drow >= HEAD_DIM * g) & (drow < HEAD_DIM * (g + 1))
        q_g = jnp.concatenate(
            [jnp.where(keep, qt[h * D_KV:(h + 1) * D_KV, :], jnp.zeros((), BF16))
             for h in range(HPG)], axis=1)

        mcol = m0
        for ct in range(N_CMP_PAD // TK):
            rows = slice(ct * TK, (ct + 1) * TK)
            r0 = pl.multiple_of(jnp.clip(TK * ct - (TQ // CMP_STRIDE) * qb + CT_ZERO,
                                         0, CT_ROWS - TK), SUBLANES)
            s = _dot(kc_ref[rows, :], q_g) + ct_ref[g, pl.ds(r0, TK), :]
            s_scr[rows, :] = s
            mcol = jnp.maximum(mcol, jnp.max(s, axis=0, keepdims=True))
        lsum = jnp.zeros((1, COLS), F32)
        acc_c = jnp.zeros((HEAD_DIM, COLS), F32)
        for ct in range(N_CMP_PAD // TK):
            rows = slice(ct * TK, (ct + 1) * TK)
            p = jnp.exp2(s_scr[rows, :] - mcol)
            p_scr[rows, :] = p
            lsum = lsum + jnp.sum(p, axis=0, keepdims=True)
            acc_c = acc_c + _dot(vct_ref[g, :, rows], p.astype(BF16))
        rinv = 1.0 / jnp.maximum(lsum, 1e-30)
        o_c = acc_c * rinv
        imp = jnp.zeros((N_SLC, TQ), F32)
        for ct in range(N_CMP_PAD // TK):
            rows = slice(ct * TK, (ct + 1) * TK)
            pn = p_scr[rows, :] * rinv
            ps = pn[:, 0:TQ] + pn[:, TQ:2 * TQ] + pn[:, 2 * TQ:3 * TQ] + pn[:, 3 * TQ:4 * TQ]
            hi = ps.astype(BF16)
            lo = (ps - hi.astype(F32)).astype(BF16)
            ov = ovt_ref[:, rows]
            imp = imp + _dot(ov, hi) + _dot(ov, lo)

        m, acc = m0, acc0
        for a in range(3):
            first = q0 + TK * a
            s = _dot(kw_ref[pl.ds(real_row(first), TK), :], q_g)
            s = s + (wt_ref[g, TK * a:TK * (a + 1), :] + pad_mask(first))
            m, acc = _tile_update(s, m, acc, vwt_ref[g, jnp.maximum(qb + a - KPAD // TK, 0)])
        o_w = _finish(acc)

        cur = jnp.right_shift(q0 + lax.broadcasted_iota(jnp.int32, (1, TQ), 1), 6)
        jcol = lax.broadcasted_iota(jnp.int32, (N_SLC, 1), 0)
        forced = (jcol == 0) | (jcol == cur) | (jcol == cur - 1)
        sc = jnp.where(forced, TAKEN, jnp.where(jcol <= cur, imp, -1.0))
        jf = lax.broadcasted_iota(jnp.int32, (N_SLC, TQ), 0).astype(F32)
        for _ in range(SLC_TOPK - N_FORCED):
            mx = jnp.max(sc, axis=0, keepdims=True)
            idx = jnp.min(jnp.where(sc == mx, jf, 1e9), axis=0, keepdims=True)
            sc = jnp.where(jf == idx, TAKEN, sc)
        selneg = jnp.where(sc == TAKEN, 0.0, NEG)
        unused = jnp.zeros((SUBLANES - BLK_PER_TILE, TQ), F32)
        for kt in range(SEQ // TK):
            sel_scr[kt] = jnp.concatenate(
                [selneg[kt * BLK_PER_TILE:(kt + 1) * BLK_PER_TILE, :], unused], axis=0)
        sel_scr[SEQ // TK] = jnp.concatenate(
            [jnp.full((BLK_PER_TILE, TQ), NEG, F32), unused], axis=0)

        zrows = jnp.zeros((2 * D_KV - D_KV - 2 * SUBLANES, COLS), BF16)
        for w_scr in (wa_scr, wb_scr):
            w_scr[0:D_KV, :] = q_g
            w_scr[D_KV + 2 * SUBLANES:, :] = zrows

        def sel_scores(w_scr, first_row, sel_idx):
            blk = sel_scr[sel_idx]
            rows8 = jnp.concatenate([blk] * HPG, axis=1).astype(BF16)
            w_scr[D_KV:D_KV + 2 * SUBLANES, :] = jnp.concatenate(
                [rows8, jnp.zeros((SUBLANES, COLS), BF16)], axis=0)
            return _dot(ksl_ref[pl.ds(real_row(first_row), TK), :], w_scr[...])

        m, acc = m0, acc0
        for a, w_scr in ((0, wa_scr), (1, wb_scr)):
            first = q0 + TQ + TK * a
            s = sel_scores(w_scr, first, jnp.maximum(qb - 1 + a, 0))
            s = s + (wt_ref[g, TQ + TK * a:TQ + TK * (a + 1), :] + pad_mask(first))
            m, acc = _tile_update(s, m, acc, vslt_ref[g, jnp.maximum(qb - 1 + a, 0)])

        n_far = jnp.maximum(qb - 1, 0)

        def far_scores(w_scr, kt):
            valid = kt < n_far
            ks = jnp.where(valid, kt, 0)
            return sel_scores(w_scr, KPAD + ks * TK, jnp.where(valid, kt, SEQ // TK))

        def far_v(kt):
            return vslt_ref[g, jnp.where(kt < n_far, kt, 0)]

        bufs = ((sa_scr, wa_scr), (sb_scr, wb_scr))

        def issue_scores(buf, kt):
            s_scr_k, w_scr_k = bufs[buf]
            s = far_scores(w_scr_k, kt)
            s_scr_k[...] = s
            return jnp.max(s, axis=0, keepdims=True)

        def far_body(i, carry):
            m_i, acc_i, smax = carry
            for u in range(FAR_UNROLL):
                kt = i * FAR_UNROLL + u
                smax_nxt = issue_scores((u + 1) % 2, kt + 1)
                m_new = jnp.maximum(m_i, smax)
                alpha = jnp.exp2(m_i - m_new)
                p = jnp.exp2(bufs[u % 2][0][...] - m_new).astype(BF16)
                acc_i = alpha * acc_i + _dot(far_v(kt), p)
                m_i, smax = m_new, smax_nxt
            return m_i, acc_i, smax

        trips = (n_far + FAR_UNROLL - 1) // FAR_UNROLL
        m, acc, _ = lax.fori_loop(0, trips, far_body, (m, acc, issue_scores(0, 0)))
        o_s = _finish(acc)

        per_head = []
        for h in range(HPG):
            c0 = 3 * (g * HPG + h)
            cs = slice(h * TQ, (h + 1) * TQ)
            per_head.append(gt[c0:c0 + 1, :] * o_c[:, cs] + gt[c0 + 1:c0 + 2, :] * o_s[:, cs]
                            + gt[c0 + 2:c0 + 3, :] * o_w[:, cs])
        comb.append(per_head)

    for h in range(HPG):
        merged = jnp.concatenate([comb[0][h], comb[1][h]], axis=0)
        o_ref[0, h] = merged.T.astype(BF16)


def _attn(qt, gatet, kc, vct, ovt, ksl, vslt, kw, vwt, ctmpl, wtmpl):
    consts = [kc, vct, ovt, ksl, vslt, kw, vwt, ctmpl, wtmpl]
    return pl.pallas_call(
        _attn_kernel,
        out_shape=jax.ShapeDtypeStruct((SEQ // TQ, HPG, TQ, LANES), BF16),
        grid=(SEQ // TQ,),
        in_specs=[pl.BlockSpec((D_ATT, TQ), lambda i: (0, i)),
                  pl.BlockSpec((LANES, TQ), lambda i: (0, i))]
                 + [_const_spec(a.shape) for a in consts],
        out_specs=pl.BlockSpec((1, HPG, TQ, LANES), lambda i: (i, 0, 0, 0)),
        scratch_shapes=[pltpu.VMEM((N_CMP_PAD, COLS), F32), pltpu.VMEM((N_CMP_PAD, COLS), F32),
                        pltpu.VMEM((SEL_ENTRIES, SUBLANES, TQ), F32),
                        pltpu.VMEM((2 * D_KV, COLS), BF16), pltpu.VMEM((2 * D_KV, COLS), BF16),
                        pltpu.VMEM((TK, COLS), F32), pltpu.VMEM((TK, COLS), F32)],
        compiler_params=_params(("arbitrary",)),
        name="nsa_attn",
    )(qt, gatet, *consts)


def _t5_bucket(dist):
    max_exact = N_BUCKETS // 2
    d = jnp.maximum(dist, 0)
    df = jnp.maximum(d, 1).astype(F32)
    large = max_exact + (jnp.log(df / max_exact) / math.log(MAX_DIST / max_exact)
                         * (N_BUCKETS - max_exact)).astype(jnp.int32)
    large = jnp.minimum(large, N_BUCKETS - 1)
    return jnp.where(d < max_exact, d, large)


def _template_kernel(f_ref, gx_ref, wt_ref, ct_ref):
    wk, lanes = wt_ref.shape[1], f_ref.shape[1]
    n_band = CMP_R_HI - CMP_R_LO + 1
    lo = CT_ZERO + CMP_R_LO
    for hd in range(N_HEADS):
        g, cols = hd // HPG, slice((hd % HPG) * TQ, (hd % HPG + 1) * TQ)
        x = jnp.broadcast_to(f_ref[hd:hd + 1, :], (wk, lanes))
        y = pltpu.roll(x, lanes - (wk - 1), axis=1, stride=1, stride_axis=0)
        wt_ref[g, :, cols] = y[:, :TQ]
        xb = jnp.broadcast_to(gx_ref[hd:hd + 1, :], (n_band, lanes))
        yb = pltpu.roll(xb, lanes - CMP_STRIDE * (n_band - 1), axis=1,
                        stride=CMP_STRIDE, stride_axis=0)
        ct_ref[g, 0:lo, cols] = jnp.zeros((lo, TQ), F32)
        ct_ref[g, lo:lo + n_band, cols] = yb[:, :TQ]
        ct_ref[g, lo + n_band:, cols] = jnp.full((CT_ROWS - lo - n_band, TQ), NEG, F32)


def _templates(fext, gxext):
    return pl.pallas_call(
        _template_kernel,
        out_shape=[jax.ShapeDtypeStruct((N_KV, WINDOW + TQ, COLS), F32),
                   jax.ShapeDtypeStruct((N_KV, CT_ROWS, COLS), F32)],
        compiler_params=_params(None),
        name="templates",
    )(fext, gxext)


def _bias_templates(rel_bias):
    biasp = rel_bias[:, _t5_bucket(jnp.arange(MAX_DIST))] - rel_bias[:, N_BUCKETS - 1:]
    biasp = biasp * LOG2E

    def by_distance(n_neg, n_zero, n_neg_after):
        return jnp.concatenate(
            [jnp.full((N_HEADS, n_neg), NEG, F32), biasp, jnp.zeros((N_HEADS, n_zero), F32),
             jnp.full((N_HEADS, n_neg_after), NEG, F32)], axis=1)

    wk = WINDOW + TQ
    f = by_distance(TQ - 1, WINDOW - MAX_DIST, wk - WINDOW)
    fext = jnp.concatenate([f, jnp.zeros((N_HEADS, 1), F32)], axis=1)

    d_min = -(CMP_LEN - 1) - CMP_STRIDE * CMP_R_HI
    d_max = TQ - 1 - (CMP_LEN - 1) - CMP_STRIDE * CMP_R_LO
    gx = by_distance(-d_min, d_max + 1 - MAX_DIST, 0)
    gxext = jnp.pad(gx, ((0, 0), (0, fext.shape[1] - gx.shape[1])))
    wtmpl, ctmpl = _templates(fext, gxext)
    return ctmpl, wtmpl


def _overlap_t():
    cmp_start = jnp.arange(N_CMP_PAD) * CMP_STRIDE
    slc_start = jnp.arange(N_SLC) * SLC_LEN
    ov = ((cmp_start[None, :] < slc_start[:, None] + SLC_LEN)
          & (cmp_start[None, :] + CMP_LEN > slc_start[:, None])
          & (jnp.arange(N_CMP_PAD)[None, :] < N_CMP))
    return ov.astype(BF16)


def _vt_tiles(vt):
    ones = jnp.concatenate([jnp.ones((1, SEQ), BF16), jnp.zeros((VROWS - HEAD_DIM - 1, SEQ), BF16)])
    t = jnp.stack([jnp.concatenate([vt[g * HEAD_DIM:(g + 1) * HEAD_DIM], ones], axis=0)
                   for g in range(N_KV)])
    return t.reshape(N_KV, VROWS, SEQ // TK, TK).transpose(0, 2, 1, 3)


def kernel(x, c, w_ada, b_ada, g_ffn1, w_gu1, w_down1, g_mix, w_in, w_dw, b_dw, ln_g, ln_b,
           pe_k, pe_v, w_ck1, w_ck2, w_cv1, w_cv2, rel_bias, w_out, g_ffn2, w_gu2, w_down2,
           g_final):
    assert x.shape == (1, SEQ, D_MODEL) and w_ada.shape[0] == 1
    x2 = x[0]
    mod = _ada(c.reshape(D_MODEL, 1), w_ada[0], b_ada)
    x1 = _ffn1(x2, g_ffn1, mod, w_gu1[0].astype(BF16), w_down1[0].astype(BF16))

    wi = w_in[0]
    wq = wi[:, 1024:1536].reshape(D_MODEL, N_KV, HPG, HEAD_DIM).transpose(0, 2, 1, 3)
    wq = wq.reshape(D_MODEL, D_ATT) * (HEAD_DIM ** -0.5 * LOG2E)
    w_in_p = jnp.concatenate(
        [wi[:, :1024], wq, wi[:, 1536:], jnp.zeros((D_MODEL, D_IN_PAD - wi.shape[1]), F32)],
        axis=1).astype(BF16)
    hc, kcvc, ksl, kw, qt, vslt, vwt, gatet = _proj(x1, g_mix, mod, w_in_p)

    y_conv = _conv(hc, w_dw[0], b_dw, ln_g, ln_b)

    cmat = kcvc.reshape(SEQ, 4, HEAD_DIM).transpose(1, 0, 2).reshape(4, N_CMP_PAD, 1024)
    pe = jnp.stack([pe_k[0].reshape(1, -1), pe_v[0].reshape(1, -1)])
    w1 = jnp.stack([w_ck1[0], w_cv1[0]]).astype(BF16)
    w2 = jnp.stack([w_ck2[0], w_cv2[0]]).astype(BF16)
    cmp = _compress(cmat, pe, w1, w2)
    kc = jnp.concatenate([cmp[0], cmp[1]], axis=1).astype(BF16)
    vct = jnp.swapaxes(cmp[2:4], 1, 2).astype(BF16)

    ctmpl, wtmpl = _bias_templates(rel_bias)
    y_att = _attn(qt, gatet, kc, vct, _overlap_t(), ksl, _vt_tiles(vslt),
                  kw, _vt_tiles(vwt), ctmpl, wtmpl)

    wo = w_out[0]
    woa = wo[D_CONV:].reshape(N_KV, HPG, HEAD_DIM, D_MODEL).transpose(1, 0, 2, 3)
    woa = woa.reshape(HPG, LANES, D_MODEL).astype(BF16)
    out = _ffn2(x1, y_conv, y_att, wo[:D_CONV].astype(BF16), woa, mod,
                g_ffn2, w_gu2[0].astype(BF16), w_down2[0].astype(BF16),
                g_final.reshape(1, D_MODEL))
    return out[None]
```

```python
import functools
import math

import jax
import jax.numpy as jnp
from jax import lax
from jax.experimental import pallas as pl
from jax.experimental.pallas import tpu as pltpu
from jax.experimental.pallas import tpu_sc as plsc

F32 = jnp.float32
BF16 = jnp.bfloat16

D_MODEL = 1024
SEQ = 16384
D_CONV = 512
CONV_WIDTH = 31
N_HEADS = 8
N_KV = 2
HPG = 4
HEAD_DIM = 64
D_ATT = 512
D_KV = 128
CMP_LEN = 32
CMP_STRIDE = 16
CMP_HIDDEN = 128
N_CMP = (SEQ - CMP_LEN) // CMP_STRIDE + 1
N_CMP_PAD = 1024
SLC_LEN = 64
N_SLC = SEQ // SLC_LEN
SLC_TOPK = 16
WINDOW = 512
N_FORCED = 3
TAKEN = -2.0
N_BUCKETS = 32
MAX_DIST = 128
D_FF = 2816
FFN_RES = 0.5
EPS = 1e-6
NEG = -1e30
M_FLOOR = -1e29
LOG2E = math.log2(math.e)

V7X_VMEM_BYTES = 64 * 1024 * 1024
VMEM_LIMIT = V7X_VMEM_BYTES - 6 * 1024 * 1024
LANES = 128
SUBLANES = 8

TQ = 256
TK = 256
COLS = HPG * TQ
KPAD = WINDOW
VROWS = HEAD_DIM + 16
BLK_PER_TILE = TK // SLC_LEN
FAR_UNROLL = 4
SEL_ENTRIES = SEQ // TK + 2
CT_ZERO = 520
CT_ROWS = 800
CMP_R_LO, CMP_R_HI = -9, 14
FFN_TM = 512
FFN_TF = D_FF
PROJ_TM = 512
CONV_TM = 256
CONV_HALO = 32
CONV_CHUNK = 32
SC_CORES, SC_SUBCORES, SC_LANES = 2, 16, 16
SC_WORKERS = SC_CORES * SC_SUBCORES
SC_ROWS = 64
D_IN_PAD = 2432


def _params(sem):
    return pltpu.CompilerParams(dimension_semantics=sem, vmem_limit_bytes=VMEM_LIMIT)


def _const_spec(shape):
    nd = len(shape)
    return pl.BlockSpec(shape, lambda *_: (0,) * nd, pipeline_mode=pl.Buffered(1))


def _sigmoid(v):
    return 1.0 / (1.0 + jnp.exp(-v))


def _dot(a, b):
    return jnp.dot(a, b, preferred_element_type=F32)


def _ada_kernel(c_ref, w_ref, b_ref, o_ref):
    c = c_ref[...]
    sc = c * _sigmoid(c)
    o_ref[...] = jnp.sum(w_ref[...] * sc, axis=0, keepdims=True) + b_ref[...]


def _ada(c_col, w, b):
    n = w.shape[1]
    tn = n // 8
    return pl.pallas_call(
        _ada_kernel,
        out_shape=jax.ShapeDtypeStruct((1, n), F32),
        grid=(8,),
        in_specs=[pl.BlockSpec((D_MODEL, 1), lambda j: (0, 0)),
                  pl.BlockSpec((D_MODEL, tn), lambda j: (0, j)),
                  pl.BlockSpec((1, tn), lambda j: (0, j))],
        out_specs=pl.BlockSpec((1, tn), lambda j: (0, j)),
        compiler_params=_params(("arbitrary",)),
        name="ada",
    )(c_col, w, b)


def _rms_mod(x, g, sh, sc):
    ms = jnp.mean(x * x, axis=-1, keepdims=True)
    y = x * lax.rsqrt(ms + EPS) * g
    return y * (1.0 + sc) + sh


def _ffn_body(x, g_ref, sh_ref, sc_ref, gt_ref, wgu_ref, wd_ref):
    hb = _rms_mod(x, g_ref[...], sh_ref[...], sc_ref[...]).astype(BF16)
    acc = jnp.zeros((x.shape[0], D_MODEL), F32)
    for f in range(D_FF // FFN_TF):
        gg = _dot(hb, wgu_ref[:, f * FFN_TF:(f + 1) * FFN_TF])
        uu = _dot(hb, wgu_ref[:, D_FF + f * FFN_TF:D_FF + (f + 1) * FFN_TF])
        a = (gg * _sigmoid(gg) * uu).astype(BF16)
        acc = acc + _dot(a, wd_ref[f * FFN_TF:(f + 1) * FFN_TF, :])
    return x + (FFN_RES * gt_ref[...]) * acc


def _ffn1_kernel(x_ref, g_ref, sh_ref, sc_ref, gt_ref, wgu_ref, wd_ref, o_ref):
    o_ref[...] = _ffn_body(x_ref[...], g_ref, sh_ref, sc_ref, gt_ref, wgu_ref, wd_ref)


def _ffn2_kernel(x_ref, yp_ref, ya_ref, woc_ref, woa_ref, gt2_ref,
                 g_ref, sh_ref, sc_ref, gt_ref, wgu_ref, wd_ref, gf_ref, cg_ref, cbb_ref, o_ref):
    tm = x_ref.shape[0]
    cv = yp_ref[...]
    mu = jnp.mean(cv, axis=-1, keepdims=True)
    dlt = cv - mu
    var = jnp.mean(dlt * dlt, axis=-1, keepdims=True)
    yn = dlt * lax.rsqrt(var + EPS) * cg_ref[...] + cbb_ref[...]
    y = _dot((yn * _sigmoid(yn)).astype(BF16), woc_ref[...])
    for h in range(HPG):
        y = y + _dot(ya_ref[:, h].reshape(tm, LANES), woa_ref[h])
    x = x_ref[...] + gt2_ref[...] * y
    out = _ffn_body(x, g_ref, sh_ref, sc_ref, gt_ref, wgu_ref, wd_ref)
    ms = jnp.mean(out * out, axis=-1, keepdims=True)
    o_ref[...] = out * lax.rsqrt(ms + EPS) * gf_ref[...]


def _row_spec(tm, n):
    return pl.BlockSpec((tm, n), lambda i: (i, 0))


def _mod_spec(k):
    return pl.BlockSpec((1, D_MODEL), lambda *_: (0, k), pipeline_mode=pl.Buffered(1))


def _ffn1(x, g, mod, wgu, wd):
    vec = _const_spec((1, D_MODEL))
    return pl.pallas_call(
        _ffn1_kernel,
        out_shape=jax.ShapeDtypeStruct((SEQ, D_MODEL), F32),
        grid=(SEQ // FFN_TM,),
        in_specs=[_row_spec(FFN_TM, D_MODEL), vec, _mod_spec(0), _mod_spec(1), _mod_spec(2),
                  _const_spec((D_MODEL, 2 * D_FF)), _const_spec((D_FF, D_MODEL))],
        out_specs=_row_spec(FFN_TM, D_MODEL),
        compiler_params=_params(("arbitrary",)),
        name="ffn1",
    )(x, g, mod, mod, mod, wgu, wd)


def _ffn2(x, yp, ya, woc, woa, mod, g, wgu, wd, gf, cg, cbb):
    vec = _const_spec((1, D_MODEL))
    cvec = _const_spec((1, D_CONV))
    nt = FFN_TM // TQ
    return pl.pallas_call(
        _ffn2_kernel,
        out_shape=jax.ShapeDtypeStruct((SEQ, D_MODEL), F32),
        grid=(SEQ // FFN_TM,),
        in_specs=[_row_spec(FFN_TM, D_MODEL), _row_spec(FFN_TM, D_CONV),
                  pl.BlockSpec((nt, HPG, TQ, LANES), lambda i: (i, 0, 0, 0)),
                  _const_spec((D_CONV, D_MODEL)), _const_spec((HPG, LANES, D_MODEL)), _mod_spec(5),
                  vec, _mod_spec(6), _mod_spec(7), _mod_spec(8),
                  _const_spec((D_MODEL, 2 * D_FF)), _const_spec((D_FF, D_MODEL)), vec, cvec, cvec],
        out_specs=_row_spec(FFN_TM, D_MODEL),
        compiler_params=_params(("arbitrary",)),
        name="ffn2",
    )(x, yp, ya, woc, woa, mod, g, mod, mod, mod, wgu, wd, gf, cg, cbb)


def _proj_kernel(x_ref, g_ref, sh_ref, sc_ref, w_ref,
                 hc_ref, kcvc_ref, ksl_ref, kw_ref, qt_ref, vslt_ref, vwt_ref, gatet_ref):
    hb = _rms_mod(x_ref[...], g_ref[...], sh_ref[...], sc_ref[...]).astype(BF16)
    u = _dot(hb, w_ref[...])
    hc_ref[...] = u[:, 0:512] * _sigmoid(u[:, 512:1024])
    kcvc_ref[...] = u[:, 1536:1792]
    tm = x_ref.shape[0]
    blk = (lax.broadcasted_iota(jnp.int32, (tm, D_KV), 0) % TK) // SLC_LEN
    ind = (blk == lax.broadcasted_iota(jnp.int32, (tm, D_KV), 1)).astype(BF16)
    ksl_ref[...] = jnp.concatenate([u[:, 1792:1920].astype(BF16), ind], axis=1)
    kw_ref[...] = u[:, 2048:2176].astype(BF16)
    qt_ref[...] = u[:, 1024:1536].T.astype(BF16)
    vslt_ref[...] = u[:, 1920:2048].T.astype(BF16)
    vwt_ref[...] = u[:, 2176:2304].T.astype(BF16)
    gatet_ref[...] = _sigmoid(u[:, 2304:2432]).T


def _proj(x, g, mod, w):
    vec = _const_spec((1, D_MODEL))
    tm = PROJ_TM
    rows = [(D_CONV, F32), (2 * D_KV, F32), (2 * D_KV, BF16), (D_KV, BF16)]
    cols = [(D_ATT, BF16), (D_KV, BF16), (D_KV, BF16), (LANES, F32)]
    return pl.pallas_call(
        _proj_kernel,
        out_shape=[jax.ShapeDtypeStruct((SEQ, n), dt) for n, dt in rows]
                  + [jax.ShapeDtypeStruct((n, SEQ), dt) for n, dt in cols],
        grid=(SEQ // tm,),
        in_specs=[_row_spec(tm, D_MODEL), vec, _mod_spec(3), _mod_spec(4),
                  _const_spec((D_MODEL, D_IN_PAD))],
        out_specs=[_row_spec(tm, n) for n, _ in rows]
                  + [pl.BlockSpec((n, tm), lambda i: (0, i)) for n, _ in cols],
        compiler_params=_params(("arbitrary",)),
        name="proj",
    )(x, g, mod, mod, w)


def _conv_kernel(prev_ref, cur_ref, w_ref, b_ref, g_ref, bb_ref, o_ref, xs_ref):
    i = pl.program_id(0)
    tm = cur_ref.shape[0]
    prev = jnp.where(i > 0, prev_ref[...], 0.0)
    xx = jnp.concatenate([prev, cur_ref[...]], axis=0)
    span = tm + CONV_HALO - SUBLANES
    xs_ref[0] = xx
    for b in range(1, SUBLANES):
        xs_ref[b, 0:span, :] = xx[b:b + span, :]
    off = CONV_HALO - (CONV_WIDTH - 1)
    wts = w_ref[...]
    for r0 in range(0, tm, CONV_CHUNK):
        acc = jnp.zeros((CONV_CHUNK, D_CONV), F32) + b_ref[...]
        for w in range(CONV_WIDTH):
            a8, b = divmod(off + w, SUBLANES)
            lo = r0 + a8 * SUBLANES
            acc = acc + xs_ref[b, lo:lo + CONV_CHUNK, :] * wts[w:w + 1, :]
        mu = jnp.mean(acc, axis=-1, keepdims=True)
        dlt = acc - mu
        var = jnp.mean(dlt * dlt, axis=-1, keepdims=True)
        y = dlt * lax.rsqrt(var + EPS) * g_ref[...] + bb_ref[...]
        o_ref[r0:r0 + CONV_CHUNK, :] = (y * _sigmoid(y)).astype(BF16)


def _conv(hc, w, b, g, bb):
    tm = CONV_TM
    r = tm // CONV_HALO
    vec = _const_spec((1, D_CONV))
    return pl.pallas_call(
        _conv_kernel,
        out_shape=jax.ShapeDtypeStruct((SEQ, D_CONV), BF16),
        grid=(SEQ // tm,),
        in_specs=[pl.BlockSpec((CONV_HALO, D_CONV), lambda i: (jnp.maximum(i * r - 1, 0), 0)),
                  _row_spec(tm, D_CONV),
                  _const_spec((CONV_WIDTH, D_CONV)), vec, vec, vec],
        out_specs=_row_spec(tm, D_CONV),
        scratch_shapes=[pltpu.VMEM((SUBLANES, CONV_HALO + tm, D_CONV), F32)],
        compiler_params=_params(("arbitrary",)),
        name="conv",
    )(hc, hc, w, b, g, bb)


def _sc_conv(hc, w32, b8):
    rows_per_worker = SEQ // SC_WORKERS
    n_chunks = rows_per_worker // SC_ROWS
    halves = (tuple(range(0, 16)), tuple(range(16, CONV_WIDTH)))
    off = CONV_HALO - (CONV_WIDTH - 1)
    mesh = plsc.VectorSubcoreMesh(core_axis_name="c", subcore_axis_name="s")

    @functools.partial(
        pl.kernel, mesh=mesh,
        out_type=jax.ShapeDtypeStruct((SEQ, D_CONV), F32),
        scratch_types=[pltpu.VMEM((CONV_HALO + SC_ROWS, D_CONV), F32),
                       pltpu.VMEM((SC_ROWS, D_CONV), F32),
                       pltpu.VMEM((32, D_CONV), F32),
                       pltpu.VMEM((SUBLANES, D_CONV), F32)],
        name="sc_conv",
    )
    def k(hc_hbm, w_hbm, b_hbm, out_hbm, x_v, o_v, w_v, b_v):
        wid = lax.axis_index("s") * SC_CORES + lax.axis_index("c")
        pltpu.sync_copy(w_hbm, w_v)
        pltpu.sync_copy(b_hbm, b_v)

        @pl.loop(0, n_chunks)
        def _(ci):
            row0 = pl.multiple_of(wid * rows_per_worker + ci * SC_ROWS, SC_ROWS)

            @pl.when(row0 == 0)
            def _():
                @pl.loop(0, CONV_HALO)
                def _(r):
                    @pl.loop(0, D_CONV // SC_LANES)
                    def _(cg):
                        x_v[r, pl.ds(cg * SC_LANES, SC_LANES)] = jnp.zeros((SC_LANES,), F32)
                pltpu.sync_copy(hc_hbm.at[pl.ds(0, SC_ROWS)], x_v.at[pl.ds(CONV_HALO, SC_ROWS)])

            @pl.when(row0 > 0)
            def _():
                pltpu.sync_copy(hc_hbm.at[pl.ds(row0 - CONV_HALO, CONV_HALO + SC_ROWS)], x_v)

            @pl.loop(0, D_CONV // SC_LANES)
            def _(cg):
                lanes = pl.ds(cg * SC_LANES, SC_LANES)
                for hi, taps in enumerate(halves):
                    wv = [w_v[t, lanes] for t in taps]

                    @pl.loop(0, SC_ROWS)
                    def _(r):
                        acc = b_v[0, lanes] if hi == 0 else o_v[r, lanes]
                        for j, t in enumerate(taps):
                            acc = acc + x_v[r + off + t, lanes] * wv[j]
                        o_v[r, lanes] = acc

            pltpu.sync_copy(o_v, out_hbm.at[pl.ds(row0, SC_ROWS)])

    return k(hc, w32, b8)


def _compress_kernel(c_ref, pe_ref, w1_ref, w2_ref, o_ref):
    half = CMP_STRIDE * HEAD_DIM
    cm = c_ref[0]
    pe = pe_ref[0]
    top = (cm + pe[:, :half]).astype(BF16)
    bot = (cm + pe[:, half:]).astype(BF16)
    a = _dot(top, w1_ref[0, :half, :])
    b = _dot(bot, w1_ref[0, half:, :])
    b_up = jnp.concatenate([b[1:], jnp.zeros((1, CMP_HIDDEN), F32)], axis=0)
    pre = a + b_up
    hid = (pre * _sigmoid(pre)).astype(BF16)
    o_ref[0] = _dot(hid, w2_ref[0])


def _compress(cmat, pe, w1, w2):
    return pl.pallas_call(
        _compress_kernel,
        out_shape=jax.ShapeDtypeStruct((4, N_CMP_PAD, HEAD_DIM), F32),
        grid=(4,),
        in_specs=[pl.BlockSpec((1, N_CMP_PAD, 1024), lambda i: (i, 0, 0)),
                  pl.BlockSpec((1, 1, 2048), lambda i: (i // 2, 0, 0)),
                  pl.BlockSpec((1, 2048, CMP_HIDDEN), lambda i: (i // 2, 0, 0)),
                  pl.BlockSpec((1, CMP_HIDDEN, HEAD_DIM), lambda i: (i // 2, 0, 0))],
        out_specs=pl.BlockSpec((1, N_CMP_PAD, HEAD_DIM), lambda i: (i, 0, 0)),
        compiler_params=_params(("arbitrary",)),
        name="compress",
    )(cmat, pe, w1, w2)


def _tile_update(s, m, acc, vt):
    m_new = jnp.maximum(m, jnp.max(s, axis=0, keepdims=True))
    alpha = jnp.exp2(m - m_new)
    p = jnp.exp2(s - m_new).astype(BF16)
    return m_new, alpha * acc + _dot(vt, p)


def _finish(acc):
    return acc[:HEAD_DIM] / jnp.maximum(acc[HEAD_DIM:HEAD_DIM + 1], 1e-30)


def _attn_kernel(qt_ref, gt_ref, kc_ref, vct_ref, ovt_ref, ksl_ref, vslt_ref, kw_ref, vwt_ref,
                 ct_ref, wt_ref, o_ref, s_scr, p_scr, sel_scr, wa_scr, wb_scr, sa_scr, sb_scr):
    qb = pl.program_id(0)
    q0 = qb * TQ
    qt = qt_ref[...]
    gt = gt_ref[...]
    drow = lax.broadcasted_iota(jnp.int32, (D_KV, TQ), 0)
    krow = lax.broadcasted_iota(jnp.int32, (TK, 1), 0)
    m0 = jnp.full((1, COLS), M_FLOOR, F32)
    acc0 = jnp.zeros((VROWS, COLS), F32)

    def pad_mask(first_row):
        return jnp.where(krow + first_row >= KPAD, 0.0, NEG)

    def real_row(first_row):
        return pl.multiple_of(jnp.maximum(first_row - KPAD, 0), TK)

    comb = []
    for g in range(N_KV):
        keep = (drow >= HEAD_DIM * g) & (drow < HEAD_DIM * (g + 1))
        q_g = jnp.concatenate(
            [jnp.where(keep, qt[h * D_KV:(h + 1) * D_KV, :], jnp.zeros((), BF16))
             for h in range(HPG)], axis=1)

        mcol = m0
        for ct in range(N_CMP_PAD // TK):
            rows = slice(ct * TK, (ct + 1) * TK)
            r0 = pl.multiple_of(jnp.clip(TK * ct - (TQ // CMP_STRIDE) * qb + CT_ZERO,
                                         0, CT_ROWS - TK), SUBLANES)
            s = _dot(kc_ref[rows, :], q_g) + ct_ref[g, pl.ds(r0, TK), :]
            s_scr[rows, :] = s
            mcol = jnp.maximum(mcol, jnp.max(s, axis=0, keepdims=True))
        lsum = jnp.zeros((1, COLS), F32)
        acc_c = jnp.zeros((HEAD_DIM, COLS), F32)
        for ct in range(N_CMP_PAD // TK):
            rows = slice(ct * TK, (ct + 1) * TK)
            p = jnp.exp2(s_scr[rows, :] - mcol)
            p_scr[rows, :] = p
            lsum = lsum + jnp.sum(p, axis=0, keepdims=True)
            acc_c = acc_c + _dot(vct_ref[g, :, rows], p.astype(BF16))
        rinv = 1.0 / jnp.maximum(lsum, 1e-30)
        o_c = acc_c * rinv
        imp = jnp.zeros((N_SLC, TQ), F32)
        for ct in range(N_CMP_PAD // TK):
            rows = slice(ct * TK, (ct + 1) * TK)
            pn = p_scr[rows, :] * rinv
            ps = pn[:, 0:TQ] + pn[:, TQ:2 * TQ] + pn[:, 2 * TQ:3 * TQ] + pn[:, 3 * TQ:4 * TQ]
            hi = ps.astype(BF16)
            lo = (ps - hi.astype(F32)).astype(BF16)
            ov = ovt_ref[:, rows]
            imp = imp + _dot(ov, hi) + _dot(ov, lo)

        m, acc = m0, acc0
        for a in range(3):
            first = q0 + TK * a
            s = _dot(kw_ref[pl.ds(real_row(first), TK), :], q_g)
            s = s + (wt_ref[g, TK * a:TK * (a + 1), :] + pad_mask(first))
            m, acc = _tile_update(s, m, acc, vwt_ref[g, jnp.maximum(qb + a - KPAD // TK, 0)])
        o_w = _finish(acc)

        cur = jnp.right_shift(q0 + lax.broadcasted_iota(jnp.int32, (1, TQ), 1), 6)
        jcol = lax.broadcasted_iota(jnp.int32, (N_SLC, 1), 0)
        forced = (jcol == 0) | (jcol == cur) | (jcol == cur - 1)
        sc = jnp.where(forced, TAKEN, jnp.where(jcol <= cur, imp, -1.0))
        jf = lax.broadcasted_iota(jnp.int32, (N_SLC, TQ), 0).astype(F32)
        for _ in range(SLC_TOPK - N_FORCED):
            mx = jnp.max(sc, axis=0, keepdims=True)
            idx = jnp.min(jnp.where(sc == mx, jf, 1e9), axis=0, keepdims=True)
            sc = jnp.where(jf == idx, TAKEN, sc)
        selneg = jnp.where(sc == TAKEN, 0.0, NEG)
        unused = jnp.zeros((SUBLANES - BLK_PER_TILE, TQ), F32)
        for kt in range(SEQ // TK):
            sel_scr[kt] = jnp.concatenate(
                [selneg[kt * BLK_PER_TILE:(kt + 1) * BLK_PER_TILE, :], unused], axis=0)
        sel_scr[SEQ // TK] = jnp.concatenate(
            [jnp.full((BLK_PER_TILE, TQ), NEG, F32), unused], axis=0)

        zrows = jnp.zeros((2 * D_KV - D_KV - 2 * SUBLANES, COLS), BF16)
        for w_scr in (wa_scr, wb_scr):
            w_scr[0:D_KV, :] = q_g
            w_scr[D_KV + 2 * SUBLANES:, :] = zrows

        def sel_scores(w_scr, first_row, sel_idx):
            blk = sel_scr[sel_idx]
            rows8 = jnp.concatenate([blk] * HPG, axis=1).astype(BF16)
            w_scr[D_KV:D_KV + 2 * SUBLANES, :] = jnp.concatenate(
                [rows8, jnp.zeros((SUBLANES, COLS), BF16)], axis=0)
            return _dot(ksl_ref[pl.ds(real_row(first_row), TK), :], w_scr[...])

        m, acc = m0, acc0
        for a, w_scr in ((0, wa_scr), (1, wb_scr)):
            first = q0 + TQ + TK * a
            s = sel_scores(w_scr, first, jnp.maximum(qb - 1 + a, 0))
            s = s + (wt_ref[g, TQ + TK * a:TQ + TK * (a + 1), :] + pad_mask(first))
            m, acc = _tile_update(s, m, acc, vslt_ref[g, jnp.maximum(qb - 1 + a, 0)])

        n_far = jnp.maximum(qb - 1, 0)

        def far_scores(w_scr, kt):
            valid = kt < n_far
            ks = jnp.where(valid, kt, 0)
            return sel_scores(w_scr, KPAD + ks * TK, jnp.where(valid, kt, SEQ // TK))

        def far_v(kt):
            return vslt_ref[g, jnp.where(kt < n_far, kt, 0)]

        bufs = ((sa_scr, wa_scr), (sb_scr, wb_scr))

        def issue_scores(buf, kt):
            s_scr_k, w_scr_k = bufs[buf]
            s = far_scores(w_scr_k, kt)
            s_scr_k[...] = s
            return jnp.max(s, axis=0, keepdims=True)

        def far_body(i, carry):
            m_i, acc_i, smax = carry
            for u in range(FAR_UNROLL):
                kt = i * FAR_UNROLL + u
                smax_nxt = issue_scores((u + 1) % 2, kt + 1)
                m_new = jnp.maximum(m_i, smax)
                alpha = jnp.exp2(m_i - m_new)
                p = jnp.exp2(bufs[u % 2][0][...] - m_new).astype(BF16)
                acc_i = alpha * acc_i + _dot(far_v(kt), p)
                m_i, smax = m_new, smax_nxt
            return m_i, acc_i, smax

        trips = (n_far + FAR_UNROLL - 1) // FAR_UNROLL
        m, acc, _ = lax.fori_loop(0, trips, far_body, (m, acc, issue_scores(0, 0)))
        o_s = _finish(acc)

        per_head = []
        for h in range(HPG):
            c0 = 3 * (g * HPG + h)
            cs = slice(h * TQ, (h + 1) * TQ)
            per_head.append(gt[c0:c0 + 1, :] * o_c[:, cs] + gt[c0 + 1:c0 + 2, :] * o_s[:, cs]
                            + gt[c0 + 2:c0 + 3, :] * o_w[:, cs])
        comb.append(per_head)

    for h in range(HPG):
        merged = jnp.concatenate([comb[0][h], comb[1][h]], axis=0)
        o_ref[0, h] = merged.T.astype(BF16)


def _attn(qt, gatet, kc, vct, ovt, ksl, vslt, kw, vwt, ctmpl, wtmpl):
    consts = [kc, vct, ovt, ksl, vslt, kw, vwt, ctmpl, wtmpl]
    return pl.pallas_call(
        _attn_kernel,
        out_shape=jax.ShapeDtypeStruct((SEQ // TQ, HPG, TQ, LANES), BF16),
        grid=(SEQ // TQ,),
        in_specs=[pl.BlockSpec((D_ATT, TQ), lambda i: (0, i)),
                  pl.BlockSpec((LANES, TQ), lambda i: (0, i))]
                 + [_const_spec(a.shape) for a in consts],
        out_specs=pl.BlockSpec((1, HPG, TQ, LANES), lambda i: (i, 0, 0, 0)),
        scratch_shapes=[pltpu.VMEM((N_CMP_PAD, COLS), F32), pltpu.VMEM((N_CMP_PAD, COLS), F32),
                        pltpu.VMEM((SEL_ENTRIES, SUBLANES, TQ), F32),
                        pltpu.VMEM((2 * D_KV, COLS), BF16), pltpu.VMEM((2 * D_KV, COLS), BF16),
                        pltpu.VMEM((TK, COLS), F32), pltpu.VMEM((TK, COLS), F32)],
        compiler_params=_params(("arbitrary",)),
        name="nsa_attn",
    )(qt, gatet, *consts)


def _t5_bucket(dist):
    max_exact = N_BUCKETS // 2
    d = jnp.maximum(dist, 0)
    df = jnp.maximum(d, 1).astype(F32)
    large = max_exact + (jnp.log(df / max_exact) / math.log(MAX_DIST / max_exact)
                         * (N_BUCKETS - max_exact)).astype(jnp.int32)
    large = jnp.minimum(large, N_BUCKETS - 1)
    return jnp.where(d < max_exact, d, large)


def _template_kernel(f_ref, gx_ref, wt_ref, ct_ref):
    wk, lanes = wt_ref.shape[1], f_ref.shape[1]
    n_band = CMP_R_HI - CMP_R_LO + 1
    lo = CT_ZERO + CMP_R_LO
    for hd in range(N_HEADS):
        g, cols = hd // HPG, slice((hd % HPG) * TQ, (hd % HPG + 1) * TQ)
        x = jnp.broadcast_to(f_ref[hd:hd + 1, :], (wk, lanes))
        y = pltpu.roll(x, lanes - (wk - 1), axis=1, stride=1, stride_axis=0)
        wt_ref[g, :, cols] = y[:, :TQ]
        xb = jnp.broadcast_to(gx_ref[hd:hd + 1, :], (n_band, lanes))
        yb = pltpu.roll(xb, lanes - CMP_STRIDE * (n_band - 1), axis=1,
                        stride=CMP_STRIDE, stride_axis=0)
        ct_ref[g, 0:lo, cols] = jnp.zeros((lo, TQ), F32)
        ct_ref[g, lo:lo + n_band, cols] = yb[:, :TQ]
        ct_ref[g, lo + n_band:, cols] = jnp.full((CT_ROWS - lo - n_band, TQ), NEG, F32)


def _templates(fext, gxext):
    return pl.pallas_call(
        _template_kernel,
        out_shape=[jax.ShapeDtypeStruct((N_KV, WINDOW + TQ, COLS), F32),
                   jax.ShapeDtypeStruct((N_KV, CT_ROWS, COLS), F32)],
        compiler_params=_params(None),
        name="templates",
    )(fext, gxext)


def _bias_templates(rel_bias):
    biasp = rel_bias[:, _t5_bucket(jnp.arange(MAX_DIST))] - rel_bias[:, N_BUCKETS - 1:]
    biasp = biasp * LOG2E

    def by_distance(n_neg, n_zero, n_neg_after):
        return jnp.concatenate(
            [jnp.full((N_HEADS, n_neg), NEG, F32), biasp, jnp.zeros((N_HEADS, n_zero), F32),
             jnp.full((N_HEADS, n_neg_after), NEG, F32)], axis=1)

    wk = WINDOW + TQ
    f = by_distance(TQ - 1, WINDOW - MAX_DIST, wk - WINDOW)
    fext = jnp.concatenate([f, jnp.zeros((N_HEADS, 1), F32)], axis=1)

    d_min = -(CMP_LEN - 1) - CMP_STRIDE * CMP_R_HI
    d_max = TQ - 1 - (CMP_LEN - 1) - CMP_STRIDE * CMP_R_LO
    gx = by_distance(-d_min, d_max + 1 - MAX_DIST, 0)
    gxext = jnp.pad(gx, ((0, 0), (0, fext.shape[1] - gx.shape[1])))
    wtmpl, ctmpl = _templates(fext, gxext)
    return ctmpl, wtmpl


def _overlap_t():
    cmp_start = jnp.arange(N_CMP_PAD) * CMP_STRIDE
    slc_start = jnp.arange(N_SLC) * SLC_LEN
    ov = ((cmp_start[None, :] < slc_start[:, None] + SLC_LEN)
          & (cmp_start[None, :] + CMP_LEN > slc_start[:, None])
          & (jnp.arange(N_CMP_PAD)[None, :] < N_CMP))
    return ov.astype(BF16)


def _vt_tiles(vt):
    ones = jnp.concatenate([jnp.ones((1, SEQ), BF16), jnp.zeros((VROWS - HEAD_DIM - 1, SEQ), BF16)])
    t = jnp.stack([jnp.concatenate([vt[g * HEAD_DIM:(g + 1) * HEAD_DIM], ones], axis=0)
                   for g in range(N_KV)])
    return t.reshape(N_KV, VROWS, SEQ // TK, TK).transpose(0, 2, 1, 3)


def kernel(x, c, w_ada, b_ada, g_ffn1, w_gu1, w_down1, g_mix, w_in, w_dw, b_dw, ln_g, ln_b,
           pe_k, pe_v, w_ck1, w_ck2, w_cv1, w_cv2, rel_bias, w_out, g_ffn2, w_gu2, w_down2,
           g_final):
    assert x.shape == (1, SEQ, D_MODEL) and w_ada.shape[0] == 1
    x2 = x[0]
    mod = _ada(c.reshape(D_MODEL, 1), w_ada[0], b_ada)
    x1 = _ffn1(x2, g_ffn1, mod, w_gu1[0].astype(BF16), w_down1[0].astype(BF16))

    wi = w_in[0]
    wq = wi[:, 1024:1536].reshape(D_MODEL, N_KV, HPG, HEAD_DIM).transpose(0, 2, 1, 3)
    wq = wq.reshape(D_MODEL, D_ATT) * (HEAD_DIM ** -0.5 * LOG2E)
    w_in_p = jnp.concatenate(
        [wi[:, :1024], wq, wi[:, 1536:], jnp.zeros((D_MODEL, D_IN_PAD - wi.shape[1]), F32)],
        axis=1).astype(BF16)
    hc, kcvc, ksl, kw, qt, vslt, vwt, gatet = _proj(x1, g_mix, mod, w_in_p)

    y_conv = _sc_conv(hc, jnp.pad(w_dw[0], ((0, 32 - CONV_WIDTH), (0, 0))),
                      jnp.broadcast_to(b_dw, (SUBLANES, D_CONV)))

    cmat = kcvc.reshape(SEQ, 4, HEAD_DIM).transpose(1, 0, 2).reshape(4, N_CMP_PAD, 1024)
    pe = jnp.stack([pe_k[0].reshape(1, -1), pe_v[0].reshape(1, -1)])
    w1 = jnp.stack([w_ck1[0], w_cv1[0]]).astype(BF16)
    w2 = jnp.stack([w_ck2[0], w_cv2[0]]).astype(BF16)
    cmp = _compress(cmat, pe, w1, w2)
    kc = jnp.concatenate([cmp[0], cmp[1]], axis=1).astype(BF16)
    vct = jnp.swapaxes(cmp[2:4], 1, 2).astype(BF16)

    ctmpl, wtmpl = _bias_templates(rel_bias)
    y_att = _attn(qt, gatet, kc, vct, _overlap_t(), ksl, _vt_tiles(vslt),
                  kw, _vt_tiles(vwt), ctmpl, wtmpl)

    wo = w_out[0]
    woa = wo[D_CONV:].reshape(N_KV, HPG, HEAD_DIM, D_MODEL).transpose(1, 0, 2, 3)
    woa = woa.reshape(HPG, LANES, D_MODEL).astype(BF16)
    out = _ffn2(x1, y_conv, y_att, wo[:D_CONV].astype(BF16), woa, mod,
                g_ffn2, w_gu2[0].astype(BF16), w_down2[0].astype(BF16),
                g_final.reshape(1, D_MODEL), ln_g, ln_b)
    return out[None]
```

```python
import functools
import math

import jax
import jax.numpy as jnp
from jax import lax
from jax.experimental import pallas as pl
from jax.experimental.pallas import tpu as pltpu
from jax.experimental.pallas import tpu_sc as plsc

F32 = jnp.float32
BF16 = jnp.bfloat16

D_MODEL = 1024
SEQ = 16384
D_CONV = 512
CONV_WIDTH = 31
N_HEADS = 8
N_KV = 2
HPG = 4
HEAD_DIM = 64
D_ATT = 512
D_KV = 128
CMP_LEN = 32
CMP_STRIDE = 16
CMP_HIDDEN = 128
N_CMP = (SEQ - CMP_LEN) // CMP_STRIDE + 1
N_CMP_PAD = 1024
SLC_LEN = 64
N_SLC = SEQ // SLC_LEN
SLC_TOPK = 16
WINDOW = 512
N_FORCED = 3
TAKEN = -2.0
N_BUCKETS = 32
MAX_DIST = 128
D_FF = 2816
FFN_RES = 0.5
EPS = 1e-6
NEG = -1e30
M_FLOOR = -1e29
LOG2E = math.log2(math.e)

V7X_VMEM_BYTES = 64 * 1024 * 1024
VMEM_LIMIT = V7X_VMEM_BYTES - 6 * 1024 * 1024
LANES = 128
SUBLANES = 8

TQ = 256
TK = 256
COLS = HPG * TQ
KPAD = WINDOW
VROWS = HEAD_DIM + 16
BLK_PER_TILE = TK // SLC_LEN
FAR_UNROLL = 4
SEL_ENTRIES = SEQ // TK + 2
CT_ZERO = 520
CT_ROWS = 800
CMP_R_LO, CMP_R_HI = -9, 14
FFN_TM = 512
FFN_TF = D_FF
PROJ_TM = 512
CONV_HALO = 32
SC_CORES, SC_SUBCORES, SC_LANES = 2, 16, 16
SC_WORKERS = SC_CORES * SC_SUBCORES
SC_ROWS = 64
D_IN_PAD = 2432


def _params(sem):
    return pltpu.CompilerParams(dimension_semantics=sem, vmem_limit_bytes=VMEM_LIMIT)


def _const_spec(shape):
    nd = len(shape)
    return pl.BlockSpec(shape, lambda *_: (0,) * nd, pipeline_mode=pl.Buffered(1))


def _sigmoid(v):
    return 1.0 / (1.0 + jnp.exp(-v))


def _dot(a, b):
    return jnp.dot(a, b, preferred_element_type=F32)


def _ada_kernel(c_ref, w_ref, b_ref, o_ref):
    c = c_ref[...]
    sc = c * _sigmoid(c)
    o_ref[...] = jnp.sum(w_ref[...] * sc, axis=0, keepdims=True) + b_ref[...]


def _ada(c_col, w, b):
    n = w.shape[1]
    tn = n // 8
    return pl.pallas_call(
        _ada_kernel,
        out_shape=jax.ShapeDtypeStruct((1, n), F32),
        grid=(8,),
        in_specs=[pl.BlockSpec((D_MODEL, 1), lambda j: (0, 0)),
                  pl.BlockSpec((D_MODEL, tn), lambda j: (0, j)),
                  pl.BlockSpec((1, tn), lambda j: (0, j))],
        out_specs=pl.BlockSpec((1, tn), lambda j: (0, j)),
        compiler_params=_params(("arbitrary",)),
        name="ada",
    )(c_col, w, b)


def _rms_mod(x, g, sh, sc):
    ms = jnp.mean(x * x, axis=-1, keepdims=True)
    y = x * lax.rsqrt(ms + EPS) * g
    return y * (1.0 + sc) + sh


def _ffn_body(x, g_ref, sh_ref, sc_ref, gt_ref, wgu_ref, wd_ref):
    hb = _rms_mod(x, g_ref[...], sh_ref[...], sc_ref[...]).astype(BF16)
    acc = jnp.zeros((x.shape[0], D_MODEL), F32)
    for f in range(D_FF // FFN_TF):
        gg = _dot(hb, wgu_ref[:, f * FFN_TF:(f + 1) * FFN_TF])
        uu = _dot(hb, wgu_ref[:, D_FF + f * FFN_TF:D_FF + (f + 1) * FFN_TF])
        a = (gg * _sigmoid(gg) * uu).astype(BF16)
        acc = acc + _dot(a, wd_ref[f * FFN_TF:(f + 1) * FFN_TF, :])
    return x + (FFN_RES * gt_ref[...]) * acc


def _ffn1_kernel(x_ref, g_ref, sh_ref, sc_ref, gt_ref, wgu_ref, wd_ref, o_ref):
    o_ref[...] = _ffn_body(x_ref[...], g_ref, sh_ref, sc_ref, gt_ref, wgu_ref, wd_ref)


def _ffn2_kernel(x_ref, yp_ref, ya_ref, woc_ref, woa_ref, gt2_ref,
                 g_ref, sh_ref, sc_ref, gt_ref, wgu_ref, wd_ref, gf_ref, cg_ref, cbb_ref, o_ref):
    tm = x_ref.shape[0]
    cv = yp_ref[...]
    mu = jnp.mean(cv, axis=-1, keepdims=True)
    dlt = cv - mu
    var = jnp.mean(dlt * dlt, axis=-1, keepdims=True)
    yn = dlt * lax.rsqrt(var + EPS) * cg_ref[...] + cbb_ref[...]
    y = _dot((yn * _sigmoid(yn)).astype(BF16), woc_ref[...])
    for h in range(HPG):
        y = y + _dot(ya_ref[:, h].reshape(tm, LANES), woa_ref[h])
    x = x_ref[...] + gt2_ref[...] * y
    out = _ffn_body(x, g_ref, sh_ref, sc_ref, gt_ref, wgu_ref, wd_ref)
    ms = jnp.mean(out * out, axis=-1, keepdims=True)
    o_ref[...] = out * lax.rsqrt(ms + EPS) * gf_ref[...]


def _row_spec(tm, n):
    return pl.BlockSpec((tm, n), lambda i: (i, 0))


def _mod_spec(k):
    return pl.BlockSpec((1, D_MODEL), lambda *_: (0, k), pipeline_mode=pl.Buffered(1))


def _ffn1(x, g, mod, wgu, wd):
    vec = _const_spec((1, D_MODEL))
    return pl.pallas_call(
        _ffn1_kernel,
        out_shape=jax.ShapeDtypeStruct((SEQ, D_MODEL), F32),
        grid=(SEQ // FFN_TM,),
        in_specs=[_row_spec(FFN_TM, D_MODEL), vec, _mod_spec(0), _mod_spec(1), _mod_spec(2),
                  _const_spec((D_MODEL, 2 * D_FF)), _const_spec((D_FF, D_MODEL))],
        out_specs=_row_spec(FFN_TM, D_MODEL),
        compiler_params=_params(("arbitrary",)),
        name="ffn1",
    )(x, g, mod, mod, mod, wgu, wd)


def _ffn2(x, yp, ya, woc, woa, mod, g, wgu, wd, gf, cg, cbb):
    vec = _const_spec((1, D_MODEL))
    cvec = _const_spec((1, D_CONV))
    nt = FFN_TM // TQ
    return pl.pallas_call(
        _ffn2_kernel,
        out_shape=jax.ShapeDtypeStruct((SEQ, D_MODEL), F32),
        grid=(SEQ // FFN_TM,),
        in_specs=[_row_spec(FFN_TM, D_MODEL), _row_spec(FFN_TM, D_CONV),
                  pl.BlockSpec((nt, HPG, TQ, LANES), lambda i: (i, 0, 0, 0)),
                  _const_spec((D_CONV, D_MODEL)), _const_spec((HPG, LANES, D_MODEL)), _mod_spec(5),
                  vec, _mod_spec(6), _mod_spec(7), _mod_spec(8),
                  _const_spec((D_MODEL, 2 * D_FF)), _const_spec((D_FF, D_MODEL)), vec, cvec, cvec],
        out_specs=_row_spec(FFN_TM, D_MODEL),
        compiler_params=_params(("arbitrary",)),
        name="ffn2",
    )(x, yp, ya, woc, woa, mod, g, mod, mod, mod, wgu, wd, gf, cg, cbb)


def _proj_kernel(x_ref, g_ref, sh_ref, sc_ref, w_ref,
                 hc_ref, kcvc_ref, ksl_ref, kw_ref, qt_ref, gatet_ref, vslt_ref, vwt_ref):
    hb = _rms_mod(x_ref[...], g_ref[...], sh_ref[...], sc_ref[...]).astype(BF16)
    u = _dot(hb, w_ref[...])
    hc_ref[...] = u[:, 0:512] * _sigmoid(u[:, 512:1024])
    kcvc_ref[...] = u[:, 1536:1792]
    tm = x_ref.shape[0]
    blk = (lax.broadcasted_iota(jnp.int32, (tm, D_KV), 0) % TK) // SLC_LEN
    ind = (blk == lax.broadcasted_iota(jnp.int32, (tm, D_KV), 1)).astype(BF16)
    ksl_ref[...] = jnp.concatenate([u[:, 1792:1920].astype(BF16), ind], axis=1)
    kw_ref[...] = u[:, 2048:2176].astype(BF16)
    qt_ref[...] = u[:, 1024:1536].T.astype(BF16)
    gatet_ref[...] = _sigmoid(u[:, 2304:2432]).T
    tail = jnp.concatenate([jnp.ones((1, TK), BF16), jnp.zeros((VROWS - HEAD_DIM - 1, TK), BF16)])
    for vt_ref, c0 in ((vslt_ref, 1920), (vwt_ref, 2176)):
        vt = u[:, c0:c0 + D_KV].T.astype(BF16)
        for g in range(N_KV):
            for t in range(tm // TK):
                vt_ref[g, t] = jnp.concatenate(
                    [vt[g * HEAD_DIM:(g + 1) * HEAD_DIM, t * TK:(t + 1) * TK], tail], axis=0)


def _proj(x, g, mod, w):
    vec = _const_spec((1, D_MODEL))
    tm = PROJ_TM
    rows = [(D_CONV, F32), (2 * D_KV, F32), (2 * D_KV, BF16), (D_KV, BF16)]
    cols = [(D_ATT, BF16), (LANES, F32)]
    vt_shape = jax.ShapeDtypeStruct((N_KV, SEQ // TK, VROWS, TK), BF16)
    vt_spec = pl.BlockSpec((N_KV, tm // TK, VROWS, TK), lambda i: (0, i, 0, 0))
    return pl.pallas_call(
        _proj_kernel,
        out_shape=[jax.ShapeDtypeStruct((SEQ, n), dt) for n, dt in rows]
                  + [jax.ShapeDtypeStruct((n, SEQ), dt) for n, dt in cols] + [vt_shape, vt_shape],
        grid=(SEQ // tm,),
        in_specs=[_row_spec(tm, D_MODEL), vec, _mod_spec(3), _mod_spec(4),
                  _const_spec((D_MODEL, D_IN_PAD))],
        out_specs=[_row_spec(tm, n) for n, _ in rows]
                  + [pl.BlockSpec((n, tm), lambda i: (0, i)) for n, _ in cols] + [vt_spec, vt_spec],
        compiler_params=_params(("arbitrary",)),
        name="proj",
    )(x, g, mod, mod, w)


def _sc_conv(hc, w32, b8):
    rows_per_worker = SEQ // SC_WORKERS
    n_chunks = rows_per_worker // SC_ROWS
    halves = (tuple(range(0, 16)), tuple(range(16, CONV_WIDTH)))
    off = CONV_HALO - (CONV_WIDTH - 1)
    mesh = plsc.VectorSubcoreMesh(core_axis_name="c", subcore_axis_name="s")

    @functools.partial(
        pl.kernel, mesh=mesh,
        out_type=jax.ShapeDtypeStruct((SEQ, D_CONV), F32),
        scratch_types=[pltpu.VMEM((CONV_HALO + SC_ROWS, D_CONV), F32),
                       pltpu.VMEM((SC_ROWS, D_CONV), F32),
                       pltpu.VMEM((32, D_CONV), F32),
                       pltpu.VMEM((SUBLANES, D_CONV), F32)],
        name="sc_conv",
    )
    def k(hc_hbm, w_hbm, b_hbm, out_hbm, x_v, o_v, w_v, b_v):
        wid = lax.axis_index("s") * SC_CORES + lax.axis_index("c")
        pltpu.sync_copy(w_hbm, w_v)
        pltpu.sync_copy(b_hbm, b_v)

        @pl.loop(0, n_chunks)
        def _(ci):
            row0 = pl.multiple_of(wid * rows_per_worker + ci * SC_ROWS, SC_ROWS)

            @pl.when(row0 == 0)
            def _():
                @pl.loop(0, CONV_HALO)
                def _(r):
                    @pl.loop(0, D_CONV // SC_LANES)
                    def _(cg):
                        x_v[r, pl.ds(cg * SC_LANES, SC_LANES)] = jnp.zeros((SC_LANES,), F32)
                pltpu.sync_copy(hc_hbm.at[pl.ds(0, SC_ROWS)], x_v.at[pl.ds(CONV_HALO, SC_ROWS)])

            @pl.when(row0 > 0)
            def _():
                pltpu.sync_copy(hc_hbm.at[pl.ds(row0 - CONV_HALO, CONV_HALO + SC_ROWS)], x_v)

            @pl.loop(0, D_CONV // SC_LANES)
            def _(cg):
                lanes = pl.ds(cg * SC_LANES, SC_LANES)
                for hi, taps in enumerate(halves):
                    wv = [w_v[t, lanes] for t in taps]

                    @pl.loop(0, SC_ROWS)
                    def _(r):
                        acc = b_v[0, lanes] if hi == 0 else o_v[r, lanes]
                        for j, t in enumerate(taps):
                            acc = acc + x_v[r + off + t, lanes] * wv[j]
                        o_v[r, lanes] = acc

            pltpu.sync_copy(o_v, out_hbm.at[pl.ds(row0, SC_ROWS)])

    return k(hc, w32, b8)


def _compress_kernel(c_ref, pe_ref, w1_ref, w2_ref, o_ref):
    half = CMP_STRIDE * HEAD_DIM
    cm = c_ref[0]
    pe = pe_ref[0]
    top = (cm + pe[:, :half]).astype(BF16)
    bot = (cm + pe[:, half:]).astype(BF16)
    a = _dot(top, w1_ref[0, :half, :])
    b = _dot(bot, w1_ref[0, half:, :])
    b_up = jnp.concatenate([b[1:], jnp.zeros((1, CMP_HIDDEN), F32)], axis=0)
    pre = a + b_up
    hid = (pre * _sigmoid(pre)).astype(BF16)
    o_ref[0] = _dot(hid, w2_ref[0])


def _compress(cmat, pe, w1, w2):
    return pl.pallas_call(
        _compress_kernel,
        out_shape=jax.ShapeDtypeStruct((4, N_CMP_PAD, HEAD_DIM), F32),
        grid=(4,),
        in_specs=[pl.BlockSpec((1, N_CMP_PAD, 1024), lambda i: (i, 0, 0)),
                  pl.BlockSpec((1, 1, 2048), lambda i: (i // 2, 0, 0)),
                  pl.BlockSpec((1, 2048, CMP_HIDDEN), lambda i: (i // 2, 0, 0)),
                  pl.BlockSpec((1, CMP_HIDDEN, HEAD_DIM), lambda i: (i // 2, 0, 0))],
        out_specs=pl.BlockSpec((1, N_CMP_PAD, HEAD_DIM), lambda i: (i, 0, 0)),
        compiler_params=_params(("arbitrary",)),
        name="compress",
    )(cmat, pe, w1, w2)


def _tile_update(s, m, acc, vt):
    m_new = jnp.maximum(m, jnp.max(s, axis=0, keepdims=True))
    alpha = jnp.exp2(m - m_new)
    p = jnp.exp2(s - m_new).astype(BF16)
    return m_new, alpha * acc + _dot(vt, p)


def _finish(acc):
    return acc[:HEAD_DIM] / jnp.maximum(acc[HEAD_DIM:HEAD_DIM + 1], 1e-30)


def _attn_kernel(qt_ref, gt_ref, kc_ref, vct_ref, ovt_ref, ksl_ref, vslt_ref, kw_ref, vwt_ref,
                 ct_ref, wt_ref, o_ref, s_scr, p_scr, sel_scr, wa_scr, wb_scr, sa_scr, sb_scr):
    qb = pl.program_id(0)
    q0 = qb * TQ
    qt = qt_ref[...]
    gt = gt_ref[...]
    drow = lax.broadcasted_iota(jnp.int32, (D_KV, TQ), 0)
    krow = lax.broadcasted_iota(jnp.int32, (TK, 1), 0)
    m0 = jnp.full((1, COLS), M_FLOOR, F32)
    acc0 = jnp.zeros((VROWS, COLS), F32)

    def pad_mask(first_row):
        return jnp.where(krow + first_row >= KPAD, 0.0, NEG)

    def real_row(first_row):
        return pl.multiple_of(jnp.maximum(first_row - KPAD, 0), TK)

    comb = []
    for g in range(N_KV):
        keep = (drow >= HEAD_DIM * g) & (drow < HEAD_DIM * (g + 1))
        q_g = jnp.concatenate(
            [jnp.where(keep, qt[h * D_KV:(h + 1) * D_KV, :], jnp.zeros((), BF16))
             for h in range(HPG)], axis=1)

        mcol = m0
        for ct in range(N_CMP_PAD // TK):
            rows = slice(ct * TK, (ct + 1) * TK)
            r0 = pl.multiple_of(jnp.clip(TK * ct - (TQ // CMP_STRIDE) * qb + CT_ZERO,
                                         0, CT_ROWS - TK), SUBLANES)
            s = _dot(kc_ref[rows, :], q_g) + ct_ref[g, pl.ds(r0, TK), :]
            s_scr[rows, :] = s
            mcol = jnp.maximum(mcol, jnp.max(s, axis=0, keepdims=True))
        lsum = jnp.zeros((1, COLS), F32)
        acc_c = jnp.zeros((HEAD_DIM, COLS), F32)
        for ct in range(N_CMP_PAD // TK):
            rows = slice(ct * TK, (ct + 1) * TK)
            p = jnp.exp2(s_scr[rows, :] - mcol)
            p_scr[rows, :] = p
            lsum = lsum + jnp.sum(p, axis=0, keepdims=True)
            acc_c = acc_c + _dot(vct_ref[g, :, rows], p.astype(BF16))
        rinv = 1.0 / jnp.maximum(lsum, 1e-30)
        o_c = acc_c * rinv
        imp = jnp.zeros((N_SLC, TQ), F32)
        for ct in range(N_CMP_PAD // TK):
            rows = slice(ct * TK, (ct + 1) * TK)
            pn = p_scr[rows, :] * rinv
            ps = pn[:, 0:TQ] + pn[:, TQ:2 * TQ] + pn[:, 2 * TQ:3 * TQ] + pn[:, 3 * TQ:4 * TQ]
            hi = ps.astype(BF16)
            lo = (ps - hi.astype(F32)).astype(BF16)
            ov = ovt_ref[:, rows]
            imp = imp + _dot(ov, hi) + _dot(ov, lo)

        m, acc = m0, acc0
        for a in range(3):
            first = q0 + TK * a
            s = _dot(kw_ref[pl.ds(real_row(first), TK), :], q_g)
            s = s + (wt_ref[g, TK * a:TK * (a + 1), :] + pad_mask(first))
            m, acc = _tile_update(s, m, acc, vwt_ref[g, jnp.maximum(qb + a - KPAD // TK, 0)])
        o_w = _finish(acc)

        cur = jnp.right_shift(q0 + lax.broadcasted_iota(jnp.int32, (1, TQ), 1), 6)
        jcol = lax.broadcasted_iota(jnp.int32, (N_SLC, 1), 0)
        forced = (jcol == 0) | (jcol == cur) | (jcol == cur - 1)
        sc = jnp.where(forced, TAKEN, jnp.where(jcol <= cur, imp, -1.0))
        jf = lax.broadcasted_iota(jnp.int32, (N_SLC, TQ), 0).astype(F32)
        for _ in range(SLC_TOPK - N_FORCED):
            mx = jnp.max(sc, axis=0, keepdims=True)
            idx = jnp.min(jnp.where(sc == mx, jf, 1e9), axis=0, keepdims=True)
            sc = jnp.where(jf == idx, TAKEN, sc)
        selneg = jnp.where(sc == TAKEN, 0.0, NEG)
        unused = jnp.zeros((SUBLANES - BLK_PER_TILE, TQ), F32)
        for kt in range(SEQ // TK):
            sel_scr[kt] = jnp.concatenate(
                [selneg[kt * BLK_PER_TILE:(kt + 1) * BLK_PER_TILE, :], unused], axis=0)
        sel_scr[SEQ // TK] = jnp.concatenate(
            [jnp.full((BLK_PER_TILE, TQ), NEG, F32), unused], axis=0)

        zrows = jnp.zeros((2 * D_KV - D_KV - 2 * SUBLANES, COLS), BF16)
        for w_scr in (wa_scr, wb_scr):
            w_scr[0:D_KV, :] = q_g
            w_scr[D_KV + 2 * SUBLANES:, :] = zrows

        def sel_scores(w_scr, first_row, sel_idx):
            blk = sel_scr[sel_idx]
            rows8 = jnp.concatenate([blk] * HPG, axis=1).astype(BF16)
            w_scr[D_KV:D_KV + 2 * SUBLANES, :] = jnp.concatenate(
                [rows8, jnp.zeros((SUBLANES, COLS), BF16)], axis=0)
            return _dot(ksl_ref[pl.ds(real_row(first_row), TK), :], w_scr[...])

        m, acc = m0, acc0
        for a, w_scr in ((0, wa_scr), (1, wb_scr)):
            first = q0 + TQ + TK * a
            s = sel_scores(w_scr, first, jnp.maximum(qb - 1 + a, 0))
            s = s + (wt_ref[g, TQ + TK * a:TQ + TK * (a + 1), :] + pad_mask(first))
            m, acc = _tile_update(s, m, acc, vslt_ref[g, jnp.maximum(qb - 1 + a, 0)])

        n_far = jnp.maximum(qb - 1, 0)

        def far_scores(w_scr, kt):
            valid = kt < n_far
            ks = jnp.where(valid, kt, 0)
            return sel_scores(w_scr, KPAD + ks * TK, jnp.where(valid, kt, SEQ // TK))

        def far_v(kt):
            return vslt_ref[g, jnp.where(kt < n_far, kt, 0)]

        bufs = ((sa_scr, wa_scr), (sb_scr, wb_scr))

        def issue_scores(buf, kt):
            s_scr_k, w_scr_k = bufs[buf]
            s = far_scores(w_scr_k, kt)
            s_scr_k[...] = s
            return jnp.max(s, axis=0, keepdims=True)

        def far_body(i, carry):
            m_i, acc_i, smax = carry
            for u in range(FAR_UNROLL):
                kt = i * FAR_UNROLL + u
                smax_nxt = issue_scores((u + 1) % 2, kt + 1)
                m_new = jnp.maximum(m_i, smax)
                alpha = jnp.exp2(m_i - m_new)
                p = jnp.exp2(bufs[u % 2][0][...] - m_new).astype(BF16)
                acc_i = alpha * acc_i + _dot(far_v(kt), p)
                m_i, smax = m_new, smax_nxt
            return m_i, acc_i, smax

        trips = (n_far + FAR_UNROLL - 1) // FAR_UNROLL
        m, acc, _ = lax.fori_loop(0, trips, far_body, (m, acc, issue_scores(0, 0)))
        o_s = _finish(acc)

        per_head = []
        for h in range(HPG):
            c0 = 3 * (g * HPG + h)
            cs = slice(h * TQ, (h + 1) * TQ)
            per_head.append(gt[c0:c0 + 1, :] * o_c[:, cs] + gt[c0 + 1:c0 + 2, :] * o_s[:, cs]
                            + gt[c0 + 2:c0 + 3, :] * o_w[:, cs])
        comb.append(per_head)

    for h in range(HPG):
        merged = jnp.concatenate([comb[0][h], comb[1][h]], axis=0)
        o_ref[0, h] = merged.T.astype(BF16)


def _attn(qt, gatet, kc, vct, ovt, ksl, vslt, kw, vwt, ctmpl, wtmpl):
    consts = [kc, vct, ovt, ksl, vslt, kw, vwt, ctmpl, wtmpl]
    return pl.pallas_call(
        _attn_kernel,
        out_shape=jax.ShapeDtypeStruct((SEQ // TQ, HPG, TQ, LANES), BF16),
        grid=(SEQ // TQ,),
        in_specs=[pl.BlockSpec((D_ATT, TQ), lambda i: (0, i)),
                  pl.BlockSpec((LANES, TQ), lambda i: (0, i))]
                 + [_const_spec(a.shape) for a in consts],
        out_specs=pl.BlockSpec((1, HPG, TQ, LANES), lambda i: (i, 0, 0, 0)),
        scratch_shapes=[pltpu.VMEM((N_CMP_PAD, COLS), F32), pltpu.VMEM((N_CMP_PAD, COLS), F32),
                        pltpu.VMEM((SEL_ENTRIES, SUBLANES, TQ), F32),
                        pltpu.VMEM((2 * D_KV, COLS), BF16), pltpu.VMEM((2 * D_KV, COLS), BF16),
                        pltpu.VMEM((TK, COLS), F32), pltpu.VMEM((TK, COLS), F32)],
        compiler_params=_params(("arbitrary",)),
        name="nsa_attn",
    )(qt, gatet, *consts)


def _t5_bucket(dist):
    max_exact = N_BUCKETS // 2
    d = jnp.maximum(dist, 0)
    df = jnp.maximum(d, 1).astype(F32)
    large = max_exact + (jnp.log(df / max_exact) / math.log(MAX_DIST / max_exact)
                         * (N_BUCKETS - max_exact)).astype(jnp.int32)
    large = jnp.minimum(large, N_BUCKETS - 1)
    return jnp.where(d < max_exact, d, large)


def _template_kernel(f_ref, gx_ref, wt_ref, ct_ref):
    wk, lanes = wt_ref.shape[1], f_ref.shape[1]
    n_band = CMP_R_HI - CMP_R_LO + 1
    lo = CT_ZERO + CMP_R_LO
    for hd in range(N_HEADS):
        g, cols = hd // HPG, slice((hd % HPG) * TQ, (hd % HPG + 1) * TQ)
        x = jnp.broadcast_to(f_ref[hd:hd + 1, :], (wk, lanes))
        y = pltpu.roll(x, lanes - (wk - 1), axis=1, stride=1, stride_axis=0)
        wt_ref[g, :, cols] = y[:, :TQ]
        xb = jnp.broadcast_to(gx_ref[hd:hd + 1, :], (n_band, lanes))
        yb = pltpu.roll(xb, lanes - CMP_STRIDE * (n_band - 1), axis=1,
                        stride=CMP_STRIDE, stride_axis=0)
        ct_ref[g, 0:lo, cols] = jnp.zeros((lo, TQ), F32)
        ct_ref[g, lo:lo + n_band, cols] = yb[:, :TQ]
        ct_ref[g, lo + n_band:, cols] = jnp.full((CT_ROWS - lo - n_band, TQ), NEG, F32)


def _templates(fext, gxext):
    return pl.pallas_call(
        _template_kernel,
        out_shape=[jax.ShapeDtypeStruct((N_KV, WINDOW + TQ, COLS), F32),
                   jax.ShapeDtypeStruct((N_KV, CT_ROWS, COLS), F32)],
        compiler_params=_params(None),
        name="templates",
    )(fext, gxext)


def _bias_templates(rel_bias):
    biasp = rel_bias[:, _t5_bucket(jnp.arange(MAX_DIST))] - rel_bias[:, N_BUCKETS - 1:]
    biasp = biasp * LOG2E

    def by_distance(n_neg, n_zero, n_neg_after):
        return jnp.concatenate(
            [jnp.full((N_HEADS, n_neg), NEG, F32), biasp, jnp.zeros((N_HEADS, n_zero), F32),
             jnp.full((N_HEADS, n_neg_after), NEG, F32)], axis=1)

    wk = WINDOW + TQ
    f = by_distance(TQ - 1, WINDOW - MAX_DIST, wk - WINDOW)
    fext = jnp.concatenate([f, jnp.zeros((N_HEADS, 1), F32)], axis=1)

    d_min = -(CMP_LEN - 1) - CMP_STRIDE * CMP_R_HI
    d_max = TQ - 1 - (CMP_LEN - 1) - CMP_STRIDE * CMP_R_LO
    gx = by_distance(-d_min, d_max + 1 - MAX_DIST, 0)
    gxext = jnp.pad(gx, ((0, 0), (0, fext.shape[1] - gx.shape[1])))
    wtmpl, ctmpl = _templates(fext, gxext)
    return ctmpl, wtmpl


def _overlap_t():
    cmp_start = jnp.arange(N_CMP_PAD) * CMP_STRIDE
    slc_start = jnp.arange(N_SLC) * SLC_LEN
    ov = ((cmp_start[None, :] < slc_start[:, None] + SLC_LEN)
          & (cmp_start[None, :] + CMP_LEN > slc_start[:, None])
          & (jnp.arange(N_CMP_PAD)[None, :] < N_CMP))
    return ov.astype(BF16)


def kernel(x, c, w_ada, b_ada, g_ffn1, w_gu1, w_down1, g_mix, w_in, w_dw, b_dw, ln_g, ln_b,
           pe_k, pe_v, w_ck1, w_ck2, w_cv1, w_cv2, rel_bias, w_out, g_ffn2, w_gu2, w_down2,
           g_final):
    assert x.shape == (1, SEQ, D_MODEL) and w_ada.shape[0] == 1
    x2 = x[0]
    mod = _ada(c.reshape(D_MODEL, 1), w_ada[0], b_ada)
    x1 = _ffn1(x2, g_ffn1, mod, w_gu1[0].astype(BF16), w_down1[0].astype(BF16))

    wi = w_in[0]
    wq = wi[:, 1024:1536].reshape(D_MODEL, N_KV, HPG, HEAD_DIM).transpose(0, 2, 1, 3)
    wq = wq.reshape(D_MODEL, D_ATT) * (HEAD_DIM ** -0.5 * LOG2E)
    w_in_p = jnp.concatenate(
        [wi[:, :1024], wq, wi[:, 1536:], jnp.zeros((D_MODEL, D_IN_PAD - wi.shape[1]), F32)],
        axis=1).astype(BF16)
    hc, kcvc, ksl, kw, qt, gatet, vslt, vwt = _proj(x1, g_mix, mod, w_in_p)

    y_conv = _sc_conv(hc, jnp.pad(w_dw[0], ((0, 32 - CONV_WIDTH), (0, 0))),
                      jnp.broadcast_to(b_dw, (SUBLANES, D_CONV)))

    cmat = kcvc.reshape(SEQ, 4, HEAD_DIM).transpose(1, 0, 2).reshape(4, N_CMP_PAD, 1024)
    pe = jnp.stack([pe_k[0].reshape(1, -1), pe_v[0].reshape(1, -1)])
    w1 = jnp.stack([w_ck1[0], w_cv1[0]]).astype(BF16)
    w2 = jnp.stack([w_ck2[0], w_cv2[0]]).astype(BF16)
    cmp = _compress(cmat, pe, w1, w2)
    kc = jnp.concatenate([cmp[0], cmp[1]], axis=1).astype(BF16)
    vct = jnp.swapaxes(cmp[2:4], 1, 2).astype(BF16)

    ctmpl, wtmpl = _bias_templates(rel_bias)
    y_att = _attn(qt, gatet, kc, vct, _overlap_t(), ksl, vslt, kw, vwt, ctmpl, wtmpl)

    wo = w_out[0]
    woa = wo[D_CONV:].reshape(N_KV, HPG, HEAD_DIM, D_MODEL).transpose(1, 0, 2, 3)
    woa = woa.reshape(HPG, LANES, D_MODEL).astype(BF16)
    out = _ffn2(x1, y_conv, y_att, wo[:D_CONV].astype(BF16), woa, mod,
                g_ffn2, w_gu2[0].astype(BF16), w_down2[0].astype(BF16),
                g_final.reshape(1, D_MODEL), ln_g, ln_b)
    return out[None]
```

```python
import functools
import math

import jax
import jax.numpy as jnp
from jax import lax
from jax.experimental import pallas as pl
from jax.experimental.pallas import tpu as pltpu
from jax.experimental.pallas import tpu_sc as plsc

F32 = jnp.float32
BF16 = jnp.bfloat16

D_MODEL = 1024
SEQ = 16384
D_CONV = 512
CONV_WIDTH = 31
N_HEADS = 8
N_KV = 2
HPG = 4
HEAD_DIM = 64
D_ATT = 512
D_KV = 128
CMP_LEN = 32
CMP_STRIDE = 16
CMP_HIDDEN = 128
N_CMP = (SEQ - CMP_LEN) // CMP_STRIDE + 1
N_CMP_PAD = 1024
SLC_LEN = 64
N_SLC = SEQ // SLC_LEN
SLC_TOPK = 16
WINDOW = 512
N_FORCED = 3
TAKEN = -2.0
N_BUCKETS = 32
MAX_DIST = 128
D_FF = 2816
FFN_RES = 0.5
EPS = 1e-6
NEG = -1e30
M_FLOOR = -1e29
LOG2E = math.log2(math.e)

V7X_VMEM_BYTES = 64 * 1024 * 1024
VMEM_LIMIT = V7X_VMEM_BYTES - 6 * 1024 * 1024
LANES = 128
SUBLANES = 8

TQ = 256
TK = 256
COLS = HPG * TQ
KPAD = WINDOW
VROWS = HEAD_DIM + 16
BLK_PER_TILE = TK // SLC_LEN
FAR_UNROLL = 4
SEL_ENTRIES = SEQ // TK + 2
CMP_R_HI = (TQ - CMP_LEN) // CMP_STRIDE
CMP_R_LO = -((MAX_DIST + CMP_LEN - 2) // CMP_STRIDE)
CT_ZERO = TK + -(CMP_R_LO // SUBLANES) * SUBLANES
CT_ROWS = CT_ZERO + -(-(CMP_R_HI + 1) // SUBLANES) * SUBLANES + TK
FFN_TM = 512
FFN_TF = D_FF
PROJ_TM = 512
CONV_HALO = 32
SC_CORES, SC_SUBCORES, SC_LANES = 2, 16, 16
SC_WORKERS = SC_CORES * SC_SUBCORES
SC_ROWS = 64
D_IN_PAD = 2432


def _params(sem):
    return pltpu.CompilerParams(dimension_semantics=sem, vmem_limit_bytes=VMEM_LIMIT)


def _const_spec(shape):
    nd = len(shape)
    return pl.BlockSpec(shape, lambda *_: (0,) * nd, pipeline_mode=pl.Buffered(1))


def _sigmoid(v):
    return 1.0 / (1.0 + jnp.exp(-v))


def _dot(a, b):
    return jnp.dot(a, b, preferred_element_type=F32)


def _ada_kernel(c_ref, w_ref, b_ref, o_ref):
    c = c_ref[...]
    sc = c * _sigmoid(c)
    o_ref[...] = jnp.sum(w_ref[...] * sc, axis=0, keepdims=True) + b_ref[...]


def _ada(c_col, w, b):
    n = w.shape[1]
    tn = n // 8
    return pl.pallas_call(
        _ada_kernel,
        out_shape=jax.ShapeDtypeStruct((1, n), F32),
        grid=(8,),
        in_specs=[pl.BlockSpec((D_MODEL, 1), lambda j: (0, 0)),
                  pl.BlockSpec((D_MODEL, tn), lambda j: (0, j)),
                  pl.BlockSpec((1, tn), lambda j: (0, j))],
        out_specs=pl.BlockSpec((1, tn), lambda j: (0, j)),
        compiler_params=_params(("arbitrary",)),
        name="ada",
    )(c_col, w, b)


def _rms_mod(x, g, sh, sc):
    ms = jnp.mean(x * x, axis=-1, keepdims=True)
    y = x * lax.rsqrt(ms + EPS) * g
    return y * (1.0 + sc) + sh


def _ffn_body(x, g_ref, sh_ref, sc_ref, gt_ref, wgu_ref, wd_ref):
    hb = _rms_mod(x, g_ref[...], sh_ref[...], sc_ref[...]).astype(BF16)
    acc = jnp.zeros((x.shape[0], D_MODEL), F32)
    for f in range(D_FF // FFN_TF):
        gg = _dot(hb, wgu_ref[:, f * FFN_TF:(f + 1) * FFN_TF])
        uu = _dot(hb, wgu_ref[:, D_FF + f * FFN_TF:D_FF + (f + 1) * FFN_TF])
        a = (gg * _sigmoid(gg) * uu).astype(BF16)
        acc = acc + _dot(a, wd_ref[f * FFN_TF:(f + 1) * FFN_TF, :])
    return x + (FFN_RES * gt_ref[...]) * acc


def _ffn1_kernel(x_ref, g_ref, sh_ref, sc_ref, gt_ref, wgu_ref, wd_ref, o_ref):
    o_ref[...] = _ffn_body(x_ref[...], g_ref, sh_ref, sc_ref, gt_ref, wgu_ref, wd_ref)


def _ffn2_kernel(x_ref, yp_ref, ya_ref, woc_ref, woa_ref, gt2_ref,
                 g_ref, sh_ref, sc_ref, gt_ref, wgu_ref, wd_ref, gf_ref, cg_ref, cbb_ref, o_ref):
    tm = x_ref.shape[0]
    cv = yp_ref[...]
    mu = jnp.mean(cv, axis=-1, keepdims=True)
    dlt = cv - mu
    var = jnp.mean(dlt * dlt, axis=-1, keepdims=True)
    yn = dlt * lax.rsqrt(var + EPS) * cg_ref[...] + cbb_ref[...]
    y = _dot((yn * _sigmoid(yn)).astype(BF16), woc_ref[...])
    for h in range(HPG):
        y = y + _dot(ya_ref[:, h].reshape(tm, LANES), woa_ref[h])
    x = x_ref[...] + gt2_ref[...] * y
    out = _ffn_body(x, g_ref, sh_ref, sc_ref, gt_ref, wgu_ref, wd_ref)
    ms = jnp.mean(out * out, axis=-1, keepdims=True)
    o_ref[...] = out * lax.rsqrt(ms + EPS) * gf_ref[...]


def _row_spec(tm, n):
    return pl.BlockSpec((tm, n), lambda i: (i, 0))


def _mod_spec(k):
    return pl.BlockSpec((1, D_MODEL), lambda *_: (0, k), pipeline_mode=pl.Buffered(1))


def _ffn1(x, g, mod, wgu, wd):
    vec = _const_spec((1, D_MODEL))
    return pl.pallas_call(
        _ffn1_kernel,
        out_shape=jax.ShapeDtypeStruct((SEQ, D_MODEL), F32),
        grid=(SEQ // FFN_TM,),
        in_specs=[_row_spec(FFN_TM, D_MODEL), vec, _mod_spec(0), _mod_spec(1), _mod_spec(2),
                  _const_spec((D_MODEL, 2 * D_FF)), _const_spec((D_FF, D_MODEL))],
        out_specs=_row_spec(FFN_TM, D_MODEL),
        compiler_params=_params(("arbitrary",)),
        name="ffn1",
    )(x, g, mod, mod, mod, wgu, wd)


def _ffn2(x, yp, ya, woc, woa, mod, g, wgu, wd, gf, cg, cbb):
    vec = _const_spec((1, D_MODEL))
    cvec = _const_spec((1, D_CONV))
    nt = FFN_TM // TQ
    return pl.pallas_call(
        _ffn2_kernel,
        out_shape=jax.ShapeDtypeStruct((SEQ, D_MODEL), F32),
        grid=(SEQ // FFN_TM,),
        in_specs=[_row_spec(FFN_TM, D_MODEL), _row_spec(FFN_TM, D_CONV),
                  pl.BlockSpec((nt, HPG, TQ, LANES), lambda i: (i, 0, 0, 0)),
                  _const_spec((D_CONV, D_MODEL)), _const_spec((HPG, LANES, D_MODEL)), _mod_spec(5),
                  vec, _mod_spec(6), _mod_spec(7), _mod_spec(8),
                  _const_spec((D_MODEL, 2 * D_FF)), _const_spec((D_FF, D_MODEL)), vec, cvec, cvec],
        out_specs=_row_spec(FFN_TM, D_MODEL),
        compiler_params=_params(("arbitrary",)),
        name="ffn2",
    )(x, yp, ya, woc, woa, mod, g, mod, mod, mod, wgu, wd, gf, cg, cbb)


def _proj_kernel(x_ref, g_ref, sh_ref, sc_ref, w_ref,
                 hc_ref, kcvc_ref, ksl_ref, kw_ref, qt_ref, gatet_ref, vslt_ref, vwt_ref):
    hb = _rms_mod(x_ref[...], g_ref[...], sh_ref[...], sc_ref[...]).astype(BF16)
    u = _dot(hb, w_ref[...])
    hc_ref[...] = u[:, 0:512] * _sigmoid(u[:, 512:1024])
    kcvc_ref[...] = u[:, 1536:1792]
    tm = x_ref.shape[0]
    blk = (lax.broadcasted_iota(jnp.int32, (tm, D_KV), 0) % TK) // SLC_LEN
    ind = (blk == lax.broadcasted_iota(jnp.int32, (tm, D_KV), 1)).astype(BF16)
    ksl_ref[...] = jnp.concatenate([u[:, 1792:1920].astype(BF16), ind], axis=1)
    kw_ref[...] = u[:, 2048:2176].astype(BF16)
    qt_ref[...] = u[:, 1024:1536].T.astype(BF16)
    gatet_ref[...] = _sigmoid(u[:, 2304:2432]).T
    tail = jnp.concatenate([jnp.ones((1, TK), BF16), jnp.zeros((VROWS - HEAD_DIM - 1, TK), BF16)])
    for vt_ref, c0 in ((vslt_ref, 1920), (vwt_ref, 2176)):
        vt = u[:, c0:c0 + D_KV].T.astype(BF16)
        for g in range(N_KV):
            for t in range(tm // TK):
                vt_ref[g, t] = jnp.concatenate(
                    [vt[g * HEAD_DIM:(g + 1) * HEAD_DIM, t * TK:(t + 1) * TK], tail], axis=0)


def _proj(x, g, mod, w):
    vec = _const_spec((1, D_MODEL))
    tm = PROJ_TM
    rows = [(D_CONV, F32), (2 * D_KV, F32), (2 * D_KV, BF16), (D_KV, BF16)]
    cols = [(D_ATT, BF16), (LANES, F32)]
    vt_shape = jax.ShapeDtypeStruct((N_KV, SEQ // TK, VROWS, TK), BF16)
    vt_spec = pl.BlockSpec((N_KV, tm // TK, VROWS, TK), lambda i: (0, i, 0, 0))
    return pl.pallas_call(
        _proj_kernel,
        out_shape=[jax.ShapeDtypeStruct((SEQ, n), dt) for n, dt in rows]
                  + [jax.ShapeDtypeStruct((n, SEQ), dt) for n, dt in cols] + [vt_shape, vt_shape],
        grid=(SEQ // tm,),
        in_specs=[_row_spec(tm, D_MODEL), vec, _mod_spec(3), _mod_spec(4),
                  _const_spec((D_MODEL, D_IN_PAD))],
        out_specs=[_row_spec(tm, n) for n, _ in rows]
                  + [pl.BlockSpec((n, tm), lambda i: (0, i)) for n, _ in cols] + [vt_spec, vt_spec],
        compiler_params=_params(("arbitrary",)),
        name="proj",
    )(x, g, mod, mod, w)


def _sc_conv(hc, w32, b8):
    rows_per_worker = SEQ // SC_WORKERS
    n_chunks = rows_per_worker // SC_ROWS
    halves = (tuple(range(0, 16)), tuple(range(16, CONV_WIDTH)))
    off = CONV_HALO - (CONV_WIDTH - 1)
    mesh = plsc.VectorSubcoreMesh(core_axis_name="c", subcore_axis_name="s")

    @functools.partial(
        pl.kernel, mesh=mesh,
        out_type=jax.ShapeDtypeStruct((SEQ, D_CONV), F32),
        scratch_types=[pltpu.VMEM((CONV_HALO + SC_ROWS, D_CONV), F32),
                       pltpu.VMEM((SC_ROWS, D_CONV), F32),
                       pltpu.VMEM((32, D_CONV), F32),
                       pltpu.VMEM((SUBLANES, D_CONV), F32)],
        compiler_params=pltpu.CompilerParams(use_tc_tiling_on_sc=True),
        name="sc_conv",
    )
    def k(hc_hbm, w_hbm, b_hbm, out_hbm, x_v, o_v, w_v, b_v):
        wid = lax.axis_index("s") * SC_CORES + lax.axis_index("c")
        pltpu.sync_copy(w_hbm, w_v)
        pltpu.sync_copy(b_hbm, b_v)

        @pl.loop(0, n_chunks)
        def _(ci):
            row0 = pl.multiple_of(wid * rows_per_worker + ci * SC_ROWS, SC_ROWS)

            @pl.when(row0 == 0)
            def _():
                @pl.loop(0, CONV_HALO)
                def _(r):
                    @pl.loop(0, D_CONV // SC_LANES)
                    def _(cg):
                        x_v[r, pl.ds(cg * SC_LANES, SC_LANES)] = jnp.zeros((SC_LANES,), F32)
                pltpu.sync_copy(hc_hbm.at[pl.ds(0, SC_ROWS)], x_v.at[pl.ds(CONV_HALO, SC_ROWS)])

            @pl.when(row0 > 0)
            def _():
                pltpu.sync_copy(hc_hbm.at[pl.ds(row0 - CONV_HALO, CONV_HALO + SC_ROWS)], x_v)

            @pl.loop(0, D_CONV // SC_LANES)
            def _(cg):
                lanes = pl.ds(cg * SC_LANES, SC_LANES)
                for hi, taps in enumerate(halves):
                    wv = [w_v[t, lanes] for t in taps]

                    @pl.loop(0, SC_ROWS)
                    def _(r):
                        acc = b_v[0, lanes] if hi == 0 else o_v[r, lanes]
                        for j, t in enumerate(taps):
                            acc = acc + x_v[r + off + t, lanes] * wv[j]
                        o_v[r, lanes] = acc

            pltpu.sync_copy(o_v, out_hbm.at[pl.ds(row0, SC_ROWS)])

    return k(hc, w32, b8)


def _compress_kernel(c_ref, pe_ref, w1_ref, w2_ref, o_ref):
    half = CMP_STRIDE * HEAD_DIM
    cm = c_ref[0]
    pe = pe_ref[0]
    top = (cm + pe[:, :half]).astype(BF16)
    bot = (cm + pe[:, half:]).astype(BF16)
    a = _dot(top, w1_ref[0, :half, :])
    b = _dot(bot, w1_ref[0, half:, :])
    b_up = jnp.concatenate([b[1:], jnp.zeros((1, CMP_HIDDEN), F32)], axis=0)
    pre = a + b_up
    hid = (pre * _sigmoid(pre)).astype(BF16)
    o_ref[0] = _dot(hid, w2_ref[0])


def _compress(cmat, pe, w1, w2):
    return pl.pallas_call(
        _compress_kernel,
        out_shape=jax.ShapeDtypeStruct((4, N_CMP_PAD, HEAD_DIM), F32),
        grid=(4,),
        in_specs=[pl.BlockSpec((1, N_CMP_PAD, 1024), lambda i: (i, 0, 0)),
                  pl.BlockSpec((1, 1, 2048), lambda i: (i // 2, 0, 0)),
                  pl.BlockSpec((1, 2048, CMP_HIDDEN), lambda i: (i // 2, 0, 0)),
                  pl.BlockSpec((1, CMP_HIDDEN, HEAD_DIM), lambda i: (i // 2, 0, 0))],
        out_specs=pl.BlockSpec((1, N_CMP_PAD, HEAD_DIM), lambda i: (i, 0, 0)),
        compiler_params=_params(("arbitrary",)),
        name="compress",
    )(cmat, pe, w1, w2)


def _tile_update(s, m, acc, vt):
    m_new = jnp.maximum(m, jnp.max(s, axis=0, keepdims=True))
    alpha = jnp.exp2(m - m_new)
    p = jnp.exp2(s - m_new).astype(BF16)
    return m_new, alpha * acc + _dot(vt, p)


def _finish(acc):
    return acc[:HEAD_DIM] / jnp.maximum(acc[HEAD_DIM:HEAD_DIM + 1], 1e-30)


def _attn_kernel(qt_ref, gt_ref, kc_ref, vct_ref, ovt_ref, ksl_ref, vslt_ref, kw_ref, vwt_ref,
                 ct_ref, wt_ref, o_ref, s_scr, p_scr, sel_scr, wa_scr, wb_scr, sa_scr, sb_scr):
    qb = pl.program_id(0)
    q0 = qb * TQ
    qt = qt_ref[...]
    gt = gt_ref[...]
    drow = lax.broadcasted_iota(jnp.int32, (D_KV, TQ), 0)
    krow = lax.broadcasted_iota(jnp.int32, (TK, 1), 0)
    m0 = jnp.full((1, COLS), M_FLOOR, F32)
    acc0 = jnp.zeros((VROWS, COLS), F32)

    def pad_mask(first_row):
        return jnp.where(krow + first_row >= KPAD, 0.0, NEG)

    def real_row(first_row):
        return pl.multiple_of(jnp.maximum(first_row - KPAD, 0), TK)

    comb = []
    for g in range(N_KV):
        keep = (drow >= HEAD_DIM * g) & (drow < HEAD_DIM * (g + 1))
        q_g = jnp.concatenate(
            [jnp.where(keep, qt[h * D_KV:(h + 1) * D_KV, :], jnp.zeros((), BF16))
             for h in range(HPG)], axis=1)

        mcol = m0
        for ct in range(N_CMP_PAD // TK):
            rows = slice(ct * TK, (ct + 1) * TK)
            r0 = pl.multiple_of(jnp.clip(TK * ct - (TQ // CMP_STRIDE) * qb + CT_ZERO,
                                         0, CT_ROWS - TK), SUBLANES)
            s = _dot(kc_ref[rows, :], q_g) + ct_ref[g, pl.ds(r0, TK), :]
            s_scr[rows, :] = s
            mcol = jnp.maximum(mcol, jnp.max(s, axis=0, keepdims=True))
        lsum = jnp.zeros((1, COLS), F32)
        acc_c = jnp.zeros((HEAD_DIM, COLS), F32)
        for ct in range(N_CMP_PAD // TK):
            rows = slice(ct * TK, (ct + 1) * TK)
            p = jnp.exp2(s_scr[rows, :] - mcol)
            p_scr[rows, :] = p
            lsum = lsum + jnp.sum(p, axis=0, keepdims=True)
            acc_c = acc_c + _dot(vct_ref[g, :, rows], p.astype(BF16))
        rinv = 1.0 / jnp.maximum(lsum, 1e-30)
        o_c = acc_c * rinv
        imp = jnp.zeros((N_SLC, TQ), F32)
        for ct in range(N_CMP_PAD // TK):
            rows = slice(ct * TK, (ct + 1) * TK)
            pn = p_scr[rows, :] * rinv
            ps = pn[:, 0:TQ] + pn[:, TQ:2 * TQ] + pn[:, 2 * TQ:3 * TQ] + pn[:, 3 * TQ:4 * TQ]
            hi = ps.astype(BF16)
            lo = (ps - hi.astype(F32)).astype(BF16)
            ov = ovt_ref[:, rows]
            imp = imp + _dot(ov, hi) + _dot(ov, lo)

        m, acc = m0, acc0
        for a in range(3):
            first = q0 + TK * a
            s = _dot(kw_ref[pl.ds(real_row(first), TK), :], q_g)
            s = s + (wt_ref[g, TK * a:TK * (a + 1), :] + pad_mask(first))
            m, acc = _tile_update(s, m, acc, vwt_ref[g, jnp.maximum(qb + a - KPAD // TK, 0)])
        o_w = _finish(acc)

        cur = jnp.right_shift(q0 + lax.broadcasted_iota(jnp.int32, (1, TQ), 1), 6)
        jcol = lax.broadcasted_iota(jnp.int32, (N_SLC, 1), 0)
        forced = (jcol == 0) | (jcol == cur) | (jcol == cur - 1)
        sc = jnp.where(forced, TAKEN, jnp.where(jcol <= cur, imp, -1.0))
        jf = lax.broadcasted_iota(jnp.int32, (N_SLC, TQ), 0).astype(F32)
        for _ in range(SLC_TOPK - N_FORCED):
            mx = jnp.max(sc, axis=0, keepdims=True)
            idx = jnp.min(jnp.where(sc == mx, jf, 1e9), axis=0, keepdims=True)
            sc = jnp.where(jf == idx, TAKEN, sc)
        selneg = jnp.where(sc == TAKEN, 0.0, NEG)
        unused = jnp.zeros((SUBLANES - BLK_PER_TILE, TQ), F32)
        for kt in range(SEQ // TK):
            sel_scr[kt] = jnp.concatenate(
                [selneg[kt * BLK_PER_TILE:(kt + 1) * BLK_PER_TILE, :], unused], axis=0)
        sel_scr[SEQ // TK] = jnp.concatenate(
            [jnp.full((BLK_PER_TILE, TQ), NEG, F32), unused], axis=0)

        zrows = jnp.zeros((2 * D_KV - D_KV - 2 * SUBLANES, COLS), BF16)
        for w_scr in (wa_scr, wb_scr):
            w_scr[0:D_KV, :] = q_g
            w_scr[D_KV + 2 * SUBLANES:, :] = zrows

        def sel_scores(w_scr, first_row, sel_idx):
            blk = sel_scr[sel_idx]
            rows8 = jnp.concatenate([blk] * HPG, axis=1).astype(BF16)
            w_scr[D_KV:D_KV + 2 * SUBLANES, :] = jnp.concatenate(
                [rows8, jnp.zeros((SUBLANES, COLS), BF16)], axis=0)
            return _dot(ksl_ref[pl.ds(real_row(first_row), TK), :], w_scr[...])

        m, acc = m0, acc0
        for a, w_scr in ((0, wa_scr), (1, wb_scr)):
            first = q0 + TQ + TK * a
            s = sel_scores(w_scr, first, jnp.maximum(qb - 1 + a, 0))
            s = s + (wt_ref[g, TQ + TK * a:TQ + TK * (a + 1), :] + pad_mask(first))
            m, acc = _tile_update(s, m, acc, vslt_ref[g, jnp.maximum(qb - 1 + a, 0)])

        n_far = jnp.maximum(qb - 1, 0)

        def far_scores(w_scr, kt):
            valid = kt < n_far
            ks = jnp.where(valid, kt, 0)
            return sel_scores(w_scr, KPAD + ks * TK, jnp.where(valid, kt, SEQ // TK))

        def far_v(kt):
            return vslt_ref[g, jnp.where(kt < n_far, kt, 0)]

        bufs = ((sa_scr, wa_scr), (sb_scr, wb_scr))

        def issue_scores(buf, kt):
            s_scr_k, w_scr_k = bufs[buf]
            s = far_scores(w_scr_k, kt)
            s_scr_k[...] = s
            return jnp.max(s, axis=0, keepdims=True)

        def far_body(i, carry):
            m_i, acc_i, smax = carry
            for u in range(FAR_UNROLL):
                kt = i * FAR_UNROLL + u
                smax_nxt = issue_scores((u + 1) % 2, kt + 1)
                m_new = jnp.maximum(m_i, smax)
                alpha = jnp.exp2(m_i - m_new)
                p = jnp.exp2(bufs[u % 2][0][...] - m_new).astype(BF16)
                acc_i = alpha * acc_i + _dot(far_v(kt), p)
                m_i, smax = m_new, smax_nxt
            return m_i, acc_i, smax

        trips = (n_far + FAR_UNROLL - 1) // FAR_UNROLL
        m, acc, _ = lax.fori_loop(0, trips, far_body, (m, acc, issue_scores(0, 0)))
        o_s = _finish(acc)

        per_head = []
        for h in range(HPG):
            c0 = 3 * (g * HPG + h)
            cs = slice(h * TQ, (h + 1) * TQ)
            per_head.append(gt[c0:c0 + 1, :] * o_c[:, cs] + gt[c0 + 1:c0 + 2, :] * o_s[:, cs]
                            + gt[c0 + 2:c0 + 3, :] * o_w[:, cs])
        comb.append(per_head)

    for h in range(HPG):
        merged = jnp.concatenate([comb[0][h], comb[1][h]], axis=0)
        o_ref[0, h] = merged.T.astype(BF16)


def _attn(qt, gatet, kc, vct, ovt, ksl, vslt, kw, vwt, ctmpl, wtmpl):
    consts = [kc, vct, ovt, ksl, vslt, kw, vwt, ctmpl, wtmpl]
    return pl.pallas_call(
        _attn_kernel,
        out_shape=jax.ShapeDtypeStruct((SEQ // TQ, HPG, TQ, LANES), BF16),
        grid=(SEQ // TQ,),
        in_specs=[pl.BlockSpec((D_ATT, TQ), lambda i: (0, i)),
                  pl.BlockSpec((LANES, TQ), lambda i: (0, i))]
                 + [_const_spec(a.shape) for a in consts],
        out_specs=pl.BlockSpec((1, HPG, TQ, LANES), lambda i: (i, 0, 0, 0)),
        scratch_shapes=[pltpu.VMEM((N_CMP_PAD, COLS), F32), pltpu.VMEM((N_CMP_PAD, COLS), F32),
                        pltpu.VMEM((SEL_ENTRIES, SUBLANES, TQ), F32),
                        pltpu.VMEM((2 * D_KV, COLS), BF16), pltpu.VMEM((2 * D_KV, COLS), BF16),
                        pltpu.VMEM((TK, COLS), F32), pltpu.VMEM((TK, COLS), F32)],
        compiler_params=_params(("arbitrary",)),
        name="nsa_attn",
    )(qt, gatet, *consts)


def _t5_bucket(dist):
    max_exact = N_BUCKETS // 2
    d = jnp.maximum(dist, 0)
    df = jnp.maximum(d, 1).astype(F32)
    large = max_exact + (jnp.log(df / max_exact) / math.log(MAX_DIST / max_exact)
                         * (N_BUCKETS - max_exact)).astype(jnp.int32)
    large = jnp.minimum(large, N_BUCKETS - 1)
    return jnp.where(d < max_exact, d, large)


def _template_kernel(f_ref, gx_ref, wt_ref, ct_ref):
    wk, lanes = wt_ref.shape[1], f_ref.shape[1]
    n_band = CMP_R_HI - CMP_R_LO + 1
    lo = CT_ZERO + CMP_R_LO
    for hd in range(N_HEADS):
        g, cols = hd // HPG, slice((hd % HPG) * TQ, (hd % HPG + 1) * TQ)
        x = jnp.broadcast_to(f_ref[hd:hd + 1, :], (wk, lanes))
        y = pltpu.roll(x, lanes - (wk - 1), axis=1, stride=1, stride_axis=0)
        wt_ref[g, :, cols] = y[:, :TQ]
        xb = jnp.broadcast_to(gx_ref[hd:hd + 1, :], (n_band, lanes))
        yb = pltpu.roll(xb, lanes - CMP_STRIDE * (n_band - 1), axis=1,
                        stride=CMP_STRIDE, stride_axis=0)
        ct_ref[g, 0:lo, cols] = jnp.zeros((lo, TQ), F32)
        ct_ref[g, lo:lo + n_band, cols] = yb[:, :TQ]
        ct_ref[g, lo + n_band:, cols] = jnp.full((CT_ROWS - lo - n_band, TQ), NEG, F32)


def _templates(fext, gxext):
    return pl.pallas_call(
        _template_kernel,
        out_shape=[jax.ShapeDtypeStruct((N_KV, WINDOW + TQ, COLS), F32),
                   jax.ShapeDtypeStruct((N_KV, CT_ROWS, COLS), F32)],
        compiler_params=_params(None),
        name="templates",
    )(fext, gxext)


def _bias_templates(rel_bias):
    biasp = rel_bias[:, _t5_bucket(jnp.arange(MAX_DIST))] - rel_bias[:, N_BUCKETS - 1:]
    biasp = biasp * LOG2E

    def by_distance(n_neg, n_zero, n_neg_after):
        return jnp.concatenate(
            [jnp.full((N_HEADS, n_neg), NEG, F32), biasp, jnp.zeros((N_HEADS, n_zero), F32),
             jnp.full((N_HEADS, n_neg_after), NEG, F32)], axis=1)

    wk = WINDOW + TQ
    f = by_distance(TQ - 1, WINDOW - MAX_DIST, wk - WINDOW)
    fext = jnp.concatenate([f, jnp.zeros((N_HEADS, 1), F32)], axis=1)

    d_min = -(CMP_LEN - 1) - CMP_STRIDE * CMP_R_HI
    d_max = TQ - 1 - (CMP_LEN - 1) - CMP_STRIDE * CMP_R_LO
    gx = by_distance(-d_min, d_max + 1 - MAX_DIST, 0)
    gxext = jnp.pad(gx, ((0, 0), (0, fext.shape[1] - gx.shape[1])))
    wtmpl, ctmpl = _templates(fext, gxext)
    return ctmpl, wtmpl


def _overlap_t():
    cmp_start = jnp.arange(N_CMP_PAD) * CMP_STRIDE
    slc_start = jnp.arange(N_SLC) * SLC_LEN
    ov = ((cmp_start[None, :] < slc_start[:, None] + SLC_LEN)
          & (cmp_start[None, :] + CMP_LEN > slc_start[:, None])
          & (jnp.arange(N_CMP_PAD)[None, :] < N_CMP))
    return ov.astype(BF16)


def kernel(x, c, w_ada, b_ada, g_ffn1, w_gu1, w_down1, g_mix, w_in, w_dw, b_dw, ln_g, ln_b,
           pe_k, pe_v, w_ck1, w_ck2, w_cv1, w_cv2, rel_bias, w_out, g_ffn2, w_gu2, w_down2,
           g_final):
    assert x.shape == (1, SEQ, D_MODEL) and w_ada.shape[0] == 1
    x2 = x[0]
    mod = _ada(c.reshape(D_MODEL, 1), w_ada[0], b_ada)
    x1 = _ffn1(x2, g_ffn1, mod, w_gu1[0].astype(BF16), w_down1[0].astype(BF16))

    wi = w_in[0]
    wq = wi[:, 1024:1536].reshape(D_MODEL, N_KV, HPG, HEAD_DIM).transpose(0, 2, 1, 3)
    wq = wq.reshape(D_MODEL, D_ATT) * (HEAD_DIM ** -0.5 * LOG2E)
    w_in_p = jnp.concatenate(
        [wi[:, :1024], wq, wi[:, 1536:], jnp.zeros((D_MODEL, D_IN_PAD - wi.shape[1]), F32)],
        axis=1).astype(BF16)
    hc, kcvc, ksl, kw, qt, gatet, vslt, vwt = _proj(x1, g_mix, mod, w_in_p)

    y_conv = _sc_conv(hc, jnp.pad(w_dw[0], ((0, 32 - CONV_WIDTH), (0, 0))),
                      jnp.broadcast_to(b_dw, (SUBLANES, D_CONV)))

    cmat = kcvc.reshape(SEQ, 4, HEAD_DIM).transpose(1, 0, 2).reshape(4, N_CMP_PAD, 1024)
    pe = jnp.stack([pe_k[0].reshape(1, -1), pe_v[0].reshape(1, -1)])
    w1 = jnp.stack([w_ck1[0], w_cv1[0]]).astype(BF16)
    w2 = jnp.stack([w_ck2[0], w_cv2[0]]).astype(BF16)
    cmp = _compress(cmat, pe, w1, w2)
    kc = jnp.concatenate([cmp[0], cmp[1]], axis=1).astype(BF16)
    vct = jnp.swapaxes(cmp[2:4], 1, 2).astype(BF16)

    ctmpl, wtmpl = _bias_templates(rel_bias)
    y_att = _attn(qt, gatet, kc, vct, _overlap_t(), ksl, vslt, kw, vwt, ctmpl, wtmpl)

    wo = w_out[0]
    woa = wo[D_CONV:].reshape(N_KV, HPG, HEAD_DIM, D_MODEL).transpose(1, 0, 2, 3)
    woa = woa.reshape(HPG, LANES, D_MODEL).astype(BF16)
    out = _ffn2(x1, y_conv, y_att, wo[:D_CONV].astype(BF16), woa, mod,
                g_ffn2, w_gu2[0].astype(BF16), w_down2[0].astype(BF16),
                g_final.reshape(1, D_MODEL), ln_g, ln_b)
    return out[None]
```

```python
import functools
import math

import jax
import jax.numpy as jnp
from jax import lax
from jax.experimental import pallas as pl
from jax.experimental.pallas import tpu as pltpu
from jax.experimental.pallas import tpu_sc as plsc

F32 = jnp.float32
BF16 = jnp.bfloat16

D_MODEL = 1024
SEQ = 16384
D_CONV = 512
CONV_WIDTH = 31
N_HEADS = 8
N_KV = 2
HPG = 4
HEAD_DIM = 64
D_ATT = 512
D_KV = 128
CMP_LEN = 32
CMP_STRIDE = 16
CMP_HIDDEN = 128
N_CMP = (SEQ - CMP_LEN) // CMP_STRIDE + 1
N_CMP_PAD = 1024
SLC_LEN = 64
N_SLC = SEQ // SLC_LEN
SLC_TOPK = 16
WINDOW = 512
N_FORCED = 3
TAKEN = -2.0
N_BUCKETS = 32
MAX_DIST = 128
D_FF = 2816
FFN_RES = 0.5
EPS = 1e-6
NEG = -1e30
M_FLOOR = -1e29
LOG2E = math.log2(math.e)

V7X_VMEM_BYTES = 64 * 1024 * 1024
VMEM_LIMIT = V7X_VMEM_BYTES - 6 * 1024 * 1024
LANES = 128
SUBLANES = 8

TQ = 256
TK = 256
COLS = HPG * TQ
KPAD = WINDOW
VROWS = HEAD_DIM + 16
BLK_PER_TILE = TK // SLC_LEN
FAR_UNROLL = 4
SEL_ENTRIES = SEQ // TK + 2
CMP_R_HI = (TQ - CMP_LEN) // CMP_STRIDE
CMP_R_LO = -((MAX_DIST + CMP_LEN - 2) // CMP_STRIDE)
CT_ZERO = TK + -(CMP_R_LO // SUBLANES) * SUBLANES
CT_ROWS = CT_ZERO + -(-(CMP_R_HI + 1) // SUBLANES) * SUBLANES + TK
FFN_TM = 512
FFN_TF = D_FF
PROJ_TM = 512
CONV_HALO = 32
SC_CORES, SC_SUBCORES, SC_LANES = 2, 16, 16
SC_WORKERS = SC_CORES * SC_SUBCORES
SC_ROWS = 64
D_IN_PAD = 2432


def _params(sem):
    return pltpu.CompilerParams(dimension_semantics=sem, vmem_limit_bytes=VMEM_LIMIT)


def _const_spec(shape):
    nd = len(shape)
    return pl.BlockSpec(shape, lambda *_: (0,) * nd, pipeline_mode=pl.Buffered(1))


def _sigmoid(v):
    return 1.0 / (1.0 + jnp.exp(-v))


def _dot(a, b):
    return jnp.dot(a, b, preferred_element_type=F32)


def _ada_kernel(c_ref, w_ref, b_ref, o_ref):
    c = c_ref[...]
    sc = c * _sigmoid(c)
    o_ref[...] = jnp.sum(w_ref[...] * sc, axis=0, keepdims=True) + b_ref[...]


def _ada(c_col, w, b):
    n = w.shape[1]
    tn = n // 8
    return pl.pallas_call(
        _ada_kernel,
        out_shape=jax.ShapeDtypeStruct((1, n), F32),
        grid=(8,),
        in_specs=[pl.BlockSpec((D_MODEL, 1), lambda j: (0, 0)),
                  pl.BlockSpec((D_MODEL, tn), lambda j: (0, j)),
                  pl.BlockSpec((1, tn), lambda j: (0, j))],
        out_specs=pl.BlockSpec((1, tn), lambda j: (0, j)),
        compiler_params=_params(("arbitrary",)),
        name="ada",
    )(c_col, w, b)


def _rms_mod(x, g, sh, sc):
    ms = jnp.mean(x * x, axis=-1, keepdims=True)
    y = x * lax.rsqrt(ms + EPS) * g
    return y * (1.0 + sc) + sh


def _ffn_body(x, g_ref, sh_ref, sc_ref, gt_ref, wgu_ref, wd_ref):
    hb = _rms_mod(x, g_ref[...], sh_ref[...], sc_ref[...]).astype(BF16)
    acc = jnp.zeros((x.shape[0], D_MODEL), F32)
    for f in range(D_FF // FFN_TF):
        gg = _dot(hb, wgu_ref[:, f * FFN_TF:(f + 1) * FFN_TF])
        uu = _dot(hb, wgu_ref[:, D_FF + f * FFN_TF:D_FF + (f + 1) * FFN_TF])
        a = (gg * _sigmoid(gg) * uu).astype(BF16)
        acc = acc + _dot(a, wd_ref[f * FFN_TF:(f + 1) * FFN_TF, :])
    return x + (FFN_RES * gt_ref[...]) * acc


def _ffn1_kernel(x_ref, g_ref, sh_ref, sc_ref, gt_ref, wgu_ref, wd_ref, o_ref):
    o_ref[...] = _ffn_body(x_ref[...], g_ref, sh_ref, sc_ref, gt_ref, wgu_ref, wd_ref)


def _ffn2_kernel(x_ref, yp_ref, ya_ref, woc_ref, woa_ref, gt2_ref,
                 g_ref, sh_ref, sc_ref, gt_ref, wgu_ref, wd_ref, gf_ref, cg_ref, cbb_ref, o_ref):
    tm = x_ref.shape[0]
    cv = yp_ref[...]
    mu = jnp.mean(cv, axis=-1, keepdims=True)
    dlt = cv - mu
    var = jnp.mean(dlt * dlt, axis=-1, keepdims=True)
    yn = dlt * lax.rsqrt(var + EPS) * cg_ref[...] + cbb_ref[...]
    y = _dot((yn * _sigmoid(yn)).astype(BF16), woc_ref[...])
    for h in range(HPG):
        y = y + _dot(ya_ref[:, h].reshape(tm, LANES), woa_ref[h])
    x = x_ref[...] + gt2_ref[...] * y
    out = _ffn_body(x, g_ref, sh_ref, sc_ref, gt_ref, wgu_ref, wd_ref)
    ms = jnp.mean(out * out, axis=-1, keepdims=True)
    o_ref[...] = out * lax.rsqrt(ms + EPS) * gf_ref[...]


def _row_spec(tm, n):
    return pl.BlockSpec((tm, n), lambda i: (i, 0))


def _mod_spec(k):
    return pl.BlockSpec((1, D_MODEL), lambda *_: (0, k), pipeline_mode=pl.Buffered(1))


def _ffn1(x, g, mod, wgu, wd):
    vec = _const_spec((1, D_MODEL))
    return pl.pallas_call(
        _ffn1_kernel,
        out_shape=jax.ShapeDtypeStruct((SEQ, D_MODEL), F32),
        grid=(SEQ // FFN_TM,),
        in_specs=[_row_spec(FFN_TM, D_MODEL), vec, _mod_spec(0), _mod_spec(1), _mod_spec(2),
                  _const_spec((D_MODEL, 2 * D_FF)), _const_spec((D_FF, D_MODEL))],
        out_specs=_row_spec(FFN_TM, D_MODEL),
        compiler_params=_params(("arbitrary",)),
        name="ffn1",
    )(x, g, mod, mod, mod, wgu, wd)


def _ffn2(x, yp, ya, woc, woa, mod, g, wgu, wd, gf, cg, cbb):
    vec = _const_spec((1, D_MODEL))
    cvec = _const_spec((1, D_CONV))
    nt = FFN_TM // TQ
    return pl.pallas_call(
        _ffn2_kernel,
        out_shape=jax.ShapeDtypeStruct((SEQ, D_MODEL), F32),
        grid=(SEQ // FFN_TM,),
        in_specs=[_row_spec(FFN_TM, D_MODEL), _row_spec(FFN_TM, D_CONV),
                  pl.BlockSpec((nt, HPG, TQ, LANES), lambda i: (i, 0, 0, 0)),
                  _const_spec((D_CONV, D_MODEL)), _const_spec((HPG, LANES, D_MODEL)), _mod_spec(5),
                  vec, _mod_spec(6), _mod_spec(7), _mod_spec(8),
                  _const_spec((D_MODEL, 2 * D_FF)), _const_spec((D_FF, D_MODEL)), vec, cvec, cvec],
        out_specs=_row_spec(FFN_TM, D_MODEL),
        compiler_params=_params(("arbitrary",)),
        name="ffn2",
    )(x, yp, ya, woc, woa, mod, g, mod, mod, mod, wgu, wd, gf, cg, cbb)


def _proj_kernel(x_ref, g_ref, sh_ref, sc_ref, w_ref,
                 hc_ref, kcvc_ref, ksl_ref, kw_ref, qt_ref, gatet_ref, vslt_ref, vwt_ref):
    hb = _rms_mod(x_ref[...], g_ref[...], sh_ref[...], sc_ref[...]).astype(BF16)
    u = _dot(hb, w_ref[...])
    hc_ref[...] = u[:, 0:512] * _sigmoid(u[:, 512:1024])
    kcvc_ref[...] = u[:, 1536:1792]
    tm = x_ref.shape[0]
    blk = (lax.broadcasted_iota(jnp.int32, (tm, D_KV), 0) % TK) // SLC_LEN
    ind = (blk == lax.broadcasted_iota(jnp.int32, (tm, D_KV), 1)).astype(BF16)
    ksl_ref[...] = jnp.concatenate([u[:, 1792:1920].astype(BF16), ind], axis=1)
    kw_ref[...] = u[:, 2048:2176].astype(BF16)
    qt_ref[...] = u[:, 1024:1536].T.astype(BF16)
    gatet_ref[...] = _sigmoid(u[:, 2304:2432]).T
    tail = jnp.concatenate([jnp.ones((1, TK), BF16), jnp.zeros((VROWS - HEAD_DIM - 1, TK), BF16)])
    for vt_ref, c0 in ((vslt_ref, 1920), (vwt_ref, 2176)):
        vt = u[:, c0:c0 + D_KV].T.astype(BF16)
        for g in range(N_KV):
            for t in range(tm // TK):
                vt_ref[g, t] = jnp.concatenate(
                    [vt[g * HEAD_DIM:(g + 1) * HEAD_DIM, t * TK:(t + 1) * TK], tail], axis=0)


def _proj(x, g, mod, w):
    vec = _const_spec((1, D_MODEL))
    tm = PROJ_TM
    rows = [(D_CONV, F32), (2 * D_KV, F32), (2 * D_KV, BF16), (D_KV, BF16)]
    cols = [(D_ATT, BF16), (LANES, F32)]
    vt_shape = jax.ShapeDtypeStruct((N_KV, SEQ // TK, VROWS, TK), BF16)
    vt_spec = pl.BlockSpec((N_KV, tm // TK, VROWS, TK), lambda i: (0, i, 0, 0))
    return pl.pallas_call(
        _proj_kernel,
        out_shape=[jax.ShapeDtypeStruct((SEQ, n), dt) for n, dt in rows]
                  + [jax.ShapeDtypeStruct((n, SEQ), dt) for n, dt in cols] + [vt_shape, vt_shape],
        grid=(SEQ // tm,),
        in_specs=[_row_spec(tm, D_MODEL), vec, _mod_spec(3), _mod_spec(4),
                  _const_spec((D_MODEL, D_IN_PAD))],
        out_specs=[_row_spec(tm, n) for n, _ in rows]
                  + [pl.BlockSpec((n, tm), lambda i: (0, i)) for n, _ in cols] + [vt_spec, vt_spec],
        compiler_params=_params(("arbitrary",)),
        name="proj",
    )(x, g, mod, mod, w)


def _sc_conv(hc, w32, b8):
    rows_per_worker = SEQ // SC_WORKERS
    n_chunks = rows_per_worker // SC_ROWS
    halves = (tuple(range(0, 16)), tuple(range(16, CONV_WIDTH)))
    off = CONV_HALO - (CONV_WIDTH - 1)
    mesh = plsc.VectorSubcoreMesh(core_axis_name="c", subcore_axis_name="s")

    @functools.partial(
        pl.kernel, mesh=mesh,
        out_type=jax.ShapeDtypeStruct((SEQ, D_CONV), F32),
        scratch_types=[pltpu.VMEM((CONV_HALO + SC_ROWS, D_CONV), F32),
                       pltpu.VMEM((SC_ROWS, D_CONV), F32),
                       pltpu.VMEM((32, D_CONV), F32),
                       pltpu.VMEM((SUBLANES, D_CONV), F32)],
        compiler_params=pltpu.CompilerParams(use_tc_tiling_on_sc=True),
        name="sc_conv",
    )
    def k(hc_hbm, w_hbm, b_hbm, out_hbm, x_v, o_v, w_v, b_v):
        wid = lax.axis_index("s") * SC_CORES + lax.axis_index("c")
        pltpu.sync_copy(w_hbm, w_v)
        pltpu.sync_copy(b_hbm, b_v)

        @pl.loop(0, n_chunks)
        def _(ci):
            row0 = pl.multiple_of(wid * rows_per_worker + ci * SC_ROWS, SC_ROWS)

            @pl.when(row0 == 0)
            def _():
                @pl.loop(0, CONV_HALO)
                def _(r):
                    @pl.loop(0, D_CONV // SC_LANES)
                    def _(cg):
                        x_v[r, pl.ds(cg * SC_LANES, SC_LANES)] = jnp.zeros((SC_LANES,), F32)
                pltpu.sync_copy(hc_hbm.at[pl.ds(0, SC_ROWS)], x_v.at[pl.ds(CONV_HALO, SC_ROWS)])

            @pl.when(row0 > 0)
            def _():
                pltpu.sync_copy(hc_hbm.at[pl.ds(row0 - CONV_HALO, CONV_HALO + SC_ROWS)], x_v)

            @pl.loop(0, D_CONV // SC_LANES)
            def _(cg):
                lanes = pl.ds(cg * SC_LANES, SC_LANES)
                for hi, taps in enumerate(halves):
                    wv = [w_v[t, lanes] for t in taps]

                    @pl.loop(0, SC_ROWS)
                    def _(r):
                        acc = b_v[0, lanes] if hi == 0 else o_v[r, lanes]
                        for j, t in enumerate(taps):
                            acc = acc + x_v[r + off + t, lanes] * wv[j]
                        o_v[r, lanes] = acc

            pltpu.sync_copy(o_v, out_hbm.at[pl.ds(row0, SC_ROWS)])

    return k(hc, w32, b8)


def _compress_kernel(c_ref, pe_ref, w1_ref, wblk_ref, w2_ref, o_ref):
    half = CMP_STRIDE * HEAD_DIM
    acc = jnp.zeros((N_CMP_PAD, 4 * CMP_HIDDEN), F32)
    for l in range(CMP_STRIDE):
        x = c_ref[pl.ds(l, N_CMP_PAD, stride=CMP_STRIDE), :].astype(BF16)
        acc = acc + _dot(x, wblk_ref[0, l])
    pe = jnp.broadcast_to(pe_ref[0], (SUBLANES, 2 * half)).astype(BF16)
    pe_term = (_dot(pe[:, :half], w1_ref[0, :half, :]) + _dot(pe[:, half:], w1_ref[0, half:, :]))[0:1]
    for g in range(N_KV):
        a = acc[:, 2 * g * CMP_HIDDEN:(2 * g + 1) * CMP_HIDDEN]
        b = acc[:, (2 * g + 1) * CMP_HIDDEN:(2 * g + 2) * CMP_HIDDEN]
        b_up = jnp.concatenate([b[1:], jnp.zeros((1, CMP_HIDDEN), F32)], axis=0)
        pre = a + b_up + pe_term
        hid = (pre * _sigmoid(pre)).astype(BF16)
        o_ref[g] = _dot(hid, w2_ref[0])


def _compress(kcvc, pe, w1, wblk, w2):
    return pl.pallas_call(
        _compress_kernel,
        out_shape=jax.ShapeDtypeStruct((2 * N_KV, N_CMP_PAD, HEAD_DIM), F32),
        grid=(2,),
        in_specs=[pl.BlockSpec((SEQ, D_KV), lambda i: (0, i)),
                  pl.BlockSpec((1, 1, 2048), lambda i: (i, 0, 0)),
                  pl.BlockSpec((1, 2048, CMP_HIDDEN), lambda i: (i, 0, 0)),
                  pl.BlockSpec((1, CMP_STRIDE, D_KV, 4 * CMP_HIDDEN), lambda i: (i, 0, 0, 0)),
                  pl.BlockSpec((1, CMP_HIDDEN, HEAD_DIM), lambda i: (i, 0, 0))],
        out_specs=pl.BlockSpec((N_KV, N_CMP_PAD, HEAD_DIM), lambda i: (i, 0, 0)),
        compiler_params=_params(("arbitrary",)),
        name="compress",
    )(kcvc, pe, w1, wblk, w2)


def _tile_update(s, m, acc, vt):
    m_new = jnp.maximum(m, jnp.max(s, axis=0, keepdims=True))
    alpha = jnp.exp2(m - m_new)
    p = jnp.exp2(s - m_new).astype(BF16)
    return m_new, alpha * acc + _dot(vt, p)


def _finish(acc):
    return acc[:HEAD_DIM] / jnp.maximum(acc[HEAD_DIM:HEAD_DIM + 1], 1e-30)


def _attn_kernel(qt_ref, gt_ref, kc_ref, vct_ref, ovt_ref, ksl_ref, vslt_ref, kw_ref, vwt_ref,
                 ct_ref, wt_ref, o_ref, s_scr, p_scr, sel_scr, wa_scr, wb_scr, sa_scr, sb_scr):
    qb = pl.program_id(0)
    q0 = qb * TQ
    qt = qt_ref[...]
    gt = gt_ref[...]
    drow = lax.broadcasted_iota(jnp.int32, (D_KV, TQ), 0)
    krow = lax.broadcasted_iota(jnp.int32, (TK, 1), 0)
    m0 = jnp.full((1, COLS), M_FLOOR, F32)
    acc0 = jnp.zeros((VROWS, COLS), F32)

    def pad_mask(first_row):
        return jnp.where(krow + first_row >= KPAD, 0.0, NEG)

    def real_row(first_row):
        return pl.multiple_of(jnp.maximum(first_row - KPAD, 0), TK)

    comb = []
    for g in range(N_KV):
        keep = (drow >= HEAD_DIM * g) & (drow < HEAD_DIM * (g + 1))
        q_g = jnp.concatenate(
            [jnp.where(keep, qt[h * D_KV:(h + 1) * D_KV, :], jnp.zeros((), BF16))
             for h in range(HPG)], axis=1)

        mcol = m0
        for ct in range(N_CMP_PAD // TK):
            rows = slice(ct * TK, (ct + 1) * TK)
            r0 = pl.multiple_of(jnp.clip(TK * ct - (TQ // CMP_STRIDE) * qb + CT_ZERO,
                                         0, CT_ROWS - TK), SUBLANES)
            s = _dot(kc_ref[rows, :], q_g) + ct_ref[g, pl.ds(r0, TK), :]
            s_scr[rows, :] = s
            mcol = jnp.maximum(mcol, jnp.max(s, axis=0, keepdims=True))
        lsum = jnp.zeros((1, COLS), F32)
        acc_c = jnp.zeros((HEAD_DIM, COLS), F32)
        for ct in range(N_CMP_PAD // TK):
            rows = slice(ct * TK, (ct + 1) * TK)
            p = jnp.exp2(s_scr[rows, :] - mcol)
            p_scr[rows, :] = p
            lsum = lsum + jnp.sum(p, axis=0, keepdims=True)
            acc_c = acc_c + _dot(vct_ref[g, :, rows], p.astype(BF16))
        rinv = 1.0 / jnp.maximum(lsum, 1e-30)
        o_c = acc_c * rinv
        imp = jnp.zeros((N_SLC, TQ), F32)
        for ct in range(N_CMP_PAD // TK):
            rows = slice(ct * TK, (ct + 1) * TK)
            pn = p_scr[rows, :] * rinv
            ps = pn[:, 0:TQ] + pn[:, TQ:2 * TQ] + pn[:, 2 * TQ:3 * TQ] + pn[:, 3 * TQ:4 * TQ]
            hi = ps.astype(BF16)
            lo = (ps - hi.astype(F32)).astype(BF16)
            ov = ovt_ref[:, rows]
            imp = imp + _dot(ov, hi) + _dot(ov, lo)

        m, acc = m0, acc0
        for a in range(3):
            first = q0 + TK * a
            s = _dot(kw_ref[pl.ds(real_row(first), TK), :], q_g)
            s = s + (wt_ref[g, TK * a:TK * (a + 1), :] + pad_mask(first))
            m, acc = _tile_update(s, m, acc, vwt_ref[g, jnp.maximum(qb + a - KPAD // TK, 0)])
        o_w = _finish(acc)

        cur = jnp.right_shift(q0 + lax.broadcasted_iota(jnp.int32, (1, TQ), 1), 6)
        jcol = lax.broadcasted_iota(jnp.int32, (N_SLC, 1), 0)
        forced = (jcol == 0) | (jcol == cur) | (jcol == cur - 1)
        sc = jnp.where(forced, TAKEN, jnp.where(jcol <= cur, imp, -1.0))
        jf = lax.broadcasted_iota(jnp.int32, (N_SLC, TQ), 0).astype(F32)
        for _ in range(SLC_TOPK - N_FORCED):
            mx = jnp.max(sc, axis=0, keepdims=True)
            idx = jnp.min(jnp.where(sc == mx, jf, 1e9), axis=0, keepdims=True)
            sc = jnp.where(jf == idx, TAKEN, sc)
        selneg = jnp.where(sc == TAKEN, 0.0, NEG)
        unused = jnp.zeros((SUBLANES - BLK_PER_TILE, TQ), F32)
        for kt in range(SEQ // TK):
            sel_scr[kt] = jnp.concatenate(
                [selneg[kt * BLK_PER_TILE:(kt + 1) * BLK_PER_TILE, :], unused], axis=0)
        sel_scr[SEQ // TK] = jnp.concatenate(
            [jnp.full((BLK_PER_TILE, TQ), NEG, F32), unused], axis=0)

        zrows = jnp.zeros((2 * D_KV - D_KV - 2 * SUBLANES, COLS), BF16)
        for w_scr in (wa_scr, wb_scr):
            w_scr[0:D_KV, :] = q_g
            w_scr[D_KV + 2 * SUBLANES:, :] = zrows

        def sel_scores(w_scr, first_row, sel_idx):
            blk = sel_scr[sel_idx]
            rows8 = jnp.concatenate([blk] * HPG, axis=1).astype(BF16)
            w_scr[D_KV:D_KV + 2 * SUBLANES, :] = jnp.concatenate(
                [rows8, jnp.zeros((SUBLANES, COLS), BF16)], axis=0)
            return _dot(ksl_ref[pl.ds(real_row(first_row), TK), :], w_scr[...])

        m, acc = m0, acc0
        for a, w_scr in ((0, wa_scr), (1, wb_scr)):
            first = q0 + TQ + TK * a
            s = sel_scores(w_scr, first, jnp.maximum(qb - 1 + a, 0))
            s = s + (wt_ref[g, TQ + TK * a:TQ + TK * (a + 1), :] + pad_mask(first))
            m, acc = _tile_update(s, m, acc, vslt_ref[g, jnp.maximum(qb - 1 + a, 0)])

        n_far = jnp.maximum(qb - 1, 0)

        def far_scores(w_scr, kt):
            valid = kt < n_far
            ks = jnp.where(valid, kt, 0)
            return sel_scores(w_scr, KPAD + ks * TK, jnp.where(valid, kt, SEQ // TK))

        def far_v(kt):
            return vslt_ref[g, jnp.where(kt < n_far, kt, 0)]

        bufs = ((sa_scr, wa_scr), (sb_scr, wb_scr))

        def issue_scores(buf, kt):
            s_scr_k, w_scr_k = bufs[buf]
            s = far_scores(w_scr_k, kt)
            s_scr_k[...] = s
            return jnp.max(s, axis=0, keepdims=True)

        def far_body(i, carry):
            m_i, acc_i, smax = carry
            for u in range(FAR_UNROLL):
                kt = i * FAR_UNROLL + u
                smax_nxt = issue_scores((u + 1) % 2, kt + 1)
                m_new = jnp.maximum(m_i, smax)
                alpha = jnp.exp2(m_i - m_new)
                p = jnp.exp2(bufs[u % 2][0][...] - m_new).astype(BF16)
                acc_i = alpha * acc_i + _dot(far_v(kt), p)
                m_i, smax = m_new, smax_nxt
            return m_i, acc_i, smax

        trips = (n_far + FAR_UNROLL - 1) // FAR_UNROLL
        m, acc, _ = lax.fori_loop(0, trips, far_body, (m, acc, issue_scores(0, 0)))
        o_s = _finish(acc)

        per_head = []
        for h in range(HPG):
            c0 = 3 * (g * HPG + h)
            cs = slice(h * TQ, (h + 1) * TQ)
            per_head.append(gt[c0:c0 + 1, :] * o_c[:, cs] + gt[c0 + 1:c0 + 2, :] * o_s[:, cs]
                            + gt[c0 + 2:c0 + 3, :] * o_w[:, cs])
        comb.append(per_head)

    for h in range(HPG):
        merged = jnp.concatenate([comb[0][h], comb[1][h]], axis=0)
        o_ref[0, h] = merged.T.astype(BF16)


def _attn(qt, gatet, kc, vct, ovt, ksl, vslt, kw, vwt, ctmpl, wtmpl):
    consts = [kc, vct, ovt, ksl, vslt, kw, vwt, ctmpl, wtmpl]
    return pl.pallas_call(
        _attn_kernel,
        out_shape=jax.ShapeDtypeStruct((SEQ // TQ, HPG, TQ, LANES), BF16),
        grid=(SEQ // TQ,),
        in_specs=[pl.BlockSpec((D_ATT, TQ), lambda i: (0, i)),
                  pl.BlockSpec((LANES, TQ), lambda i: (0, i))]
                 + [_const_spec(a.shape) for a in consts],
        out_specs=pl.BlockSpec((1, HPG, TQ, LANES), lambda i: (i, 0, 0, 0)),
        scratch_shapes=[pltpu.VMEM((N_CMP_PAD, COLS), F32), pltpu.VMEM((N_CMP_PAD, COLS), F32),
                        pltpu.VMEM((SEL_ENTRIES, SUBLANES, TQ), F32),
                        pltpu.VMEM((2 * D_KV, COLS), BF16), pltpu.VMEM((2 * D_KV, COLS), BF16),
                        pltpu.VMEM((TK, COLS), F32), pltpu.VMEM((TK, COLS), F32)],
        compiler_params=_params(("arbitrary",)),
        name="nsa_attn",
    )(qt, gatet, *consts)


def _t5_bucket(dist):
    max_exact = N_BUCKETS // 2
    d = jnp.maximum(dist, 0)
    df = jnp.maximum(d, 1).astype(F32)
    large = max_exact + (jnp.log(df / max_exact) / math.log(MAX_DIST / max_exact)
                         * (N_BUCKETS - max_exact)).astype(jnp.int32)
    large = jnp.minimum(large, N_BUCKETS - 1)
    return jnp.where(d < max_exact, d, large)


def _template_kernel(f_ref, gx_ref, wt_ref, ct_ref):
    wk, lanes = wt_ref.shape[1], f_ref.shape[1]
    n_band = CMP_R_HI - CMP_R_LO + 1
    lo = CT_ZERO + CMP_R_LO
    for hd in range(N_HEADS):
        g, cols = hd // HPG, slice((hd % HPG) * TQ, (hd % HPG + 1) * TQ)
        x = jnp.broadcast_to(f_ref[hd:hd + 1, :], (wk, lanes))
        y = pltpu.roll(x, lanes - (wk - 1), axis=1, stride=1, stride_axis=0)
        wt_ref[g, :, cols] = y[:, :TQ]
        xb = jnp.broadcast_to(gx_ref[hd:hd + 1, :], (n_band, lanes))
        yb = pltpu.roll(xb, lanes - CMP_STRIDE * (n_band - 1), axis=1,
                        stride=CMP_STRIDE, stride_axis=0)
        ct_ref[g, 0:lo, cols] = jnp.zeros((lo, TQ), F32)
        ct_ref[g, lo:lo + n_band, cols] = yb[:, :TQ]
        ct_ref[g, lo + n_band:, cols] = jnp.full((CT_ROWS - lo - n_band, TQ), NEG, F32)


def _templates(fext, gxext):
    return pl.pallas_call(
        _template_kernel,
        out_shape=[jax.ShapeDtypeStruct((N_KV, WINDOW + TQ, COLS), F32),
                   jax.ShapeDtypeStruct((N_KV, CT_ROWS, COLS), F32)],
        compiler_params=_params(None),
        name="templates",
    )(fext, gxext)


def _bias_templates(rel_bias):
    biasp = rel_bias[:, _t5_bucket(jnp.arange(MAX_DIST))] - rel_bias[:, N_BUCKETS - 1:]
    biasp = biasp * LOG2E

    def by_distance(n_neg, n_zero, n_neg_after):
        return jnp.concatenate(
            [jnp.full((N_HEADS, n_neg), NEG, F32), biasp, jnp.zeros((N_HEADS, n_zero), F32),
             jnp.full((N_HEADS, n_neg_after), NEG, F32)], axis=1)

    wk = WINDOW + TQ
    f = by_distance(TQ - 1, WINDOW - MAX_DIST, wk - WINDOW)
    fext = jnp.concatenate([f, jnp.zeros((N_HEADS, 1), F32)], axis=1)

    d_min = -(CMP_LEN - 1) - CMP_STRIDE * CMP_R_HI
    d_max = TQ - 1 - (CMP_LEN - 1) - CMP_STRIDE * CMP_R_LO
    gx = by_distance(-d_min, d_max + 1 - MAX_DIST, 0)
    gxext = jnp.pad(gx, ((0, 0), (0, fext.shape[1] - gx.shape[1])))
    wtmpl, ctmpl = _templates(fext, gxext)
    return ctmpl, wtmpl


def _overlap_t():
    cmp_start = jnp.arange(N_CMP_PAD) * CMP_STRIDE
    slc_start = jnp.arange(N_SLC) * SLC_LEN
    ov = ((cmp_start[None, :] < slc_start[:, None] + SLC_LEN)
          & (cmp_start[None, :] + CMP_LEN > slc_start[:, None])
          & (jnp.arange(N_CMP_PAD)[None, :] < N_CMP))
    return ov.astype(BF16)


def _compress_block_weights(w1):
    half = CMP_STRIDE * HEAD_DIM
    top = w1[:, :half].reshape(2, CMP_STRIDE, HEAD_DIM, CMP_HIDDEN)
    bot = w1[:, half:].reshape(2, CMP_STRIDE, HEAD_DIM, CMP_HIDDEN)
    z = jnp.zeros_like(top)
    return jnp.concatenate([jnp.concatenate([top, bot, z, z], axis=-1),
                            jnp.concatenate([z, z, top, bot], axis=-1)], axis=2)


def kernel(x, c, w_ada, b_ada, g_ffn1, w_gu1, w_down1, g_mix, w_in, w_dw, b_dw, ln_g, ln_b,
           pe_k, pe_v, w_ck1, w_ck2, w_cv1, w_cv2, rel_bias, w_out, g_ffn2, w_gu2, w_down2,
           g_final):
    assert x.shape == (1, SEQ, D_MODEL) and w_ada.shape[0] == 1
    x2 = x[0]
    mod = _ada(c.reshape(D_MODEL, 1), w_ada[0], b_ada)
    x1 = _ffn1(x2, g_ffn1, mod, w_gu1[0].astype(BF16), w_down1[0].astype(BF16))

    wi = w_in[0]
    wq = wi[:, 1024:1536].reshape(D_MODEL, N_KV, HPG, HEAD_DIM).transpose(0, 2, 1, 3)
    wq = wq.reshape(D_MODEL, D_ATT) * (HEAD_DIM ** -0.5 * LOG2E)
    w_in_p = jnp.concatenate(
        [wi[:, :1024], wq, wi[:, 1536:], jnp.zeros((D_MODEL, D_IN_PAD - wi.shape[1]), F32)],
        axis=1).astype(BF16)
    hc, kcvc, ksl, kw, qt, gatet, vslt, vwt = _proj(x1, g_mix, mod, w_in_p)

    y_conv = _sc_conv(hc, jnp.pad(w_dw[0], ((0, 32 - CONV_WIDTH), (0, 0))),
                      jnp.broadcast_to(b_dw, (SUBLANES, D_CONV)))

    pe = jnp.stack([pe_k[0].reshape(1, -1), pe_v[0].reshape(1, -1)])
    w1 = jnp.stack([w_ck1[0], w_cv1[0]]).astype(BF16)
    w2 = jnp.stack([w_ck2[0], w_cv2[0]]).astype(BF16)
    cmp = _compress(kcvc, pe, w1, _compress_block_weights(w1), w2)
    kc = jnp.concatenate([cmp[0], cmp[1]], axis=1).astype(BF16)
    vct = jnp.swapaxes(cmp[2:4], 1, 2).astype(BF16)

    ctmpl, wtmpl = _bias_templates(rel_bias)
    y_att = _attn(qt, gatet, kc, vct, _overlap_t(), ksl, vslt, kw, vwt, ctmpl, wtmpl)

    wo = w_out[0]
    woa = wo[D_CONV:].reshape(N_KV, HPG, HEAD_DIM, D_MODEL).transpose(1, 0, 2, 3)
    woa = woa.reshape(HPG, LANES, D_MODEL).astype(BF16)
    out = _ffn2(x1, y_conv, y_att, wo[:D_CONV].astype(BF16), woa, mod,
                g_ffn2, w_gu2[0].astype(BF16), w_down2[0].astype(BF16),
                g_final.reshape(1, D_MODEL), ln_g, ln_b)
    return out[None]
```

```python
import functools
import math

import jax
import jax.numpy as jnp
from jax import lax
from jax.experimental import pallas as pl
from jax.experimental.pallas import tpu as pltpu
from jax.experimental.pallas import tpu_sc as plsc

F32 = jnp.float32
BF16 = jnp.bfloat16

D_MODEL = 1024
SEQ = 16384
D_CONV = 512
CONV_WIDTH = 31
N_HEADS = 8
N_KV = 2
HPG = 4
HEAD_DIM = 64
D_ATT = 512
D_KV = 128
CMP_LEN = 32
CMP_STRIDE = 16
CMP_HIDDEN = 128
N_CMP = (SEQ - CMP_LEN) // CMP_STRIDE + 1
N_CMP_PAD = 1024
SLC_LEN = 64
N_SLC = SEQ // SLC_LEN
SLC_TOPK = 16
WINDOW = 512
N_FORCED = 3
TAKEN = -2.0
N_BUCKETS = 32
MAX_DIST = 128
D_FF = 2816
FFN_RES = 0.5
EPS = 1e-6
NEG = -1e30
M_FLOOR = -1e29
LOG2E = math.log2(math.e)

V7X_VMEM_BYTES = 64 * 1024 * 1024
VMEM_LIMIT = V7X_VMEM_BYTES - 6 * 1024 * 1024
LANES = 128
SUBLANES = 8

TQ = 256
TK = 256
COLS = HPG * TQ
KPAD = WINDOW
VROWS = HEAD_DIM + 16
BLK_PER_TILE = TK // SLC_LEN
FAR_UNROLL = 4
SEL_ENTRIES = SEQ // TK + 2
CMP_R_HI = (TQ - CMP_LEN) // CMP_STRIDE
CMP_R_LO = -((MAX_DIST + CMP_LEN - 2) // CMP_STRIDE)
CT_ZERO = TK + -(CMP_R_LO // SUBLANES) * SUBLANES
CT_ROWS = CT_ZERO + -(-(CMP_R_HI + 1) // SUBLANES) * SUBLANES + TK
FFN_TM = 512
FFN_TF = D_FF
PROJ_TM = 512
CONV_HALO = 32
SC_CORES, SC_SUBCORES, SC_LANES = 2, 16, 16
SC_WORKERS = SC_CORES * SC_SUBCORES
SC_ROWS = 64
D_IN_PAD = 2432


def _params(sem):
    return pltpu.CompilerParams(dimension_semantics=sem, vmem_limit_bytes=VMEM_LIMIT)


def _const_spec(shape):
    nd = len(shape)
    return pl.BlockSpec(shape, lambda *_: (0,) * nd, pipeline_mode=pl.Buffered(1))


def _sigmoid(v):
    return 1.0 / (1.0 + jnp.exp(-v))


def _dot(a, b):
    return jnp.dot(a, b, preferred_element_type=F32)


def _ada_kernel(c_ref, w_ref, b_ref, o_ref):
    c = c_ref[...]
    sc = c * _sigmoid(c)
    o_ref[...] = jnp.sum(w_ref[...] * sc, axis=0, keepdims=True) + b_ref[...]


def _ada(c_col, w, b):
    n = w.shape[1]
    tn = n // 8
    return pl.pallas_call(
        _ada_kernel,
        out_shape=jax.ShapeDtypeStruct((1, n), F32),
        grid=(8,),
        in_specs=[pl.BlockSpec((D_MODEL, 1), lambda j: (0, 0)),
                  pl.BlockSpec((D_MODEL, tn), lambda j: (0, j)),
                  pl.BlockSpec((1, tn), lambda j: (0, j))],
        out_specs=pl.BlockSpec((1, tn), lambda j: (0, j)),
        compiler_params=_params(("arbitrary",)),
        name="ada",
    )(c_col, w, b)


def _rms_mod(x, g, sh, sc):
    ms = jnp.mean(x * x, axis=-1, keepdims=True)
    y = x * lax.rsqrt(ms + EPS) * g
    return y * (1.0 + sc) + sh


def _ffn_body(x, g_ref, sh_ref, sc_ref, gt_ref, wgu_ref, wd_ref):
    hb = _rms_mod(x, g_ref[...], sh_ref[...], sc_ref[...]).astype(BF16)
    acc = jnp.zeros((x.shape[0], D_MODEL), F32)
    for f in range(D_FF // FFN_TF):
        gg = _dot(hb, wgu_ref[:, f * FFN_TF:(f + 1) * FFN_TF])
        uu = _dot(hb, wgu_ref[:, D_FF + f * FFN_TF:D_FF + (f + 1) * FFN_TF])
        a = (gg * _sigmoid(gg) * uu).astype(BF16)
        acc = acc + _dot(a, wd_ref[f * FFN_TF:(f + 1) * FFN_TF, :])
    return x + (FFN_RES * gt_ref[...]) * acc


def _ffn1_kernel(x_ref, g_ref, sh_ref, sc_ref, gt_ref, wgu_ref, wd_ref, o_ref):
    o_ref[...] = _ffn_body(x_ref[...], g_ref, sh_ref, sc_ref, gt_ref, wgu_ref, wd_ref)


def _ffn2_kernel(x_ref, yp_ref, ya_ref, woc_ref, woa_ref, gt2_ref,
                 g_ref, sh_ref, sc_ref, gt_ref, wgu_ref, wd_ref, gf_ref, cg_ref, cbb_ref, o_ref):
    tm = x_ref.shape[0]
    cv = yp_ref[...]
    mu = jnp.mean(cv, axis=-1, keepdims=True)
    dlt = cv - mu
    var = jnp.mean(dlt * dlt, axis=-1, keepdims=True)
    yn = dlt * lax.rsqrt(var + EPS) * cg_ref[...] + cbb_ref[...]
    y = _dot((yn * _sigmoid(yn)).astype(BF16), woc_ref[...])
    for h in range(HPG):
        y = y + _dot(ya_ref[:, h].reshape(tm, LANES), woa_ref[h])
    x = x_ref[...] + gt2_ref[...] * y
    out = _ffn_body(x, g_ref, sh_ref, sc_ref, gt_ref, wgu_ref, wd_ref)
    ms = jnp.mean(out * out, axis=-1, keepdims=True)
    o_ref[...] = out * lax.rsqrt(ms + EPS) * gf_ref[...]


def _row_spec(tm, n):
    return pl.BlockSpec((tm, n), lambda i: (i, 0))


def _mod_spec(k):
    return pl.BlockSpec((1, D_MODEL), lambda *_: (0, k), pipeline_mode=pl.Buffered(1))


def _ffn1(x, g, mod, wgu, wd):
    vec = _const_spec((1, D_MODEL))
    return pl.pallas_call(
        _ffn1_kernel,
        out_shape=jax.ShapeDtypeStruct((SEQ, D_MODEL), F32),
        grid=(SEQ // FFN_TM,),
        in_specs=[_row_spec(FFN_TM, D_MODEL), vec, _mod_spec(0), _mod_spec(1), _mod_spec(2),
                  _const_spec((D_MODEL, 2 * D_FF)), _const_spec((D_FF, D_MODEL))],
        out_specs=_row_spec(FFN_TM, D_MODEL),
        compiler_params=_params(("arbitrary",)),
        name="ffn1",
    )(x, g, mod, mod, mod, wgu, wd)


def _ffn2(x, yp, ya, woc, woa, mod, g, wgu, wd, gf, cg, cbb):
    vec = _const_spec((1, D_MODEL))
    cvec = _const_spec((1, D_CONV))
    nt = FFN_TM // TQ
    return pl.pallas_call(
        _ffn2_kernel,
        out_shape=jax.ShapeDtypeStruct((SEQ, D_MODEL), F32),
        grid=(SEQ // FFN_TM,),
        in_specs=[_row_spec(FFN_TM, D_MODEL), _row_spec(FFN_TM, D_CONV),
                  pl.BlockSpec((nt, HPG, TQ, LANES), lambda i: (i, 0, 0, 0)),
                  _const_spec((D_CONV, D_MODEL)), _const_spec((HPG, LANES, D_MODEL)), _mod_spec(5),
                  vec, _mod_spec(6), _mod_spec(7), _mod_spec(8),
                  _const_spec((D_MODEL, 2 * D_FF)), _const_spec((D_FF, D_MODEL)), vec, cvec, cvec],
        out_specs=_row_spec(FFN_TM, D_MODEL),
        compiler_params=_params(("arbitrary",)),
        name="ffn2",
    )(x, yp, ya, woc, woa, mod, g, mod, mod, mod, wgu, wd, gf, cg, cbb)


def _proj_kernel(x_ref, g_ref, sh_ref, sc_ref, w_ref,
                 hc_ref, kcvc_ref, ksl_ref, kw_ref, qt_ref, gatet_ref, vslt_ref, vwt_ref):
    hb = _rms_mod(x_ref[...], g_ref[...], sh_ref[...], sc_ref[...]).astype(BF16)
    u = _dot(hb, w_ref[...])
    hc_ref[...] = u[:, 0:512] * _sigmoid(u[:, 512:1024])
    kcvc_ref[...] = u[:, 1536:1792]
    tm = x_ref.shape[0]
    blk = (lax.broadcasted_iota(jnp.int32, (tm, D_KV), 0) % TK) // SLC_LEN
    ind = (blk == lax.broadcasted_iota(jnp.int32, (tm, D_KV), 1)).astype(BF16)
    ksl_ref[...] = jnp.concatenate([u[:, 1792:1920].astype(BF16), ind], axis=1)
    kw_ref[...] = u[:, 2048:2176].astype(BF16)
    qt_ref[...] = u[:, 1024:1536].T.astype(BF16)
    gatet_ref[...] = _sigmoid(u[:, 2304:2432]).T
    tail = jnp.concatenate([jnp.ones((1, TK), BF16), jnp.zeros((VROWS - HEAD_DIM - 1, TK), BF16)])
    for vt_ref, c0 in ((vslt_ref, 1920), (vwt_ref, 2176)):
        vt = u[:, c0:c0 + D_KV].T.astype(BF16)
        for g in range(N_KV):
            for t in range(tm // TK):
                vt_ref[g, t] = jnp.concatenate(
                    [vt[g * HEAD_DIM:(g + 1) * HEAD_DIM, t * TK:(t + 1) * TK], tail], axis=0)


def _proj(x, g, mod, w):
    vec = _const_spec((1, D_MODEL))
    tm = PROJ_TM
    rows = [(D_CONV, F32), (2 * D_KV, F32), (2 * D_KV, BF16), (D_KV, BF16)]
    cols = [(D_ATT, BF16), (LANES, F32)]
    vt_shape = jax.ShapeDtypeStruct((N_KV, SEQ // TK, VROWS, TK), BF16)
    vt_spec = pl.BlockSpec((N_KV, tm // TK, VROWS, TK), lambda i: (0, i, 0, 0))
    return pl.pallas_call(
        _proj_kernel,
        out_shape=[jax.ShapeDtypeStruct((SEQ, n), dt) for n, dt in rows]
                  + [jax.ShapeDtypeStruct((n, SEQ), dt) for n, dt in cols] + [vt_shape, vt_shape],
        grid=(SEQ // tm,),
        in_specs=[_row_spec(tm, D_MODEL), vec, _mod_spec(3), _mod_spec(4),
                  _const_spec((D_MODEL, D_IN_PAD))],
        out_specs=[_row_spec(tm, n) for n, _ in rows]
                  + [pl.BlockSpec((n, tm), lambda i: (0, i)) for n, _ in cols] + [vt_spec, vt_spec],
        compiler_params=_params(("arbitrary",)),
        name="proj",
    )(x, g, mod, mod, w)


def _sc_conv(hc, w32, b8):
    rows_per_worker = SEQ // SC_WORKERS
    n_chunks = rows_per_worker // SC_ROWS
    halves = (tuple(range(0, 16)), tuple(range(16, CONV_WIDTH)))
    off = CONV_HALO - (CONV_WIDTH - 1)
    mesh = plsc.VectorSubcoreMesh(core_axis_name="c", subcore_axis_name="s")

    @functools.partial(
        pl.kernel, mesh=mesh,
        out_type=jax.ShapeDtypeStruct((SEQ, D_CONV), F32),
        scratch_types=[pltpu.VMEM((CONV_HALO + SC_ROWS, D_CONV), F32),
                       pltpu.VMEM((SC_ROWS, D_CONV), F32),
                       pltpu.VMEM((32, D_CONV), F32),
                       pltpu.VMEM((SUBLANES, D_CONV), F32)],
        compiler_params=pltpu.CompilerParams(use_tc_tiling_on_sc=True),
        name="sc_conv",
    )
    def k(hc_hbm, w_hbm, b_hbm, out_hbm, x_v, o_v, w_v, b_v):
        wid = lax.axis_index("s") * SC_CORES + lax.axis_index("c")
        pltpu.sync_copy(w_hbm, w_v)
        pltpu.sync_copy(b_hbm, b_v)

        @pl.loop(0, n_chunks)
        def _(ci):
            row0 = pl.multiple_of(wid * rows_per_worker + ci * SC_ROWS, SC_ROWS)

            @pl.when(row0 == 0)
            def _():
                @pl.loop(0, CONV_HALO)
                def _(r):
                    @pl.loop(0, D_CONV // SC_LANES)
                    def _(cg):
                        x_v[r, pl.ds(cg * SC_LANES, SC_LANES)] = jnp.zeros((SC_LANES,), F32)
                pltpu.sync_copy(hc_hbm.at[pl.ds(0, SC_ROWS)], x_v.at[pl.ds(CONV_HALO, SC_ROWS)])

            @pl.when(row0 > 0)
            def _():
                pltpu.sync_copy(hc_hbm.at[pl.ds(row0 - CONV_HALO, CONV_HALO + SC_ROWS)], x_v)

            @pl.loop(0, D_CONV // SC_LANES)
            def _(cg):
                lanes = pl.ds(cg * SC_LANES, SC_LANES)
                for hi, taps in enumerate(halves):
                    wv = [w_v[t, lanes] for t in taps]

                    @pl.loop(0, SC_ROWS)
                    def _(r):
                        acc = b_v[0, lanes] if hi == 0 else o_v[r, lanes]
                        for j, t in enumerate(taps):
                            acc = acc + x_v[r + off + t, lanes] * wv[j]
                        o_v[r, lanes] = acc

            pltpu.sync_copy(o_v, out_hbm.at[pl.ds(row0, SC_ROWS)])

    return k(hc, w32, b8)


def _compress_kernel(c_ref, pe_ref, w1_ref, wblk_ref, w2_ref, o_ref):
    half = CMP_STRIDE * HEAD_DIM
    acc = jnp.zeros((N_CMP_PAD, 4 * CMP_HIDDEN), F32)
    for l in range(CMP_STRIDE):
        x = c_ref[pl.ds(l, N_CMP_PAD, stride=CMP_STRIDE), :].astype(BF16)
        acc = acc + _dot(x, wblk_ref[0, l])
    pe = jnp.broadcast_to(pe_ref[0], (SUBLANES, 2 * half)).astype(BF16)
    pe_term = (_dot(pe[:, :half], w1_ref[0, :half, :]) + _dot(pe[:, half:], w1_ref[0, half:, :]))[0:1]
    for g in range(N_KV):
        a = acc[:, 2 * g * CMP_HIDDEN:(2 * g + 1) * CMP_HIDDEN]
        b = acc[:, (2 * g + 1) * CMP_HIDDEN:(2 * g + 2) * CMP_HIDDEN]
        b_up = jnp.concatenate([b[1:], jnp.zeros((1, CMP_HIDDEN), F32)], axis=0)
        pre = a + b_up + pe_term
        hid = (pre * _sigmoid(pre)).astype(BF16)
        o_ref[g] = _dot(hid, w2_ref[0])


def _compress(kcvc, pe, w1, wblk, w2):
    return pl.pallas_call(
        _compress_kernel,
        out_shape=jax.ShapeDtypeStruct((2 * N_KV, N_CMP_PAD, HEAD_DIM), F32),
        grid=(2,),
        in_specs=[pl.BlockSpec((SEQ, D_KV), lambda i: (0, i)),
                  pl.BlockSpec((1, 1, 2048), lambda i: (i, 0, 0)),
                  pl.BlockSpec((1, 2048, CMP_HIDDEN), lambda i: (i, 0, 0)),
                  pl.BlockSpec((1, CMP_STRIDE, D_KV, 4 * CMP_HIDDEN), lambda i: (i, 0, 0, 0)),
                  pl.BlockSpec((1, CMP_HIDDEN, HEAD_DIM), lambda i: (i, 0, 0))],
        out_specs=pl.BlockSpec((N_KV, N_CMP_PAD, HEAD_DIM), lambda i: (i, 0, 0)),
        compiler_params=_params(("arbitrary",)),
        name="compress",
    )(kcvc, pe, w1, wblk, w2)


def _tile_update(s, m, acc, vt):
    m_new = jnp.maximum(m, jnp.max(s, axis=0, keepdims=True))
    alpha = jnp.exp2(m - m_new)
    p = jnp.exp2(s - m_new).astype(BF16)
    return m_new, alpha * acc + _dot(vt, p)


def _finish(acc):
    return acc[:HEAD_DIM] / jnp.maximum(acc[HEAD_DIM:HEAD_DIM + 1], 1e-30)


def _attn_kernel(qt_ref, gt_ref, kc_ref, vct_ref, ovt_ref, ksl_ref, vslt_ref, kw_ref, vwt_ref,
                 ct_ref, wt_ref, o_ref, s_scr, p_scr, sel_scr, wa_scr, wb_scr, sa_scr, sb_scr):
    qb = pl.program_id(0)
    q0 = qb * TQ
    qt = qt_ref[...]
    gt = gt_ref[...]
    drow = lax.broadcasted_iota(jnp.int32, (D_KV, TQ), 0)
    krow = lax.broadcasted_iota(jnp.int32, (TK, 1), 0)
    m0 = jnp.full((1, COLS), M_FLOOR, F32)
    acc0 = jnp.zeros((VROWS, COLS), F32)

    def pad_mask(first_row):
        return jnp.where(krow + first_row >= KPAD, 0.0, NEG)

    def real_row(first_row):
        return pl.multiple_of(jnp.maximum(first_row - KPAD, 0), TK)

    comb = []
    for g in range(N_KV):
        keep = (drow >= HEAD_DIM * g) & (drow < HEAD_DIM * (g + 1))
        q_g = jnp.concatenate(
            [jnp.where(keep, qt[h * D_KV:(h + 1) * D_KV, :], jnp.zeros((), BF16))
             for h in range(HPG)], axis=1)

        mcol = m0
        for ct in range(N_CMP_PAD // TK):
            rows = slice(ct * TK, (ct + 1) * TK)
            r0 = pl.multiple_of(jnp.clip(TK * ct - (TQ // CMP_STRIDE) * qb + CT_ZERO,
                                         0, CT_ROWS - TK), SUBLANES)
            s = _dot(kc_ref[rows, :], q_g) + ct_ref[g, pl.ds(r0, TK), :]
            s_scr[rows, :] = s
            mcol = jnp.maximum(mcol, jnp.max(s, axis=0, keepdims=True))
        lsum = jnp.zeros((1, COLS), F32)
        acc_c = jnp.zeros((HEAD_DIM, COLS), F32)
        for ct in range(N_CMP_PAD // TK):
            rows = slice(ct * TK, (ct + 1) * TK)
            p = jnp.exp2(s_scr[rows, :] - mcol)
            p_scr[rows, :] = p
            lsum = lsum + jnp.sum(p, axis=0, keepdims=True)
            acc_c = acc_c + _dot(vct_ref[g, :, rows], p.astype(BF16))
        rinv = 1.0 / jnp.maximum(lsum, 1e-30)
        o_c = acc_c * rinv
        imp = jnp.zeros((N_SLC, TQ), F32)
        for ct in range(N_CMP_PAD // TK):
            rows = slice(ct * TK, (ct + 1) * TK)
            pn = p_scr[rows, :] * rinv
            ps = pn[:, 0:TQ] + pn[:, TQ:2 * TQ] + pn[:, 2 * TQ:3 * TQ] + pn[:, 3 * TQ:4 * TQ]
            hi = ps.astype(BF16)
            lo = (ps - hi.astype(F32)).astype(BF16)
            ov = ovt_ref[:, rows]
            imp = imp + _dot(ov, hi) + _dot(ov, lo)

        m, acc = m0, acc0
        for a in range(3):
            first = q0 + TK * a
            s = _dot(kw_ref[pl.ds(real_row(first), TK), :], q_g)
            s = s + (wt_ref[g, TK * a:TK * (a + 1), :] + pad_mask(first))
            m, acc = _tile_update(s, m, acc, vwt_ref[g, jnp.maximum(qb + a - KPAD // TK, 0)])
        o_w = _finish(acc)

        cur = jnp.right_shift(q0 + lax.broadcasted_iota(jnp.int32, (1, TQ), 1), 6)
        jcol = lax.broadcasted_iota(jnp.int32, (N_SLC, 1), 0)
        forced = (jcol == 0) | (jcol == cur) | (jcol == cur - 1)
        sc = jnp.where(forced, TAKEN, jnp.where(jcol <= cur, imp, -1.0))
        jf = lax.broadcasted_iota(jnp.int32, (N_SLC, TQ), 0).astype(F32)
        for _ in range(SLC_TOPK - N_FORCED):
            mx = jnp.max(sc, axis=0, keepdims=True)
            idx = jnp.min(jnp.where(sc == mx, jf, 1e9), axis=0, keepdims=True)
            sc = jnp.where(jf == idx, TAKEN, sc)
        selneg = jnp.where(sc == TAKEN, 0.0, NEG)
        unused = jnp.zeros((SUBLANES - BLK_PER_TILE, TQ), F32)
        for kt in range(SEQ // TK):
            sel_scr[kt] = jnp.concatenate(
                [selneg[kt * BLK_PER_TILE:(kt + 1) * BLK_PER_TILE, :], unused], axis=0)
        sel_scr[SEQ // TK] = jnp.concatenate(
            [jnp.full((BLK_PER_TILE, TQ), NEG, F32), unused], axis=0)

        zrows = jnp.zeros((2 * D_KV - D_KV - 2 * SUBLANES, COLS), BF16)
        for w_scr in (wa_scr, wb_scr):
            w_scr[0:D_KV, :] = q_g
            w_scr[D_KV + 2 * SUBLANES:, :] = zrows

        def sel_scores(w_scr, first_row, sel_idx):
            blk = sel_scr[sel_idx]
            rows8 = jnp.concatenate([blk] * HPG, axis=1).astype(BF16)
            w_scr[D_KV:D_KV + 2 * SUBLANES, :] = jnp.concatenate(
                [rows8, jnp.zeros((SUBLANES, COLS), BF16)], axis=0)
            return _dot(ksl_ref[pl.ds(real_row(first_row), TK), :], w_scr[...])

        m, acc = m0, acc0
        for a, w_scr in ((0, wa_scr), (1, wb_scr)):
            first = q0 + TQ + TK * a
            s = sel_scores(w_scr, first, jnp.maximum(qb - 1 + a, 0))
            s = s + (wt_ref[g, TQ + TK * a:TQ + TK * (a + 1), :] + pad_mask(first))
            m, acc = _tile_update(s, m, acc, vslt_ref[g, jnp.maximum(qb - 1 + a, 0)])

        n_far = jnp.maximum(qb - 1, 0)

        def far_scores(w_scr, kt):
            valid = kt < n_far
            ks = jnp.where(valid, kt, 0)
            return sel_scores(w_scr, KPAD + ks * TK, jnp.where(valid, kt, SEQ // TK))

        def far_v(kt):
            return vslt_ref[g, jnp.where(kt < n_far, kt, 0)]

        bufs = ((sa_scr, wa_scr), (sb_scr, wb_scr))

        def issue_scores(buf, kt):
            s_scr_k, w_scr_k = bufs[buf]
            s = far_scores(w_scr_k, kt)
            s_scr_k[...] = s
            return jnp.max(s, axis=0, keepdims=True)

        def far_body(i, carry):
            m_i, acc_i, smax = carry
            for u in range(FAR_UNROLL):
                kt = i * FAR_UNROLL + u
                smax_nxt = issue_scores((u + 1) % 2, kt + 1)
                m_new = jnp.maximum(m_i, smax)
                alpha = jnp.exp2(m_i - m_new)
                p = jnp.exp2(bufs[u % 2][0][...] - m_new).astype(BF16)
                acc_i = alpha * acc_i + _dot(far_v(kt), p)
                m_i, smax = m_new, smax_nxt
            return m_i, acc_i, smax

        trips = (n_far + FAR_UNROLL - 1) // FAR_UNROLL
        m, acc, _ = lax.fori_loop(0, trips, far_body, (m, acc, issue_scores(0, 0)))
        o_s = _finish(acc)

        per_head = []
        for h in range(HPG):
            c0 = 3 * (g * HPG + h)
            cs = slice(h * TQ, (h + 1) * TQ)
            per_head.append(gt[c0:c0 + 1, :] * o_c[:, cs] + gt[c0 + 1:c0 + 2, :] * o_s[:, cs]
                            + gt[c0 + 2:c0 + 3, :] * o_w[:, cs])
        comb.append(per_head)

    for h in range(HPG):
        merged = jnp.concatenate([comb[0][h], comb[1][h]], axis=0)
        o_ref[0, h] = merged.T.astype(BF16)


def _attn(qt, gatet, kc, vct, ovt, ksl, vslt, kw, vwt, ctmpl, wtmpl):
    consts = [kc, vct, ovt, ksl, vslt, kw, vwt, ctmpl, wtmpl]
    return pl.pallas_call(
        _attn_kernel,
        out_shape=jax.ShapeDtypeStruct((SEQ // TQ, HPG, TQ, LANES), BF16),
        grid=(SEQ // TQ,),
        in_specs=[pl.BlockSpec((D_ATT, TQ), lambda i: (0, i)),
                  pl.BlockSpec((LANES, TQ), lambda i: (0, i))]
                 + [_const_spec(a.shape) for a in consts],
        out_specs=pl.BlockSpec((1, HPG, TQ, LANES), lambda i: (i, 0, 0, 0)),
        scratch_shapes=[pltpu.VMEM((N_CMP_PAD, COLS), F32), pltpu.VMEM((N_CMP_PAD, COLS), F32),
                        pltpu.VMEM((SEL_ENTRIES, SUBLANES, TQ), F32),
                        pltpu.VMEM((2 * D_KV, COLS), BF16), pltpu.VMEM((2 * D_KV, COLS), BF16),
                        pltpu.VMEM((TK, COLS), F32), pltpu.VMEM((TK, COLS), F32)],
        compiler_params=_params(("arbitrary",)),
        name="nsa_attn",
    )(qt, gatet, *consts)


def _t5_bucket(dist):
    max_exact = N_BUCKETS // 2
    d = jnp.maximum(dist, 0)
    df = jnp.maximum(d, 1).astype(F32)
    large = max_exact + (jnp.log(df / max_exact) / math.log(MAX_DIST / max_exact)
                         * (N_BUCKETS - max_exact)).astype(jnp.int32)
    large = jnp.minimum(large, N_BUCKETS - 1)
    return jnp.where(d < max_exact, d, large)


def _template_kernel(f_ref, gx_ref, wt_ref, ct_ref):
    wk, lanes = wt_ref.shape[1], f_ref.shape[1]
    n_band = CMP_R_HI - CMP_R_LO + 1
    lo = CT_ZERO + CMP_R_LO
    for hd in range(N_HEADS):
        g, cols = hd // HPG, slice((hd % HPG) * TQ, (hd % HPG + 1) * TQ)
        x = jnp.broadcast_to(f_ref[hd:hd + 1, :], (wk, lanes))
        y = pltpu.roll(x, lanes - (wk - 1), axis=1, stride=1, stride_axis=0)
        wt_ref[g, :, cols] = y[:, :TQ]
        xb = jnp.broadcast_to(gx_ref[hd:hd + 1, :], (n_band, lanes))
        yb = pltpu.roll(xb, lanes - CMP_STRIDE * (n_band - 1), axis=1,
                        stride=CMP_STRIDE, stride_axis=0)
        ct_ref[g, 0:lo, cols] = jnp.zeros((lo, TQ), F32)
        ct_ref[g, lo:lo + n_band, cols] = yb[:, :TQ]
        ct_ref[g, lo + n_band:, cols] = jnp.full((CT_ROWS - lo - n_band, TQ), NEG, F32)


def _templates(fext, gxext):
    return pl.pallas_call(
        _template_kernel,
        out_shape=[jax.ShapeDtypeStruct((N_KV, WINDOW + TQ, COLS), F32),
                   jax.ShapeDtypeStruct((N_KV, CT_ROWS, COLS), F32)],
        compiler_params=_params(None),
        name="templates",
    )(fext, gxext)


def _bias_templates(rel_bias):
    biasp = rel_bias[:, _t5_bucket(jnp.arange(MAX_DIST))] - rel_bias[:, N_BUCKETS - 1:]
    biasp = biasp * LOG2E

    def by_distance(n_neg, n_zero, n_neg_after):
        return jnp.concatenate(
            [jnp.full((N_HEADS, n_neg), NEG, F32), biasp, jnp.zeros((N_HEADS, n_zero), F32),
             jnp.full((N_HEADS, n_neg_after), NEG, F32)], axis=1)

    wk = WINDOW + TQ
    f = by_distance(TQ - 1, WINDOW - MAX_DIST, wk - WINDOW)
    fext = jnp.concatenate([f, jnp.zeros((N_HEADS, 1), F32)], axis=1)

    d_min = -(CMP_LEN - 1) - CMP_STRIDE * CMP_R_HI
    d_max = TQ - 1 - (CMP_LEN - 1) - CMP_STRIDE * CMP_R_LO
    gx = by_distance(-d_min, d_max + 1 - MAX_DIST, 0)
    gxext = jnp.pad(gx, ((0, 0), (0, fext.shape[1] - gx.shape[1])))
    wtmpl, ctmpl = _templates(fext, gxext)
    return ctmpl, wtmpl


def _overlap_t():
    cmp_start = jnp.arange(N_CMP_PAD) * CMP_STRIDE
    slc_start = jnp.arange(N_SLC) * SLC_LEN
    ov = ((cmp_start[None, :] < slc_start[:, None] + SLC_LEN)
          & (cmp_start[None, :] + CMP_LEN > slc_start[:, None])
          & (jnp.arange(N_CMP_PAD)[None, :] < N_CMP))
    return ov.astype(BF16)


def _compress_block_weights(w1):
    half = CMP_STRIDE * HEAD_DIM
    top = w1[:, :half].reshape(2, CMP_STRIDE, HEAD_DIM, CMP_HIDDEN)
    bot = w1[:, half:].reshape(2, CMP_STRIDE, HEAD_DIM, CMP_HIDDEN)
    z = jnp.zeros_like(top)
    return jnp.concatenate([jnp.concatenate([top, bot, z, z], axis=-1),
                            jnp.concatenate([z, z, top, bot], axis=-1)], axis=2)


def kernel(x, c, w_ada, b_ada, g_ffn1, w_gu1, w_down1, g_mix, w_in, w_dw, b_dw, ln_g, ln_b,
           pe_k, pe_v, w_ck1, w_ck2, w_cv1, w_cv2, rel_bias, w_out, g_ffn2, w_gu2, w_down2,
           g_final):
    assert x.shape == (1, SEQ, D_MODEL) and w_ada.shape[0] == 1
    x2 = x[0]
    mod = _ada(c.reshape(D_MODEL, 1), w_ada[0], b_ada)
    x1 = _ffn1(x2, g_ffn1, mod, w_gu1[0].astype(BF16), w_down1[0].astype(BF16))

    wi = w_in[0]
    wq = wi[:, 1024:1536].reshape(D_MODEL, N_KV, HPG, HEAD_DIM).transpose(0, 2, 1, 3)
    wq = wq.reshape(D_MODEL, D_ATT) * (HEAD_DIM ** -0.5 * LOG2E)
    w_in_p = jnp.concatenate(
        [wi[:, :1024], wq, wi[:, 1536:], jnp.zeros((D_MODEL, D_IN_PAD - wi.shape[1]), F32)],
        axis=1).astype(BF16)
    hc, kcvc, ksl, kw, qt, gatet, vslt, vwt = _proj(x1, g_mix, mod, w_in_p)

    pe = jnp.stack([pe_k[0].reshape(1, -1), pe_v[0].reshape(1, -1)])
    w1 = jnp.stack([w_ck1[0], w_cv1[0]]).astype(BF16)
    w2 = jnp.stack([w_ck2[0], w_cv2[0]]).astype(BF16)
    cmp = _compress(kcvc, pe, w1, _compress_block_weights(w1), w2)
    kc = jnp.concatenate([cmp[0], cmp[1]], axis=1).astype(BF16)
    vct = jnp.swapaxes(cmp[2:4], 1, 2).astype(BF16)

    ctmpl, wtmpl = _bias_templates(rel_bias)
    y_att = _attn(qt, gatet, kc, vct, _overlap_t(), ksl, vslt, kw, vwt, ctmpl, wtmpl)

    y_conv = _sc_conv(hc, jnp.pad(w_dw[0], ((0, 32 - CONV_WIDTH), (0, 0))),
                      jnp.broadcast_to(b_dw, (SUBLANES, D_CONV)))

    wo = w_out[0]
    woa = wo[D_CONV:].reshape(N_KV, HPG, HEAD_DIM, D_MODEL).transpose(1, 0, 2, 3)
    woa = woa.reshape(HPG, LANES, D_MODEL).astype(BF16)
    out = _ffn2(x1, y_conv, y_att, wo[:D_CONV].astype(BF16), woa, mod,
                g_ffn2, w_gu2[0].astype(BF16), w_down2[0].astype(BF16),
                g_final.reshape(1, D_MODEL), ln_g, ln_b)
    return out[None]
```

```python
import functools
import math

import jax
import jax.numpy as jnp
from jax import lax
from jax.experimental import pallas as pl
from jax.experimental.pallas import tpu as pltpu
from jax.experimental.pallas import tpu_sc as plsc

F32 = jnp.float32
BF16 = jnp.bfloat16

D_MODEL = 1024
SEQ = 16384
D_CONV = 512
CONV_WIDTH = 31
N_HEADS = 8
N_KV = 2
HPG = 4
HEAD_DIM = 64
D_ATT = 512
D_KV = 128
CMP_LEN = 32
CMP_STRIDE = 16
CMP_HIDDEN = 128
N_CMP = (SEQ - CMP_LEN) // CMP_STRIDE + 1
N_CMP_PAD = 1024
SLC_LEN = 64
N_SLC = SEQ // SLC_LEN
SLC_TOPK = 16
WINDOW = 512
N_FORCED = 3
TAKEN = -2.0
N_BUCKETS = 32
MAX_DIST = 128
D_FF = 2816
FFN_RES = 0.5
EPS = 1e-6
NEG = -1e30
M_FLOOR = -1e29
LOG2E = math.log2(math.e)

V7X_VMEM_BYTES = 64 * 1024 * 1024
VMEM_LIMIT = V7X_VMEM_BYTES - 6 * 1024 * 1024
LANES = 128
SUBLANES = 8

TQ = 256
TK = 256
COLS = HPG * TQ
KPAD = WINDOW
VROWS = HEAD_DIM + 16
BLK_PER_TILE = TK // SLC_LEN
FAR_UNROLL = 4
SEL_ENTRIES = SEQ // TK + 2
CMP_R_HI = (TQ - CMP_LEN) // CMP_STRIDE
CMP_R_LO = -((MAX_DIST + CMP_LEN - 2) // CMP_STRIDE)
CT_ZERO = TK + -(CMP_R_LO // SUBLANES) * SUBLANES
CT_ROWS = CT_ZERO + -(-(CMP_R_HI + 1) // SUBLANES) * SUBLANES + TK
FFN_TM = 512
FFN_TF = D_FF
PROJ_TM = 512
CONV_HALO = 32
SC_CORES, SC_SUBCORES, SC_LANES = 2, 16, 16
SC_WORKERS = SC_CORES * SC_SUBCORES
SC_ROWS = 64
D_IN_PAD = 2432


def _params(sem):
    return pltpu.CompilerParams(dimension_semantics=sem, vmem_limit_bytes=VMEM_LIMIT)


def _const_spec(shape):
    nd = len(shape)
    return pl.BlockSpec(shape, lambda *_: (0,) * nd, pipeline_mode=pl.Buffered(1))


def _sigmoid(v):
    return 1.0 / (1.0 + jnp.exp(-v))


def _dot(a, b):
    return jnp.dot(a, b, preferred_element_type=F32)


def _ada_kernel(c_ref, w_ref, b_ref, o_ref):
    c = c_ref[...]
    sc = c * _sigmoid(c)
    o_ref[...] = jnp.sum(w_ref[...] * sc, axis=0, keepdims=True) + b_ref[...]


def _ada(c_col, w, b):
    n = w.shape[1]
    tn = n // 8
    return pl.pallas_call(
        _ada_kernel,
        out_shape=jax.ShapeDtypeStruct((1, n), F32),
        grid=(8,),
        in_specs=[pl.BlockSpec((D_MODEL, 1), lambda j: (0, 0)),
                  pl.BlockSpec((D_MODEL, tn), lambda j: (0, j)),
                  pl.BlockSpec((1, tn), lambda j: (0, j))],
        out_specs=pl.BlockSpec((1, tn), lambda j: (0, j)),
        compiler_params=_params(("arbitrary",)),
        name="ada",
    )(c_col, w, b)


def _rms_mod(x, g, sh, sc):
    ms = jnp.mean(x * x, axis=-1, keepdims=True)
    y = x * lax.rsqrt(ms + EPS) * g
    return y * (1.0 + sc) + sh


def _ffn_body(x, g_ref, sh_ref, sc_ref, gt_ref, wgu_ref, wd_ref):
    hb = _rms_mod(x, g_ref[...], sh_ref[...], sc_ref[...]).astype(BF16)
    acc = jnp.zeros((x.shape[0], D_MODEL), F32)
    for f in range(D_FF // FFN_TF):
        gg = _dot(hb, wgu_ref[:, f * FFN_TF:(f + 1) * FFN_TF])
        uu = _dot(hb, wgu_ref[:, D_FF + f * FFN_TF:D_FF + (f + 1) * FFN_TF])
        a = (gg * _sigmoid(gg) * uu).astype(BF16)
        acc = acc + _dot(a, wd_ref[f * FFN_TF:(f + 1) * FFN_TF, :])
    return x + (FFN_RES * gt_ref[...]) * acc


def _ffn1_kernel(x_ref, g_ref, sh_ref, sc_ref, gt_ref, wgu_ref, wd_ref, o_ref):
    o_ref[...] = _ffn_body(x_ref[...], g_ref, sh_ref, sc_ref, gt_ref, wgu_ref, wd_ref)


def _ffn2_kernel(x_ref, yp_ref, ya_ref, wo_ref, gt2_ref,
                 g_ref, sh_ref, sc_ref, gt_ref, wgu_ref, wd_ref, gf_ref, cg_ref, cbb_ref, o_ref):
    tm = x_ref.shape[0]
    cv = yp_ref[...]
    mu = jnp.mean(cv, axis=-1, keepdims=True)
    dlt = cv - mu
    var = jnp.mean(dlt * dlt, axis=-1, keepdims=True)
    yn = dlt * lax.rsqrt(var + EPS) * cg_ref[...] + cbb_ref[...]
    y = _dot((yn * _sigmoid(yn)).astype(BF16), wo_ref[0:D_CONV, :])
    for h in range(HPG):
        rows = [slice(D_CONV + (g * HPG + h) * HEAD_DIM, D_CONV + (g * HPG + h + 1) * HEAD_DIM)
                for g in range(N_KV)]
        w_h = jnp.concatenate([wo_ref[r, :] for r in rows], axis=0)
        y = y + _dot(ya_ref[:, h].reshape(tm, LANES), w_h)
    x = x_ref[...] + gt2_ref[...] * y
    out = _ffn_body(x, g_ref, sh_ref, sc_ref, gt_ref, wgu_ref, wd_ref)
    ms = jnp.mean(out * out, axis=-1, keepdims=True)
    o_ref[...] = out * lax.rsqrt(ms + EPS) * gf_ref[...]


def _row_spec(tm, n):
    return pl.BlockSpec((tm, n), lambda i: (i, 0))


def _mod_spec(k):
    return pl.BlockSpec((1, D_MODEL), lambda *_: (0, k), pipeline_mode=pl.Buffered(1))


def _ffn1(x, g, mod, wgu, wd):
    vec = _const_spec((1, D_MODEL))
    return pl.pallas_call(
        _ffn1_kernel,
        out_shape=jax.ShapeDtypeStruct((SEQ, D_MODEL), F32),
        grid=(SEQ // FFN_TM,),
        in_specs=[_row_spec(FFN_TM, D_MODEL), vec, _mod_spec(0), _mod_spec(1), _mod_spec(2),
                  _const_spec((D_MODEL, 2 * D_FF)), _const_spec((D_FF, D_MODEL))],
        out_specs=_row_spec(FFN_TM, D_MODEL),
        compiler_params=_params(("arbitrary",)),
        name="ffn1",
    )(x, g, mod, mod, mod, wgu, wd)


def _ffn2(x, yp, ya, wo, mod, g, wgu, wd, gf, cg, cbb):
    vec = _const_spec((1, D_MODEL))
    cvec = _const_spec((1, D_CONV))
    nt = FFN_TM // TQ
    return pl.pallas_call(
        _ffn2_kernel,
        out_shape=jax.ShapeDtypeStruct((SEQ, D_MODEL), F32),
        grid=(SEQ // FFN_TM,),
        in_specs=[_row_spec(FFN_TM, D_MODEL), _row_spec(FFN_TM, D_CONV),
                  pl.BlockSpec((nt, HPG, TQ, LANES), lambda i: (i, 0, 0, 0)),
                  _const_spec((D_CONV + D_ATT, D_MODEL)), _mod_spec(5),
                  vec, _mod_spec(6), _mod_spec(7), _mod_spec(8),
                  _const_spec((D_MODEL, 2 * D_FF)), _const_spec((D_FF, D_MODEL)), vec, cvec, cvec],
        out_specs=_row_spec(FFN_TM, D_MODEL),
        compiler_params=_params(("arbitrary",)),
        name="ffn2",
    )(x, yp, ya, wo, mod, g, mod, mod, mod, wgu, wd, gf, cg, cbb)


def _proj_kernel(x_ref, g_ref, sh_ref, sc_ref, w_ref,
                 hc_ref, kcvc_ref, ksl_ref, kw_ref, qt_ref, gatet_ref, vslt_ref, vwt_ref):
    hb = _rms_mod(x_ref[...], g_ref[...], sh_ref[...], sc_ref[...]).astype(BF16)
    u = _dot(hb, w_ref[...])
    hc_ref[...] = u[:, 0:512] * _sigmoid(u[:, 512:1024])
    kcvc_ref[...] = u[:, 1536:1792]
    tm = x_ref.shape[0]
    blk = (lax.broadcasted_iota(jnp.int32, (tm, D_KV), 0) % TK) // SLC_LEN
    ind = (blk == lax.broadcasted_iota(jnp.int32, (tm, D_KV), 1)).astype(BF16)
    ksl_ref[...] = jnp.concatenate([u[:, 1792:1920].astype(BF16), ind], axis=1)
    kw_ref[...] = u[:, 2048:2176].astype(BF16)
    qt_ref[...] = u[:, 1024:1536].T.astype(BF16)
    gatet_ref[...] = _sigmoid(u[:, 2304:2432]).T
    tail = jnp.concatenate([jnp.ones((1, TK), BF16), jnp.zeros((VROWS - HEAD_DIM - 1, TK), BF16)])
    for vt_ref, c0 in ((vslt_ref, 1920), (vwt_ref, 2176)):
        vt = u[:, c0:c0 + D_KV].T.astype(BF16)
        for g in range(N_KV):
            for t in range(tm // TK):
                vt_ref[g, t] = jnp.concatenate(
                    [vt[g * HEAD_DIM:(g + 1) * HEAD_DIM, t * TK:(t + 1) * TK], tail], axis=0)


def _proj(x, g, mod, w):
    vec = _const_spec((1, D_MODEL))
    tm = PROJ_TM
    rows = [(D_CONV, F32), (2 * D_KV, F32), (2 * D_KV, BF16), (D_KV, BF16)]
    cols = [(D_ATT, BF16), (LANES, F32)]
    vt_shape = jax.ShapeDtypeStruct((N_KV, SEQ // TK, VROWS, TK), BF16)
    vt_spec = pl.BlockSpec((N_KV, tm // TK, VROWS, TK), lambda i: (0, i, 0, 0))
    return pl.pallas_call(
        _proj_kernel,
        out_shape=[jax.ShapeDtypeStruct((SEQ, n), dt) for n, dt in rows]
                  + [jax.ShapeDtypeStruct((n, SEQ), dt) for n, dt in cols] + [vt_shape, vt_shape],
        grid=(SEQ // tm,),
        in_specs=[_row_spec(tm, D_MODEL), vec, _mod_spec(3), _mod_spec(4),
                  _const_spec((D_MODEL, D_IN_PAD))],
        out_specs=[_row_spec(tm, n) for n, _ in rows]
                  + [pl.BlockSpec((n, tm), lambda i: (0, i)) for n, _ in cols] + [vt_spec, vt_spec],
        compiler_params=_params(("arbitrary",)),
        name="proj",
    )(x, g, mod, mod, w)


def _sc_conv(hc, w32, b8):
    rows_per_worker = SEQ // SC_WORKERS
    n_chunks = rows_per_worker // SC_ROWS
    halves = (tuple(range(0, 16)), tuple(range(16, CONV_WIDTH)))
    off = CONV_HALO - (CONV_WIDTH - 1)
    mesh = plsc.VectorSubcoreMesh(core_axis_name="c", subcore_axis_name="s")

    @functools.partial(
        pl.kernel, mesh=mesh,
        out_type=jax.ShapeDtypeStruct((SEQ, D_CONV), F32),
        scratch_types=[pltpu.VMEM((CONV_HALO + SC_ROWS, D_CONV), F32),
                       pltpu.VMEM((SC_ROWS, D_CONV), F32),
                       pltpu.VMEM((32, D_CONV), F32),
                       pltpu.VMEM((SUBLANES, D_CONV), F32)],
        compiler_params=pltpu.CompilerParams(use_tc_tiling_on_sc=True),
        name="sc_conv",
    )
    def k(hc_hbm, w_hbm, b_hbm, out_hbm, x_v, o_v, w_v, b_v):
        wid = lax.axis_index("s") * SC_CORES + lax.axis_index("c")
        pltpu.sync_copy(w_hbm, w_v)
        pltpu.sync_copy(b_hbm, b_v)

        @pl.loop(0, n_chunks)
        def _(ci):
            row0 = pl.multiple_of(wid * rows_per_worker + ci * SC_ROWS, SC_ROWS)

            @pl.when(row0 == 0)
            def _():
                @pl.loop(0, CONV_HALO)
                def _(r):
                    @pl.loop(0, D_CONV // SC_LANES)
                    def _(cg):
                        x_v[r, pl.ds(cg * SC_LANES, SC_LANES)] = jnp.zeros((SC_LANES,), F32)
                pltpu.sync_copy(hc_hbm.at[pl.ds(0, SC_ROWS)], x_v.at[pl.ds(CONV_HALO, SC_ROWS)])

            @pl.when(row0 > 0)
            def _():
                pltpu.sync_copy(hc_hbm.at[pl.ds(row0 - CONV_HALO, CONV_HALO + SC_ROWS)], x_v)

            @pl.loop(0, D_CONV // SC_LANES)
            def _(cg):
                lanes = pl.ds(cg * SC_LANES, SC_LANES)
                for hi, taps in enumerate(halves):
                    wv = [w_v[t, lanes] for t in taps]

                    @pl.loop(0, SC_ROWS)
                    def _(r):
                        acc = b_v[0, lanes] if hi == 0 else o_v[r, lanes]
                        for j, t in enumerate(taps):
                            acc = acc + x_v[r + off + t, lanes] * wv[j]
                        o_v[r, lanes] = acc

            pltpu.sync_copy(o_v, out_hbm.at[pl.ds(row0, SC_ROWS)])

    return k(hc, w32, b8)


def _compress_kernel(c_ref, pe_ref, w1_ref, wblk_ref, w2_ref, o_ref):
    half = CMP_STRIDE * HEAD_DIM
    acc = jnp.zeros((N_CMP_PAD, 4 * CMP_HIDDEN), F32)
    for l in range(CMP_STRIDE):
        x = c_ref[pl.ds(l, N_CMP_PAD, stride=CMP_STRIDE), :].astype(BF16)
        acc = acc + _dot(x, wblk_ref[0, l])
    pe = jnp.broadcast_to(pe_ref[0], (SUBLANES, 2 * half)).astype(BF16)
    pe_term = (_dot(pe[:, :half], w1_ref[0, :half, :]) + _dot(pe[:, half:], w1_ref[0, half:, :]))[0:1]
    for g in range(N_KV):
        a = acc[:, 2 * g * CMP_HIDDEN:(2 * g + 1) * CMP_HIDDEN]
        b = acc[:, (2 * g + 1) * CMP_HIDDEN:(2 * g + 2) * CMP_HIDDEN]
        b_up = jnp.concatenate([b[1:], jnp.zeros((1, CMP_HIDDEN), F32)], axis=0)
        pre = a + b_up + pe_term
        hid = (pre * _sigmoid(pre)).astype(BF16)
        o_ref[g] = _dot(hid, w2_ref[0])


def _compress(kcvc, pe, w1, wblk, w2):
    return pl.pallas_call(
        _compress_kernel,
        out_shape=jax.ShapeDtypeStruct((2 * N_KV, N_CMP_PAD, HEAD_DIM), F32),
        grid=(2,),
        in_specs=[pl.BlockSpec((SEQ, D_KV), lambda i: (0, i)),
                  pl.BlockSpec((1, 1, 2048), lambda i: (i, 0, 0)),
                  pl.BlockSpec((1, 2048, CMP_HIDDEN), lambda i: (i, 0, 0)),
                  pl.BlockSpec((1, CMP_STRIDE, D_KV, 4 * CMP_HIDDEN), lambda i: (i, 0, 0, 0)),
                  pl.BlockSpec((1, CMP_HIDDEN, HEAD_DIM), lambda i: (i, 0, 0))],
        out_specs=pl.BlockSpec((N_KV, N_CMP_PAD, HEAD_DIM), lambda i: (i, 0, 0)),
        compiler_params=_params(("arbitrary",)),
        name="compress",
    )(kcvc, pe, w1, wblk, w2)


def _tile_update(s, m, acc, vt):
    m_new = jnp.maximum(m, jnp.max(s, axis=0, keepdims=True))
    alpha = jnp.exp2(m - m_new)
    p = jnp.exp2(s - m_new).astype(BF16)
    return m_new, alpha * acc + _dot(vt, p)


def _finish(acc):
    return acc[:HEAD_DIM] / jnp.maximum(acc[HEAD_DIM:HEAD_DIM + 1], 1e-30)


def _attn_kernel(qt_ref, gt_ref, kc_ref, vct_ref, ovt_ref, ksl_ref, vslt_ref, kw_ref, vwt_ref,
                 ct_ref, wt_ref, o_ref, s_scr, p_scr, sel_scr, wa_scr, wb_scr, sa_scr, sb_scr):
    qb = pl.program_id(0)
    q0 = qb * TQ
    qt = qt_ref[...]
    gt = gt_ref[...]
    krow = lax.broadcasted_iota(jnp.int32, (TK, 1), 0)
    m0 = jnp.full((1, COLS), M_FLOOR, F32)
    acc0 = jnp.zeros((VROWS, COLS), F32)

    def pad_mask(first_row):
        return jnp.where(krow + first_row >= KPAD, 0.0, NEG)

    def real_row(first_row):
        return pl.multiple_of(jnp.maximum(first_row - KPAD, 0), TK)

    comb = []
    zq = jnp.zeros((HEAD_DIM, TQ), BF16)
    for g in range(N_KV):
        heads = [qt[(g * HPG + h) * HEAD_DIM:(g * HPG + h + 1) * HEAD_DIM, :] for h in range(HPG)]
        q_g = jnp.concatenate(
            [jnp.concatenate([q_h, zq] if g == 0 else [zq, q_h], axis=0) for q_h in heads],
            axis=1)

        mcol = m0
        for ct in range(N_CMP_PAD // TK):
            rows = slice(ct * TK, (ct + 1) * TK)
            r0 = pl.multiple_of(jnp.clip(TK * ct - (TQ // CMP_STRIDE) * qb + CT_ZERO,
                                         0, CT_ROWS - TK), SUBLANES)
            s = _dot(kc_ref[rows, :], q_g) + ct_ref[g, pl.ds(r0, TK), :]
            s_scr[rows, :] = s
            mcol = jnp.maximum(mcol, jnp.max(s, axis=0, keepdims=True))
        lsum = jnp.zeros((1, COLS), F32)
        acc_c = jnp.zeros((HEAD_DIM, COLS), F32)
        for ct in range(N_CMP_PAD // TK):
            rows = slice(ct * TK, (ct + 1) * TK)
            p = jnp.exp2(s_scr[rows, :] - mcol)
            p_scr[rows, :] = p
            lsum = lsum + jnp.sum(p, axis=0, keepdims=True)
            acc_c = acc_c + _dot(vct_ref[g, :, rows], p.astype(BF16))
        rinv = 1.0 / jnp.maximum(lsum, 1e-30)
        o_c = acc_c * rinv
        imp = jnp.zeros((N_SLC, TQ), F32)
        for ct in range(N_CMP_PAD // TK):
            rows = slice(ct * TK, (ct + 1) * TK)
            pn = p_scr[rows, :] * rinv
            ps = pn[:, 0:TQ] + pn[:, TQ:2 * TQ] + pn[:, 2 * TQ:3 * TQ] + pn[:, 3 * TQ:4 * TQ]
            hi = ps.astype(BF16)
            lo = (ps - hi.astype(F32)).astype(BF16)
            ov = ovt_ref[:, rows]
            imp = imp + _dot(ov, hi) + _dot(ov, lo)

        m, acc = m0, acc0
        for a in range(3):
            first = q0 + TK * a
            s = _dot(kw_ref[pl.ds(real_row(first), TK), :], q_g)
            s = s + (wt_ref[g, TK * a:TK * (a + 1), :] + pad_mask(first))
            m, acc = _tile_update(s, m, acc, vwt_ref[g, jnp.maximum(qb + a - KPAD // TK, 0)])
        o_w = _finish(acc)

        cur = jnp.right_shift(q0 + lax.broadcasted_iota(jnp.int32, (1, TQ), 1), 6)
        jcol = lax.broadcasted_iota(jnp.int32, (N_SLC, 1), 0)
        forced = (jcol == 0) | (jcol == cur) | (jcol == cur - 1)
        sc = jnp.where(forced, TAKEN, jnp.where(jcol <= cur, imp, -1.0))
        jf = lax.broadcasted_iota(jnp.int32, (N_SLC, TQ), 0).astype(F32)
        for _ in range(SLC_TOPK - N_FORCED):
            mx = jnp.max(sc, axis=0, keepdims=True)
            idx = jnp.min(jnp.where(sc == mx, jf, 1e9), axis=0, keepdims=True)
            sc = jnp.where(jf == idx, TAKEN, sc)
        selneg = jnp.where(sc == TAKEN, 0.0, NEG)
        unused = jnp.zeros((SUBLANES - BLK_PER_TILE, TQ), F32)
        for kt in range(SEQ // TK):
            sel_scr[kt] = jnp.concatenate(
                [selneg[kt * BLK_PER_TILE:(kt + 1) * BLK_PER_TILE, :], unused], axis=0)
        sel_scr[SEQ // TK] = jnp.concatenate(
            [jnp.full((BLK_PER_TILE, TQ), NEG, F32), unused], axis=0)

        zrows = jnp.zeros((2 * D_KV - D_KV - 2 * SUBLANES, COLS), BF16)
        for w_scr in (wa_scr, wb_scr):
            w_scr[0:D_KV, :] = q_g
            w_scr[D_KV + 2 * SUBLANES:, :] = zrows

        def sel_scores(w_scr, first_row, sel_idx):
            blk = sel_scr[sel_idx]
            rows8 = jnp.concatenate([blk] * HPG, axis=1).astype(BF16)
            w_scr[D_KV:D_KV + 2 * SUBLANES, :] = jnp.concatenate(
                [rows8, jnp.zeros((SUBLANES, COLS), BF16)], axis=0)
            return _dot(ksl_ref[pl.ds(real_row(first_row), TK), :], w_scr[...])

        m, acc = m0, acc0
        for a, w_scr in ((0, wa_scr), (1, wb_scr)):
            first = q0 + TQ + TK * a
            s = sel_scores(w_scr, first, jnp.maximum(qb - 1 + a, 0))
            s = s + (wt_ref[g, TQ + TK * a:TQ + TK * (a + 1), :] + pad_mask(first))
            m, acc = _tile_update(s, m, acc, vslt_ref[g, jnp.maximum(qb - 1 + a, 0)])

        n_far = jnp.maximum(qb - 1, 0)

        def far_scores(w_scr, kt):
            valid = kt < n_far
            ks = jnp.where(valid, kt, 0)
            return sel_scores(w_scr, KPAD + ks * TK, jnp.where(valid, kt, SEQ // TK))

        def far_v(kt):
            return vslt_ref[g, jnp.where(kt < n_far, kt, 0)]

        bufs = ((sa_scr, wa_scr), (sb_scr, wb_scr))

        def issue_scores(buf, kt):
            s_scr_k, w_scr_k = bufs[buf]
            s = far_scores(w_scr_k, kt)
            s_scr_k[...] = s
            return jnp.max(s, axis=0, keepdims=True)

        def far_body(i, carry):
            m_i, acc_i, smax = carry
            for u in range(FAR_UNROLL):
                kt = i * FAR_UNROLL + u
                smax_nxt = issue_scores((u + 1) % 2, kt + 1)
                m_new = jnp.maximum(m_i, smax)
                alpha = jnp.exp2(m_i - m_new)
                p = jnp.exp2(bufs[u % 2][0][...] - m_new).astype(BF16)
                acc_i = alpha * acc_i + _dot(far_v(kt), p)
                m_i, smax = m_new, smax_nxt
            return m_i, acc_i, smax

        trips = (n_far + FAR_UNROLL - 1) // FAR_UNROLL
        m, acc, _ = lax.fori_loop(0, trips, far_body, (m, acc, issue_scores(0, 0)))
        o_s = _finish(acc)

        per_head = []
        for h in range(HPG):
            c0 = 3 * (g * HPG + h)
            cs = slice(h * TQ, (h + 1) * TQ)
            per_head.append(gt[c0:c0 + 1, :] * o_c[:, cs] + gt[c0 + 1:c0 + 2, :] * o_s[:, cs]
                            + gt[c0 + 2:c0 + 3, :] * o_w[:, cs])
        comb.append(per_head)

    for h in range(HPG):
        merged = jnp.concatenate([comb[0][h], comb[1][h]], axis=0)
        o_ref[0, h] = merged.T.astype(BF16)


def _attn(qt, gatet, kc, vct, ovt, ksl, vslt, kw, vwt, ctmpl, wtmpl):
    consts = [kc, vct, ovt, ksl, vslt, kw, vwt, ctmpl, wtmpl]
    return pl.pallas_call(
        _attn_kernel,
        out_shape=jax.ShapeDtypeStruct((SEQ // TQ, HPG, TQ, LANES), BF16),
        grid=(SEQ // TQ,),
        in_specs=[pl.BlockSpec((D_ATT, TQ), lambda i: (0, i)),
                  pl.BlockSpec((LANES, TQ), lambda i: (0, i))]
                 + [_const_spec(a.shape) for a in consts],
        out_specs=pl.BlockSpec((1, HPG, TQ, LANES), lambda i: (i, 0, 0, 0)),
        scratch_shapes=[pltpu.VMEM((N_CMP_PAD, COLS), F32), pltpu.VMEM((N_CMP_PAD, COLS), F32),
                        pltpu.VMEM((SEL_ENTRIES, SUBLANES, TQ), F32),
                        pltpu.VMEM((2 * D_KV, COLS), BF16), pltpu.VMEM((2 * D_KV, COLS), BF16),
                        pltpu.VMEM((TK, COLS), F32), pltpu.VMEM((TK, COLS), F32)],
        compiler_params=_params(("arbitrary",)),
        name="nsa_attn",
    )(qt, gatet, *consts)


def _t5_bucket(dist):
    max_exact = N_BUCKETS // 2
    d = jnp.maximum(dist, 0)
    df = jnp.maximum(d, 1).astype(F32)
    large = max_exact + (jnp.log(df / max_exact) / math.log(MAX_DIST / max_exact)
                         * (N_BUCKETS - max_exact)).astype(jnp.int32)
    large = jnp.minimum(large, N_BUCKETS - 1)
    return jnp.where(d < max_exact, d, large)


def _template_kernel(f_ref, gx_ref, wt_ref, ct_ref):
    wk, lanes = wt_ref.shape[1], f_ref.shape[1]
    n_band = CMP_R_HI - CMP_R_LO + 1
    lo = CT_ZERO + CMP_R_LO
    for hd in range(N_HEADS):
        g, cols = hd // HPG, slice((hd % HPG) * TQ, (hd % HPG + 1) * TQ)
        x = jnp.broadcast_to(f_ref[hd:hd + 1, :], (wk, lanes))
        y = pltpu.roll(x, lanes - (wk - 1), axis=1, stride=1, stride_axis=0)
        wt_ref[g, :, cols] = y[:, :TQ]
        xb = jnp.broadcast_to(gx_ref[hd:hd + 1, :], (n_band, lanes))
        yb = pltpu.roll(xb, lanes - CMP_STRIDE * (n_band - 1), axis=1,
                        stride=CMP_STRIDE, stride_axis=0)
        ct_ref[g, 0:lo, cols] = jnp.zeros((lo, TQ), F32)
        ct_ref[g, lo:lo + n_band, cols] = yb[:, :TQ]
        ct_ref[g, lo + n_band:, cols] = jnp.full((CT_ROWS - lo - n_band, TQ), NEG, F32)


def _templates(fext, gxext):
    return pl.pallas_call(
        _template_kernel,
        out_shape=[jax.ShapeDtypeStruct((N_KV, WINDOW + TQ, COLS), F32),
                   jax.ShapeDtypeStruct((N_KV, CT_ROWS, COLS), F32)],
        compiler_params=_params(None),
        name="templates",
    )(fext, gxext)


def _bias_templates(rel_bias):
    biasp = rel_bias[:, _t5_bucket(jnp.arange(MAX_DIST))] - rel_bias[:, N_BUCKETS - 1:]
    biasp = biasp * LOG2E

    def by_distance(n_neg, n_zero, n_neg_after):
        return jnp.concatenate(
            [jnp.full((N_HEADS, n_neg), NEG, F32), biasp, jnp.zeros((N_HEADS, n_zero), F32),
             jnp.full((N_HEADS, n_neg_after), NEG, F32)], axis=1)

    wk = WINDOW + TQ
    f = by_distance(TQ - 1, WINDOW - MAX_DIST, wk - WINDOW)
    fext = jnp.concatenate([f, jnp.zeros((N_HEADS, 1), F32)], axis=1)

    d_min = -(CMP_LEN - 1) - CMP_STRIDE * CMP_R_HI
    d_max = TQ - 1 - (CMP_LEN - 1) - CMP_STRIDE * CMP_R_LO
    gx = by_distance(-d_min, d_max + 1 - MAX_DIST, 0)
    gxext = jnp.pad(gx, ((0, 0), (0, fext.shape[1] - gx.shape[1])))
    wtmpl, ctmpl = _templates(fext, gxext)
    return ctmpl, wtmpl


def _overlap_t():
    cmp_start = jnp.arange(N_CMP_PAD) * CMP_STRIDE
    slc_start = jnp.arange(N_SLC) * SLC_LEN
    ov = ((cmp_start[None, :] < slc_start[:, None] + SLC_LEN)
          & (cmp_start[None, :] + CMP_LEN > slc_start[:, None])
          & (jnp.arange(N_CMP_PAD)[None, :] < N_CMP))
    return ov.astype(BF16)


def _compress_block_weights(w1):
    half = CMP_STRIDE * HEAD_DIM
    top = w1[:, :half].reshape(2, CMP_STRIDE, HEAD_DIM, CMP_HIDDEN)
    bot = w1[:, half:].reshape(2, CMP_STRIDE, HEAD_DIM, CMP_HIDDEN)
    z = jnp.zeros_like(top)
    return jnp.concatenate([jnp.concatenate([top, bot, z, z], axis=-1),
                            jnp.concatenate([z, z, top, bot], axis=-1)], axis=2)


def kernel(x, c, w_ada, b_ada, g_ffn1, w_gu1, w_down1, g_mix, w_in, w_dw, b_dw, ln_g, ln_b,
           pe_k, pe_v, w_ck1, w_ck2, w_cv1, w_cv2, rel_bias, w_out, g_ffn2, w_gu2, w_down2,
           g_final):
    assert x.shape == (1, SEQ, D_MODEL) and w_ada.shape[0] == 1
    x2 = x[0]
    mod = _ada(c.reshape(D_MODEL, 1), w_ada[0], b_ada)
    x1 = _ffn1(x2, g_ffn1, mod, w_gu1[0].astype(BF16), w_down1[0].astype(BF16))

    wi = w_in[0]
    n_in = wi.shape[1]
    col = jnp.arange(n_in)
    q_cols = (col >= 2 * D_CONV) & (col < 2 * D_CONV + D_ATT)
    w_in_p = jnp.pad(wi * jnp.where(q_cols, HEAD_DIM ** -0.5 * LOG2E, 1.0)[None, :],
                     ((0, 0), (0, D_IN_PAD - n_in))).astype(BF16)
    hc, kcvc, ksl, kw, qt, gatet, vslt, vwt = _proj(x1, g_mix, mod, w_in_p)

    pe = jnp.stack([pe_k[0].reshape(1, -1), pe_v[0].reshape(1, -1)])
    w1 = jnp.stack([w_ck1[0], w_cv1[0]]).astype(BF16)
    w2 = jnp.stack([w_ck2[0], w_cv2[0]]).astype(BF16)
    cmp = _compress(kcvc, pe, w1, _compress_block_weights(w1), w2)
    kc = jnp.concatenate([cmp[0], cmp[1]], axis=1).astype(BF16)
    vct = jnp.swapaxes(cmp[2:4], 1, 2).astype(BF16)

    ctmpl, wtmpl = _bias_templates(rel_bias)
    y_att = _attn(qt, gatet, kc, vct, _overlap_t(), ksl, vslt, kw, vwt, ctmpl, wtmpl)

    y_conv = _sc_conv(hc, jnp.pad(w_dw[0], ((0, 32 - CONV_WIDTH), (0, 0))),
                      jnp.broadcast_to(b_dw, (SUBLANES, D_CONV)))

    out = _ffn2(x1, y_conv, y_att, w_out[0].astype(BF16), mod,
                g_ffn2, w_gu2[0].astype(BF16), w_down2[0].astype(BF16),
                g_final.reshape(1, D_MODEL), ln_g, ln_b)
    return out[None]
```

```python
import functools
import math

import jax
import jax.numpy as jnp
from jax import lax
from jax.experimental import pallas as pl
from jax.experimental.pallas import tpu as pltpu
from jax.experimental.pallas import tpu_sc as plsc

F32 = jnp.float32
BF16 = jnp.bfloat16

D_MODEL = 1024
SEQ = 16384
D_CONV = 512
CONV_WIDTH = 31
N_HEADS = 8
N_KV = 2
HPG = 4
HEAD_DIM = 64
D_ATT = 512
D_KV = 128
CMP_LEN = 32
CMP_STRIDE = 16
CMP_HIDDEN = 128
N_CMP = (SEQ - CMP_LEN) // CMP_STRIDE + 1
N_CMP_PAD = 1024
SLC_LEN = 64
N_SLC = SEQ // SLC_LEN
SLC_TOPK = 16
WINDOW = 512
N_FORCED = 3
TAKEN = -2.0
N_BUCKETS = 32
MAX_DIST = 128
D_FF = 2816
FFN_RES = 0.5
EPS = 1e-6
NEG = -1e30
M_FLOOR = -1e29
LOG2E = math.log2(math.e)

V7X_VMEM_BYTES = 64 * 1024 * 1024
VMEM_LIMIT = V7X_VMEM_BYTES - 6 * 1024 * 1024
LANES = 128
SUBLANES = 8

TQ = 256
TK = 256
COLS = HPG * TQ
KPAD = WINDOW
VROWS = HEAD_DIM + 16
BLK_PER_TILE = TK // SLC_LEN
FAR_UNROLL = 4
SEL_ENTRIES = SEQ // TK + 2
CMP_R_HI = (TQ - CMP_LEN) // CMP_STRIDE
CMP_R_LO = -((MAX_DIST + CMP_LEN - 2) // CMP_STRIDE)
CT_ZERO = TK + -(CMP_R_LO // SUBLANES) * SUBLANES
CT_ROWS = CT_ZERO + -(-(CMP_R_HI + 1) // SUBLANES) * SUBLANES + TK
FFN_TM = 512
FFN_TF = D_FF
PROJ_TM = 512
CONV_HALO = 32
SC_CORES, SC_SUBCORES, SC_LANES = 2, 16, 16
SC_WORKERS = SC_CORES * SC_SUBCORES
SC_ROWS = 64
D_IN_PAD = 2432


def _params(sem):
    return pltpu.CompilerParams(dimension_semantics=sem, vmem_limit_bytes=VMEM_LIMIT)


def _const_spec(shape):
    nd = len(shape)
    return pl.BlockSpec(shape, lambda *_: (0,) * nd, pipeline_mode=pl.Buffered(1))


def _sigmoid(v):
    return 1.0 / (1.0 + jnp.exp(-v))


def _dot(a, b):
    return jnp.dot(a, b, preferred_element_type=F32)


def _ada_kernel(c_ref, w_ref, b_ref, o_ref):
    c = c_ref[...]
    sc = c * _sigmoid(c)
    o_ref[...] = jnp.sum(w_ref[...] * sc, axis=0, keepdims=True) + b_ref[...]


def _ada(c_col, w, b):
    n = w.shape[1]
    tn = n // 8
    return pl.pallas_call(
        _ada_kernel,
        out_shape=jax.ShapeDtypeStruct((1, n), F32),
        grid=(8,),
        in_specs=[pl.BlockSpec((D_MODEL, 1), lambda j: (0, 0)),
                  pl.BlockSpec((D_MODEL, tn), lambda j: (0, j)),
                  pl.BlockSpec((1, tn), lambda j: (0, j))],
        out_specs=pl.BlockSpec((1, tn), lambda j: (0, j)),
        compiler_params=_params(("arbitrary",)),
        name="ada",
    )(c_col, w, b)


def _rms_mod(x, g, sh, sc):
    ms = jnp.mean(x * x, axis=-1, keepdims=True)
    y = x * lax.rsqrt(ms + EPS) * g
    return y * (1.0 + sc) + sh


def _ffn_body(x, g_ref, sh_ref, sc_ref, gt_ref, wgu_ref, wd_ref):
    hb = _rms_mod(x, g_ref[...], sh_ref[...], sc_ref[...]).astype(BF16)
    acc = jnp.zeros((x.shape[0], D_MODEL), F32)
    for f in range(D_FF // FFN_TF):
        gg = _dot(hb, wgu_ref[:, f * FFN_TF:(f + 1) * FFN_TF])
        uu = _dot(hb, wgu_ref[:, D_FF + f * FFN_TF:D_FF + (f + 1) * FFN_TF])
        a = (gg * _sigmoid(gg) * uu).astype(BF16)
        acc = acc + _dot(a, wd_ref[f * FFN_TF:(f + 1) * FFN_TF, :])
    return x + (FFN_RES * gt_ref[...]) * acc


def _ffn1_kernel(x_ref, g_ref, sh_ref, sc_ref, gt_ref, wgu_ref, wd_ref, o_ref):
    o_ref[...] = _ffn_body(x_ref[...], g_ref, sh_ref, sc_ref, gt_ref, wgu_ref, wd_ref)


def _ffn2_kernel(x_ref, yp_ref, ya_ref, wo_ref, gt2_ref,
                 g_ref, sh_ref, sc_ref, gt_ref, wgu_ref, wd_ref, gf_ref, cg_ref, cbb_ref, o_ref):
    tm = x_ref.shape[0]
    cv = yp_ref[...]
    mu = jnp.mean(cv, axis=-1, keepdims=True)
    dlt = cv - mu
    var = jnp.mean(dlt * dlt, axis=-1, keepdims=True)
    yn = dlt * lax.rsqrt(var + EPS) * cg_ref[...] + cbb_ref[...]
    y = _dot((yn * _sigmoid(yn)).astype(BF16), wo_ref[0:D_CONV, :])
    for h in range(HPG):
        rows = [slice(D_CONV + (g * HPG + h) * HEAD_DIM, D_CONV + (g * HPG + h + 1) * HEAD_DIM)
                for g in range(N_KV)]
        w_h = jnp.concatenate([wo_ref[r, :] for r in rows], axis=0)
        y = y + _dot(ya_ref[:, h].reshape(tm, LANES), w_h)
    x = x_ref[...] + gt2_ref[...] * y
    out = _ffn_body(x, g_ref, sh_ref, sc_ref, gt_ref, wgu_ref, wd_ref)
    ms = jnp.mean(out * out, axis=-1, keepdims=True)
    o_ref[...] = out * lax.rsqrt(ms + EPS) * gf_ref[...]


def _row_spec(tm, n):
    return pl.BlockSpec((tm, n), lambda i: (i, 0))


def _mod_spec(k):
    return pl.BlockSpec((1, D_MODEL), lambda *_: (0, k), pipeline_mode=pl.Buffered(1))


def _ffn1(x, g, mod, wgu, wd):
    vec = _const_spec((1, D_MODEL))
    return pl.pallas_call(
        _ffn1_kernel,
        out_shape=jax.ShapeDtypeStruct((SEQ, D_MODEL), F32),
        grid=(SEQ // FFN_TM,),
        in_specs=[_row_spec(FFN_TM, D_MODEL), vec, _mod_spec(0), _mod_spec(1), _mod_spec(2),
                  _const_spec((D_MODEL, 2 * D_FF)), _const_spec((D_FF, D_MODEL))],
        out_specs=_row_spec(FFN_TM, D_MODEL),
        compiler_params=_params(("arbitrary",)),
        name="ffn1",
    )(x, g, mod, mod, mod, wgu, wd)


def _ffn2(x, yp, ya, wo, mod, g, wgu, wd, gf, cg, cbb):
    vec = _const_spec((1, D_MODEL))
    cvec = _const_spec((1, D_CONV))
    nt = FFN_TM // TQ
    return pl.pallas_call(
        _ffn2_kernel,
        out_shape=jax.ShapeDtypeStruct((SEQ, D_MODEL), F32),
        grid=(SEQ // FFN_TM,),
        in_specs=[_row_spec(FFN_TM, D_MODEL), _row_spec(FFN_TM, D_CONV),
                  pl.BlockSpec((nt, HPG, TQ, LANES), lambda i: (i, 0, 0, 0)),
                  _const_spec((D_CONV + D_ATT, D_MODEL)), _mod_spec(5),
                  vec, _mod_spec(6), _mod_spec(7), _mod_spec(8),
                  _const_spec((D_MODEL, 2 * D_FF)), _const_spec((D_FF, D_MODEL)), vec, cvec, cvec],
        out_specs=_row_spec(FFN_TM, D_MODEL),
        compiler_params=_params(("arbitrary",)),
        name="ffn2",
    )(x, yp, ya, wo, mod, g, mod, mod, mod, wgu, wd, gf, cg, cbb)


def _proj_kernel(x_ref, g_ref, sh_ref, sc_ref, w_ref,
                 hc_ref, kcvc_ref, ksl_ref, kw_ref, qt_ref, gatet_ref, vslt_ref, vwt_ref):
    hb = _rms_mod(x_ref[...], g_ref[...], sh_ref[...], sc_ref[...]).astype(BF16)
    u = _dot(hb, w_ref[...])
    hc_ref[...] = u[:, 0:512] * _sigmoid(u[:, 512:1024])
    kcvc_ref[...] = u[:, 1536:1792]
    tm = x_ref.shape[0]
    blk = (lax.broadcasted_iota(jnp.int32, (tm, D_KV), 0) % TK) // SLC_LEN
    ind = (blk == lax.broadcasted_iota(jnp.int32, (tm, D_KV), 1)).astype(BF16)
    ksl_ref[...] = jnp.concatenate([u[:, 1792:1920].astype(BF16), ind], axis=1)
    kw_ref[...] = u[:, 2048:2176].astype(BF16)
    qt_ref[...] = u[:, 1024:1536].T.astype(BF16)
    gatet_ref[...] = _sigmoid(u[:, 2304:2432]).T
    tail = jnp.concatenate([jnp.ones((1, TK), BF16), jnp.zeros((VROWS - HEAD_DIM - 1, TK), BF16)])
    for vt_ref, c0 in ((vslt_ref, 1920), (vwt_ref, 2176)):
        vt = u[:, c0:c0 + D_KV].T.astype(BF16)
        for g in range(N_KV):
            for t in range(tm // TK):
                vt_ref[g, t] = jnp.concatenate(
                    [vt[g * HEAD_DIM:(g + 1) * HEAD_DIM, t * TK:(t + 1) * TK], tail], axis=0)


def _proj(x, g, mod, w):
    vec = _const_spec((1, D_MODEL))
    tm = PROJ_TM
    rows = [(D_CONV, F32), (2 * D_KV, F32), (2 * D_KV, BF16), (D_KV, BF16)]
    cols = [(D_ATT, BF16), (LANES, F32)]
    vt_shape = jax.ShapeDtypeStruct((N_KV, SEQ // TK, VROWS, TK), BF16)
    vt_spec = pl.BlockSpec((N_KV, tm // TK, VROWS, TK), lambda i: (0, i, 0, 0))
    return pl.pallas_call(
        _proj_kernel,
        out_shape=[jax.ShapeDtypeStruct((SEQ, n), dt) for n, dt in rows]
                  + [jax.ShapeDtypeStruct((n, SEQ), dt) for n, dt in cols] + [vt_shape, vt_shape],
        grid=(SEQ // tm,),
        in_specs=[_row_spec(tm, D_MODEL), vec, _mod_spec(3), _mod_spec(4),
                  _const_spec((D_MODEL, D_IN_PAD))],
        out_specs=[_row_spec(tm, n) for n, _ in rows]
                  + [pl.BlockSpec((n, tm), lambda i: (0, i)) for n, _ in cols] + [vt_spec, vt_spec],
        compiler_params=_params(("arbitrary",)),
        name="proj",
    )(x, g, mod, mod, w)


def _sc_conv(hc, w32, b8):
    rows_per_worker = SEQ // SC_WORKERS
    n_chunks = rows_per_worker // SC_ROWS
    halves = (tuple(range(0, 16)), tuple(range(16, CONV_WIDTH)))
    off = CONV_HALO - (CONV_WIDTH - 1)
    mesh = plsc.VectorSubcoreMesh(core_axis_name="c", subcore_axis_name="s")

    @functools.partial(
        pl.kernel, mesh=mesh,
        out_type=jax.ShapeDtypeStruct((SEQ, D_CONV), F32),
        scratch_types=[pltpu.VMEM((CONV_HALO + SC_ROWS, D_CONV), F32),
                       pltpu.VMEM((SC_ROWS, D_CONV), F32),
                       pltpu.VMEM((32, D_CONV), F32),
                       pltpu.VMEM((SUBLANES, D_CONV), F32)],
        compiler_params=pltpu.CompilerParams(use_tc_tiling_on_sc=True),
        name="sc_conv",
    )
    def k(hc_hbm, w_hbm, b_hbm, out_hbm, x_v, o_v, w_v, b_v):
        wid = lax.axis_index("s") * SC_CORES + lax.axis_index("c")
        pltpu.sync_copy(w_hbm, w_v)
        pltpu.sync_copy(b_hbm, b_v)

        @pl.loop(0, n_chunks)
        def _(ci):
            row0 = pl.multiple_of(wid * rows_per_worker + ci * SC_ROWS, SC_ROWS)

            @pl.when(row0 == 0)
            def _():
                @pl.loop(0, CONV_HALO)
                def _(r):
                    @pl.loop(0, D_CONV // SC_LANES)
                    def _(cg):
                        x_v[r, pl.ds(cg * SC_LANES, SC_LANES)] = jnp.zeros((SC_LANES,), F32)
                pltpu.sync_copy(hc_hbm.at[pl.ds(0, SC_ROWS)], x_v.at[pl.ds(CONV_HALO, SC_ROWS)])

            @pl.when(row0 > 0)
            def _():
                pltpu.sync_copy(hc_hbm.at[pl.ds(row0 - CONV_HALO, CONV_HALO + SC_ROWS)], x_v)

            @pl.loop(0, D_CONV // SC_LANES)
            def _(cg):
                lanes = pl.ds(cg * SC_LANES, SC_LANES)
                for hi, taps in enumerate(halves):
                    wv = [w_v[t, lanes] for t in taps]

                    @pl.loop(0, SC_ROWS)
                    def _(r):
                        acc = b_v[0, lanes] if hi == 0 else o_v[r, lanes]
                        for j, t in enumerate(taps):
                            acc = acc + x_v[r + off + t, lanes] * wv[j]
                        o_v[r, lanes] = acc

            pltpu.sync_copy(o_v, out_hbm.at[pl.ds(row0, SC_ROWS)])

    return k(hc, w32, b8)


def _compress_kernel(c_ref, pe_ref, w1_ref, wblk_ref, w2_ref, o_ref):
    half = CMP_STRIDE * HEAD_DIM
    acc = jnp.zeros((N_CMP_PAD, 4 * CMP_HIDDEN), F32)
    for l in range(CMP_STRIDE):
        x = c_ref[pl.ds(l, N_CMP_PAD, stride=CMP_STRIDE), :].astype(BF16)
        acc = acc + _dot(x, wblk_ref[0, l])
    pe = jnp.broadcast_to(pe_ref[0], (SUBLANES, 2 * half)).astype(BF16)
    pe_term = (_dot(pe[:, :half], w1_ref[0, :half, :]) + _dot(pe[:, half:], w1_ref[0, half:, :]))[0:1]
    for g in range(N_KV):
        a = acc[:, 2 * g * CMP_HIDDEN:(2 * g + 1) * CMP_HIDDEN]
        b = acc[:, (2 * g + 1) * CMP_HIDDEN:(2 * g + 2) * CMP_HIDDEN]
        b_up = jnp.concatenate([b[1:], jnp.zeros((1, CMP_HIDDEN), F32)], axis=0)
        pre = a + b_up + pe_term
        hid = (pre * _sigmoid(pre)).astype(BF16)
        o_ref[g] = _dot(hid, w2_ref[0])


def _compress(kcvc, pe, w1, wblk, w2):
    return pl.pallas_call(
        _compress_kernel,
        out_shape=jax.ShapeDtypeStruct((2 * N_KV, N_CMP_PAD, HEAD_DIM), F32),
        grid=(2,),
        in_specs=[pl.BlockSpec((SEQ, D_KV), lambda i: (0, i)),
                  pl.BlockSpec((1, 1, 2048), lambda i: (i, 0, 0)),
                  pl.BlockSpec((1, 2048, CMP_HIDDEN), lambda i: (i, 0, 0)),
                  pl.BlockSpec((1, CMP_STRIDE, D_KV, 4 * CMP_HIDDEN), lambda i: (i, 0, 0, 0)),
                  pl.BlockSpec((1, CMP_HIDDEN, HEAD_DIM), lambda i: (i, 0, 0))],
        out_specs=pl.BlockSpec((N_KV, N_CMP_PAD, HEAD_DIM), lambda i: (i, 0, 0)),
        compiler_params=_params(("arbitrary",)),
        name="compress",
    )(kcvc, pe, w1, wblk, w2)


def _tile_update(s, m, acc, vt):
    m_new = jnp.maximum(m, jnp.max(s, axis=0, keepdims=True))
    alpha = jnp.exp2(m - m_new)
    p = jnp.exp2(s - m_new).astype(BF16)
    return m_new, alpha * acc + _dot(vt, p)


def _finish(acc):
    return acc[:HEAD_DIM] / jnp.maximum(acc[HEAD_DIM:HEAD_DIM + 1], 1e-30)


def _attn_kernel(qt_ref, gt_ref, kc_ref, vct_ref, ovt_ref, ksl_ref, vslt_ref, kw_ref, vwt_ref,
                 ct_ref, wt_ref, o_ref, s_scr, p_scr, sel_scr, wa_scr, wb_scr, sa_scr, sb_scr):
    qb = pl.program_id(0)
    q0 = qb * TQ
    qt = qt_ref[...]
    gt = gt_ref[...]
    m0 = jnp.full((1, COLS), M_FLOOR, F32)
    acc0 = jnp.zeros((VROWS, COLS), F32)

    def template(g, first_row, rows0):
        r0 = jnp.where(first_row >= KPAD, rows0, WINDOW + TQ)
        return wt_ref[g, pl.ds(pl.multiple_of(r0, TK), TK), :]

    def real_row(first_row):
        return pl.multiple_of(jnp.maximum(first_row - KPAD, 0), TK)

    comb = []
    zq = jnp.zeros((HEAD_DIM, TQ), BF16)
    for g in range(N_KV):
        heads = [qt[(g * HPG + h) * HEAD_DIM:(g * HPG + h + 1) * HEAD_DIM, :] for h in range(HPG)]
        q_g = jnp.concatenate(
            [jnp.concatenate([q_h, zq] if g == 0 else [zq, q_h], axis=0) for q_h in heads],
            axis=1)

        mcol = m0
        for ct in range(N_CMP_PAD // TK):
            rows = slice(ct * TK, (ct + 1) * TK)
            r0 = pl.multiple_of(jnp.clip(TK * ct - (TQ // CMP_STRIDE) * qb + CT_ZERO,
                                         0, CT_ROWS - TK), SUBLANES)
            s = _dot(kc_ref[rows, :], q_g) + ct_ref[g, pl.ds(r0, TK), :]
            s_scr[rows, :] = s
            mcol = jnp.maximum(mcol, jnp.max(s, axis=0, keepdims=True))
        lsum = jnp.zeros((1, COLS), F32)
        acc_c = jnp.zeros((HEAD_DIM, COLS), F32)
        for ct in range(N_CMP_PAD // TK):
            rows = slice(ct * TK, (ct + 1) * TK)
            p = jnp.exp2(s_scr[rows, :] - mcol)
            p_scr[rows, :] = p
            lsum = lsum + jnp.sum(p, axis=0, keepdims=True)
            acc_c = acc_c + _dot(vct_ref[g, :, rows], p.astype(BF16))
        rinv = 1.0 / jnp.maximum(lsum, 1e-30)
        o_c = acc_c * rinv
        imp = jnp.zeros((N_SLC, TQ), F32)
        for ct in range(N_CMP_PAD // TK):
            rows = slice(ct * TK, (ct + 1) * TK)
            pn = p_scr[rows, :] * rinv
            ps = pn[:, 0:TQ] + pn[:, TQ:2 * TQ] + pn[:, 2 * TQ:3 * TQ] + pn[:, 3 * TQ:4 * TQ]
            hi = ps.astype(BF16)
            lo = (ps - hi.astype(F32)).astype(BF16)
            ov = ovt_ref[:, rows]
            imp = imp + _dot(ov, hi) + _dot(ov, lo)

        m, acc = m0, acc0
        for a in range(3):
            first = q0 + TK * a
            s = _dot(kw_ref[pl.ds(real_row(first), TK), :], q_g)
            s = s + template(g, first, TK * a)
            m, acc = _tile_update(s, m, acc, vwt_ref[g, jnp.maximum(qb + a - KPAD // TK, 0)])
        o_w = _finish(acc)

        cur = jnp.right_shift(q0 + lax.broadcasted_iota(jnp.int32, (1, TQ), 1), 6)
        jcol = lax.broadcasted_iota(jnp.int32, (N_SLC, 1), 0)
        forced = (jcol == 0) | (jcol == cur) | (jcol == cur - 1)
        sc = jnp.where(forced, TAKEN, jnp.where(jcol <= cur, imp, -1.0))
        jf = lax.broadcasted_iota(jnp.int32, (N_SLC, TQ), 0).astype(F32)
        for _ in range(SLC_TOPK - N_FORCED):
            mx = jnp.max(sc, axis=0, keepdims=True)
            idx = jnp.min(jnp.where(sc == mx, jf, 1e9), axis=0, keepdims=True)
            sc = jnp.where(jf == idx, TAKEN, sc)
        selneg = jnp.where(sc == TAKEN, 0.0, NEG)
        unused = jnp.zeros((SUBLANES - BLK_PER_TILE, TQ), F32)
        for kt in range(SEQ // TK):
            sel_scr[kt] = jnp.concatenate(
                [selneg[kt * BLK_PER_TILE:(kt + 1) * BLK_PER_TILE, :], unused], axis=0)
        sel_scr[SEQ // TK] = jnp.concatenate(
            [jnp.full((BLK_PER_TILE, TQ), NEG, F32), unused], axis=0)

        zrows = jnp.zeros((2 * D_KV - D_KV - 2 * SUBLANES, COLS), BF16)
        for w_scr in (wa_scr, wb_scr):
            w_scr[0:D_KV, :] = q_g
            w_scr[D_KV + 2 * SUBLANES:, :] = zrows

        def sel_scores(w_scr, first_row, sel_idx):
            blk = sel_scr[sel_idx]
            rows8 = jnp.concatenate([blk] * HPG, axis=1).astype(BF16)
            w_scr[D_KV:D_KV + 2 * SUBLANES, :] = jnp.concatenate(
                [rows8, jnp.zeros((SUBLANES, COLS), BF16)], axis=0)
            return _dot(ksl_ref[pl.ds(real_row(first_row), TK), :], w_scr[...])

        m, acc = m0, acc0
        for a, w_scr in ((0, wa_scr), (1, wb_scr)):
            first = q0 + TQ + TK * a
            s = sel_scores(w_scr, first, jnp.maximum(qb - 1 + a, 0))
            s = s + template(g, first, TQ + TK * a)
            m, acc = _tile_update(s, m, acc, vslt_ref[g, jnp.maximum(qb - 1 + a, 0)])

        n_far = jnp.maximum(qb - 1, 0)

        def far_scores(w_scr, kt):
            valid = kt < n_far
            ks = jnp.where(valid, kt, 0)
            return sel_scores(w_scr, KPAD + ks * TK, jnp.where(valid, kt, SEQ // TK))

        def far_v(kt):
            return vslt_ref[g, jnp.where(kt < n_far, kt, 0)]

        bufs = ((sa_scr, wa_scr), (sb_scr, wb_scr))

        def issue_scores(buf, kt):
            s_scr_k, w_scr_k = bufs[buf]
            s = far_scores(w_scr_k, kt)
            s_scr_k[...] = s
            return jnp.max(s, axis=0, keepdims=True)

        def far_body(i, carry):
            m_i, acc_i, smax = carry
            for u in range(FAR_UNROLL):
                kt = i * FAR_UNROLL + u
                smax_nxt = issue_scores((u + 1) % 2, kt + 1)
                m_new = jnp.maximum(m_i, smax)
                alpha = jnp.exp2(m_i - m_new)
                p = jnp.exp2(bufs[u % 2][0][...] - m_new).astype(BF16)
                acc_i = alpha * acc_i + _dot(far_v(kt), p)
                m_i, smax = m_new, smax_nxt
            return m_i, acc_i, smax

        trips = (n_far + FAR_UNROLL - 1) // FAR_UNROLL
        m, acc, _ = lax.fori_loop(0, trips, far_body, (m, acc, issue_scores(0, 0)))
        o_s = _finish(acc)

        per_head = []
        for h in range(HPG):
            c0 = 3 * (g * HPG + h)
            cs = slice(h * TQ, (h + 1) * TQ)
            per_head.append(gt[c0:c0 + 1, :] * o_c[:, cs] + gt[c0 + 1:c0 + 2, :] * o_s[:, cs]
                            + gt[c0 + 2:c0 + 3, :] * o_w[:, cs])
        comb.append(per_head)

    for h in range(HPG):
        merged = jnp.concatenate([comb[0][h], comb[1][h]], axis=0)
        o_ref[0, h] = merged.T.astype(BF16)


def _attn(qt, gatet, kc, vct, ovt, ksl, vslt, kw, vwt, ctmpl, wtmpl):
    consts = [kc, vct, ovt, ksl, vslt, kw, vwt, ctmpl, wtmpl]
    return pl.pallas_call(
        _attn_kernel,
        out_shape=jax.ShapeDtypeStruct((SEQ // TQ, HPG, TQ, LANES), BF16),
        grid=(SEQ // TQ,),
        in_specs=[pl.BlockSpec((D_ATT, TQ), lambda i: (0, i)),
                  pl.BlockSpec((LANES, TQ), lambda i: (0, i))]
                 + [_const_spec(a.shape) for a in consts],
        out_specs=pl.BlockSpec((1, HPG, TQ, LANES), lambda i: (i, 0, 0, 0)),
        scratch_shapes=[pltpu.VMEM((N_CMP_PAD, COLS), F32), pltpu.VMEM((N_CMP_PAD, COLS), F32),
                        pltpu.VMEM((SEL_ENTRIES, SUBLANES, TQ), F32),
                        pltpu.VMEM((2 * D_KV, COLS), BF16), pltpu.VMEM((2 * D_KV, COLS), BF16),
                        pltpu.VMEM((TK, COLS), F32), pltpu.VMEM((TK, COLS), F32)],
        compiler_params=_params(("arbitrary",)),
        name="nsa_attn",
    )(qt, gatet, *consts)


def _t5_bucket(dist):
    max_exact = N_BUCKETS // 2
    d = jnp.maximum(dist, 0)
    df = jnp.maximum(d, 1).astype(F32)
    large = max_exact + (jnp.log(df / max_exact) / math.log(MAX_DIST / max_exact)
                         * (N_BUCKETS - max_exact)).astype(jnp.int32)
    large = jnp.minimum(large, N_BUCKETS - 1)
    return jnp.where(d < max_exact, d, large)


def _template_kernel(f_ref, gx_ref, wt_ref, ct_ref):
    wk, lanes = WINDOW + TQ, f_ref.shape[1]
    n_band = CMP_R_HI - CMP_R_LO + 1
    lo = CT_ZERO + CMP_R_LO
    for hd in range(N_HEADS):
        g, cols = hd // HPG, slice((hd % HPG) * TQ, (hd % HPG + 1) * TQ)
        x = jnp.broadcast_to(f_ref[hd:hd + 1, :], (wk, lanes))
        y = pltpu.roll(x, lanes - (wk - 1), axis=1, stride=1, stride_axis=0)
        wt_ref[g, 0:wk, cols] = y[:, :TQ]
        wt_ref[g, wk:, cols] = jnp.full((TK, TQ), NEG, F32)
        xb = jnp.broadcast_to(gx_ref[hd:hd + 1, :], (n_band, lanes))
        yb = pltpu.roll(xb, lanes - CMP_STRIDE * (n_band - 1), axis=1,
                        stride=CMP_STRIDE, stride_axis=0)
        ct_ref[g, 0:lo, cols] = jnp.zeros((lo, TQ), F32)
        ct_ref[g, lo:lo + n_band, cols] = yb[:, :TQ]
        ct_ref[g, lo + n_band:, cols] = jnp.full((CT_ROWS - lo - n_band, TQ), NEG, F32)


def _templates(fext, gxext):
    return pl.pallas_call(
        _template_kernel,
        out_shape=[jax.ShapeDtypeStruct((N_KV, WINDOW + TQ + TK, COLS), F32),
                   jax.ShapeDtypeStruct((N_KV, CT_ROWS, COLS), F32)],
        compiler_params=_params(None),
        name="templates",
    )(fext, gxext)


def _bias_templates(rel_bias):
    biasp = rel_bias[:, _t5_bucket(jnp.arange(MAX_DIST))] - rel_bias[:, N_BUCKETS - 1:]
    biasp = biasp * LOG2E

    def by_distance(n_neg, n_zero, n_neg_after):
        return jnp.concatenate(
            [jnp.full((N_HEADS, n_neg), NEG, F32), biasp, jnp.zeros((N_HEADS, n_zero), F32),
             jnp.full((N_HEADS, n_neg_after), NEG, F32)], axis=1)

    wk = WINDOW + TQ
    f = by_distance(TQ - 1, WINDOW - MAX_DIST, wk - WINDOW)
    fext = jnp.concatenate([f, jnp.zeros((N_HEADS, 1), F32)], axis=1)

    d_min = -(CMP_LEN - 1) - CMP_STRIDE * CMP_R_HI
    d_max = TQ - 1 - (CMP_LEN - 1) - CMP_STRIDE * CMP_R_LO
    gx = by_distance(-d_min, d_max + 1 - MAX_DIST, 0)
    gxext = jnp.pad(gx, ((0, 0), (0, fext.shape[1] - gx.shape[1])))
    wtmpl, ctmpl = _templates(fext, gxext)
    return ctmpl, wtmpl


def _overlap_t():
    cmp_start = jnp.arange(N_CMP_PAD) * CMP_STRIDE
    slc_start = jnp.arange(N_SLC) * SLC_LEN
    ov = ((cmp_start[None, :] < slc_start[:, None] + SLC_LEN)
          & (cmp_start[None, :] + CMP_LEN > slc_start[:, None])
          & (jnp.arange(N_CMP_PAD)[None, :] < N_CMP))
    return ov.astype(BF16)


def _compress_block_weights(w1):
    half = CMP_STRIDE * HEAD_DIM
    top = w1[:, :half].reshape(2, CMP_STRIDE, HEAD_DIM, CMP_HIDDEN)
    bot = w1[:, half:].reshape(2, CMP_STRIDE, HEAD_DIM, CMP_HIDDEN)
    z = jnp.zeros_like(top)
    return jnp.concatenate([jnp.concatenate([top, bot, z, z], axis=-1),
                            jnp.concatenate([z, z, top, bot], axis=-1)], axis=2)


def kernel(x, c, w_ada, b_ada, g_ffn1, w_gu1, w_down1, g_mix, w_in, w_dw, b_dw, ln_g, ln_b,
           pe_k, pe_v, w_ck1, w_ck2, w_cv1, w_cv2, rel_bias, w_out, g_ffn2, w_gu2, w_down2,
           g_final):
    assert x.shape == (1, SEQ, D_MODEL) and w_ada.shape[0] == 1
    x2 = x[0]
    mod = _ada(c.reshape(D_MODEL, 1), w_ada[0], b_ada)
    x1 = _ffn1(x2, g_ffn1, mod, w_gu1[0].astype(BF16), w_down1[0].astype(BF16))

    wi = w_in[0]
    n_in = wi.shape[1]
    col = jnp.arange(n_in)
    q_cols = (col >= 2 * D_CONV) & (col < 2 * D_CONV + D_ATT)
    w_in_p = jnp.pad(wi * jnp.where(q_cols, HEAD_DIM ** -0.5 * LOG2E, 1.0)[None, :],
                     ((0, 0), (0, D_IN_PAD - n_in))).astype(BF16)
    hc, kcvc, ksl, kw, qt, gatet, vslt, vwt = _proj(x1, g_mix, mod, w_in_p)

    pe = jnp.stack([pe_k[0].reshape(1, -1), pe_v[0].reshape(1, -1)])
    w1 = jnp.stack([w_ck1[0], w_cv1[0]]).astype(BF16)
    w2 = jnp.stack([w_ck2[0], w_cv2[0]]).astype(BF16)
    cmp = _compress(kcvc, pe, w1, _compress_block_weights(w1), w2)
    kc = jnp.concatenate([cmp[0], cmp[1]], axis=1).astype(BF16)
    vct = jnp.swapaxes(cmp[2:4], 1, 2).astype(BF16)

    ctmpl, wtmpl = _bias_templates(rel_bias)
    y_att = _attn(qt, gatet, kc, vct, _overlap_t(), ksl, vslt, kw, vwt, ctmpl, wtmpl)

    y_conv = _sc_conv(hc, jnp.pad(w_dw[0], ((0, 32 - CONV_WIDTH), (0, 0))),
                      jnp.broadcast_to(b_dw, (SUBLANES, D_CONV)))

    out = _ffn2(x1, y_conv, y_att, w_out[0].astype(BF16), mod,
                g_ffn2, w_gu2[0].astype(BF16), w_down2[0].astype(BF16),
                g_final.reshape(1, D_MODEL), ln_g, ln_b)
    return out[None]
```

```python
import functools
import math

import jax
import jax.numpy as jnp
from jax import lax
from jax.experimental import pallas as pl
from jax.experimental.pallas import tpu as pltpu
from jax.experimental.pallas import tpu_sc as plsc

F32 = jnp.float32
BF16 = jnp.bfloat16

D_MODEL = 1024
SEQ = 16384
D_CONV = 512
CONV_WIDTH = 31
N_HEADS = 8
N_KV = 2
HPG = 4
HEAD_DIM = 64
D_ATT = 512
D_KV = 128
CMP_LEN = 32
CMP_STRIDE = 16
CMP_HIDDEN = 128
N_CMP = (SEQ - CMP_LEN) // CMP_STRIDE + 1
N_CMP_PAD = 1024
SLC_LEN = 64
N_SLC = SEQ // SLC_LEN
SLC_TOPK = 16
WINDOW = 512
N_FORCED = 3
TAKEN = -2.0
N_BUCKETS = 32
MAX_DIST = 128
D_FF = 2816
FFN_RES = 0.5
EPS = 1e-6
NEG = -1e30
M_FLOOR = -1e29
LOG2E = math.log2(math.e)

V7X_VMEM_BYTES = 64 * 1024 * 1024
VMEM_LIMIT = V7X_VMEM_BYTES - 6 * 1024 * 1024
LANES = 128
SUBLANES = 8

TQ = 256
TK = 256
COLS = HPG * TQ
KPAD = WINDOW
VROWS = HEAD_DIM + 16
BLK_PER_TILE = TK // SLC_LEN
FAR_UNROLL = 4
SEL_ENTRIES = SEQ // TK + 2
CMP_R_HI = (TQ - CMP_LEN) // CMP_STRIDE
CMP_R_LO = -((MAX_DIST + CMP_LEN - 2) // CMP_STRIDE)
CT_ZERO = TK + -(CMP_R_LO // SUBLANES) * SUBLANES
CT_ROWS = CT_ZERO + -(-(CMP_R_HI + 1) // SUBLANES) * SUBLANES + TK
FFN_TM = 512
FFN_TF = D_FF
PROJ_TM = 512
CONV_HALO = 32
SC_CORES, SC_SUBCORES, SC_LANES = 2, 16, 16
SC_WORKERS = SC_CORES * SC_SUBCORES
SC_ROWS = 64
D_IN_PAD = 2432


def _params(sem):
    return pltpu.CompilerParams(dimension_semantics=sem, vmem_limit_bytes=VMEM_LIMIT)


def _const_spec(shape):
    nd = len(shape)
    return pl.BlockSpec(shape, lambda *_: (0,) * nd, pipeline_mode=pl.Buffered(1))


def _sigmoid(v):
    return 1.0 / (1.0 + jnp.exp(-v))


def _dot(a, b):
    return jnp.dot(a, b, preferred_element_type=F32)


def _ada_kernel(c_ref, w_ref, b_ref, o_ref):
    c = c_ref[...]
    sc = c * _sigmoid(c)
    o_ref[...] = jnp.sum(w_ref[...] * sc, axis=0, keepdims=True) + b_ref[...]


def _ada(c_col, w, b):
    n = w.shape[1]
    tn = n // 8
    return pl.pallas_call(
        _ada_kernel,
        out_shape=jax.ShapeDtypeStruct((1, n), F32),
        grid=(8,),
        in_specs=[pl.BlockSpec((D_MODEL, 1), lambda j: (0, 0)),
                  pl.BlockSpec((D_MODEL, tn), lambda j: (0, j)),
                  pl.BlockSpec((1, tn), lambda j: (0, j))],
        out_specs=pl.BlockSpec((1, tn), lambda j: (0, j)),
        compiler_params=_params(("arbitrary",)),
        name="ada",
    )(c_col, w, b)


def _rms_mod(x, g, sh, sc):
    ms = jnp.mean(x * x, axis=-1, keepdims=True)
    y = x * lax.rsqrt(ms + EPS) * g
    return y * (1.0 + sc) + sh


def _ffn_body(x, g_ref, sh_ref, sc_ref, gt_ref, wgu_ref, wd_ref):
    hb = _rms_mod(x, g_ref[...], sh_ref[...], sc_ref[...]).astype(BF16)
    acc = jnp.zeros((x.shape[0], D_MODEL), F32)
    for f in range(D_FF // FFN_TF):
        gg = _dot(hb, wgu_ref[:, f * FFN_TF:(f + 1) * FFN_TF])
        uu = _dot(hb, wgu_ref[:, D_FF + f * FFN_TF:D_FF + (f + 1) * FFN_TF])
        a = (gg * _sigmoid(gg) * uu).astype(BF16)
        acc = acc + _dot(a, wd_ref[f * FFN_TF:(f + 1) * FFN_TF, :])
    return x + (FFN_RES * gt_ref[...]) * acc


def _ffn1_kernel(x_ref, g_ref, sh_ref, sc_ref, gt_ref, wgu_ref, wd_ref, o_ref):
    o_ref[...] = _ffn_body(x_ref[...], g_ref, sh_ref, sc_ref, gt_ref, wgu_ref, wd_ref)


def _ffn2_kernel(x_ref, yp_ref, ya_ref, wo_ref, gt2_ref,
                 g_ref, sh_ref, sc_ref, gt_ref, wgu_ref, wd_ref, gf_ref, cg_ref, cbb_ref, o_ref):
    tm = x_ref.shape[0]
    cv = yp_ref[...]
    mu = jnp.mean(cv, axis=-1, keepdims=True)
    dlt = cv - mu
    var = jnp.mean(dlt * dlt, axis=-1, keepdims=True)
    yn = dlt * lax.rsqrt(var + EPS) * cg_ref[...] + cbb_ref[...]
    y = _dot((yn * _sigmoid(yn)).astype(BF16), wo_ref[0:D_CONV, :])
    for h in range(HPG):
        rows = [slice(D_CONV + (g * HPG + h) * HEAD_DIM, D_CONV + (g * HPG + h + 1) * HEAD_DIM)
                for g in range(N_KV)]
        w_h = jnp.concatenate([wo_ref[r, :] for r in rows], axis=0)
        y = y + _dot(ya_ref[:, h].reshape(tm, LANES), w_h)
    x = x_ref[...] + gt2_ref[...] * y
    out = _ffn_body(x, g_ref, sh_ref, sc_ref, gt_ref, wgu_ref, wd_ref)
    ms = jnp.mean(out * out, axis=-1, keepdims=True)
    o_ref[...] = out * lax.rsqrt(ms + EPS) * gf_ref[...]


def _row_spec(tm, n):
    return pl.BlockSpec((tm, n), lambda i: (i, 0))


def _mod_spec(k):
    return pl.BlockSpec((1, D_MODEL), lambda *_: (0, k), pipeline_mode=pl.Buffered(1))


def _ffn1(x, g, mod, wgu, wd):
    vec = _const_spec((1, D_MODEL))
    return pl.pallas_call(
        _ffn1_kernel,
        out_shape=jax.ShapeDtypeStruct((SEQ, D_MODEL), F32),
        grid=(SEQ // FFN_TM,),
        in_specs=[_row_spec(FFN_TM, D_MODEL), vec, _mod_spec(0), _mod_spec(1), _mod_spec(2),
                  _const_spec((D_MODEL, 2 * D_FF)), _const_spec((D_FF, D_MODEL))],
        out_specs=_row_spec(FFN_TM, D_MODEL),
        compiler_params=_params(("arbitrary",)),
        name="ffn1",
    )(x, g, mod, mod, mod, wgu, wd)


def _ffn2(x, yp, ya, wo, mod, g, wgu, wd, gf, cg, cbb):
    vec = _const_spec((1, D_MODEL))
    cvec = _const_spec((1, D_CONV))
    nt = FFN_TM // TQ
    return pl.pallas_call(
        _ffn2_kernel,
        out_shape=jax.ShapeDtypeStruct((SEQ, D_MODEL), F32),
        grid=(SEQ // FFN_TM,),
        in_specs=[_row_spec(FFN_TM, D_MODEL), _row_spec(FFN_TM, D_CONV),
                  pl.BlockSpec((nt, HPG, TQ, LANES), lambda i: (i, 0, 0, 0)),
                  _const_spec((D_CONV + D_ATT, D_MODEL)), _mod_spec(5),
                  vec, _mod_spec(6), _mod_spec(7), _mod_spec(8),
                  _const_spec((D_MODEL, 2 * D_FF)), _const_spec((D_FF, D_MODEL)), vec, cvec, cvec],
        out_specs=_row_spec(FFN_TM, D_MODEL),
        compiler_params=_params(("arbitrary",)),
        name="ffn2",
    )(x, yp, ya, wo, mod, g, mod, mod, mod, wgu, wd, gf, cg, cbb)


def _proj_kernel(x_ref, g_ref, sh_ref, sc_ref, w_ref,
                 hc_ref, kcvc_ref, ksl_ref, kw_ref, qt_ref, gatet_ref, vslt_ref, vwt_ref):
    hb = _rms_mod(x_ref[...], g_ref[...], sh_ref[...], sc_ref[...]).astype(BF16)
    u = _dot(hb, w_ref[...])
    hc_ref[...] = u[:, 0:512] * _sigmoid(u[:, 512:1024])
    kcvc_ref[...] = u[:, 1536:1792]
    tm = x_ref.shape[0]
    blk = (lax.broadcasted_iota(jnp.int32, (tm, D_KV), 0) % TK) // SLC_LEN
    ind = (blk == lax.broadcasted_iota(jnp.int32, (tm, D_KV), 1)).astype(BF16)
    ksl_ref[...] = jnp.concatenate([u[:, 1792:1920].astype(BF16), ind], axis=1)
    kw_ref[...] = u[:, 2048:2176].astype(BF16)
    qt_ref[...] = u[:, 1024:1536].T.astype(BF16)
    gatet_ref[...] = _sigmoid(u[:, 2304:2432]).T
    tail = jnp.concatenate([jnp.ones((1, TK), BF16), jnp.zeros((VROWS - HEAD_DIM - 1, TK), BF16)])
    for vt_ref, c0 in ((vslt_ref, 1920), (vwt_ref, 2176)):
        vt = u[:, c0:c0 + D_KV].T.astype(BF16)
        for g in range(N_KV):
            for t in range(tm // TK):
                vt_ref[g, t] = jnp.concatenate(
                    [vt[g * HEAD_DIM:(g + 1) * HEAD_DIM, t * TK:(t + 1) * TK], tail], axis=0)


def _proj(x, g, mod, w):
    vec = _const_spec((1, D_MODEL))
    tm = PROJ_TM
    rows = [(D_CONV, F32), (2 * D_KV, F32), (2 * D_KV, BF16), (D_KV, BF16)]
    cols = [(D_ATT, BF16), (LANES, F32)]
    vt_shape = jax.ShapeDtypeStruct((N_KV, SEQ // TK, VROWS, TK), BF16)
    vt_spec = pl.BlockSpec((N_KV, tm // TK, VROWS, TK), lambda i: (0, i, 0, 0))
    return pl.pallas_call(
        _proj_kernel,
        out_shape=[jax.ShapeDtypeStruct((SEQ, n), dt) for n, dt in rows]
                  + [jax.ShapeDtypeStruct((n, SEQ), dt) for n, dt in cols] + [vt_shape, vt_shape],
        grid=(SEQ // tm,),
        in_specs=[_row_spec(tm, D_MODEL), vec, _mod_spec(3), _mod_spec(4),
                  _const_spec((D_MODEL, D_IN_PAD))],
        out_specs=[_row_spec(tm, n) for n, _ in rows]
                  + [pl.BlockSpec((n, tm), lambda i: (0, i)) for n, _ in cols] + [vt_spec, vt_spec],
        compiler_params=_params(("arbitrary",)),
        name="proj",
    )(x, g, mod, mod, w)


def _sc_conv(hc, w32, b8):
    rows_per_worker = SEQ // SC_WORKERS
    n_chunks = rows_per_worker // SC_ROWS
    halves = (tuple(range(0, 16)), tuple(range(16, CONV_WIDTH)))
    off = CONV_HALO - (CONV_WIDTH - 1)
    mesh = plsc.VectorSubcoreMesh(core_axis_name="c", subcore_axis_name="s")

    @functools.partial(
        pl.kernel, mesh=mesh,
        out_type=jax.ShapeDtypeStruct((SEQ, D_CONV), F32),
        scratch_types=[pltpu.VMEM((CONV_HALO + SC_ROWS, D_CONV), F32),
                       pltpu.VMEM((SC_ROWS, D_CONV), F32),
                       pltpu.VMEM((32, D_CONV), F32),
                       pltpu.VMEM((SUBLANES, D_CONV), F32)],
        compiler_params=pltpu.CompilerParams(use_tc_tiling_on_sc=True),
        name="sc_conv",
    )
    def k(hc_hbm, w_hbm, b_hbm, out_hbm, x_v, o_v, w_v, b_v):
        wid = lax.axis_index("s") * SC_CORES + lax.axis_index("c")
        pltpu.sync_copy(w_hbm, w_v)
        pltpu.sync_copy(b_hbm, b_v)

        @pl.loop(0, n_chunks)
        def _(ci):
            row0 = pl.multiple_of(wid * rows_per_worker + ci * SC_ROWS, SC_ROWS)

            @pl.when(row0 == 0)
            def _():
                @pl.loop(0, CONV_HALO)
                def _(r):
                    @pl.loop(0, D_CONV // SC_LANES)
                    def _(cg):
                        x_v[r, pl.ds(cg * SC_LANES, SC_LANES)] = jnp.zeros((SC_LANES,), F32)
                pltpu.sync_copy(hc_hbm.at[pl.ds(0, SC_ROWS)], x_v.at[pl.ds(CONV_HALO, SC_ROWS)])

            @pl.when(row0 > 0)
            def _():
                pltpu.sync_copy(hc_hbm.at[pl.ds(row0 - CONV_HALO, CONV_HALO + SC_ROWS)], x_v)

            @pl.loop(0, D_CONV // SC_LANES)
            def _(cg):
                lanes = pl.ds(cg * SC_LANES, SC_LANES)
                for hi, taps in enumerate(halves):
                    wv = [w_v[t, lanes] for t in taps]

                    @pl.loop(0, SC_ROWS)
                    def _(r):
                        acc = b_v[0, lanes] if hi == 0 else o_v[r, lanes]
                        for j, t in enumerate(taps):
                            acc = acc + x_v[r + off + t, lanes] * wv[j]
                        o_v[r, lanes] = acc

            pltpu.sync_copy(o_v, out_hbm.at[pl.ds(row0, SC_ROWS)])

    return k(hc, w32, b8)


def _compress_kernel(c_ref, pe_ref, w1_ref, wblk_ref, w2_ref, o_ref):
    half = CMP_STRIDE * HEAD_DIM
    acc = jnp.zeros((N_CMP_PAD, 4 * CMP_HIDDEN), F32)
    for l in range(CMP_STRIDE):
        x = c_ref[pl.ds(l, N_CMP_PAD, stride=CMP_STRIDE), :].astype(BF16)
        acc = acc + _dot(x, wblk_ref[0, l])
    pe = jnp.broadcast_to(pe_ref[0], (SUBLANES, 2 * half)).astype(BF16)
    pe_term = (_dot(pe[:, :half], w1_ref[0, :half, :]) + _dot(pe[:, half:], w1_ref[0, half:, :]))[0:1]
    for g in range(N_KV):
        a = acc[:, 2 * g * CMP_HIDDEN:(2 * g + 1) * CMP_HIDDEN]
        b = acc[:, (2 * g + 1) * CMP_HIDDEN:(2 * g + 2) * CMP_HIDDEN]
        b_up = jnp.concatenate([b[1:], jnp.zeros((1, CMP_HIDDEN), F32)], axis=0)
        pre = a + b_up + pe_term
        hid = (pre * _sigmoid(pre)).astype(BF16)
        o_ref[g] = _dot(hid, w2_ref[0])


def _compress(kcvc, pe, w1, wblk, w2):
    return pl.pallas_call(
        _compress_kernel,
        out_shape=jax.ShapeDtypeStruct((2 * N_KV, N_CMP_PAD, HEAD_DIM), F32),
        grid=(2,),
        in_specs=[pl.BlockSpec((SEQ, D_KV), lambda i: (0, i)),
                  pl.BlockSpec((1, 1, 2048), lambda i: (i, 0, 0)),
                  pl.BlockSpec((1, 2048, CMP_HIDDEN), lambda i: (i, 0, 0)),
                  pl.BlockSpec((1, CMP_STRIDE, D_KV, 4 * CMP_HIDDEN), lambda i: (i, 0, 0, 0)),
                  pl.BlockSpec((1, CMP_HIDDEN, HEAD_DIM), lambda i: (i, 0, 0))],
        out_specs=pl.BlockSpec((N_KV, N_CMP_PAD, HEAD_DIM), lambda i: (i, 0, 0)),
        compiler_params=_params(("arbitrary",)),
        name="compress",
    )(kcvc, pe, w1, wblk, w2)


def _tile_update(s, m, acc, vt):
    m_new = jnp.maximum(m, jnp.max(s, axis=0, keepdims=True))
    alpha = jnp.exp2(m - m_new)
    p = jnp.exp2(s - m_new).astype(BF16)
    return m_new, alpha * acc + _dot(vt, p)


def _finish(acc):
    return acc[:HEAD_DIM] / jnp.maximum(acc[HEAD_DIM:HEAD_DIM + 1], 1e-30)


def _attn_kernel(qt_ref, gt_ref, kc_ref, vct_ref, ovt_ref, ksl_ref, vslt_ref, kw_ref, vwt_ref,
                 ct_ref, wt_ref, o_ref, s_scr, p_scr, sel_scr, wa_scr, wb_scr, sa_scr, sb_scr):
    qb = pl.program_id(0)
    q0 = qb * TQ
    qt = qt_ref[...]
    gt = gt_ref[...]
    m0 = jnp.full((1, COLS), M_FLOOR, F32)
    acc0 = jnp.zeros((VROWS, COLS), F32)

    def template(g, first_row, rows0):
        r0 = jnp.where(first_row >= KPAD, rows0, WINDOW + TQ)
        return wt_ref[g, pl.ds(pl.multiple_of(r0, TK), TK), :]

    def real_row(first_row):
        return pl.multiple_of(jnp.maximum(first_row - KPAD, 0), TK)

    comb = []
    zq = jnp.zeros((HEAD_DIM, TQ), BF16)
    for g in range(N_KV):
        heads = [qt[(g * HPG + h) * HEAD_DIM:(g * HPG + h + 1) * HEAD_DIM, :] for h in range(HPG)]
        q_g = jnp.concatenate(
            [jnp.concatenate([q_h, zq] if g == 0 else [zq, q_h], axis=0) for q_h in heads],
            axis=1)

        mcol = m0
        for ct in range(N_CMP_PAD // TK):
            rows = slice(ct * TK, (ct + 1) * TK)
            r0 = pl.multiple_of(jnp.clip(TK * ct - (TQ // CMP_STRIDE) * qb + CT_ZERO,
                                         0, CT_ROWS - TK), SUBLANES)
            s = _dot(kc_ref[rows, :], q_g) + ct_ref[g, pl.ds(r0, TK), :]
            s_scr[rows, :] = s
            mcol = jnp.maximum(mcol, jnp.max(s, axis=0, keepdims=True))
        acc_c = acc0
        for ct in range(N_CMP_PAD // TK):
            rows = slice(ct * TK, (ct + 1) * TK)
            p = jnp.exp2(s_scr[rows, :] - mcol)
            p_scr[rows, :] = p
            acc_c = acc_c + _dot(vct_ref[g, :, rows], p.astype(BF16))
        rinv = 1.0 / jnp.maximum(acc_c[HEAD_DIM:HEAD_DIM + 1], 1e-30)
        o_c = acc_c[:HEAD_DIM] * rinv
        imp = jnp.zeros((N_SLC, TQ), F32)
        for ct in range(N_CMP_PAD // TK):
            rows = slice(ct * TK, (ct + 1) * TK)
            pn = p_scr[rows, :] * rinv
            ps = pn[:, 0:TQ] + pn[:, TQ:2 * TQ] + pn[:, 2 * TQ:3 * TQ] + pn[:, 3 * TQ:4 * TQ]
            hi = ps.astype(BF16)
            lo = (ps - hi.astype(F32)).astype(BF16)
            ov = ovt_ref[:, rows]
            imp = imp + _dot(ov, hi) + _dot(ov, lo)

        m, acc = m0, acc0
        for a in range(3):
            first = q0 + TK * a
            s = _dot(kw_ref[pl.ds(real_row(first), TK), :], q_g)
            s = s + template(g, first, TK * a)
            m, acc = _tile_update(s, m, acc, vwt_ref[g, jnp.maximum(qb + a - KPAD // TK, 0)])
        o_w = _finish(acc)

        cur = jnp.right_shift(q0 + lax.broadcasted_iota(jnp.int32, (1, TQ), 1), 6)
        jcol = lax.broadcasted_iota(jnp.int32, (N_SLC, 1), 0)
        forced = (jcol == 0) | (jcol == cur) | (jcol == cur - 1)
        sc = jnp.where(forced, TAKEN, jnp.where(jcol <= cur, imp, -1.0))
        jf = lax.broadcasted_iota(jnp.int32, (N_SLC, TQ), 0).astype(F32)
        for _ in range(SLC_TOPK - N_FORCED):
            mx = jnp.max(sc, axis=0, keepdims=True)
            idx = jnp.min(jnp.where(sc == mx, jf, 1e9), axis=0, keepdims=True)
            sc = jnp.where(jf == idx, TAKEN, sc)
        selneg = jnp.where(sc == TAKEN, 0.0, NEG)
        unused = jnp.zeros((SUBLANES - BLK_PER_TILE, TQ), F32)
        for kt in range(SEQ // TK):
            sel_scr[kt] = jnp.concatenate(
                [selneg[kt * BLK_PER_TILE:(kt + 1) * BLK_PER_TILE, :], unused], axis=0)
        sel_scr[SEQ // TK] = jnp.concatenate(
            [jnp.full((BLK_PER_TILE, TQ), NEG, F32), unused], axis=0)

        zrows = jnp.zeros((2 * D_KV - D_KV - 2 * SUBLANES, COLS), BF16)
        for w_scr in (wa_scr, wb_scr):
            w_scr[0:D_KV, :] = q_g
            w_scr[D_KV + 2 * SUBLANES:, :] = zrows

        def sel_scores(w_scr, first_row, sel_idx):
            blk = sel_scr[sel_idx]
            rows8 = jnp.concatenate([blk] * HPG, axis=1).astype(BF16)
            w_scr[D_KV:D_KV + 2 * SUBLANES, :] = jnp.concatenate(
                [rows8, jnp.zeros((SUBLANES, COLS), BF16)], axis=0)
            return _dot(ksl_ref[pl.ds(real_row(first_row), TK), :], w_scr[...])

        m, acc = m0, acc0
        for a, w_scr in ((0, wa_scr), (1, wb_scr)):
            first = q0 + TQ + TK * a
            s = sel_scores(w_scr, first, jnp.maximum(qb - 1 + a, 0))
            s = s + template(g, first, TQ + TK * a)
            m, acc = _tile_update(s, m, acc, vslt_ref[g, jnp.maximum(qb - 1 + a, 0)])

        n_far = jnp.maximum(qb - 1, 0)

        def far_scores(w_scr, kt):
            valid = kt < n_far
            ks = jnp.where(valid, kt, 0)
            return sel_scores(w_scr, KPAD + ks * TK, jnp.where(valid, kt, SEQ // TK))

        def far_v(kt):
            return vslt_ref[g, jnp.where(kt < n_far, kt, 0)]

        bufs = ((sa_scr, wa_scr), (sb_scr, wb_scr))

        def issue_scores(buf, kt):
            s_scr_k, w_scr_k = bufs[buf]
            s = far_scores(w_scr_k, kt)
            s_scr_k[...] = s
            return jnp.max(s, axis=0, keepdims=True)

        def far_body(i, carry):
            m_i, acc_i, smax = carry
            for u in range(FAR_UNROLL):
                kt = i * FAR_UNROLL + u
                smax_nxt = issue_scores((u + 1) % 2, kt + 1)
                m_new = jnp.maximum(m_i, smax)
                alpha = jnp.exp2(m_i - m_new)
                p = jnp.exp2(bufs[u % 2][0][...] - m_new).astype(BF16)
                acc_i = alpha * acc_i + _dot(far_v(kt), p)
                m_i, smax = m_new, smax_nxt
            return m_i, acc_i, smax

        trips = (n_far + FAR_UNROLL - 1) // FAR_UNROLL
        m, acc, _ = lax.fori_loop(0, trips, far_body, (m, acc, issue_scores(0, 0)))
        o_s = _finish(acc)

        per_head = []
        for h in range(HPG):
            c0 = 3 * (g * HPG + h)
            cs = slice(h * TQ, (h + 1) * TQ)
            per_head.append(gt[c0:c0 + 1, :] * o_c[:, cs] + gt[c0 + 1:c0 + 2, :] * o_s[:, cs]
                            + gt[c0 + 2:c0 + 3, :] * o_w[:, cs])
        comb.append(per_head)

    for h in range(HPG):
        merged = jnp.concatenate([comb[0][h], comb[1][h]], axis=0)
        o_ref[0, h] = merged.T.astype(BF16)


def _attn(qt, gatet, kc, vct, ovt, ksl, vslt, kw, vwt, ctmpl, wtmpl):
    consts = [kc, vct, ovt, ksl, vslt, kw, vwt, ctmpl, wtmpl]
    return pl.pallas_call(
        _attn_kernel,
        out_shape=jax.ShapeDtypeStruct((SEQ // TQ, HPG, TQ, LANES), BF16),
        grid=(SEQ // TQ,),
        in_specs=[pl.BlockSpec((D_ATT, TQ), lambda i: (0, i)),
                  pl.BlockSpec((LANES, TQ), lambda i: (0, i))]
                 + [_const_spec(a.shape) for a in consts],
        out_specs=pl.BlockSpec((1, HPG, TQ, LANES), lambda i: (i, 0, 0, 0)),
        scratch_shapes=[pltpu.VMEM((N_CMP_PAD, COLS), F32), pltpu.VMEM((N_CMP_PAD, COLS), F32),
                        pltpu.VMEM((SEL_ENTRIES, SUBLANES, TQ), F32),
                        pltpu.VMEM((2 * D_KV, COLS), BF16), pltpu.VMEM((2 * D_KV, COLS), BF16),
                        pltpu.VMEM((TK, COLS), F32), pltpu.VMEM((TK, COLS), F32)],
        compiler_params=_params(("arbitrary",)),
        name="nsa_attn",
    )(qt, gatet, *consts)


def _t5_bucket(dist):
    max_exact = N_BUCKETS // 2
    d = jnp.maximum(dist, 0)
    df = jnp.maximum(d, 1).astype(F32)
    large = max_exact + (jnp.log(df / max_exact) / math.log(MAX_DIST / max_exact)
                         * (N_BUCKETS - max_exact)).astype(jnp.int32)
    large = jnp.minimum(large, N_BUCKETS - 1)
    return jnp.where(d < max_exact, d, large)


def _template_kernel(f_ref, gx_ref, wt_ref, ct_ref):
    wk, lanes = WINDOW + TQ, f_ref.shape[1]
    n_band = CMP_R_HI - CMP_R_LO + 1
    lo = CT_ZERO + CMP_R_LO
    for hd in range(N_HEADS):
        g, cols = hd // HPG, slice((hd % HPG) * TQ, (hd % HPG + 1) * TQ)
        x = jnp.broadcast_to(f_ref[hd:hd + 1, :], (wk, lanes))
        y = pltpu.roll(x, lanes - (wk - 1), axis=1, stride=1, stride_axis=0)
        wt_ref[g, 0:wk, cols] = y[:, :TQ]
        wt_ref[g, wk:, cols] = jnp.full((TK, TQ), NEG, F32)
        xb = jnp.broadcast_to(gx_ref[hd:hd + 1, :], (n_band, lanes))
        yb = pltpu.roll(xb, lanes - CMP_STRIDE * (n_band - 1), axis=1,
                        stride=CMP_STRIDE, stride_axis=0)
        ct_ref[g, 0:lo, cols] = jnp.zeros((lo, TQ), F32)
        ct_ref[g, lo:lo + n_band, cols] = yb[:, :TQ]
        ct_ref[g, lo + n_band:, cols] = jnp.full((CT_ROWS - lo - n_band, TQ), NEG, F32)


def _templates(fext, gxext):
    return pl.pallas_call(
        _template_kernel,
        out_shape=[jax.ShapeDtypeStruct((N_KV, WINDOW + TQ + TK, COLS), F32),
                   jax.ShapeDtypeStruct((N_KV, CT_ROWS, COLS), F32)],
        compiler_params=_params(None),
        name="templates",
    )(fext, gxext)


def _bias_templates(rel_bias):
    biasp = rel_bias[:, _t5_bucket(jnp.arange(MAX_DIST))] - rel_bias[:, N_BUCKETS - 1:]
    biasp = biasp * LOG2E

    def by_distance(n_neg, n_zero, n_neg_after):
        return jnp.concatenate(
            [jnp.full((N_HEADS, n_neg), NEG, F32), biasp, jnp.zeros((N_HEADS, n_zero), F32),
             jnp.full((N_HEADS, n_neg_after), NEG, F32)], axis=1)

    wk = WINDOW + TQ
    f = by_distance(TQ - 1, WINDOW - MAX_DIST, wk - WINDOW)
    fext = jnp.concatenate([f, jnp.zeros((N_HEADS, 1), F32)], axis=1)

    d_min = -(CMP_LEN - 1) - CMP_STRIDE * CMP_R_HI
    d_max = TQ - 1 - (CMP_LEN - 1) - CMP_STRIDE * CMP_R_LO
    gx = by_distance(-d_min, d_max + 1 - MAX_DIST, 0)
    gxext = jnp.pad(gx, ((0, 0), (0, fext.shape[1] - gx.shape[1])))
    wtmpl, ctmpl = _templates(fext, gxext)
    return ctmpl, wtmpl


def _overlap_t():
    cmp_start = jnp.arange(N_CMP_PAD) * CMP_STRIDE
    slc_start = jnp.arange(N_SLC) * SLC_LEN
    ov = ((cmp_start[None, :] < slc_start[:, None] + SLC_LEN)
          & (cmp_start[None, :] + CMP_LEN > slc_start[:, None])
          & (jnp.arange(N_CMP_PAD)[None, :] < N_CMP))
    return ov.astype(BF16)


def _compress_block_weights(w1):
    half = CMP_STRIDE * HEAD_DIM
    top = w1[:, :half].reshape(2, CMP_STRIDE, HEAD_DIM, CMP_HIDDEN)
    bot = w1[:, half:].reshape(2, CMP_STRIDE, HEAD_DIM, CMP_HIDDEN)
    z = jnp.zeros_like(top)
    return jnp.concatenate([jnp.concatenate([top, bot, z, z], axis=-1),
                            jnp.concatenate([z, z, top, bot], axis=-1)], axis=2)


def kernel(x, c, w_ada, b_ada, g_ffn1, w_gu1, w_down1, g_mix, w_in, w_dw, b_dw, ln_g, ln_b,
           pe_k, pe_v, w_ck1, w_ck2, w_cv1, w_cv2, rel_bias, w_out, g_ffn2, w_gu2, w_down2,
           g_final):
    assert x.shape == (1, SEQ, D_MODEL) and w_ada.shape[0] == 1
    x2 = x[0]
    mod = _ada(c.reshape(D_MODEL, 1), w_ada[0], b_ada)
    x1 = _ffn1(x2, g_ffn1, mod, w_gu1[0].astype(BF16), w_down1[0].astype(BF16))

    wi = w_in[0]
    n_in = wi.shape[1]
    col = jnp.arange(n_in)
    q_cols = (col >= 2 * D_CONV) & (col < 2 * D_CONV + D_ATT)
    w_in_p = jnp.pad(wi * jnp.where(q_cols, HEAD_DIM ** -0.5 * LOG2E, 1.0)[None, :],
                     ((0, 0), (0, D_IN_PAD - n_in))).astype(BF16)
    hc, kcvc, ksl, kw, qt, gatet, vslt, vwt = _proj(x1, g_mix, mod, w_in_p)

    pe = jnp.stack([pe_k[0].reshape(1, -1), pe_v[0].reshape(1, -1)])
    w1 = jnp.stack([w_ck1[0], w_cv1[0]]).astype(BF16)
    w2 = jnp.stack([w_ck2[0], w_cv2[0]]).astype(BF16)
    cmp = _compress(kcvc, pe, w1, _compress_block_weights(w1), w2)
    kc = jnp.concatenate([cmp[0], cmp[1]], axis=1).astype(BF16)
    ones_rows = jnp.zeros((N_KV, VROWS - HEAD_DIM, N_CMP_PAD), BF16).at[:, 0].set(1)
    vct = jnp.concatenate([jnp.swapaxes(cmp[2:4], 1, 2).astype(BF16), ones_rows], axis=1)

    ctmpl, wtmpl = _bias_templates(rel_bias)
    y_att = _attn(qt, gatet, kc, vct, _overlap_t(), ksl, vslt, kw, vwt, ctmpl, wtmpl)

    y_conv = _sc_conv(hc, jnp.pad(w_dw[0], ((0, 32 - CONV_WIDTH), (0, 0))),
                      jnp.broadcast_to(b_dw, (SUBLANES, D_CONV)))

    out = _ffn2(x1, y_conv, y_att, w_out[0].astype(BF16), mod,
                g_ffn2, w_gu2[0].astype(BF16), w_down2[0].astype(BF16),
                g_final.reshape(1, D_MODEL), ln_g, ln_b)
    return out[None]
```

```python
import functools
import math

import jax
import jax.numpy as jnp
from jax import lax
from jax.experimental import pallas as pl
from jax.experimental.pallas import tpu as pltpu
from jax.experimental.pallas import tpu_sc as plsc

F32 = jnp.float32
BF16 = jnp.bfloat16

D_MODEL = 1024
SEQ = 16384
D_CONV = 512
CONV_WIDTH = 31
N_HEADS = 8
N_KV = 2
HPG = 4
HEAD_DIM = 64
D_ATT = 512
D_KV = 128
CMP_LEN = 32
CMP_STRIDE = 16
CMP_HIDDEN = 128
N_CMP = (SEQ - CMP_LEN) // CMP_STRIDE + 1
N_CMP_PAD = 1024
SLC_LEN = 64
N_SLC = SEQ // SLC_LEN
SLC_TOPK = 16
WINDOW = 512
N_FORCED = 3
TAKEN = -2.0
N_BUCKETS = 32
MAX_DIST = 128
D_FF = 2816
FFN_RES = 0.5
EPS = 1e-6
NEG = -1e30
M_FLOOR = -1e29
LOG2E = math.log2(math.e)

V7X_VMEM_BYTES = 64 * 1024 * 1024
VMEM_LIMIT = V7X_VMEM_BYTES - 6 * 1024 * 1024
LANES = 128
SUBLANES = 8

TQ = 256
TK = 256
COLS = HPG * TQ
KPAD = WINDOW
VROWS = HEAD_DIM + 16
BLK_PER_TILE = TK // SLC_LEN
FAR_UNROLL = 4
SEL_ENTRIES = SEQ // TK + 2
CMP_R_HI = (TQ - CMP_LEN) // CMP_STRIDE
CMP_R_LO = -((MAX_DIST + CMP_LEN - 2) // CMP_STRIDE)
CT_ZERO = TK + -(CMP_R_LO // SUBLANES) * SUBLANES
CT_ROWS = CT_ZERO + -(-(CMP_R_HI + 1) // SUBLANES) * SUBLANES + TK
FFN_TM = 512
FFN_TF = D_FF
PROJ_TM = 512
CONV_HALO = 32
SC_CORES, SC_SUBCORES, SC_LANES = 2, 16, 16
SC_WORKERS = SC_CORES * SC_SUBCORES
SC_ROWS = 64
D_IN_PAD = 2432


def _params(sem, vmem_mib=None):
    limit = VMEM_LIMIT if vmem_mib is None else vmem_mib * 1024 * 1024
    return pltpu.CompilerParams(dimension_semantics=sem, vmem_limit_bytes=limit)


def _const_spec(shape):
    nd = len(shape)
    return pl.BlockSpec(shape, lambda *_: (0,) * nd, pipeline_mode=pl.Buffered(1))


def _sigmoid(v):
    return 1.0 / (1.0 + jnp.exp(-v))


def _dot(a, b):
    return jnp.dot(a, b, preferred_element_type=F32)


def _ada_kernel(c_ref, w_ref, b_ref, o_ref):
    c = c_ref[...]
    sc = c * _sigmoid(c)
    o_ref[...] = jnp.sum(w_ref[...] * sc, axis=0, keepdims=True) + b_ref[...]


def _ada(c_col, w, b):
    n = w.shape[1]
    tn = n // 8
    return pl.pallas_call(
        _ada_kernel,
        out_shape=jax.ShapeDtypeStruct((1, n), F32),
        grid=(8,),
        in_specs=[pl.BlockSpec((D_MODEL, 1), lambda j: (0, 0)),
                  pl.BlockSpec((D_MODEL, tn), lambda j: (0, j)),
                  pl.BlockSpec((1, tn), lambda j: (0, j))],
        out_specs=pl.BlockSpec((1, tn), lambda j: (0, j)),
        compiler_params=_params(("arbitrary",), 16),
        name="ada",
    )(c_col, w, b)


def _rms_mod(x, g, sh, sc):
    ms = jnp.mean(x * x, axis=-1, keepdims=True)
    y = x * lax.rsqrt(ms + EPS) * g
    return y * (1.0 + sc) + sh


def _ffn_body(x, g_ref, sh_ref, sc_ref, gt_ref, wgu_ref, wd_ref):
    hb = _rms_mod(x, g_ref[...], sh_ref[...], sc_ref[...]).astype(BF16)
    acc = jnp.zeros((x.shape[0], D_MODEL), F32)
    for f in range(D_FF // FFN_TF):
        gg = _dot(hb, wgu_ref[:, f * FFN_TF:(f + 1) * FFN_TF])
        uu = _dot(hb, wgu_ref[:, D_FF + f * FFN_TF:D_FF + (f + 1) * FFN_TF])
        a = (gg * _sigmoid(gg) * uu).astype(BF16)
        acc = acc + _dot(a, wd_ref[f * FFN_TF:(f + 1) * FFN_TF, :])
    return x + (FFN_RES * gt_ref[...]) * acc


def _ffn1_kernel(x_ref, g_ref, sh_ref, sc_ref, gt_ref, wgu_ref, wd_ref, o_ref):
    o_ref[...] = _ffn_body(x_ref[...], g_ref, sh_ref, sc_ref, gt_ref, wgu_ref, wd_ref)


def _ffn2_kernel(x_ref, yp_ref, ya_ref, wo_ref, gt2_ref,
                 g_ref, sh_ref, sc_ref, gt_ref, wgu_ref, wd_ref, gf_ref, cg_ref, cbb_ref, o_ref):
    tm = x_ref.shape[0]
    cv = yp_ref[...]
    mu = jnp.mean(cv, axis=-1, keepdims=True)
    dlt = cv - mu
    var = jnp.mean(dlt * dlt, axis=-1, keepdims=True)
    yn = dlt * lax.rsqrt(var + EPS) * cg_ref[...] + cbb_ref[...]
    y = _dot((yn * _sigmoid(yn)).astype(BF16), wo_ref[0:D_CONV, :])
    for h in range(HPG):
        rows = [slice(D_CONV + (g * HPG + h) * HEAD_DIM, D_CONV + (g * HPG + h + 1) * HEAD_DIM)
                for g in range(N_KV)]
        w_h = jnp.concatenate([wo_ref[r, :] for r in rows], axis=0)
        y = y + _dot(ya_ref[:, h].reshape(tm, LANES), w_h)
    x = x_ref[...] + gt2_ref[...] * y
    out = _ffn_body(x, g_ref, sh_ref, sc_ref, gt_ref, wgu_ref, wd_ref)
    ms = jnp.mean(out * out, axis=-1, keepdims=True)
    o_ref[...] = out * lax.rsqrt(ms + EPS) * gf_ref[...]


def _row_spec(tm, n):
    return pl.BlockSpec((tm, n), lambda i: (i, 0))


def _mod_spec(k):
    return pl.BlockSpec((1, D_MODEL), lambda *_: (0, k), pipeline_mode=pl.Buffered(1))


def _ffn1(x, g, mod, wgu, wd):
    vec = _const_spec((1, D_MODEL))
    return pl.pallas_call(
        _ffn1_kernel,
        out_shape=jax.ShapeDtypeStruct((SEQ, D_MODEL), F32),
        grid=(SEQ // FFN_TM,),
        in_specs=[_row_spec(FFN_TM, D_MODEL), vec, _mod_spec(0), _mod_spec(1), _mod_spec(2),
                  _const_spec((D_MODEL, 2 * D_FF)), _const_spec((D_FF, D_MODEL))],
        out_specs=_row_spec(FFN_TM, D_MODEL),
        compiler_params=_params(("arbitrary",)),
        name="ffn1",
    )(x, g, mod, mod, mod, wgu, wd)


def _ffn2(x, yp, ya, wo, mod, g, wgu, wd, gf, cg, cbb):
    vec = _const_spec((1, D_MODEL))
    cvec = _const_spec((1, D_CONV))
    nt = FFN_TM // TQ
    return pl.pallas_call(
        _ffn2_kernel,
        out_shape=jax.ShapeDtypeStruct((SEQ, D_MODEL), F32),
        grid=(SEQ // FFN_TM,),
        in_specs=[_row_spec(FFN_TM, D_MODEL), _row_spec(FFN_TM, D_CONV),
                  pl.BlockSpec((nt, HPG, TQ, LANES), lambda i: (i, 0, 0, 0)),
                  _const_spec((D_CONV + D_ATT, D_MODEL)), _mod_spec(5),
                  vec, _mod_spec(6), _mod_spec(7), _mod_spec(8),
                  _const_spec((D_MODEL, 2 * D_FF)), _const_spec((D_FF, D_MODEL)), vec, cvec, cvec],
        out_specs=_row_spec(FFN_TM, D_MODEL),
        compiler_params=_params(("arbitrary",)),
        name="ffn2",
    )(x, yp, ya, wo, mod, g, mod, mod, mod, wgu, wd, gf, cg, cbb)


def _proj_kernel(x_ref, g_ref, sh_ref, sc_ref, w_ref,
                 hc_ref, kcvc_ref, ksl_ref, kw_ref, qt_ref, gatet_ref, vslt_ref, vwt_ref):
    hb = _rms_mod(x_ref[...], g_ref[...], sh_ref[...], sc_ref[...]).astype(BF16)
    u = _dot(hb, w_ref[...])
    hc_ref[...] = u[:, 0:512] * _sigmoid(u[:, 512:1024])
    kcvc_ref[...] = u[:, 1536:1792]
    tm = x_ref.shape[0]
    blk = (lax.broadcasted_iota(jnp.int32, (tm, D_KV), 0) % TK) // SLC_LEN
    ind = (blk == lax.broadcasted_iota(jnp.int32, (tm, D_KV), 1)).astype(BF16)
    ksl_ref[...] = jnp.concatenate([u[:, 1792:1920].astype(BF16), ind], axis=1)
    kw_ref[...] = u[:, 2048:2176].astype(BF16)
    qt_ref[...] = u[:, 1024:1536].T.astype(BF16)
    gatet_ref[...] = _sigmoid(u[:, 2304:2432]).T
    tail = jnp.concatenate([jnp.ones((1, TK), BF16), jnp.zeros((VROWS - HEAD_DIM - 1, TK), BF16)])
    for vt_ref, c0 in ((vslt_ref, 1920), (vwt_ref, 2176)):
        vt = u[:, c0:c0 + D_KV].T.astype(BF16)
        for g in range(N_KV):
            for t in range(tm // TK):
                vt_ref[g, t] = jnp.concatenate(
                    [vt[g * HEAD_DIM:(g + 1) * HEAD_DIM, t * TK:(t + 1) * TK], tail], axis=0)


def _proj(x, g, mod, w):
    vec = _const_spec((1, D_MODEL))
    tm = PROJ_TM
    rows = [(D_CONV, F32), (2 * D_KV, F32), (2 * D_KV, BF16), (D_KV, BF16)]
    cols = [(D_ATT, BF16), (LANES, F32)]
    vt_shape = jax.ShapeDtypeStruct((N_KV, SEQ // TK, VROWS, TK), BF16)
    vt_spec = pl.BlockSpec((N_KV, tm // TK, VROWS, TK), lambda i: (0, i, 0, 0))
    return pl.pallas_call(
        _proj_kernel,
        out_shape=[jax.ShapeDtypeStruct((SEQ, n), dt) for n, dt in rows]
                  + [jax.ShapeDtypeStruct((n, SEQ), dt) for n, dt in cols] + [vt_shape, vt_shape],
        grid=(SEQ // tm,),
        in_specs=[_row_spec(tm, D_MODEL), vec, _mod_spec(3), _mod_spec(4),
                  _const_spec((D_MODEL, D_IN_PAD))],
        out_specs=[_row_spec(tm, n) for n, _ in rows]
                  + [pl.BlockSpec((n, tm), lambda i: (0, i)) for n, _ in cols] + [vt_spec, vt_spec],
        compiler_params=_params(("arbitrary",), 40),
        name="proj",
    )(x, g, mod, mod, w)


def _sc_conv(hc, w32, b8):
    rows_per_worker = SEQ // SC_WORKERS
    n_chunks = rows_per_worker // SC_ROWS
    halves = (tuple(range(0, 16)), tuple(range(16, CONV_WIDTH)))
    off = CONV_HALO - (CONV_WIDTH - 1)
    mesh = plsc.VectorSubcoreMesh(core_axis_name="c", subcore_axis_name="s")

    @functools.partial(
        pl.kernel, mesh=mesh,
        out_type=jax.ShapeDtypeStruct((SEQ, D_CONV), F32),
        scratch_types=[pltpu.VMEM((CONV_HALO + SC_ROWS, D_CONV), F32),
                       pltpu.VMEM((SC_ROWS, D_CONV), F32),
                       pltpu.VMEM((32, D_CONV), F32),
                       pltpu.VMEM((SUBLANES, D_CONV), F32)],
        compiler_params=pltpu.CompilerParams(use_tc_tiling_on_sc=True),
        name="sc_conv",
    )
    def k(hc_hbm, w_hbm, b_hbm, out_hbm, x_v, o_v, w_v, b_v):
        wid = lax.axis_index("s") * SC_CORES + lax.axis_index("c")
        pltpu.sync_copy(w_hbm, w_v)
        pltpu.sync_copy(b_hbm, b_v)

        @pl.loop(0, n_chunks)
        def _(ci):
            row0 = pl.multiple_of(wid * rows_per_worker + ci * SC_ROWS, SC_ROWS)

            @pl.when(row0 == 0)
            def _():
                @pl.loop(0, CONV_HALO)
                def _(r):
                    @pl.loop(0, D_CONV // SC_LANES)
                    def _(cg):
                        x_v[r, pl.ds(cg * SC_LANES, SC_LANES)] = jnp.zeros((SC_LANES,), F32)
                pltpu.sync_copy(hc_hbm.at[pl.ds(0, SC_ROWS)], x_v.at[pl.ds(CONV_HALO, SC_ROWS)])

            @pl.when(row0 > 0)
            def _():
                pltpu.sync_copy(hc_hbm.at[pl.ds(row0 - CONV_HALO, CONV_HALO + SC_ROWS)], x_v)

            @pl.loop(0, D_CONV // SC_LANES)
            def _(cg):
                lanes = pl.ds(cg * SC_LANES, SC_LANES)
                for hi, taps in enumerate(halves):
                    wv = [w_v[t, lanes] for t in taps]

                    @pl.loop(0, SC_ROWS)
                    def _(r):
                        acc = b_v[0, lanes] if hi == 0 else o_v[r, lanes]
                        for j, t in enumerate(taps):
                            acc = acc + x_v[r + off + t, lanes] * wv[j]
                        o_v[r, lanes] = acc

            pltpu.sync_copy(o_v, out_hbm.at[pl.ds(row0, SC_ROWS)])

    return k(hc, w32, b8)


def _compress_kernel(c_ref, pe_ref, w1_ref, wblk_ref, w2_ref, o_ref):
    half = CMP_STRIDE * HEAD_DIM
    acc = jnp.zeros((N_CMP_PAD, 4 * CMP_HIDDEN), F32)
    for l in range(CMP_STRIDE):
        x = c_ref[pl.ds(l, N_CMP_PAD, stride=CMP_STRIDE), :].astype(BF16)
        acc = acc + _dot(x, wblk_ref[0, l])
    pe = jnp.broadcast_to(pe_ref[0], (SUBLANES, 2 * half)).astype(BF16)
    pe_term = (_dot(pe[:, :half], w1_ref[0, :half, :]) + _dot(pe[:, half:], w1_ref[0, half:, :]))[0:1]
    for g in range(N_KV):
        a = acc[:, 2 * g * CMP_HIDDEN:(2 * g + 1) * CMP_HIDDEN]
        b = acc[:, (2 * g + 1) * CMP_HIDDEN:(2 * g + 2) * CMP_HIDDEN]
        b_up = jnp.concatenate([b[1:], jnp.zeros((1, CMP_HIDDEN), F32)], axis=0)
        pre = a + b_up + pe_term
        hid = (pre * _sigmoid(pre)).astype(BF16)
        o_ref[g] = _dot(hid, w2_ref[0])


def _compress(kcvc, pe, w1, wblk, w2):
    return pl.pallas_call(
        _compress_kernel,
        out_shape=jax.ShapeDtypeStruct((2 * N_KV, N_CMP_PAD, HEAD_DIM), F32),
        grid=(2,),
        in_specs=[pl.BlockSpec((SEQ, D_KV), lambda i: (0, i)),
                  pl.BlockSpec((1, 1, 2048), lambda i: (i, 0, 0)),
                  pl.BlockSpec((1, 2048, CMP_HIDDEN), lambda i: (i, 0, 0)),
                  pl.BlockSpec((1, CMP_STRIDE, D_KV, 4 * CMP_HIDDEN), lambda i: (i, 0, 0, 0)),
                  pl.BlockSpec((1, CMP_HIDDEN, HEAD_DIM), lambda i: (i, 0, 0))],
        out_specs=pl.BlockSpec((N_KV, N_CMP_PAD, HEAD_DIM), lambda i: (i, 0, 0)),
        compiler_params=_params(("arbitrary",), 32),
        name="compress",
    )(kcvc, pe, w1, wblk, w2)


def _tile_update(s, m, acc, vt):
    m_new = jnp.maximum(m, jnp.max(s, axis=0, keepdims=True))
    alpha = jnp.exp2(m - m_new)
    p = jnp.exp2(s - m_new).astype(BF16)
    return m_new, alpha * acc + _dot(vt, p)


def _finish(acc):
    return acc[:HEAD_DIM] / jnp.maximum(acc[HEAD_DIM:HEAD_DIM + 1], 1e-30)


def _attn_kernel(qt_ref, gt_ref, kc_ref, vct_ref, ovt_ref, ksl_ref, vslt_ref, kw_ref, vwt_ref,
                 ct_ref, wt_ref, o_ref, s_scr, p_scr, sel_scr, wa_scr, wb_scr, sa_scr, sb_scr):
    qb = pl.program_id(0)
    q0 = qb * TQ
    qt = qt_ref[...]
    gt = gt_ref[...]
    m0 = jnp.full((1, COLS), M_FLOOR, F32)
    acc0 = jnp.zeros((VROWS, COLS), F32)

    def template(g, first_row, rows0):
        r0 = jnp.where(first_row >= KPAD, rows0, WINDOW + TQ)
        return wt_ref[g, pl.ds(pl.multiple_of(r0, TK), TK), :]

    def real_row(first_row):
        return pl.multiple_of(jnp.maximum(first_row - KPAD, 0), TK)

    comb = []
    zq = jnp.zeros((HEAD_DIM, TQ), BF16)
    for g in range(N_KV):
        heads = [qt[(g * HPG + h) * HEAD_DIM:(g * HPG + h + 1) * HEAD_DIM, :] for h in range(HPG)]
        q_g = jnp.concatenate(
            [jnp.concatenate([q_h, zq] if g == 0 else [zq, q_h], axis=0) for q_h in heads],
            axis=1)

        mcol = m0
        for ct in range(N_CMP_PAD // TK):
            rows = slice(ct * TK, (ct + 1) * TK)
            r0 = pl.multiple_of(jnp.clip(TK * ct - (TQ // CMP_STRIDE) * qb + CT_ZERO,
                                         0, CT_ROWS - TK), SUBLANES)
            s = _dot(kc_ref[rows, :], q_g) + ct_ref[g, pl.ds(r0, TK), :]
            s_scr[rows, :] = s
            mcol = jnp.maximum(mcol, jnp.max(s, axis=0, keepdims=True))
        acc_c = acc0
        for ct in range(N_CMP_PAD // TK):
            rows = slice(ct * TK, (ct + 1) * TK)
            p = jnp.exp2(s_scr[rows, :] - mcol)
            p_scr[rows, :] = p
            acc_c = acc_c + _dot(vct_ref[g, :, rows], p.astype(BF16))
        rinv = 1.0 / jnp.maximum(acc_c[HEAD_DIM:HEAD_DIM + 1], 1e-30)
        o_c = acc_c[:HEAD_DIM] * rinv
        imp = jnp.zeros((N_SLC, TQ), F32)
        for ct in range(N_CMP_PAD // TK):
            rows = slice(ct * TK, (ct + 1) * TK)
            pn = p_scr[rows, :] * rinv
            ps = pn[:, 0:TQ] + pn[:, TQ:2 * TQ] + pn[:, 2 * TQ:3 * TQ] + pn[:, 3 * TQ:4 * TQ]
            hi = ps.astype(BF16)
            lo = (ps - hi.astype(F32)).astype(BF16)
            ov = ovt_ref[:, rows]
            imp = imp + _dot(ov, hi) + _dot(ov, lo)

        m, acc = m0, acc0
        for a in range(3):
            first = q0 + TK * a
            s = _dot(kw_ref[pl.ds(real_row(first), TK), :], q_g)
            s = s + template(g, first, TK * a)
            m, acc = _tile_update(s, m, acc, vwt_ref[g, jnp.maximum(qb + a - KPAD // TK, 0)])
        o_w = _finish(acc)

        cur = jnp.right_shift(q0 + lax.broadcasted_iota(jnp.int32, (1, TQ), 1), 6)
        jcol = lax.broadcasted_iota(jnp.int32, (N_SLC, 1), 0)
        forced = (jcol == 0) | (jcol == cur) | (jcol == cur - 1)
        sc = jnp.where(forced, TAKEN, jnp.where(jcol <= cur, imp, -1.0))
        jf = lax.broadcasted_iota(jnp.int32, (N_SLC, TQ), 0).astype(F32)
        for _ in range(SLC_TOPK - N_FORCED):
            mx = jnp.max(sc, axis=0, keepdims=True)
            idx = jnp.min(jnp.where(sc == mx, jf, 1e9), axis=0, keepdims=True)
            sc = jnp.where(jf == idx, TAKEN, sc)
        selneg = jnp.where(sc == TAKEN, 0.0, NEG)
        unused = jnp.zeros((SUBLANES - BLK_PER_TILE, TQ), F32)
        for kt in range(SEQ // TK):
            sel_scr[kt] = jnp.concatenate(
                [selneg[kt * BLK_PER_TILE:(kt + 1) * BLK_PER_TILE, :], unused], axis=0)
        sel_scr[SEQ // TK] = jnp.concatenate(
            [jnp.full((BLK_PER_TILE, TQ), NEG, F32), unused], axis=0)

        zrows = jnp.zeros((2 * D_KV - D_KV - 2 * SUBLANES, COLS), BF16)
        for w_scr in (wa_scr, wb_scr):
            w_scr[0:D_KV, :] = q_g
            w_scr[D_KV + 2 * SUBLANES:, :] = zrows

        def sel_scores(w_scr, first_row, sel_idx):
            blk = sel_scr[sel_idx]
            rows8 = jnp.concatenate([blk] * HPG, axis=1).astype(BF16)
            w_scr[D_KV:D_KV + 2 * SUBLANES, :] = jnp.concatenate(
                [rows8, jnp.zeros((SUBLANES, COLS), BF16)], axis=0)
            return _dot(ksl_ref[pl.ds(real_row(first_row), TK), :], w_scr[...])

        m, acc = m0, acc0
        for a, w_scr in ((0, wa_scr), (1, wb_scr)):
            first = q0 + TQ + TK * a
            s = sel_scores(w_scr, first, jnp.maximum(qb - 1 + a, 0))
            s = s + template(g, first, TQ + TK * a)
            m, acc = _tile_update(s, m, acc, vslt_ref[g, jnp.maximum(qb - 1 + a, 0)])

        n_far = jnp.maximum(qb - 1, 0)

        def far_scores(w_scr, kt):
            valid = kt < n_far
            ks = jnp.where(valid, kt, 0)
            return sel_scores(w_scr, KPAD + ks * TK, jnp.where(valid, kt, SEQ // TK))

        def far_v(kt):
            return vslt_ref[g, jnp.where(kt < n_far, kt, 0)]

        bufs = ((sa_scr, wa_scr), (sb_scr, wb_scr))

        def issue_scores(buf, kt):
            s_scr_k, w_scr_k = bufs[buf]
            s = far_scores(w_scr_k, kt)
            s_scr_k[...] = s
            return jnp.max(s, axis=0, keepdims=True)

        def far_body(i, carry):
            m_i, acc_i, smax = carry
            for u in range(FAR_UNROLL):
                kt = i * FAR_UNROLL + u
                smax_nxt = issue_scores((u + 1) % 2, kt + 1)
                m_new = jnp.maximum(m_i, smax)
                alpha = jnp.exp2(m_i - m_new)
                p = jnp.exp2(bufs[u % 2][0][...] - m_new).astype(BF16)
                acc_i = alpha * acc_i + _dot(far_v(kt), p)
                m_i, smax = m_new, smax_nxt
            return m_i, acc_i, smax

        trips = (n_far + FAR_UNROLL - 1) // FAR_UNROLL
        m, acc, _ = lax.fori_loop(0, trips, far_body, (m, acc, issue_scores(0, 0)))
        o_s = _finish(acc)

        per_head = []
        for h in range(HPG):
            c0 = 3 * (g * HPG + h)
            cs = slice(h * TQ, (h + 1) * TQ)
            per_head.append(gt[c0:c0 + 1, :] * o_c[:, cs] + gt[c0 + 1:c0 + 2, :] * o_s[:, cs]
                            + gt[c0 + 2:c0 + 3, :] * o_w[:, cs])
        comb.append(per_head)

    for h in range(HPG):
        merged = jnp.concatenate([comb[0][h], comb[1][h]], axis=0)
        o_ref[0, h] = merged.T.astype(BF16)


def _attn(qt, gatet, kc, vct, ovt, ksl, vslt, kw, vwt, ctmpl, wtmpl):
    consts = [kc, vct, ovt, ksl, vslt, kw, vwt, ctmpl, wtmpl]
    return pl.pallas_call(
        _attn_kernel,
        out_shape=jax.ShapeDtypeStruct((SEQ // TQ, HPG, TQ, LANES), BF16),
        grid=(SEQ // TQ,),
        in_specs=[pl.BlockSpec((D_ATT, TQ), lambda i: (0, i)),
                  pl.BlockSpec((LANES, TQ), lambda i: (0, i))]
                 + [_const_spec(a.shape) for a in consts],
        out_specs=pl.BlockSpec((1, HPG, TQ, LANES), lambda i: (i, 0, 0, 0)),
        scratch_shapes=[pltpu.VMEM((N_CMP_PAD, COLS), F32), pltpu.VMEM((N_CMP_PAD, COLS), F32),
                        pltpu.VMEM((SEL_ENTRIES, SUBLANES, TQ), F32),
                        pltpu.VMEM((2 * D_KV, COLS), BF16), pltpu.VMEM((2 * D_KV, COLS), BF16),
                        pltpu.VMEM((TK, COLS), F32), pltpu.VMEM((TK, COLS), F32)],
        compiler_params=_params(("arbitrary",)),
        name="nsa_attn",
    )(qt, gatet, *consts)


def _t5_bucket(dist):
    max_exact = N_BUCKETS // 2
    d = jnp.maximum(dist, 0)
    df = jnp.maximum(d, 1).astype(F32)
    large = max_exact + (jnp.log(df / max_exact) / math.log(MAX_DIST / max_exact)
                         * (N_BUCKETS - max_exact)).astype(jnp.int32)
    large = jnp.minimum(large, N_BUCKETS - 1)
    return jnp.where(d < max_exact, d, large)


def _template_kernel(f_ref, gx_ref, wt_ref, ct_ref):
    wk, lanes = WINDOW + TQ, f_ref.shape[1]
    n_band = CMP_R_HI - CMP_R_LO + 1
    lo = CT_ZERO + CMP_R_LO
    for hd in range(N_HEADS):
        g, cols = hd // HPG, slice((hd % HPG) * TQ, (hd % HPG + 1) * TQ)
        x = jnp.broadcast_to(f_ref[hd:hd + 1, :], (wk, lanes))
        y = pltpu.roll(x, lanes - (wk - 1), axis=1, stride=1, stride_axis=0)
        wt_ref[g, 0:wk, cols] = y[:, :TQ]
        wt_ref[g, wk:, cols] = jnp.full((TK, TQ), NEG, F32)
        xb = jnp.broadcast_to(gx_ref[hd:hd + 1, :], (n_band, lanes))
        yb = pltpu.roll(xb, lanes - CMP_STRIDE * (n_band - 1), axis=1,
                        stride=CMP_STRIDE, stride_axis=0)
        ct_ref[g, 0:lo, cols] = jnp.zeros((lo, TQ), F32)
        ct_ref[g, lo:lo + n_band, cols] = yb[:, :TQ]
        ct_ref[g, lo + n_band:, cols] = jnp.full((CT_ROWS - lo - n_band, TQ), NEG, F32)


def _templates(fext, gxext):
    return pl.pallas_call(
        _template_kernel,
        out_shape=[jax.ShapeDtypeStruct((N_KV, WINDOW + TQ + TK, COLS), F32),
                   jax.ShapeDtypeStruct((N_KV, CT_ROWS, COLS), F32)],
        compiler_params=_params(None, 32),
        name="templates",
    )(fext, gxext)


def _bias_templates(rel_bias):
    biasp = rel_bias[:, _t5_bucket(jnp.arange(MAX_DIST))] - rel_bias[:, N_BUCKETS - 1:]
    biasp = biasp * LOG2E

    def by_distance(n_neg, n_zero, n_neg_after):
        return jnp.concatenate(
            [jnp.full((N_HEADS, n_neg), NEG, F32), biasp, jnp.zeros((N_HEADS, n_zero), F32),
             jnp.full((N_HEADS, n_neg_after), NEG, F32)], axis=1)

    wk = WINDOW + TQ
    f = by_distance(TQ - 1, WINDOW - MAX_DIST, wk - WINDOW)
    fext = jnp.concatenate([f, jnp.zeros((N_HEADS, 1), F32)], axis=1)

    d_min = -(CMP_LEN - 1) - CMP_STRIDE * CMP_R_HI
    d_max = TQ - 1 - (CMP_LEN - 1) - CMP_STRIDE * CMP_R_LO
    gx = by_distance(-d_min, d_max + 1 - MAX_DIST, 0)
    gxext = jnp.pad(gx, ((0, 0), (0, fext.shape[1] - gx.shape[1])))
    wtmpl, ctmpl = _templates(fext, gxext)
    return ctmpl, wtmpl


def _overlap_t():
    cmp_start = jnp.arange(N_CMP_PAD) * CMP_STRIDE
    slc_start = jnp.arange(N_SLC) * SLC_LEN
    ov = ((cmp_start[None, :] < slc_start[:, None] + SLC_LEN)
          & (cmp_start[None, :] + CMP_LEN > slc_start[:, None])
          & (jnp.arange(N_CMP_PAD)[None, :] < N_CMP))
    return ov.astype(BF16)


def _compress_block_weights(w1):
    half = CMP_STRIDE * HEAD_DIM
    top = w1[:, :half].reshape(2, CMP_STRIDE, HEAD_DIM, CMP_HIDDEN)
    bot = w1[:, half:].reshape(2, CMP_STRIDE, HEAD_DIM, CMP_HIDDEN)
    z = jnp.zeros_like(top)
    return jnp.concatenate([jnp.concatenate([top, bot, z, z], axis=-1),
                            jnp.concatenate([z, z, top, bot], axis=-1)], axis=2)


def kernel(x, c, w_ada, b_ada, g_ffn1, w_gu1, w_down1, g_mix, w_in, w_dw, b_dw, ln_g, ln_b,
           pe_k, pe_v, w_ck1, w_ck2, w_cv1, w_cv2, rel_bias, w_out, g_ffn2, w_gu2, w_down2,
           g_final):
    assert x.shape == (1, SEQ, D_MODEL) and w_ada.shape[0] == 1
    x2 = x[0]
    mod = _ada(c.reshape(D_MODEL, 1), w_ada[0], b_ada)
    x1 = _ffn1(x2, g_ffn1, mod, w_gu1[0].astype(BF16), w_down1[0].astype(BF16))

    wi = w_in[0]
    n_in = wi.shape[1]
    col = jnp.arange(n_in)
    q_cols = (col >= 2 * D_CONV) & (col < 2 * D_CONV + D_ATT)
    w_in_p = jnp.pad(wi * jnp.where(q_cols, HEAD_DIM ** -0.5 * LOG2E, 1.0)[None, :],
                     ((0, 0), (0, D_IN_PAD - n_in))).astype(BF16)
    hc, kcvc, ksl, kw, qt, gatet, vslt, vwt = _proj(x1, g_mix, mod, w_in_p)

    pe = jnp.stack([pe_k[0].reshape(1, -1), pe_v[0].reshape(1, -1)])
    w1 = jnp.stack([w_ck1[0], w_cv1[0]]).astype(BF16)
    w2 = jnp.stack([w_ck2[0], w_cv2[0]]).astype(BF16)
    cmp = _compress(kcvc, pe, w1, _compress_block_weights(w1), w2)
    kc = jnp.concatenate([cmp[0], cmp[1]], axis=1).astype(BF16)
    ones_rows = jnp.zeros((N_KV, VROWS - HEAD_DIM, N_CMP_PAD), BF16).at[:, 0].set(1)
    vct = jnp.concatenate([jnp.swapaxes(cmp[2:4], 1, 2).astype(BF16), ones_rows], axis=1)

    ctmpl, wtmpl = _bias_templates(rel_bias)
    y_att = _attn(qt, gatet, kc, vct, _overlap_t(), ksl, vslt, kw, vwt, ctmpl, wtmpl)

    y_conv = _sc_conv(hc, jnp.pad(w_dw[0], ((0, 32 - CONV_WIDTH), (0, 0))),
                      jnp.broadcast_to(b_dw, (SUBLANES, D_CONV)))

    out = _ffn2(x1, y_conv, y_att, w_out[0].astype(BF16), mod,
                g_ffn2, w_gu2[0].astype(BF16), w_down2[0].astype(BF16),
                g_final.reshape(1, D_MODEL), ln_g, ln_b)
    return out[None]
```

```python
import functools
import math

import jax
import jax.numpy as jnp
from jax import lax
from jax.experimental import pallas as pl
from jax.experimental.pallas import tpu as pltpu
from jax.experimental.pallas import tpu_sc as plsc

F32 = jnp.float32
BF16 = jnp.bfloat16

D_MODEL = 1024
SEQ = 16384
D_CONV = 512
CONV_WIDTH = 31
N_HEADS = 8
N_KV = 2
HPG = 4
HEAD_DIM = 64
D_ATT = 512
D_KV = 128
CMP_LEN = 32
CMP_STRIDE = 16
CMP_HIDDEN = 128
N_CMP = (SEQ - CMP_LEN) // CMP_STRIDE + 1
N_CMP_PAD = 1024
SLC_LEN = 64
N_SLC = SEQ // SLC_LEN
SLC_TOPK = 16
WINDOW = 512
N_FORCED = 3
TAKEN = -2.0
N_BUCKETS = 32
MAX_DIST = 128
D_FF = 2816
FFN_RES = 0.5
EPS = 1e-6
NEG = -1e30
M_FLOOR = -1e29
LOG2E = math.log2(math.e)

V7X_VMEM_BYTES = 64 * 1024 * 1024
VMEM_LIMIT = V7X_VMEM_BYTES - 6 * 1024 * 1024
LANES = 128
SUBLANES = 8

TQ = 256
TK = 256
COLS = HPG * TQ
KPAD = WINDOW
VROWS = HEAD_DIM + 16
BLK_PER_TILE = TK // SLC_LEN
FAR_UNROLL = 4
SEL_ENTRIES = SEQ // TK + 2
CMP_R_HI = (TQ - CMP_LEN) // CMP_STRIDE
CMP_R_LO = -((MAX_DIST + CMP_LEN - 2) // CMP_STRIDE)
CT_ZERO = TK + -(CMP_R_LO // SUBLANES) * SUBLANES
CT_ROWS = CT_ZERO + -(-(CMP_R_HI + 1) // SUBLANES) * SUBLANES + TK
FFN_TM = 512
FFN_TF = D_FF
PROJ_TM = 512
CONV_HALO = 32
SC_CORES, SC_SUBCORES, SC_LANES = 2, 16, 16
SC_WORKERS = SC_CORES * SC_SUBCORES
SC_ROWS = 64
D_IN_PAD = 2432


def _params(sem):
    return pltpu.CompilerParams(dimension_semantics=sem, vmem_limit_bytes=VMEM_LIMIT)


def _const_spec(shape):
    nd = len(shape)
    return pl.BlockSpec(shape, lambda *_: (0,) * nd, pipeline_mode=pl.Buffered(1))


def _sigmoid(v):
    return 1.0 / (1.0 + jnp.exp(-v))


def _dot(a, b):
    return jnp.dot(a, b, preferred_element_type=F32)


def _ada_kernel(c_ref, w_ref, b_ref, o_ref):
    c = c_ref[...]
    sc = c * _sigmoid(c)
    o_ref[...] = jnp.sum(w_ref[...] * sc, axis=0, keepdims=True) + b_ref[...]


def _ada(c_col, w, b):
    n = w.shape[1]
    tn = n // 8
    return pl.pallas_call(
        _ada_kernel,
        out_shape=jax.ShapeDtypeStruct((1, n), F32),
        grid=(8,),
        in_specs=[pl.BlockSpec((D_MODEL, 1), lambda j: (0, 0)),
                  pl.BlockSpec((D_MODEL, tn), lambda j: (0, j)),
                  pl.BlockSpec((1, tn), lambda j: (0, j))],
        out_specs=pl.BlockSpec((1, tn), lambda j: (0, j)),
        compiler_params=_params(("arbitrary",)),
        name="ada",
    )(c_col, w, b)


def _rms_mod(x, g, sh, sc):
    ms = jnp.mean(x * x, axis=-1, keepdims=True)
    y = x * lax.rsqrt(ms + EPS) * g
    return y * (1.0 + sc) + sh


def _ffn_body(x, g_ref, sh_ref, sc_ref, gt_ref, wgu_ref, wd_ref):
    hb = _rms_mod(x, g_ref[...], sh_ref[...], sc_ref[...]).astype(BF16)
    acc = jnp.zeros((x.shape[0], D_MODEL), F32)
    for f in range(D_FF // FFN_TF):
        gg = _dot(hb, wgu_ref[:, f * FFN_TF:(f + 1) * FFN_TF])
        uu = _dot(hb, wgu_ref[:, D_FF + f * FFN_TF:D_FF + (f + 1) * FFN_TF])
        a = (gg * _sigmoid(gg) * uu).astype(BF16)
        acc = acc + _dot(a, wd_ref[f * FFN_TF:(f + 1) * FFN_TF, :])
    return x + (FFN_RES * gt_ref[...]) * acc


def _ffn1_kernel(x_ref, g_ref, sh_ref, sc_ref, gt_ref, wgu_ref, wd_ref, o_ref):
    o_ref[...] = _ffn_body(x_ref[...], g_ref, sh_ref, sc_ref, gt_ref, wgu_ref, wd_ref)


def _ffn2_kernel(x_ref, yp_ref, ya_ref, wo_ref, gt2_ref,
                 g_ref, sh_ref, sc_ref, gt_ref, wgu_ref, wd_ref, gf_ref, cg_ref, cbb_ref, o_ref):
    tm = x_ref.shape[0]
    cv = yp_ref[...]
    mu = jnp.mean(cv, axis=-1, keepdims=True)
    dlt = cv - mu
    var = jnp.mean(dlt * dlt, axis=-1, keepdims=True)
    yn = dlt * lax.rsqrt(var + EPS) * cg_ref[...] + cbb_ref[...]
    y = _dot((yn * _sigmoid(yn)).astype(BF16), wo_ref[0:D_CONV, :])
    for h in range(HPG):
        rows = [slice(D_CONV + (g * HPG + h) * HEAD_DIM, D_CONV + (g * HPG + h + 1) * HEAD_DIM)
                for g in range(N_KV)]
        w_h = jnp.concatenate([wo_ref[r, :] for r in rows], axis=0)
        y = y + _dot(ya_ref[:, h].reshape(tm, LANES), w_h)
    x = x_ref[...] + gt2_ref[...] * y
    out = _ffn_body(x, g_ref, sh_ref, sc_ref, gt_ref, wgu_ref, wd_ref)
    ms = jnp.mean(out * out, axis=-1, keepdims=True)
    o_ref[...] = out * lax.rsqrt(ms + EPS) * gf_ref[...]


def _row_spec(tm, n):
    return pl.BlockSpec((tm, n), lambda i: (i, 0))


def _mod_spec(k):
    return pl.BlockSpec((1, D_MODEL), lambda *_: (0, k), pipeline_mode=pl.Buffered(1))


def _ffn1(x, g, mod, wgu, wd):
    vec = _const_spec((1, D_MODEL))
    return pl.pallas_call(
        _ffn1_kernel,
        out_shape=jax.ShapeDtypeStruct((SEQ, D_MODEL), F32),
        grid=(SEQ // FFN_TM,),
        in_specs=[_row_spec(FFN_TM, D_MODEL), vec, _mod_spec(0), _mod_spec(1), _mod_spec(2),
                  _const_spec((D_MODEL, 2 * D_FF)), _const_spec((D_FF, D_MODEL))],
        out_specs=_row_spec(FFN_TM, D_MODEL),
        compiler_params=_params(("arbitrary",)),
        name="ffn1",
    )(x, g, mod, mod, mod, wgu, wd)


def _ffn2(x, yp, ya, wo, mod, g, wgu, wd, gf, cg, cbb):
    vec = _const_spec((1, D_MODEL))
    cvec = _const_spec((1, D_CONV))
    nt = FFN_TM // TQ
    return pl.pallas_call(
        _ffn2_kernel,
        out_shape=jax.ShapeDtypeStruct((SEQ, D_MODEL), F32),
        grid=(SEQ // FFN_TM,),
        in_specs=[_row_spec(FFN_TM, D_MODEL), _row_spec(FFN_TM, D_CONV),
                  pl.BlockSpec((nt, HPG, TQ, LANES), lambda i: (i, 0, 0, 0)),
                  _const_spec((D_CONV + D_ATT, D_MODEL)), _mod_spec(5),
                  vec, _mod_spec(6), _mod_spec(7), _mod_spec(8),
                  _const_spec((D_MODEL, 2 * D_FF)), _const_spec((D_FF, D_MODEL)), vec, cvec, cvec],
        out_specs=_row_spec(FFN_TM, D_MODEL),
        compiler_params=_params(("arbitrary",)),
        name="ffn2",
    )(x, yp, ya, wo, mod, g, mod, mod, mod, wgu, wd, gf, cg, cbb)


def _proj_kernel(x_ref, g_ref, sh_ref, sc_ref, w_ref,
                 hc_ref, kcvc_ref, ksl_ref, kw_ref, qt_ref, gatet_ref, vslt_ref, vwt_ref):
    hb = _rms_mod(x_ref[...], g_ref[...], sh_ref[...], sc_ref[...]).astype(BF16)
    u = _dot(hb, w_ref[...])
    hc_ref[...] = u[:, 0:512] * _sigmoid(u[:, 512:1024])
    kcvc_ref[...] = u[:, 1536:1792]
    tm = x_ref.shape[0]
    blk = (lax.broadcasted_iota(jnp.int32, (tm, D_KV), 0) % TK) // SLC_LEN
    ind = (blk == lax.broadcasted_iota(jnp.int32, (tm, D_KV), 1)).astype(BF16)
    ksl_ref[...] = jnp.concatenate([u[:, 1792:1920].astype(BF16), ind], axis=1)
    kw_ref[...] = u[:, 2048:2176].astype(BF16)
    qt_ref[...] = u[:, 1024:1536].T.astype(BF16)
    gatet_ref[...] = _sigmoid(u[:, 2304:2432]).T
    tail = jnp.concatenate([jnp.ones((1, TK), BF16), jnp.zeros((VROWS - HEAD_DIM - 1, TK), BF16)])
    for vt_ref, c0 in ((vslt_ref, 1920), (vwt_ref, 2176)):
        vt = u[:, c0:c0 + D_KV].T.astype(BF16)
        for g in range(N_KV):
            for t in range(tm // TK):
                vt_ref[g, t] = jnp.concatenate(
                    [vt[g * HEAD_DIM:(g + 1) * HEAD_DIM, t * TK:(t + 1) * TK], tail], axis=0)


def _proj(x, g, mod, w):
    vec = _const_spec((1, D_MODEL))
    tm = PROJ_TM
    rows = [(D_CONV, F32), (2 * D_KV, F32), (2 * D_KV, BF16), (D_KV, BF16)]
    cols = [(D_ATT, BF16), (LANES, F32)]
    vt_shape = jax.ShapeDtypeStruct((N_KV, SEQ // TK, VROWS, TK), BF16)
    vt_spec = pl.BlockSpec((N_KV, tm // TK, VROWS, TK), lambda i: (0, i, 0, 0))
    return pl.pallas_call(
        _proj_kernel,
        out_shape=[jax.ShapeDtypeStruct((SEQ, n), dt) for n, dt in rows]
                  + [jax.ShapeDtypeStruct((n, SEQ), dt) for n, dt in cols] + [vt_shape, vt_shape],
        grid=(SEQ // tm,),
        in_specs=[_row_spec(tm, D_MODEL), vec, _mod_spec(3), _mod_spec(4),
                  _const_spec((D_MODEL, D_IN_PAD))],
        out_specs=[_row_spec(tm, n) for n, _ in rows]
                  + [pl.BlockSpec((n, tm), lambda i: (0, i)) for n, _ in cols] + [vt_spec, vt_spec],
        compiler_params=_params(("arbitrary",)),
        name="proj",
    )(x, g, mod, mod, w)


def _sc_conv(hc, w32, b8):
    rows_per_worker = SEQ // SC_WORKERS
    n_chunks = rows_per_worker // SC_ROWS
    halves = (tuple(range(0, 16)), tuple(range(16, CONV_WIDTH)))
    off = CONV_HALO - (CONV_WIDTH - 1)
    mesh = plsc.VectorSubcoreMesh(core_axis_name="c", subcore_axis_name="s")

    @functools.partial(
        pl.kernel, mesh=mesh,
        out_type=jax.ShapeDtypeStruct((SEQ, D_CONV), F32),
        scratch_types=[pltpu.VMEM((CONV_HALO + SC_ROWS, D_CONV), F32),
                       pltpu.VMEM((SC_ROWS, D_CONV), F32),
                       pltpu.VMEM((32, D_CONV), F32),
                       pltpu.VMEM((SUBLANES, D_CONV), F32)],
        compiler_params=pltpu.CompilerParams(use_tc_tiling_on_sc=True),
        name="sc_conv",
    )
    def k(hc_hbm, w_hbm, b_hbm, out_hbm, x_v, o_v, w_v, b_v):
        wid = lax.axis_index("s") * SC_CORES + lax.axis_index("c")
        pltpu.sync_copy(w_hbm, w_v)
        pltpu.sync_copy(b_hbm, b_v)

        @pl.loop(0, n_chunks)
        def _(ci):
            row0 = pl.multiple_of(wid * rows_per_worker + ci * SC_ROWS, SC_ROWS)

            @pl.when(row0 == 0)
            def _():
                @pl.loop(0, CONV_HALO)
                def _(r):
                    @pl.loop(0, D_CONV // SC_LANES)
                    def _(cg):
                        x_v[r, pl.ds(cg * SC_LANES, SC_LANES)] = jnp.zeros((SC_LANES,), F32)
                pltpu.sync_copy(hc_hbm.at[pl.ds(0, SC_ROWS)], x_v.at[pl.ds(CONV_HALO, SC_ROWS)])

            @pl.when(row0 > 0)
            def _():
                pltpu.sync_copy(hc_hbm.at[pl.ds(row0 - CONV_HALO, CONV_HALO + SC_ROWS)], x_v)

            @pl.loop(0, D_CONV // SC_LANES)
            def _(cg):
                lanes = pl.ds(cg * SC_LANES, SC_LANES)
                for hi, taps in enumerate(halves):
                    wv = [w_v[t, lanes] for t in taps]

                    @pl.loop(0, SC_ROWS)
                    def _(r):
                        acc = b_v[0, lanes] if hi == 0 else o_v[r, lanes]
                        for j, t in enumerate(taps):
                            acc = acc + x_v[r + off + t, lanes] * wv[j]
                        o_v[r, lanes] = acc

            pltpu.sync_copy(o_v, out_hbm.at[pl.ds(row0, SC_ROWS)])

    return k(hc, w32, b8)


def _compress_kernel(c_ref, pe_ref, w1_ref, wblk_ref, w2_ref, o_ref):
    half = CMP_STRIDE * HEAD_DIM
    acc = jnp.zeros((N_CMP_PAD, 4 * CMP_HIDDEN), F32)
    for l in range(CMP_STRIDE):
        x = c_ref[pl.ds(l, N_CMP_PAD, stride=CMP_STRIDE), :].astype(BF16)
        acc = acc + _dot(x, wblk_ref[0, l])
    pe = jnp.broadcast_to(pe_ref[0], (SUBLANES, 2 * half)).astype(BF16)
    pe_term = (_dot(pe[:, :half], w1_ref[0, :half, :]) + _dot(pe[:, half:], w1_ref[0, half:, :]))[0:1]
    for g in range(N_KV):
        a = acc[:, 2 * g * CMP_HIDDEN:(2 * g + 1) * CMP_HIDDEN]
        b = acc[:, (2 * g + 1) * CMP_HIDDEN:(2 * g + 2) * CMP_HIDDEN]
        b_up = jnp.concatenate([b[1:], jnp.zeros((1, CMP_HIDDEN), F32)], axis=0)
        pre = a + b_up + pe_term
        hid = (pre * _sigmoid(pre)).astype(BF16)
        o_ref[g] = _dot(hid, w2_ref[0])


def _compress(kcvc, pe, w1, wblk, w2):
    return pl.pallas_call(
        _compress_kernel,
        out_shape=jax.ShapeDtypeStruct((2 * N_KV, N_CMP_PAD, HEAD_DIM), F32),
        grid=(2,),
        in_specs=[pl.BlockSpec((SEQ, D_KV), lambda i: (0, i)),
                  pl.BlockSpec((1, 1, 2048), lambda i: (i, 0, 0)),
                  pl.BlockSpec((1, 2048, CMP_HIDDEN), lambda i: (i, 0, 0)),
                  pl.BlockSpec((1, CMP_STRIDE, D_KV, 4 * CMP_HIDDEN), lambda i: (i, 0, 0, 0)),
                  pl.BlockSpec((1, CMP_HIDDEN, HEAD_DIM), lambda i: (i, 0, 0))],
        out_specs=pl.BlockSpec((N_KV, N_CMP_PAD, HEAD_DIM), lambda i: (i, 0, 0)),
        compiler_params=_params(("arbitrary",)),
        name="compress",
    )(kcvc, pe, w1, wblk, w2)


def _tile_update(s, m, acc, vt):
    m_new = jnp.maximum(m, jnp.max(s, axis=0, keepdims=True))
    alpha = jnp.exp2(m - m_new)
    p = jnp.exp2(s - m_new).astype(BF16)
    return m_new, alpha * acc + _dot(vt, p)


def _finish(acc):
    return acc[:HEAD_DIM] / jnp.maximum(acc[HEAD_DIM:HEAD_DIM + 1], 1e-30)


def _attn_kernel(qt_ref, gt_ref, kc_ref, vct_ref, ovt_ref, ksl_ref, vslt_ref, kw_ref, vwt_ref,
                 ct_ref, wt_ref, o_ref, s_scr, p_scr, sel_scr, wa_scr, wb_scr, sa_scr, sb_scr):
    qb = pl.program_id(0)
    q0 = qb * TQ
    qt = qt_ref[...]
    gt = gt_ref[...]
    m0 = jnp.full((1, COLS), M_FLOOR, F32)
    acc0 = jnp.zeros((VROWS, COLS), F32)

    def template(g, first_row, rows0):
        r0 = jnp.where(first_row >= KPAD, rows0, WINDOW + TQ)
        return wt_ref[g, pl.ds(pl.multiple_of(r0, TK), TK), :]

    def real_row(first_row):
        return pl.multiple_of(jnp.maximum(first_row - KPAD, 0), TK)

    comb = []
    zq = jnp.zeros((HEAD_DIM, TQ), BF16)
    for g in range(N_KV):
        heads = [qt[(g * HPG + h) * HEAD_DIM:(g * HPG + h + 1) * HEAD_DIM, :] for h in range(HPG)]
        q_g = jnp.concatenate(
            [jnp.concatenate([q_h, zq] if g == 0 else [zq, q_h], axis=0) for q_h in heads],
            axis=1)

        mcol = m0
        for ct in range(N_CMP_PAD // TK):
            rows = slice(ct * TK, (ct + 1) * TK)
            r0 = pl.multiple_of(jnp.clip(TK * ct - (TQ // CMP_STRIDE) * qb + CT_ZERO,
                                         0, CT_ROWS - TK), SUBLANES)
            s = _dot(kc_ref[rows, :], q_g) + ct_ref[g, pl.ds(r0, TK), :]
            s_scr[rows, :] = s
            mcol = jnp.maximum(mcol, jnp.max(s, axis=0, keepdims=True))
        acc_c = acc0
        for ct in range(N_CMP_PAD // TK):
            rows = slice(ct * TK, (ct + 1) * TK)
            p = jnp.exp2(s_scr[rows, :] - mcol)
            p_scr[rows, :] = p
            acc_c = acc_c + _dot(vct_ref[g, :, rows], p.astype(BF16))
        rinv = 1.0 / jnp.maximum(acc_c[HEAD_DIM:HEAD_DIM + 1], 1e-30)
        o_c = acc_c[:HEAD_DIM] * rinv
        imp = jnp.zeros((N_SLC, TQ), F32)
        for ct in range(N_CMP_PAD // TK):
            rows = slice(ct * TK, (ct + 1) * TK)
            pn = p_scr[rows, :] * rinv
            ps = pn[:, 0:TQ] + pn[:, TQ:2 * TQ] + pn[:, 2 * TQ:3 * TQ] + pn[:, 3 * TQ:4 * TQ]
            hi = ps.astype(BF16)
            lo = (ps - hi.astype(F32)).astype(BF16)
            ov = ovt_ref[:, rows]
            imp = imp + _dot(ov, hi) + _dot(ov, lo)

        m, acc = m0, acc0
        for a in range(3):
            first = q0 + TK * a
            s = _dot(kw_ref[pl.ds(real_row(first), TK), :], q_g)
            s = s + template(g, first, TK * a)
            m, acc = _tile_update(s, m, acc, vwt_ref[g, jnp.maximum(qb + a - KPAD // TK, 0)])
        o_w = _finish(acc)

        cur = jnp.right_shift(q0 + lax.broadcasted_iota(jnp.int32, (1, TQ), 1), 6)
        jcol = lax.broadcasted_iota(jnp.int32, (N_SLC, 1), 0)
        forced = (jcol == 0) | (jcol == cur) | (jcol == cur - 1)
        sc = jnp.where(forced, TAKEN, jnp.where(jcol <= cur, imp, -1.0))
        jf = lax.broadcasted_iota(jnp.int32, (N_SLC, TQ), 0).astype(F32)
        for _ in range(SLC_TOPK - N_FORCED):
            mx = jnp.max(sc, axis=0, keepdims=True)
            idx = jnp.min(jnp.where(sc == mx, jf, 1e9), axis=0, keepdims=True)
            sc = jnp.where(jf == idx, TAKEN, sc)
        selneg = jnp.where(sc == TAKEN, 0.0, NEG)
        unused = jnp.zeros((SUBLANES - BLK_PER_TILE, TQ), F32)
        for kt in range(SEQ // TK):
            sel_scr[kt] = jnp.concatenate(
                [selneg[kt * BLK_PER_TILE:(kt + 1) * BLK_PER_TILE, :], unused], axis=0)
        sel_scr[SEQ // TK] = jnp.concatenate(
            [jnp.full((BLK_PER_TILE, TQ), NEG, F32), unused], axis=0)

        zrows = jnp.zeros((2 * D_KV - D_KV - 2 * SUBLANES, COLS), BF16)
        for w_scr in (wa_scr, wb_scr):
            w_scr[0:D_KV, :] = q_g
            w_scr[D_KV + 2 * SUBLANES:, :] = zrows

        def sel_scores(w_scr, first_row, sel_idx):
            blk = sel_scr[sel_idx]
            rows8 = jnp.concatenate([blk] * HPG, axis=1).astype(BF16)
            w_scr[D_KV:D_KV + 2 * SUBLANES, :] = jnp.concatenate(
                [rows8, jnp.zeros((SUBLANES, COLS), BF16)], axis=0)
            return _dot(ksl_ref[pl.ds(real_row(first_row), TK), :], w_scr[...])

        m, acc = m0, acc0
        for a, w_scr in ((0, wa_scr), (1, wb_scr)):
            first = q0 + TQ + TK * a
            s = sel_scores(w_scr, first, jnp.maximum(qb - 1 + a, 0))
            s = s + template(g, first, TQ + TK * a)
            m, acc = _tile_update(s, m, acc, vslt_ref[g, jnp.maximum(qb - 1 + a, 0)])

        n_far = jnp.maximum(qb - 1, 0)

        def far_scores(w_scr, kt):
            valid = kt < n_far
            ks = jnp.where(valid, kt, 0)
            return sel_scores(w_scr, KPAD + ks * TK, jnp.where(valid, kt, SEQ // TK))

        def far_v(kt):
            return vslt_ref[g, jnp.where(kt < n_far, kt, 0)]

        bufs = ((sa_scr, wa_scr), (sb_scr, wb_scr))

        def issue_scores(buf, kt):
            s_scr_k, w_scr_k = bufs[buf]
            s = far_scores(w_scr_k, kt)
            s_scr_k[...] = s
            return jnp.max(s, axis=0, keepdims=True)

        def far_body(i, carry):
            states, smax = [list(c) for c in carry[0]], carry[1]
            for u in range(FAR_UNROLL):
                kt = i * FAR_UNROLL + u
                smax_nxt = issue_scores((u + 1) % 2, kt + 1)
                m_i, acc_i = states[u % 2]
                m_new = jnp.maximum(m_i, smax)
                alpha = jnp.exp2(m_i - m_new)
                p = jnp.exp2(bufs[u % 2][0][...] - m_new).astype(BF16)
                states[u % 2] = [m_new, alpha * acc_i + _dot(far_v(kt), p)]
                smax = smax_nxt
            return tuple(tuple(st) for st in states), smax

        trips = (n_far + FAR_UNROLL - 1) // FAR_UNROLL
        ((m_a, acc_a), (m_b, acc_b)), _ = lax.fori_loop(
            0, trips, far_body, (((m, acc), (m0, acc0)), issue_scores(0, 0)))
        m_f = jnp.maximum(m_a, m_b)
        acc = jnp.exp2(m_a - m_f) * acc_a + jnp.exp2(m_b - m_f) * acc_b
        o_s = _finish(acc)

        per_head = []
        for h in range(HPG):
            c0 = 3 * (g * HPG + h)
            cs = slice(h * TQ, (h + 1) * TQ)
            per_head.append(gt[c0:c0 + 1, :] * o_c[:, cs] + gt[c0 + 1:c0 + 2, :] * o_s[:, cs]
                            + gt[c0 + 2:c0 + 3, :] * o_w[:, cs])
        comb.append(per_head)

    for h in range(HPG):
        merged = jnp.concatenate([comb[0][h], comb[1][h]], axis=0)
        o_ref[0, h] = merged.T.astype(BF16)


def _attn(qt, gatet, kc, vct, ovt, ksl, vslt, kw, vwt, ctmpl, wtmpl):
    consts = [kc, vct, ovt, ksl, vslt, kw, vwt, ctmpl, wtmpl]
    return pl.pallas_call(
        _attn_kernel,
        out_shape=jax.ShapeDtypeStruct((SEQ // TQ, HPG, TQ, LANES), BF16),
        grid=(SEQ // TQ,),
        in_specs=[pl.BlockSpec((D_ATT, TQ), lambda i: (0, i)),
                  pl.BlockSpec((LANES, TQ), lambda i: (0, i))]
                 + [_const_spec(a.shape) for a in consts],
        out_specs=pl.BlockSpec((1, HPG, TQ, LANES), lambda i: (i, 0, 0, 0)),
        scratch_shapes=[pltpu.VMEM((N_CMP_PAD, COLS), F32), pltpu.VMEM((N_CMP_PAD, COLS), F32),
                        pltpu.VMEM((SEL_ENTRIES, SUBLANES, TQ), F32),
                        pltpu.VMEM((2 * D_KV, COLS), BF16), pltpu.VMEM((2 * D_KV, COLS), BF16),
                        pltpu.VMEM((TK, COLS), F32), pltpu.VMEM((TK, COLS), F32)],
        compiler_params=_params(("arbitrary",)),
        name="nsa_attn",
    )(qt, gatet, *consts)


def _t5_bucket(dist):
    max_exact = N_BUCKETS // 2
    d = jnp.maximum(dist, 0)
    df = jnp.maximum(d, 1).astype(F32)
    large = max_exact + (jnp.log(df / max_exact) / math.log(MAX_DIST / max_exact)
                         * (N_BUCKETS - max_exact)).astype(jnp.int32)
    large = jnp.minimum(large, N_BUCKETS - 1)
    return jnp.where(d < max_exact, d, large)


def _template_kernel(f_ref, gx_ref, wt_ref, ct_ref):
    wk, lanes = WINDOW + TQ, f_ref.shape[1]
    n_band = CMP_R_HI - CMP_R_LO + 1
    lo = CT_ZERO + CMP_R_LO
    for hd in range(N_HEADS):
        g, cols = hd // HPG, slice((hd % HPG) * TQ, (hd % HPG + 1) * TQ)
        x = jnp.broadcast_to(f_ref[hd:hd + 1, :], (wk, lanes))
        y = pltpu.roll(x, lanes - (wk - 1), axis=1, stride=1, stride_axis=0)
        wt_ref[g, 0:wk, cols] = y[:, :TQ]
        wt_ref[g, wk:, cols] = jnp.full((TK, TQ), NEG, F32)
        xb = jnp.broadcast_to(gx_ref[hd:hd + 1, :], (n_band, lanes))
        yb = pltpu.roll(xb, lanes - CMP_STRIDE * (n_band - 1), axis=1,
                        stride=CMP_STRIDE, stride_axis=0)
        ct_ref[g, 0:lo, cols] = jnp.zeros((lo, TQ), F32)
        ct_ref[g, lo:lo + n_band, cols] = yb[:, :TQ]
        ct_ref[g, lo + n_band:, cols] = jnp.full((CT_ROWS - lo - n_band, TQ), NEG, F32)


def _templates(fext, gxext):
    return pl.pallas_call(
        _template_kernel,
        out_shape=[jax.ShapeDtypeStruct((N_KV, WINDOW + TQ + TK, COLS), F32),
                   jax.ShapeDtypeStruct((N_KV, CT_ROWS, COLS), F32)],
        compiler_params=_params(None),
        name="templates",
    )(fext, gxext)


def _bias_templates(rel_bias):
    biasp = rel_bias[:, _t5_bucket(jnp.arange(MAX_DIST))] - rel_bias[:, N_BUCKETS - 1:]
    biasp = biasp * LOG2E

    def by_distance(n_neg, n_zero, n_neg_after):
        return jnp.concatenate(
            [jnp.full((N_HEADS, n_neg), NEG, F32), biasp, jnp.zeros((N_HEADS, n_zero), F32),
             jnp.full((N_HEADS, n_neg_after), NEG, F32)], axis=1)

    wk = WINDOW + TQ
    f = by_distance(TQ - 1, WINDOW - MAX_DIST, wk - WINDOW)
    fext = jnp.concatenate([f, jnp.zeros((N_HEADS, 1), F32)], axis=1)

    d_min = -(CMP_LEN - 1) - CMP_STRIDE * CMP_R_HI
    d_max = TQ - 1 - (CMP_LEN - 1) - CMP_STRIDE * CMP_R_LO
    gx = by_distance(-d_min, d_max + 1 - MAX_DIST, 0)
    gxext = jnp.pad(gx, ((0, 0), (0, fext.shape[1] - gx.shape[1])))
    wtmpl, ctmpl = _templates(fext, gxext)
    return ctmpl, wtmpl


def _overlap_t():
    cmp_start = jnp.arange(N_CMP_PAD) * CMP_STRIDE
    slc_start = jnp.arange(N_SLC) * SLC_LEN
    ov = ((cmp_start[None, :] < slc_start[:, None] + SLC_LEN)
          & (cmp_start[None, :] + CMP_LEN > slc_start[:, None])
          & (jnp.arange(N_CMP_PAD)[None, :] < N_CMP))
    return ov.astype(BF16)


def _compress_block_weights(w1):
    half = CMP_STRIDE * HEAD_DIM
    top = w1[:, :half].reshape(2, CMP_STRIDE, HEAD_DIM, CMP_HIDDEN)
    bot = w1[:, half:].reshape(2, CMP_STRIDE, HEAD_DIM, CMP_HIDDEN)
    z = jnp.zeros_like(top)
    return jnp.concatenate([jnp.concatenate([top, bot, z, z], axis=-1),
                            jnp.concatenate([z, z, top, bot], axis=-1)], axis=2)


def kernel(x, c, w_ada, b_ada, g_ffn1, w_gu1, w_down1, g_mix, w_in, w_dw, b_dw, ln_g, ln_b,
           pe_k, pe_v, w_ck1, w_ck2, w_cv1, w_cv2, rel_bias, w_out, g_ffn2, w_gu2, w_down2,
           g_final):
    assert x.shape == (1, SEQ, D_MODEL) and w_ada.shape[0] == 1
    x2 = x[0]
    mod = _ada(c.reshape(D_MODEL, 1), w_ada[0], b_ada)
    x1 = _ffn1(x2, g_ffn1, mod, w_gu1[0].astype(BF16), w_down1[0].astype(BF16))

    wi = w_in[0]
    n_in = wi.shape[1]
    col = jnp.arange(n_in)
    q_cols = (col >= 2 * D_CONV) & (col < 2 * D_CONV + D_ATT)
    w_in_p = jnp.pad(wi * jnp.where(q_cols, HEAD_DIM ** -0.5 * LOG2E, 1.0)[None, :],
                     ((0, 0), (0, D_IN_PAD - n_in))).astype(BF16)
    hc, kcvc, ksl, kw, qt, gatet, vslt, vwt = _proj(x1, g_mix, mod, w_in_p)

    pe = jnp.stack([pe_k[0].reshape(1, -1), pe_v[0].reshape(1, -1)])
    w1 = jnp.stack([w_ck1[0], w_cv1[0]]).astype(BF16)
    w2 = jnp.stack([w_ck2[0], w_cv2[0]]).astype(BF16)
    cmp = _compress(kcvc, pe, w1, _compress_block_weights(w1), w2)
    kc = jnp.concatenate([cmp[0], cmp[1]], axis=1).astype(BF16)
    ones_rows = jnp.zeros((N_KV, VROWS - HEAD_DIM, N_CMP_PAD), BF16).at[:, 0].set(1)
    vct = jnp.concatenate([jnp.swapaxes(cmp[2:4], 1, 2).astype(BF16), ones_rows], axis=1)

    ctmpl, wtmpl = _bias_templates(rel_bias)
    y_att = _attn(qt, gatet, kc, vct, _overlap_t(), ksl, vslt, kw, vwt, ctmpl, wtmpl)

    y_conv = _sc_conv(hc, jnp.pad(w_dw[0], ((0, 32 - CONV_WIDTH), (0, 0))),
                      jnp.broadcast_to(b_dw, (SUBLANES, D_CONV)))

    out = _ffn2(x1, y_conv, y_att, w_out[0].astype(BF16), mod,
                g_ffn2, w_gu2[0].astype(BF16), w_down2[0].astype(BF16),
                g_final.reshape(1, D_MODEL), ln_g, ln_b)
    return out[None]
```
